```python
import jax
import jax.numpy as jnp
from jax import lax
import numpy as np

D_MODEL = 1024
BATCH = 1
SEQ = 16384
DEPTH = 1

GRID_W = 64
CTX_LEN = 256
D_MIX = D_MODEL
D_CONV = D_MIX // 2
CONV_WIDTH = 3
D_RWKV = D_MIX - D_CONV
RWKV_HEAD_DIM = 64
RWKV_HEADS = D_RWKV // RWKV_HEAD_DIM
DECAY_RANK = 64
ICLR_RANK = 64
GATE_RANK = 128
P_RWKV = 3 * D_RWKV + 2 * DECAY_RANK + 2 * ICLR_RANK + GATE_RANK
P_IN = 3 * D_CONV + P_RWKV
N_EXPERTS = 256
TOP_K = 8
D_EXPERT = 256
ROUTED_SCALE = 2.5
MOE_BLOCK = 128
N_MOD = 6
RMS_EPS = 1e-6
GN_EPS = 64e-5
DECAY_SCALE = 0.6065306597126334
NORM_EPS = 1e-12

kernel_name = 'hybrid_conv_rwkv7_moe_dit_block'


def rmsnorm(x, g):
    xf = x.astype(jnp.float32)
    y = xf * lax.rsqrt(jnp.mean(xf * xf, axis=-1, keepdims=True) + RMS_EPS)
    return (y * g.astype(jnp.float32)).astype(x.dtype)


def modulate(h, shift, scale):
    return h * (1.0 + scale) + shift


def token_shift_bidir(p, mu):
    zero = jnp.zeros_like(p[:, :1])
    prev = jnp.concatenate([zero, p[:, :-1]], axis=1)
    nxt = jnp.concatenate([p[:, 1:], zero], axis=1)
    return p + mu[0] * (prev - p) + mu[1] * (nxt - p)


def dwconv_centred(z, w):
    half = CONV_WIDTH // 2
    n = z.shape[-2]
    zp = jnp.pad(z, [(0, 0)] * (z.ndim - 2) + [(half, half), (0, 0)])
    return sum(w[i] * zp[..., i:i + n, :] for i in range(CONV_WIDTH))


def short_conv_group(p_conv, conv_w, on_grid):
    b_gate, c_gate, v = jnp.split(p_conv, 3, axis=-1)
    z = c_gate * v
    if on_grid:
        bsz, n_tok, ch = z.shape
        rows = n_tok // GRID_W
        y = dwconv_centred(z.reshape(bsz, rows, GRID_W, ch), conv_w).reshape(bsz, n_tok, ch)
    else:
        y = dwconv_centred(z, conv_w)
    return b_gate * y


def rwkv_inputs(p_rw, lp):
    f32 = lambda t: t.astype(jnp.float32)
    p = f32(p_rw)
    bsz, n_tok, _ = p.shape
    hd = (RWKV_HEADS, RWKV_HEAD_DIM)
    r = p[..., :D_RWKV]
    k = p[..., D_RWKV:2 * D_RWKV]
    v = p[..., 2 * D_RWKV:3 * D_RWKV]
    o = 3 * D_RWKV
    w_lo = p[..., o:o + 2 * DECAY_RANK].reshape(bsz, n_tok, 2, DECAY_RANK)
    o += 2 * DECAY_RANK
    a_lo = p[..., o:o + 2 * ICLR_RANK].reshape(bsz, n_tok, 2, ICLR_RANK)
    o += 2 * ICLR_RANK
    g_lo = p[..., o:]
    d = f32(lp['decay_w0']) + jnp.einsum('bldr,drc->bldc', jnp.tanh(w_lo), f32(lp['decay_up']))
    w = jnp.exp(-DECAY_SCALE * jax.nn.sigmoid(d))
    a = jax.nn.sigmoid(f32(lp['iclr_a0']) + jnp.einsum('bldr,drc->bldc', a_lo, f32(lp['iclr_up'])))
    kap = (k[:, :, None, :] * f32(lp['key_xi'])).reshape(bsz, n_tok, 2, *hd)
    kh = kap * lax.rsqrt(jnp.sum(kap * kap, axis=-1, keepdims=True) + NORM_EPS)
    kt = k[:, :, None, :] * (1.0 + (a - 1.0) * f32(lp['key_alpha']))
    g = jax.nn.sigmoid(g_lo) @ f32(lp['gate_up'])
    heads = lambda t: t.reshape(bsz, n_tok, 2, *hd)
    return (r.reshape(bsz, n_tok, *hd), heads(w), heads(kt), kh, heads(a),
            v.reshape(bsz, n_tok, *hd), g)


def project(h, lp):
    p = h @ lp['w_in']
    p_conv = p[..., :3 * D_CONV]
    p_rw = token_shift_bidir(p[..., 3 * D_CONV:], lp['shift_mu'])
    return p_conv, rwkv_inputs(p_rw, lp)


def wkv_scan(s0, ins, direction, reverse):
    r, w, kt, kh, a, v, _ = ins
    xs = tuple(jnp.moveaxis(t, 1, 0) for t in (r, w[:, :, direction], kt[:, :, direction],
                                                  kh[:, :, direction], a[:, :, direction], v))

    def step(s, inp):
        r_t, w_t, kt_t, kh_t, a_t, v_t = inp
        s_kh = jnp.einsum('bhvk,bhk->bhv', s, kh_t)
        s = (s * w_t[:, :, None, :]
             - s_kh[..., None] * (kh_t * a_t)[:, :, None, :]
             + v_t[..., None] * kt_t[:, :, None, :])
        return s, jnp.einsum('bhvk,bhk->bhv', s, r_t)

    s_final, ys = lax.scan(step, s0, xs, reverse=reverse)
    return jnp.moveaxis(ys, 0, 1), s_final


def rwkv_readout(ins, y_f, y_b, lp):
    r, _, kt, _, _, v, g = ins
    bsz, n_tok = r.shape[:2]
    hd = (RWKV_HEADS, RWKV_HEAD_DIM)
    y = y_f + y_b
    mu = jnp.mean(y, axis=-1, keepdims=True)
    var = jnp.mean(jnp.square(y - mu), axis=-1, keepdims=True)
    yn = ((y - mu) * lax.rsqrt(var + GN_EPS) * lp['gn_w'].astype(jnp.float32).reshape(hd)
          + lp['gn_b'].astype(jnp.float32).reshape(hd))
    rho = lp['bonus_rho'].astype(jnp.float32).reshape(hd)
    bonus = jnp.einsum('blhn,bldhn->blh', r * rho, kt)[..., None] * v
    return (yn + bonus).reshape(bsz, n_tok, D_RWKV) * g


def mix_out(p_conv, ins, y_f, y_b, lp, on_grid):
    y_conv = short_conv_group(p_conv, lp['conv_w'], on_grid)
    y_rw = rwkv_readout(ins, y_f, y_b, lp).astype(y_conv.dtype)
    return jnp.concatenate([y_conv, y_rw], axis=-1) @ lp['w_out']


def swiglu(h, w1, w3, w2):
    return (jax.nn.silu(h @ w1) * (h @ w3)) @ w2


def routed_experts(h, router_w, router_bias, w1, w3, w2):
    n_tok, dm = h.shape
    scores = jax.nn.sigmoid((h @ router_w).astype(jnp.float32))
    _, idx = lax.top_k(scores + router_bias.astype(jnp.float32), TOP_K)
    gate = jnp.take_along_axis(scores, idx, axis=-1)
    gate = gate / jnp.sum(gate, axis=-1, keepdims=True) * ROUTED_SCALE
    n_assign = n_tok * TOP_K
    flat_e = idx.reshape(-1)
    flat_tok = jnp.arange(n_assign, dtype=jnp.int32) // TOP_K
    order = jnp.argsort(flat_e)
    e_sorted = flat_e[order]
    counts = jnp.zeros((N_EXPERTS,), jnp.int32).at[flat_e].add(1)
    padded = (counts + MOE_BLOCK - 1) // MOE_BLOCK * MOE_BLOCK
    pad_end = jnp.cumsum(padded)
    pad_start = pad_end - padded
    start = jnp.cumsum(counts) - counts
    rank = jnp.arange(n_assign, dtype=jnp.int32) - start[e_sorted]
    dest = pad_start[e_sorted] + rank
    n_blocks = -(-n_assign // MOE_BLOCK) + N_EXPERTS
    slot_tok = jnp.full((n_blocks * MOE_BLOCK,), n_tok, jnp.int32).at[dest].set(flat_tok[order])
    slot_gate = jnp.zeros((n_blocks * MOE_BLOCK,), jnp.float32).at[dest].set(gate.reshape(-1)[order])
    block_e = jnp.minimum(jnp.searchsorted(pad_end, jnp.arange(n_blocks, dtype=jnp.int32) * MOE_BLOCK,
                                           side='right'), N_EXPERTS - 1)
    h_pad = jnp.concatenate([h, jnp.zeros((1, dm), h.dtype)], axis=0)

    def body(acc, blk):
        tok, gt, e = blk
        yb = swiglu(h_pad[tok], w1[e], w3[e], w2[e])
        return acc.at[tok].add(yb * gt[:, None].astype(yb.dtype)), None

    acc, _ = lax.scan(body, jnp.zeros((n_tok + 1, dm), h.dtype),
                      (slot_tok.reshape(n_blocks, MOE_BLOCK), slot_gate.reshape(n_blocks, MOE_BLOCK), block_e))
    return acc[:n_tok]


def channel_mixer(h, lp):
    bsz, n_tok, dm = h.shape
    hf = h.reshape(bsz * n_tok, dm)
    y = (routed_experts(hf, lp['router_w'], lp['router_bias'], lp['exp_w1'], lp['exp_w3'], lp['exp_w2'])
         + swiglu(hf, lp['sh_w1'], lp['sh_w3'], lp['sh_w2']))
    return y.reshape(bsz, n_tok, dm)


def layer(x, ctx, c, c_ctx, lp, update_ctx):
    bsz = x.shape[0]
    mod = (jax.nn.silu(c) @ lp['ada_w'] + lp['ada_b'])[:, None, :]
    mod_c = (jax.nn.silu(c_ctx) @ lp['ada_w'] + lp['ada_b'])[None, None, :]
    sh1, sc1, g1, sh2, sc2, g2 = jnp.split(mod, N_MOD, axis=-1)
    csh1, csc1, cg1, csh2, csc2, cg2 = jnp.split(mod_c, N_MOD, axis=-1)

    hc = modulate(rmsnorm(ctx, lp['norm1_g']), csh1, csc1)
    pc_conv, ins_c = project(hc, lp)
    s0 = jnp.zeros((bsz, RWKV_HEADS, RWKV_HEAD_DIM, RWKV_HEAD_DIM), jnp.float32)
    yc_f, s_f = wkv_scan(s0, ins_c, 0, reverse=False)
    yc_b, s_b = wkv_scan(s0, ins_c, 1, reverse=True)

    hx = modulate(rmsnorm(x, lp['norm1_g']), sh1, sc1)
    px_conv, ins_x = project(hx, lp)
    yx_f, _ = wkv_scan(s_f, ins_x, 0, reverse=False)
    yx_b, _ = wkv_scan(s_b, ins_x, 1, reverse=True)
    x = x + g1 * mix_out(px_conv, ins_x, yx_f, yx_b, lp, on_grid=True)
    x = x + g2 * channel_mixer(modulate(rmsnorm(x, lp['norm2_g']), sh2, sc2), lp)

    if update_ctx:
        ctx = ctx + cg1 * mix_out(pc_conv, ins_c, yc_f, yc_b, lp, on_grid=False)
        ctx = ctx + cg2 * channel_mixer(modulate(rmsnorm(ctx, lp['norm2_g']), csh2, csc2), lp)
    return x, ctx


def setup_inputs(seed: int = 0) -> dict:
    key = jax.random.key(seed)
    ks = jax.random.split(key, 32)

    def nrm(k, shape, scale):
        return scale * jax.random.normal(k, shape, jnp.float32)

    L = DEPTH
    return {
        'x': nrm(ks[0], (BATCH, SEQ, D_MODEL), 1.0),
        'c': nrm(ks[1], (BATCH, D_MODEL), 1.0),
        'ctx': nrm(ks[2], (BATCH, CTX_LEN, D_MODEL), 1.0),
        'c_ctx': nrm(ks[3], (D_MODEL,), 1.0),
        'ada_w': nrm(ks[4], (L, D_MODEL, N_MOD * D_MODEL), 0.5 * D_MODEL ** -0.5),
        'ada_b': nrm(ks[5], (L, N_MOD * D_MODEL), 0.02),
        'norm1_g': 1.0 + nrm(ks[6], (L, D_MODEL), 0.02),
        'norm2_g': 1.0 + nrm(ks[7], (L, D_MODEL), 0.02),
        'w_in': nrm(ks[8], (L, D_MODEL, P_IN), D_MODEL ** -0.5),
        'conv_w': nrm(ks[9], (L, CONV_WIDTH, D_CONV), CONV_WIDTH ** -0.5),
        'shift_mu': jax.random.uniform(ks[10], (L, 2, P_RWKV), jnp.float32, 0.0, 0.5),
        'decay_w0': nrm(ks[11], (L, 2, D_RWKV), 1.0) - 1.0,
        'decay_up': nrm(ks[12], (L, 2, DECAY_RANK, D_RWKV), DECAY_RANK ** -0.5),
        'iclr_a0': nrm(ks[13], (L, 2, D_RWKV), 0.5),
        'iclr_up': nrm(ks[14], (L, 2, ICLR_RANK, D_RWKV), ICLR_RANK ** -0.5),
        'key_xi': 0.85 + nrm(ks[15], (L, 2, D_RWKV), 0.05),
        'key_alpha': 1.0 + nrm(ks[16], (L, 2, D_RWKV), 0.05),
        'bonus_rho': nrm(ks[17], (L, D_RWKV), 0.1),
        'gate_up': nrm(ks[18], (L, GATE_RANK, D_RWKV), GATE_RANK ** -0.5),
        'gn_w': 1.0 + nrm(ks[19], (L, D_RWKV), 0.02),
        'gn_b': nrm(ks[20], (L, D_RWKV), 0.02),
        'w_out': nrm(ks[21], (L, D_MIX, D_MODEL), D_MIX ** -0.5),
        'router_w': nrm(ks[22], (L, D_MODEL, N_EXPERTS), D_MODEL ** -0.5),
        'router_bias': nrm(ks[23], (L, N_EXPERTS), 0.01),
        'exp_w1': nrm(ks[24], (L, N_EXPERTS, D_MODEL, D_EXPERT), D_MODEL ** -0.5),
        'exp_w3': nrm(ks[25], (L, N_EXPERTS, D_MODEL, D_EXPERT), D_MODEL ** -0.5),
        'exp_w2': nrm(ks[26], (L, N_EXPERTS, D_EXPERT, D_MODEL), D_EXPERT ** -0.5),
        'sh_w1': nrm(ks[27], (L, D_MODEL, D_EXPERT), D_MODEL ** -0.5),
        'sh_w3': nrm(ks[28], (L, D_MODEL, D_EXPERT), D_MODEL ** -0.5),
        'sh_w2': nrm(ks[29], (L, D_EXPERT, D_MODEL), D_EXPERT ** -0.5),
        'final_g': 1.0 + nrm(ks[30], (D_MODEL,), 0.02),
    }


def reference(x, c, ctx, c_ctx, ada_w, ada_b, norm1_g, norm2_g, w_in, conv_w, shift_mu,
              decay_w0, decay_up, iclr_a0, iclr_up, key_xi, key_alpha, bonus_rho, gate_up,
              gn_w, gn_b, w_out, router_w, router_bias, exp_w1, exp_w3, exp_w2,
              sh_w1, sh_w3, sh_w2, final_g):
    for l in range(DEPTH):
        lp = dict(ada_w=ada_w[l], ada_b=ada_b[l], norm1_g=norm1_g[l], norm2_g=norm2_g[l],
                  w_in=w_in[l], conv_w=conv_w[l], shift_mu=shift_mu[l],
                  decay_w0=decay_w0[l], decay_up=decay_up[l], iclr_a0=iclr_a0[l], iclr_up=iclr_up[l],
                  key_xi=key_xi[l], key_alpha=key_alpha[l], bonus_rho=bonus_rho[l], gate_up=gate_up[l],
                  gn_w=gn_w[l], gn_b=gn_b[l], w_out=w_out[l],
                  router_w=router_w[l], router_bias=router_bias[l],
                  exp_w1=exp_w1[l], exp_w3=exp_w3[l], exp_w2=exp_w2[l],
                  sh_w1=sh_w1[l], sh_w3=sh_w3[l], sh_w2=sh_w2[l])
        x, ctx = layer(x, ctx, c, c_ctx, lp, update_ctx=(l < DEPTH - 1))
    return rmsnorm(x, final_g)
```

```python
import functools

import jax
import jax.numpy as jnp
from jax import lax
from jax.experimental import pallas as pl
from jax.experimental.pallas import tpu as pltpu

F32 = jnp.float32
BF16 = jnp.bfloat16

D_MODEL = 1024
D_CONV = 512
D_RWKV = 512
HEAD_DIM = 64
N_HEADS = D_RWKV // HEAD_DIM
N_PAIRS = N_HEADS // 2
LORA = 128
P_RWKV = 3 * D_RWKV + 3 * LORA
P_IN = 3 * D_CONV + P_RWKV
GRID_W = 64
N_EXPERTS = 256
TOP_K = 8
D_EXPERT = 256
ROUTED_SCALE = 2.5
RMS_EPS = 1e-6
GN_EPS = 64e-5
DECAY_SCALE = 0.6065306597126334
NORM_EPS = 1e-12

TOK_TILE = 256
SCAN_BLOCK = 256
CHUNK = 64
SLOT_BLOCK = 128
COMB_TILE = 128
LANES = 128
VMEM_LIMIT = 56 * 1024 * 1024

NN = ((1,), (0,))
NT = ((1,), (1,))
TN = ((0,), (0,))


def _dg(a, b, dims=NN):
    return lax.dot_general(a, b, (dims, ((), ())), preferred_element_type=F32)


def _split2(a):
    hi = a.astype(BF16)
    lo = (a - hi.astype(F32)).astype(BF16)
    return hi, lo


def _split3(a):
    hi = a.astype(BF16)
    r1 = a - hi.astype(F32)
    mid = r1.astype(BF16)
    lo = (r1 - mid.astype(F32)).astype(BF16)
    return hi, mid, lo


def _mm1(a, b, dims=NN):
    return _dg(a.astype(BF16), b.astype(BF16), dims)


def _mm3(a, b, dims=NN):
    ah, al = _split2(a)
    bh, bl = _split2(b)
    return _dg(ah, bh, dims) + (_dg(ah, bl, dims) + _dg(al, bh, dims))


def _mm_ones_rhs(a, ones_bf16, dims=NN):
    h, m, l = _split3(a)
    return _dg(h, ones_bf16, dims) + (_dg(m, ones_bf16, dims) + _dg(l, ones_bf16, dims))


def _mm_ones_lhs(ones_bf16, b, dims=NN):
    h, m, l = _split3(b)
    return _dg(ones_bf16, h, dims) + (_dg(ones_bf16, m, dims) + _dg(ones_bf16, l, dims))


def _sigmoid(x):
    return 1.0 / (1.0 + jnp.exp(-x))


def _silu(x):
    return x * _sigmoid(x)


def _rms(xv, g):
    ms = jnp.mean(xv * xv, axis=-1, keepdims=True)
    return xv * lax.rsqrt(ms + RMS_EPS) * g


def _cparams(sem):
    return pltpu.CompilerParams(dimension_semantics=sem, vmem_limit_bytes=VMEM_LIMIT)


def _ada_kernel(c_ref, w_ref, b_ref, o_ref):
    o_ref[...] = _mm3(_silu(c_ref[...]), w_ref[...]) + b_ref[...]


def _ada(cc, ada_w, ada_b):
    n = ada_w.shape[1]
    tn = 1024
    return pl.pallas_call(
        _ada_kernel,
        grid=(n // tn,),
        in_specs=[pl.BlockSpec((8, D_MODEL), lambda j: (0, 0)),
                  pl.BlockSpec((D_MODEL, tn), lambda j: (0, j)),
                  pl.BlockSpec((1, tn), lambda j: (0, j))],
        out_specs=pl.BlockSpec((8, tn), lambda j: (0, j)),
        out_shape=jax.ShapeDtypeStruct((8, n), F32),
        compiler_params=_cparams(("arbitrary",)),
        name="ada",
    )(cc, ada_w, ada_b)


def _proj_kernel(x_ref, xp_ref, xn_ref, mod_ref, n1g_ref, win_ref, convw_ref, mu_ref,
                 dw0_ref, dup_ref, ia0_ref, iup_ref, xi_ref, kal_ref, rho_ref, gup_ref, bsum_ref,
                 r_ref, v_ref, g_ref, bonus_ref, yconv_ref, lw_ref, kh_ref, kt_ref, a_ref):
    i = pl.program_id(0)
    nt = pl.num_programs(0)
    tm = x_ref.shape[0]
    is_ctx = i == 0
    sh = jnp.where(is_ctx, mod_ref[1:2, 0:D_MODEL], mod_ref[0:1, 0:D_MODEL])
    sc = jnp.where(is_ctx, mod_ref[1:2, D_MODEL:2 * D_MODEL], mod_ref[0:1, D_MODEL:2 * D_MODEL])
    n1g = n1g_ref[...]

    def norm_mod(xv):
        return _rms(xv, n1g) * (1.0 + sc) + sh

    h = norm_mod(x_ref[...]).astype(BF16)
    p = _dg(h, win_ref[...])
    hh = norm_mod(jnp.concatenate([xp_ref[...], xn_ref[...]], axis=0)).astype(BF16)
    ph = _dg(hh, win_ref[:, 3 * D_CONV:])

    rows = lax.broadcasted_iota(jnp.int32, (tm, 1), 0)

    bg = p[:, 0:D_CONV]
    z = p[:, D_CONV:2 * D_CONV] * p[:, 2 * D_CONV:3 * D_CONV]
    col = rows % GRID_W
    zp = jnp.where(col == 0, 0.0, pltpu.roll(z, 1, 0))
    zn = jnp.where(col == GRID_W - 1, 0.0, pltpu.roll(z, tm - 1, 0))
    yconv = bg * (convw_ref[0:1, :] * zp + convw_ref[1:2, :] * z + convw_ref[2:3, :] * zn)
    yconv_ref[...] = yconv.astype(BF16)

    prev_ok = jnp.logical_and(i != 0, i != 1).astype(F32)
    next_ok = jnp.logical_and(i != 0, i != nt - 1).astype(F32)
    cur = p[:, 3 * D_CONV:]
    prev = jnp.where(rows == 0, ph[7:8, :] * prev_ok, pltpu.roll(cur, 1, 0))
    nxt = jnp.where(rows == tm - 1, ph[8:9, :] * next_ok, pltpu.roll(cur, tm - 1, 0))
    ps = cur + mu_ref[0:1, :] * (prev - cur) + mu_ref[1:2, :] * (nxt - cur)

    r = ps[:, 0:D_RWKV]
    k = ps[:, D_RWKV:2 * D_RWKV]
    v = ps[:, 2 * D_RWKV:3 * D_RWKV]
    o = 3 * D_RWKV
    wlo = ps[:, o:o + LORA]
    alo = ps[:, o + LORA:o + 2 * LORA]
    glo = ps[:, o + 2 * LORA:o + 3 * LORA]

    dd = dw0_ref[...] + _mm3(jnp.tanh(wlo), dup_ref[...])
    lw = -DECAY_SCALE * _sigmoid(dd)
    a = _sigmoid(ia0_ref[...] + _mm3(alo, iup_ref[...]))
    g = _mm3(_sigmoid(glo), gup_ref[...])
    k2 = jnp.concatenate([k, k], axis=1)
    kap = k2 * xi_ref[...]
    kap2 = kap * kap
    bs = bsum_ref[...]
    ss = jnp.concatenate([_mm_ones_rhs(kap2[:, 0:D_RWKV], bs),
                          _mm_ones_rhs(kap2[:, D_RWKV:], bs)], axis=1)
    kh = kap * lax.rsqrt(ss + NORM_EPS)
    kt = k2 * (1.0 + (a - 1.0) * kal_ref[...])
    bon = _mm_ones_rhs(r * rho_ref[...] * (kt[:, 0:D_RWKV] + kt[:, D_RWKV:]), bs)

    g_ref[...] = g
    bonus_ref[...] = bon * v
    for pr in range(N_PAIRS):
        ls = slice(pr * LANES, (pr + 1) * LANES)
        r_ref[pr] = r[:, ls]
        v_ref[pr] = v[:, ls]
        for d in range(2):
            ld = slice(d * D_RWKV + pr * LANES, d * D_RWKV + (pr + 1) * LANES)
            lw_ref[d, pr] = lw[:, ld]
            kh_ref[d, pr] = kh[:, ld]
            kt_ref[d, pr] = kt[:, ld]
            a_ref[d, pr] = a[:, ld]


def _proj(xcat, mod, n1g, win, convw, mu, dw0, dup, ia0, iup, xi, kal, rho, gup, bsum):
    t = xcat.shape[0]
    tm = TOK_TILE
    nt = t // tm
    nb8 = t // 8
    full = lambda arr: pl.BlockSpec(arr.shape, lambda i: (0,) * arr.ndim)
    pair_spec = pl.BlockSpec((N_PAIRS, tm, LANES), lambda i: (0, i, 0))
    dpair_spec = pl.BlockSpec((2, N_PAIRS, tm, LANES), lambda i: (0, 0, i, 0))
    row_spec = pl.BlockSpec((tm, D_RWKV), lambda i: (i, 0))
    pair_shape = jax.ShapeDtypeStruct((N_PAIRS, t, LANES), F32)
    dpair_shape = jax.ShapeDtypeStruct((2, N_PAIRS, t, LANES), F32)
    consts = (mod, n1g, win, convw, mu, dw0, dup, ia0, iup, xi, kal, rho, gup, bsum)
    return pl.pallas_call(
        _proj_kernel,
        grid=(nt,),
        in_specs=[pl.BlockSpec((tm, D_MODEL), lambda i: (i, 0)),
                  pl.BlockSpec((8, D_MODEL), lambda i: (jnp.maximum(i * (tm // 8) - 1, 0), 0)),
                  pl.BlockSpec((8, D_MODEL), lambda i: (jnp.minimum((i + 1) * (tm // 8), nb8 - 1), 0))]
                 + [full(c) for c in consts],
        out_specs=[pair_spec, pair_spec, row_spec, row_spec, row_spec,
                   dpair_spec, dpair_spec, dpair_spec, dpair_spec],
        out_shape=[pair_shape, pair_shape,
                   jax.ShapeDtypeStruct((t, D_RWKV), F32), jax.ShapeDtypeStruct((t, D_RWKV), F32),
                   jax.ShapeDtypeStruct((t, D_CONV), BF16),
                   dpair_shape, dpair_shape, dpair_shape, dpair_shape],
        compiler_params=_cparams(("arbitrary",)),
        name="proj",
    )(xcat, xcat, xcat, *consts)


def _scan_kernel(r_ref, v_ref, lw_ref, kh_ref, kt_ref, a_ref, y_ref, s_ref):
    d = pl.program_id(0)
    j = pl.program_id(2)
    rev = d == 1
    c = CHUNK
    nsub = r_ref.shape[0] // c

    @pl.when(j == 0)
    def _():
        s_ref[...] = jnp.zeros(s_ref.shape, F32)

    row = lax.broadcasted_iota(jnp.int32, (c, c), 0)
    col = lax.broadcasted_iota(jnp.int32, (c, c), 1)
    strict = (col - row) * (1 - 2 * d) < 0
    eye = row == col
    incl = jnp.logical_or(strict, eye)
    incl_bf = jnp.where(incl, 1.0, 0.0).astype(BF16)
    eye_f = jnp.where(eye, 1.0, 0.0)

    def sub(s, carry):
        ci = jnp.where(rev, nsub - 1 - s, s)
        sl = pl.ds(pl.multiple_of(ci * c, c), c)
        r = r_ref[sl, :]
        v = v_ref[sl, :]
        lw = lw_ref[sl, :]
        kh = kh_ref[sl, :]
        kt = kt_ref[sl, :]
        a = a_ref[sl, :]

        cum = _mm_ones_lhs(incl_bf, lw)
        cum_last = jnp.where(rev, cum[0:1, :], cum[c - 1:c, :])
        e_in = jnp.exp(cum)
        e_ex = jnp.exp(cum - lw)
        e_neg = jnp.exp(-cum)
        e_end = jnp.exp(cum_last - cum)
        wc = jnp.exp(cum_last)
        kha = kh * a
        al_all = kh * e_ex
        be_all = -(kha * e_neg)
        kk_all = kt * e_neg
        rr_all = r * e_in
        bew_all = -(kha * e_end)
        kkw_all = kt * e_end

        ys = []
        for h in range(2):
            ls = slice(h * HEAD_DIM, (h + 1) * HEAD_DIM)
            al, be, kk, rr = al_all[:, ls], be_all[:, ls], kk_all[:, ls], rr_all[:, ls]
            vh = v[:, ls]
            st = s_ref[h]
            a_ab = jnp.where(strict, _mm3(al, be, NT), 0.0)
            a_ak = jnp.where(strict, _mm3(al, kk, NT), 0.0)
            a_rb = jnp.where(incl, _mm3(rr, be, NT), 0.0)
            a_rk = jnp.where(incl, _mm3(rr, kk, NT), 0.0)
            pw = a_ab
            tinv = eye_f + a_ab
            steps = c.bit_length() - 2
            for _ in range(steps):
                pw = _mm3(pw, pw)
                tinv = tinv + _mm3(tinv, pw)
            rhs = _mm3(al, st, NT) + _mm3(a_ak, vh)
            u = _mm3(tinv, rhs)
            y = _mm3(rr, st, NT) + _mm3(a_rb, u) + _mm3(a_rk, vh)
            s_ref[h] = (st * wc[:, ls] + _mm3(u, bew_all[:, ls], TN) + _mm3(vh, kkw_all[:, ls], TN))
            ys.append(y)
        y_ref[sl, :] = jnp.concatenate(ys, axis=1)
        return carry

    lax.fori_loop(0, nsub, sub, 0)


def _scan(r4, v4, lw, kh, kt, a):
    t = r4.shape[1]
    tb = SCAN_BLOCK
    nb = t // tb

    def blk(d, j):
        return jnp.where(d == 0, j, jnp.where(j == 0, 0, nb - j))

    shared = pl.BlockSpec((None, tb, LANES), lambda d, p, j: (p, blk(d, j), 0))
    perdir = pl.BlockSpec((None, None, tb, LANES), lambda d, p, j: (d, p, blk(d, j), 0))
    return pl.pallas_call(
        _scan_kernel,
        grid=(2, N_PAIRS, nb),
        in_specs=[shared, shared, perdir, perdir, perdir, perdir],
        out_specs=perdir,
        out_shape=jax.ShapeDtypeStruct((2, N_PAIRS, t, LANES), F32),
        scratch_shapes=[pltpu.VMEM((2, HEAD_DIM, HEAD_DIM), F32)],
        compiler_params=_cparams(("arbitrary", "arbitrary", "arbitrary")),
        name="scan",
    )(r4, v4, lw, kh, kt, a)


def _mix_kernel(x_ref, y_ref, g_ref, bonus_ref, yconv_ref, mod_ref, gnw_ref, gnb_ref, bsum_ref,
                wout_ref, n2g_ref, rw_ref, rb_ref, sw1_ref, sw3_ref, sw2_ref,
                base_ref, h2_ref, idx_ref, gate_ref):
    tm = x_ref.shape[0]
    g1 = mod_ref[0:1, 2 * D_MODEL:3 * D_MODEL]
    sh2 = mod_ref[0:1, 3 * D_MODEL:4 * D_MODEL]
    sc2 = mod_ref[0:1, 4 * D_MODEL:5 * D_MODEL]
    g2 = mod_ref[0:1, 5 * D_MODEL:6 * D_MODEL]

    y = jnp.concatenate([y_ref[0, pr] + y_ref[1, pr] for pr in range(N_PAIRS)], axis=1)
    bs = bsum_ref[...]
    mu = _mm_ones_rhs(y, bs) * (1.0 / HEAD_DIM)
    yc = y - mu
    var = _mm_ones_rhs(yc * yc, bs) * (1.0 / HEAD_DIM)
    yn = yc * lax.rsqrt(var + GN_EPS) * gnw_ref[...] + gnb_ref[...]
    yrw = ((yn + bonus_ref[...]) * g_ref[...]).astype(BF16)
    mix = _dg(yconv_ref[...], wout_ref[0:D_CONV, :]) + _dg(yrw, wout_ref[D_CONV:, :])
    x1 = x_ref[...] + g1 * mix

    h2 = _rms(x1, n2g_ref[...]) * (1.0 + sc2) + sh2
    h2_ref[...] = h2

    scores = _sigmoid(_mm3(h2, rw_ref[...]))
    work = scores + rb_ref[...]
    lane_e = lax.broadcasted_iota(jnp.int32, (tm, N_EXPERTS), 1).astype(F32)
    lane_o = lax.broadcasted_iota(jnp.int32, (tm, LANES), 1)
    idx_acc = jnp.zeros((tm, LANES), F32)
    gate_acc = jnp.zeros((tm, LANES), F32)
    gsum = jnp.zeros((tm, 1), F32)
    for kk in range(TOP_K):
        m = jnp.max(work, axis=-1, keepdims=True)
        sel = jnp.min(jnp.where(work == m, lane_e, float(N_EXPERTS)), axis=-1, keepdims=True)
        hit = lane_e == sel
        sk = jnp.sum(jnp.where(hit, scores, 0.0), axis=-1, keepdims=True)
        idx_acc = jnp.where(lane_o == kk, sel, idx_acc)
        gate_acc = jnp.where(lane_o == kk, sk, gate_acc)
        gsum = gsum + sk
        work = jnp.where(hit, -jnp.inf, work)
    idx_ref[...] = idx_acc.astype(jnp.int32)
    gate_ref[...] = gate_acc / gsum * ROUTED_SCALE

    hb = h2.astype(BF16)
    act = (_silu(_dg(hb, sw1_ref[...])) * _dg(hb, sw3_ref[...])).astype(BF16)
    base_ref[...] = x1 + g2 * _dg(act, sw2_ref[...])


def _mix(x, y, g, bonus, yconv, mod, gnw, gnb, bsum, wout, n2g, rw, rb, sw1, sw3, sw2):
    t = x.shape[0]
    tm = TOK_TILE
    off = (y.shape[2] - t) // tm
    full = lambda arr: pl.BlockSpec(arr.shape, lambda i: (0,) * arr.ndim)
    consts = (mod, gnw, gnb, bsum, wout, n2g, rw, rb, sw1, sw3, sw2)
    cat_spec = lambda w: pl.BlockSpec((tm, w), lambda i: (i + off, 0))
    tok_spec = lambda w: pl.BlockSpec((tm, w), lambda i: (i, 0))
    return pl.pallas_call(
        _mix_kernel,
        grid=(t // tm,),
        in_specs=[tok_spec(D_MODEL),
                  pl.BlockSpec((2, N_PAIRS, tm, LANES), lambda i: (0, 0, i + off, 0)),
                  cat_spec(D_RWKV), cat_spec(D_RWKV), cat_spec(D_CONV)]
                 + [full(c) for c in consts],
        out_specs=[tok_spec(D_MODEL), tok_spec(D_MODEL), tok_spec(LANES), tok_spec(LANES)],
        out_shape=[jax.ShapeDtypeStruct((t, D_MODEL), F32), jax.ShapeDtypeStruct((t, D_MODEL), F32),
                   jax.ShapeDtypeStruct((t, LANES), jnp.int32), jax.ShapeDtypeStruct((t, LANES), F32)],
        compiler_params=_cparams(("arbitrary",)),
        name="mix",
    )(x, y, g, bonus, yconv, *consts)


def _experts_kernel(be_ref, tok_hbm, h_hbm, w1_ref, w3_ref, w2_ref, o_ref,
                    idx_smem, rows, idx_sem, row_sem):
    b = pl.program_id(0)
    nb = pl.num_programs(0)
    slot = b % 2
    nslot = 1 - slot
    nrow = rows.shape[1]

    def idx_copy(blk, s):
        return pltpu.make_async_copy(tok_hbm.at[blk], idx_smem.at[s], idx_sem.at[s])

    def row_copy(s, i, tok):
        return pltpu.make_async_copy(h_hbm.at[pl.ds(tok, 1)], rows.at[s, pl.ds(i, 1)], row_sem.at[s])

    def issue_rows(s):
        def body(i, carry):
            row_copy(s, i, idx_smem[s, i]).start()
            return carry
        lax.fori_loop(0, nrow, body, 0)

    def wait_rows(s):
        def body(i, carry):
            row_copy(s, i, 0).wait()
            return carry
        lax.fori_loop(0, nrow, body, 0)

    @pl.when(b == 0)
    def _():
        idx_copy(0, 0).start()
        idx_copy(0, 0).wait()
        issue_rows(0)

        @pl.when(nb > 1)
        def _():
            idx_copy(1, 1).start()

    @pl.when(b + 1 < nb)
    def _():
        idx_copy(b + 1, nslot).wait()
        issue_rows(nslot)

    @pl.when(b + 2 < nb)
    def _():
        idx_copy(b + 2, slot).start()

    wait_rows(slot)
    hb = rows[slot].astype(BF16)
    a1 = _dg(hb, w1_ref[...].astype(BF16))
    a3 = _dg(hb, w3_ref[...].astype(BF16))
    act = (_silu(a1) * a3).astype(BF16)
    o_ref[...] = _dg(act, w2_ref[...].astype(BF16))


def _experts(block_e, slot_tok, h2, w1, w3, w2):
    nblk = block_e.shape[0]
    sb = SLOT_BLOCK
    grid_spec = pltpu.PrefetchScalarGridSpec(
        num_scalar_prefetch=1,
        grid=(nblk,),
        in_specs=[pl.BlockSpec(memory_space=pl.ANY),
                  pl.BlockSpec(memory_space=pl.ANY),
                  pl.BlockSpec((None, D_MODEL, D_EXPERT), lambda b, be: (be[b], 0, 0)),
                  pl.BlockSpec((None, D_MODEL, D_EXPERT), lambda b, be: (be[b], 0, 0)),
                  pl.BlockSpec((None, D_EXPERT, D_MODEL), lambda b, be: (be[b], 0, 0))],
        out_specs=pl.BlockSpec((sb, D_MODEL), lambda b, be: (b, 0)),
        scratch_shapes=[pltpu.SMEM((2, sb), jnp.int32),
                        pltpu.VMEM((2, sb, D_MODEL), F32),
                        pltpu.SemaphoreType.DMA((2,)),
                        pltpu.SemaphoreType.DMA((2,))],
    )
    return pl.pallas_call(
        _experts_kernel,
        grid_spec=grid_spec,
        out_shape=jax.ShapeDtypeStruct((nblk * sb, D_MODEL), F32),
        compiler_params=_cparams(("arbitrary",)),
        name="experts",
    )(block_e, slot_tok.reshape(nblk, sb), h2, w1, w3, w2)


def _combine_kernel(dest_hbm, yb_hbm, base_ref, gate_ref, mod_ref, fg_ref, o_ref,
                    idx_smem, buf, idx_sem, row_sem):
    b = pl.program_id(0)
    nb = pl.num_programs(0)
    slot = b % 2
    nslot = 1 - slot
    tm = base_ref.shape[0]
    nrow = tm * TOP_K

    def idx_copy(blk, s):
        return pltpu.make_async_copy(dest_hbm.at[blk], idx_smem.at[s], idx_sem.at[s])

    def row_copy(s, q, src):
        return pltpu.make_async_copy(yb_hbm.at[pl.ds(src, 1)],
                                     buf.at[s, q % TOP_K, pl.ds(q // TOP_K, 1)], row_sem.at[s])

    def issue_rows(s):
        def body(q, carry):
            row_copy(s, q, idx_smem[s, q]).start()
            return carry
        lax.fori_loop(0, nrow, body, 0)

    def wait_rows(s):
        def body(q, carry):
            row_copy(s, q, 0).wait()
            return carry
        lax.fori_loop(0, nrow, body, 0)

    @pl.when(b == 0)
    def _():
        idx_copy(0, 0).start()
        idx_copy(0, 0).wait()
        issue_rows(0)

        @pl.when(nb > 1)
        def _():
            idx_copy(1, 1).start()

    @pl.when(b + 1 < nb)
    def _():
        idx_copy(b + 1, nslot).wait()
        issue_rows(nslot)

    @pl.when(b + 2 < nb)
    def _():
        idx_copy(b + 2, slot).start()

    wait_rows(slot)
    gate = gate_ref[...]
    acc = gate[:, 0:1] * buf[slot, 0]
    for kk in range(1, TOP_K):
        acc = acc + gate[:, kk:kk + 1] * buf[slot, kk]
    g2 = mod_ref[0:1, 5 * D_MODEL:6 * D_MODEL]
    o_ref[...] = _rms(base_ref[...] + g2 * acc, fg_ref[...])


def _combine(dest, yb, base, gate, mod, fg):
    t = base.shape[0]
    tm = COMB_TILE
    nt = t // tm
    full = lambda arr: pl.BlockSpec(arr.shape, lambda i: (0,) * arr.ndim)
    return pl.pallas_call(
        _combine_kernel,
        grid=(nt,),
        in_specs=[pl.BlockSpec(memory_space=pl.ANY),
                  pl.BlockSpec(memory_space=pl.ANY),
                  pl.BlockSpec((tm, D_MODEL), lambda i: (i, 0)),
                  pl.BlockSpec((tm, LANES), lambda i: (i, 0)),
                  full(mod), full(fg)],
        out_specs=pl.BlockSpec((tm, D_MODEL), lambda i: (i, 0)),
        out_shape=jax.ShapeDtypeStruct((t, D_MODEL), F32),
        scratch_shapes=[pltpu.SMEM((2, tm * TOP_K), jnp.int32),
                        pltpu.VMEM((2, TOP_K, tm, D_MODEL), F32),
                        pltpu.SemaphoreType.DMA((2,)),
                        pltpu.SemaphoreType.DMA((2,))],
        compiler_params=_cparams(("arbitrary",)),
        name="combine",
    )(dest.reshape(nt, tm * TOP_K), yb, base, gate, mod, fg)


def _routing_tables(idx):
    n_tok = idx.shape[0]
    n_assign = n_tok * TOP_K
    sb = SLOT_BLOCK
    flat_e = idx.reshape(-1)
    order = jnp.argsort(flat_e).astype(jnp.int32)
    e_sorted = flat_e[order]
    counts = jnp.zeros((N_EXPERTS,), jnp.int32).at[flat_e].add(1)
    padded = (counts + sb - 1) // sb * sb
    pad_end = jnp.cumsum(padded)
    pad_start = pad_end - padded
    start = jnp.cumsum(counts) - counts
    rank = jnp.arange(n_assign, dtype=jnp.int32) - start[e_sorted]
    dest_sorted = pad_start[e_sorted] + rank
    n_blocks = n_assign // sb + N_EXPERTS
    slot_tok = jnp.zeros((n_blocks * sb,), jnp.int32).at[dest_sorted].set(order // TOP_K)
    dest = jnp.zeros((n_assign,), jnp.int32).at[order].set(dest_sorted)
    block_e = jnp.minimum(jnp.searchsorted(pad_end, jnp.arange(n_blocks, dtype=jnp.int32) * sb, side='right'),
                          N_EXPERTS - 1).astype(jnp.int32)
    return block_e, slot_tok, dest


def _blockdiag2(w):
    z = jnp.zeros_like(w[0])
    return jnp.concatenate([jnp.concatenate([w[0], z], axis=1), jnp.concatenate([z, w[1]], axis=1)], axis=0)


def _pad_rows(w, n=8):
    return jnp.concatenate([w, jnp.zeros((n - w.shape[0],) + w.shape[1:], w.dtype)], axis=0)


def kernel(x, c, ctx, c_ctx, ada_w, ada_b, norm1_g, norm2_g, w_in, conv_w, shift_mu, decay_w0, decay_up, iclr_a0, iclr_up, key_xi, key_alpha, bonus_rho, gate_up, gn_w, gn_b, w_out, router_w, router_bias, exp_w1, exp_w3, exp_w2, sh_w1, sh_w3, sh_w2, final_g):
    assert x.shape[0] == 1 and ada_w.shape[0] == 1
    assert ctx.shape[1] == TOK_TILE and x.shape[1] % TOK_TILE == 0
    l = 0
    xs = x[0]
    row = lambda w: w.reshape(1, -1)

    cc = _pad_rows(jnp.stack([c[0], c_ctx], axis=0))
    mod = _ada(cc, ada_w[l], row(ada_b[l]))

    hid = lax.broadcasted_iota(jnp.int32, (D_RWKV, D_RWKV), 0) // HEAD_DIM
    bsum = (hid == hid.T).astype(BF16)

    xcat = jnp.concatenate([ctx[0], xs], axis=0)
    r4, v4, g, bonus, yconv, lw, kh, kt, a = _proj(
        xcat, mod, row(norm1_g[l]), w_in[l].astype(BF16), _pad_rows(conv_w[l]), _pad_rows(shift_mu[l]),
        row(decay_w0[l]), _blockdiag2(decay_up[l]), row(iclr_a0[l]), _blockdiag2(iclr_up[l]),
        row(key_xi[l]), row(key_alpha[l]), row(bonus_rho[l]), gate_up[l], bsum)

    y = _scan(r4, v4, lw, kh, kt, a)

    base, h2, idx, gate = _mix(
        xs, y, g, bonus, yconv, mod, row(gn_w[l]), row(gn_b[l]), bsum, w_out[l].astype(BF16),
        row(norm2_g[l]), router_w[l], row(router_bias[l]),
        sh_w1[l].astype(BF16), sh_w3[l].astype(BF16), sh_w2[l].astype(BF16))

    block_e, slot_tok, dest = _routing_tables(idx[:, :TOP_K])
    yb = _experts(block_e, slot_tok, h2, exp_w1[l], exp_w3[l], exp_w2[l])
    out = _combine(dest, yb, base, gate, mod, row(final_g))
    return out[None]
```

```python
import functools

import jax
import jax.numpy as jnp
from jax import lax
from jax.experimental import pallas as pl
from jax.experimental.pallas import tpu as pltpu

F32 = jnp.float32
BF16 = jnp.bfloat16

D_MODEL = 1024
D_CONV = 512
D_RWKV = 512
HEAD_DIM = 64
N_HEADS = D_RWKV // HEAD_DIM
N_PAIRS = N_HEADS // 2
LORA = 128
P_RWKV = 3 * D_RWKV + 3 * LORA
P_IN = 3 * D_CONV + P_RWKV
GRID_W = 64
N_EXPERTS = 256
TOP_K = 8
D_EXPERT = 256
ROUTED_SCALE = 2.5
RMS_EPS = 1e-6
GN_EPS = 64e-5
DECAY_SCALE = 0.6065306597126334
NORM_EPS = 1e-12

TOK_TILE = 256
SCAN_BLOCK = 256
CHUNK = 64
SLOT_BLOCK = 128
COMB_TILE = 128
LANES = 128
VMEM_LIMIT = 56 * 1024 * 1024

NN = ((1,), (0,))
NT = ((1,), (1,))
TN = ((0,), (0,))


def _dg(a, b, dims=NN):
    return lax.dot_general(a, b, (dims, ((), ())), preferred_element_type=F32)


def _split2(a):
    hi = a.astype(BF16)
    lo = (a - hi.astype(F32)).astype(BF16)
    return hi, lo


def _split3(a):
    hi = a.astype(BF16)
    r1 = a - hi.astype(F32)
    mid = r1.astype(BF16)
    lo = (r1 - mid.astype(F32)).astype(BF16)
    return hi, mid, lo


def _mm1(a, b, dims=NN):
    return _dg(a.astype(BF16), b.astype(BF16), dims)


def _mm3(a, b, dims=NN):
    ah, al = _split2(a)
    bh, bl = _split2(b)
    return _dg(ah, bh, dims) + (_dg(ah, bl, dims) + _dg(al, bh, dims))


def _mm_ones_rhs(a, ones_bf16, dims=NN):
    h, m, l = _split3(a)
    return _dg(h, ones_bf16, dims) + (_dg(m, ones_bf16, dims) + _dg(l, ones_bf16, dims))


def _mm_ones_lhs(ones_bf16, b, dims=NN):
    h, m, l = _split3(b)
    return _dg(ones_bf16, h, dims) + (_dg(ones_bf16, m, dims) + _dg(ones_bf16, l, dims))


def _sigmoid(x):
    return 1.0 / (1.0 + jnp.exp(-x))


def _silu(x):
    return x * _sigmoid(x)


def _rms(xv, g):
    ms = jnp.mean(xv * xv, axis=-1, keepdims=True)
    return xv * lax.rsqrt(ms + RMS_EPS) * g


def _cparams(sem):
    return pltpu.CompilerParams(dimension_semantics=sem, vmem_limit_bytes=VMEM_LIMIT)


def _ada_kernel(c_ref, w_ref, b_ref, o_ref):
    o_ref[...] = _mm3(_silu(c_ref[...]), w_ref[...]) + b_ref[...]


def _ada(cc, ada_w, ada_b):
    n = ada_w.shape[1]
    tn = 1024
    return pl.pallas_call(
        _ada_kernel,
        grid=(n // tn,),
        in_specs=[pl.BlockSpec((8, D_MODEL), lambda j: (0, 0)),
                  pl.BlockSpec((D_MODEL, tn), lambda j: (0, j)),
                  pl.BlockSpec((1, tn), lambda j: (0, j))],
        out_specs=pl.BlockSpec((8, tn), lambda j: (0, j)),
        out_shape=jax.ShapeDtypeStruct((8, n), F32),
        compiler_params=_cparams(("arbitrary",)),
        name="ada",
    )(cc, ada_w, ada_b)


def _proj_kernel(x_ref, xp_ref, xn_ref, mod_ref, n1g_ref, win_ref, convw_ref, mu_ref,
                 dw0_ref, dup_ref, ia0_ref, iup_ref, xi_ref, kal_ref, rho_ref, gup_ref, bsum_ref,
                 r_ref, v_ref, g_ref, bonus_ref, yconv_ref, lw_ref, kh_ref, kt_ref, a_ref):
    i = pl.program_id(0)
    nt = pl.num_programs(0)
    tm = x_ref.shape[0]
    is_ctx = i == 0
    sh = jnp.where(is_ctx, mod_ref[1:2, 0:D_MODEL], mod_ref[0:1, 0:D_MODEL])
    sc = jnp.where(is_ctx, mod_ref[1:2, D_MODEL:2 * D_MODEL], mod_ref[0:1, D_MODEL:2 * D_MODEL])
    n1g = n1g_ref[...]

    def norm_mod(xv):
        return _rms(xv, n1g) * (1.0 + sc) + sh

    h = norm_mod(x_ref[...]).astype(BF16)
    p = _dg(h, win_ref[...])
    hh = norm_mod(jnp.concatenate([xp_ref[...], xn_ref[...]], axis=0)).astype(BF16)
    ph = _dg(hh, win_ref[:, 3 * D_CONV:])

    rows = lax.broadcasted_iota(jnp.int32, (tm, 1), 0)

    bg = p[:, 0:D_CONV]
    z = p[:, D_CONV:2 * D_CONV] * p[:, 2 * D_CONV:3 * D_CONV]
    col = rows % GRID_W
    zp = jnp.where(col == 0, 0.0, pltpu.roll(z, 1, 0))
    zn = jnp.where(col == GRID_W - 1, 0.0, pltpu.roll(z, tm - 1, 0))
    yconv = bg * (convw_ref[0:1, :] * zp + convw_ref[1:2, :] * z + convw_ref[2:3, :] * zn)
    yconv_ref[...] = yconv.astype(BF16)

    prev_ok = jnp.logical_and(i != 0, i != 1).astype(F32)
    next_ok = jnp.logical_and(i != 0, i != nt - 1).astype(F32)
    cur = p[:, 3 * D_CONV:]
    prev = jnp.where(rows == 0, ph[7:8, :] * prev_ok, pltpu.roll(cur, 1, 0))
    nxt = jnp.where(rows == tm - 1, ph[8:9, :] * next_ok, pltpu.roll(cur, tm - 1, 0))
    ps = cur + mu_ref[0:1, :] * (prev - cur) + mu_ref[1:2, :] * (nxt - cur)

    r = ps[:, 0:D_RWKV]
    k = ps[:, D_RWKV:2 * D_RWKV]
    v = ps[:, 2 * D_RWKV:3 * D_RWKV]
    o = 3 * D_RWKV
    wlo = ps[:, o:o + LORA]
    alo = ps[:, o + LORA:o + 2 * LORA]
    glo = ps[:, o + 2 * LORA:o + 3 * LORA]

    dd = dw0_ref[...] + _mm3(jnp.tanh(wlo), dup_ref[...])
    lw = -DECAY_SCALE * _sigmoid(dd)
    a = _sigmoid(ia0_ref[...] + _mm3(alo, iup_ref[...]))
    g = _mm3(_sigmoid(glo), gup_ref[...])
    k2 = jnp.concatenate([k, k], axis=1)
    kap = k2 * xi_ref[...]
    kap2 = kap * kap
    bs = bsum_ref[...]
    ss = jnp.concatenate([_mm_ones_rhs(kap2[:, 0:D_RWKV], bs),
                          _mm_ones_rhs(kap2[:, D_RWKV:], bs)], axis=1)
    kh = kap * lax.rsqrt(ss + NORM_EPS)
    kt = k2 * (1.0 + (a - 1.0) * kal_ref[...])
    bon = _mm_ones_rhs(r * rho_ref[...] * (kt[:, 0:D_RWKV] + kt[:, D_RWKV:]), bs)

    g_ref[...] = g
    bonus_ref[...] = bon * v
    for pr in range(N_PAIRS):
        ls = slice(pr * LANES, (pr + 1) * LANES)
        r_ref[pr] = r[:, ls]
        v_ref[pr] = v[:, ls]
        for d in range(2):
            ld = slice(d * D_RWKV + pr * LANES, d * D_RWKV + (pr + 1) * LANES)
            lw_ref[d, pr] = lw[:, ld]
            kh_ref[d, pr] = kh[:, ld]
            kt_ref[d, pr] = kt[:, ld]
            a_ref[d, pr] = a[:, ld]


def _proj(xcat, mod, n1g, win, convw, mu, dw0, dup, ia0, iup, xi, kal, rho, gup, bsum):
    t = xcat.shape[0]
    tm = TOK_TILE
    nt = t // tm
    nb8 = t // 8
    full = lambda arr: pl.BlockSpec(arr.shape, lambda i: (0,) * arr.ndim)
    pair_spec = pl.BlockSpec((N_PAIRS, tm, LANES), lambda i: (0, i, 0))
    dpair_spec = pl.BlockSpec((2, N_PAIRS, tm, LANES), lambda i: (0, 0, i, 0))
    row_spec = pl.BlockSpec((tm, D_RWKV), lambda i: (i, 0))
    pair_shape = jax.ShapeDtypeStruct((N_PAIRS, t, LANES), F32)
    dpair_shape = jax.ShapeDtypeStruct((2, N_PAIRS, t, LANES), F32)
    consts = (mod, n1g, win, convw, mu, dw0, dup, ia0, iup, xi, kal, rho, gup, bsum)
    return pl.pallas_call(
        _proj_kernel,
        grid=(nt,),
        in_specs=[pl.BlockSpec((tm, D_MODEL), lambda i: (i, 0)),
                  pl.BlockSpec((8, D_MODEL), lambda i: (jnp.maximum(i * (tm // 8) - 1, 0), 0)),
                  pl.BlockSpec((8, D_MODEL), lambda i: (jnp.minimum((i + 1) * (tm // 8), nb8 - 1), 0))]
                 + [full(c) for c in consts],
        out_specs=[pair_spec, pair_spec, row_spec, row_spec, row_spec,
                   dpair_spec, dpair_spec, dpair_spec, dpair_spec],
        out_shape=[pair_shape, pair_shape,
                   jax.ShapeDtypeStruct((t, D_RWKV), F32), jax.ShapeDtypeStruct((t, D_RWKV), F32),
                   jax.ShapeDtypeStruct((t, D_CONV), BF16),
                   dpair_shape, dpair_shape, dpair_shape, dpair_shape],
        compiler_params=_cparams(("arbitrary",)),
        name="proj",
    )(xcat, xcat, xcat, *consts)


def _chunk_kernel(r_ref, v_ref, lw_ref, kh_ref, kt_ref, a_ref, rh_ref, yh_ref, mt_ref, dt_ref):
    d = pl.program_id(0)
    rev = d == 1
    c = CHUNK
    nsub = r_ref.shape[0] // c

    row = lax.broadcasted_iota(jnp.int32, (c, c), 0)
    col = lax.broadcasted_iota(jnp.int32, (c, c), 1)
    strict = (col - row) * (1 - 2 * d) < 0
    eye = row == col
    incl = jnp.logical_or(strict, eye)
    incl_bf = jnp.where(incl, 1.0, 0.0).astype(BF16)
    eye_f = jnp.where(eye, 1.0, 0.0)
    head0 = lax.broadcasted_iota(jnp.int32, (c, LANES), 1) < HEAD_DIM
    row2 = lax.broadcasted_iota(jnp.int32, (LANES, LANES), 0)
    col2 = lax.broadcasted_iota(jnp.int32, (LANES, LANES), 1)
    same_head = (row2 < HEAD_DIM) == (col2 < HEAD_DIM)
    eye2 = row2 == col2

    subs = range(nsub)
    units = [(s, h) for s in subs for h in range(2)]
    sls = [slice(s * c, (s + 1) * c) for s in subs]
    v = [v_ref[sl, :] for sl in sls]
    lw = [lw_ref[sl, :] for sl in sls]
    cum = [_mm_ones_lhs(incl_bf, lw[s]) for s in subs]
    al, be, kk, rr, bew, kkw, wc = [], [], [], [], [], [], []
    for s in subs:
        cum_last = jnp.where(rev, cum[s][0:1, :], cum[s][c - 1:c, :])
        e_end = jnp.exp(cum_last - cum[s])
        e_neg = jnp.exp(-cum[s])
        kh = kh_ref[sls[s], :]
        kt = kt_ref[sls[s], :]
        kha = kh * a_ref[sls[s], :]
        al.append(kh * jnp.exp(cum[s] - lw[s]))
        be.append(-(kha * e_neg))
        kk.append(kt * e_neg)
        rr.append(r_ref[sls[s], :] * jnp.exp(cum[s]))
        bew.append(-(kha * e_end))
        kkw.append(kt * e_end)
        wc.append(jnp.exp(cum_last))

    lhs = {}
    for s, h in units:
        hm = head0 if h == 0 else jnp.logical_not(head0)
        lhs[s, h] = jnp.concatenate([jnp.where(hm, al[s], 0.0), jnp.where(hm, rr[s], 0.0)], axis=0)
    gb = {u: _mm1(lhs[u], be[u[0]], NT) for u in units}
    gk = {u: _mm1(lhs[u], kk[u[0]], NT) for u in units}
    a_ab = {u: jnp.where(strict, gb[u][0:c], 0.0) for u in units}
    a_ak = {u: jnp.where(strict, gk[u][0:c], 0.0) for u in units}
    a_rb = {u: jnp.where(incl, gb[u][c:2 * c], 0.0) for u in units}
    a_rk = {u: jnp.where(incl, gk[u][c:2 * c], 0.0) for u in units}

    pw = dict(a_ab)
    tinv = {u: eye_f + a_ab[u] for u in units}
    avk = {u: _mm1(a_ak[u], v[u[0]]) for u in units}
    for _ in range(c.bit_length() - 2):
        pw = {u: _mm1(pw[u], pw[u]) for u in units}
        tinv = {u: tinv[u] + _mm1(tinv[u], pw[u]) for u in units}
    tu = {u: _mm1(tinv[u], jnp.concatenate([al[u[0]], avk[u]], axis=1)) for u in units}
    at = [jnp.where(head0, tu[s, 0][:, 0:LANES], tu[s, 1][:, 0:LANES]) for s in subs]
    ut = [jnp.where(head0, tu[s, 0][:, LANES:], tu[s, 1][:, LANES:]) for s in subs]

    rb_at = {u: _mm1(a_rb[u], at[u[0]]) for u in units}
    rb_ut = {u: _mm1(a_rb[u], ut[u[0]]) for u in units}
    rk_v = {u: _mm1(a_rk[u], v[u[0]]) for u in units}
    m_off = [_mm1(bew[s], at[s], TN) for s in subs]
    d_u = [_mm1(bew[s], ut[s], TN) for s in subs]
    d_v = [_mm1(kkw[s], v[s], TN) for s in subs]
    for s in subs:
        rh_ref[sls[s], :] = rr[s] + jnp.where(head0, rb_at[s, 0], rb_at[s, 1])
        yh_ref[sls[s], :] = jnp.where(head0, rb_ut[s, 0] + rk_v[s, 0], rb_ut[s, 1] + rk_v[s, 1])
        sl2 = slice(2 * s * c, 2 * (s + 1) * c)
        mt_ref[sl2, :] = jnp.where(eye2, wc[s], 0.0) + jnp.where(same_head, m_off[s], 0.0)
        dt_ref[sl2, :] = jnp.where(same_head, d_u[s] + d_v[s], 0.0)


def _chunks(r4, v4, lw, kh, kt, a):
    t = r4.shape[1]
    tb = SCAN_BLOCK
    nb = t // tb
    shared = pl.BlockSpec((None, tb, LANES), lambda d, p, j: (p, j, 0))
    perdir = pl.BlockSpec((None, None, tb, LANES), lambda d, p, j: (d, p, j, 0))
    perdir2 = pl.BlockSpec((None, None, 2 * tb, LANES), lambda d, p, j: (d, p, j, 0))
    shp = jax.ShapeDtypeStruct((2, N_PAIRS, t, LANES), F32)
    shp2 = jax.ShapeDtypeStruct((2, N_PAIRS, 2 * t, LANES), F32)
    return pl.pallas_call(
        _chunk_kernel,
        grid=(2, N_PAIRS, nb),
        in_specs=[shared, shared, perdir, perdir, perdir, perdir],
        out_specs=[perdir, perdir, perdir2, perdir2],
        out_shape=[shp, shp, shp2, shp2],
        compiler_params=_cparams(("arbitrary", "arbitrary", "arbitrary")),
        name="chunks",
    )(r4, v4, lw, kh, kt, a)


def _carry_kernel(rhf_ref, yhf_ref, mtf_ref, dtf_ref, rhb_ref, yhb_ref, mtb_ref, dtb_ref,
                  yf_ref, yb_ref, q_ref):
    j = pl.program_id(0)
    c = CHUNK
    nsub = rhf_ref.shape[1] // c

    @pl.when(j == 0)
    def _():
        q_ref[...] = jnp.zeros(q_ref.shape, F32)

    dirs = ((rhf_ref, yhf_ref, mtf_ref, dtf_ref, yf_ref), (rhb_ref, yhb_ref, mtb_ref, dtb_ref, yb_ref))
    chains = [(dirn, p) for dirn in range(2) for p in range(N_PAIRS)]
    q = {ch: q_ref[ch[0], ch[1]] for ch in chains}
    for s in range(nsub):
        ys, qn = {}, {}
        for dirn, p in chains:
            rh_ref, yh_ref, mt_ref, dt_ref, _ = dirs[dirn]
            cs = s if dirn == 0 else nsub - 1 - s
            ys[dirn, p] = _mm3(rh_ref[p, cs * c:(cs + 1) * c, :], q[dirn, p])
            qn[dirn, p] = _mm3(mt_ref[p, 2 * cs * c:2 * (cs + 1) * c, :], q[dirn, p])
        for dirn, p in chains:
            _, yh_ref, _, dt_ref, y_ref = dirs[dirn]
            cs = s if dirn == 0 else nsub - 1 - s
            y_ref[p, cs * c:(cs + 1) * c, :] = ys[dirn, p] + yh_ref[p, cs * c:(cs + 1) * c, :]
            q[dirn, p] = qn[dirn, p] + dt_ref[p, 2 * cs * c:2 * (cs + 1) * c, :]
    for dirn, p in chains:
        q_ref[dirn, p] = q[dirn, p]


def _carry(rh, yh, mt, dt):
    t = rh.shape[2]
    tb = SCAN_BLOCK
    nb = t // tb
    bwd = lambda j: jnp.where(j == 0, 0, nb - j)
    f1 = pl.BlockSpec((None, N_PAIRS, tb, LANES), lambda j: (0, 0, j, 0))
    f2 = pl.BlockSpec((None, N_PAIRS, 2 * tb, LANES), lambda j: (0, 0, j, 0))
    b1 = pl.BlockSpec((None, N_PAIRS, tb, LANES), lambda j: (1, 0, bwd(j), 0))
    b2 = pl.BlockSpec((None, N_PAIRS, 2 * tb, LANES), lambda j: (1, 0, bwd(j), 0))
    yshape = jax.ShapeDtypeStruct((N_PAIRS, t, LANES), F32)
    return pl.pallas_call(
        _carry_kernel,
        grid=(nb,),
        in_specs=[f1, f1, f2, f2, b1, b1, b2, b2],
        out_specs=[pl.BlockSpec((N_PAIRS, tb, LANES), lambda j: (0, j, 0)),
                   pl.BlockSpec((N_PAIRS, tb, LANES), lambda j: (0, bwd(j), 0))],
        out_shape=[yshape, yshape],
        scratch_shapes=[pltpu.VMEM((2, N_PAIRS, LANES, LANES), F32)],
        compiler_params=_cparams(("arbitrary",)),
        name="carry",
    )(rh, yh, mt, dt, rh, yh, mt, dt)


def _mix_kernel(x_ref, yf_ref, yb_ref, g_ref, bonus_ref, yconv_ref, mod_ref, gnw_ref, gnb_ref, bsum_ref,
                wout_ref, n2g_ref, rw_ref, rb_ref, sw1_ref, sw3_ref, sw2_ref,
                base_ref, h2_ref, idx_ref, gate_ref, rank_ref, cnt_ref, run_ref):
    tm = x_ref.shape[0]
    g1 = mod_ref[0:1, 2 * D_MODEL:3 * D_MODEL]
    sh2 = mod_ref[0:1, 3 * D_MODEL:4 * D_MODEL]
    sc2 = mod_ref[0:1, 4 * D_MODEL:5 * D_MODEL]
    g2 = mod_ref[0:1, 5 * D_MODEL:6 * D_MODEL]

    @pl.when(pl.program_id(0) == 0)
    def _():
        run_ref[...] = jnp.zeros(run_ref.shape, F32)

    y = jnp.concatenate([yf_ref[pr] + yb_ref[pr] for pr in range(N_PAIRS)], axis=1)
    bs = bsum_ref[...]
    mu = _mm_ones_rhs(y, bs) * (1.0 / HEAD_DIM)
    yc = y - mu
    var = _mm_ones_rhs(yc * yc, bs) * (1.0 / HEAD_DIM)
    yn = yc * lax.rsqrt(var + GN_EPS) * gnw_ref[...] + gnb_ref[...]
    yrw = ((yn + bonus_ref[...]) * g_ref[...]).astype(BF16)
    mix = _dg(yconv_ref[...], wout_ref[0:D_CONV, :]) + _dg(yrw, wout_ref[D_CONV:, :])
    x1 = x_ref[...] + g1 * mix

    h2 = _rms(x1, n2g_ref[...]) * (1.0 + sc2) + sh2
    h2_ref[...] = h2

    scores = _sigmoid(_mm3(h2, rw_ref[...]))
    work = scores + rb_ref[...]
    lane_e = lax.broadcasted_iota(jnp.int32, (tm, N_EXPERTS), 1).astype(F32)
    lane_o = lax.broadcasted_iota(jnp.int32, (tm, LANES), 1)
    idx_acc = jnp.zeros((tm, LANES), F32)
    gate_acc = jnp.zeros((tm, LANES), F32)
    gsum = jnp.zeros((tm, 1), F32)
    chosen = jnp.zeros((tm, N_EXPERTS), F32)
    sels = []
    for kk in range(TOP_K):
        m = jnp.max(work, axis=-1, keepdims=True)
        sel = jnp.min(jnp.where(work == m, lane_e, float(N_EXPERTS)), axis=-1, keepdims=True)
        hit = lane_e == sel
        sk = jnp.sum(jnp.where(hit, scores, 0.0), axis=-1, keepdims=True)
        idx_acc = jnp.where(lane_o == kk, sel, idx_acc)
        gate_acc = jnp.where(lane_o == kk, sk, gate_acc)
        gsum = gsum + sk
        work = jnp.where(hit, -jnp.inf, work)
        chosen = jnp.where(hit, 1.0, chosen)
        sels.append(sel)
    idx_ref[...] = idx_acc.astype(jnp.int32)
    gate_ref[...] = gate_acc / gsum * ROUTED_SCALE

    trow = lax.broadcasted_iota(jnp.int32, (tm, tm), 0)
    tcol = lax.broadcasted_iota(jnp.int32, (tm, tm), 1)
    earlier = jnp.where(tcol < trow, 1.0, 0.0).astype(BF16)
    before = _dg(earlier, chosen.astype(BF16)) + run_ref[0:1, :]
    rank_acc = jnp.zeros((tm, LANES), F32)
    for kk in range(TOP_K):
        rk = jnp.sum(jnp.where(lane_e == sels[kk], before, 0.0), axis=-1, keepdims=True)
        rank_acc = jnp.where(lane_o == kk, rk, rank_acc)
    rank_ref[...] = rank_acc.astype(jnp.int32)
    run_ref[...] = run_ref[...] + jnp.sum(chosen, axis=0, keepdims=True)
    cnt_ref[...] = run_ref[...]

    hb = h2.astype(BF16)
    act = (_silu(_dg(hb, sw1_ref[...])) * _dg(hb, sw3_ref[...])).astype(BF16)
    base_ref[...] = x1 + g2 * _dg(act, sw2_ref[...])


def _mix(x, yf, yb, g, bonus, yconv, mod, gnw, gnb, bsum, wout, n2g, rw, rb, sw1, sw3, sw2):
    t = x.shape[0]
    tm = TOK_TILE
    off = (yf.shape[1] - t) // tm
    full = lambda arr: pl.BlockSpec(arr.shape, lambda i: (0,) * arr.ndim)
    consts = (mod, gnw, gnb, bsum, wout, n2g, rw, rb, sw1, sw3, sw2)
    cat_spec = lambda w: pl.BlockSpec((tm, w), lambda i: (i + off, 0))
    tok_spec = lambda w: pl.BlockSpec((tm, w), lambda i: (i, 0))
    y_spec = pl.BlockSpec((N_PAIRS, tm, LANES), lambda i: (0, i + off, 0))
    return pl.pallas_call(
        _mix_kernel,
        grid=(t // tm,),
        in_specs=[tok_spec(D_MODEL), y_spec, y_spec,
                  cat_spec(D_RWKV), cat_spec(D_RWKV), cat_spec(D_CONV)]
                 + [full(c) for c in consts],
        out_specs=[tok_spec(D_MODEL), tok_spec(D_MODEL), tok_spec(LANES), tok_spec(LANES), tok_spec(LANES),
                   pl.BlockSpec((8, N_EXPERTS), lambda i: (0, 0))],
        out_shape=[jax.ShapeDtypeStruct((t, D_MODEL), F32), jax.ShapeDtypeStruct((t, D_MODEL), F32),
                   jax.ShapeDtypeStruct((t, LANES), jnp.int32), jax.ShapeDtypeStruct((t, LANES), F32),
                   jax.ShapeDtypeStruct((t, LANES), jnp.int32), jax.ShapeDtypeStruct((8, N_EXPERTS), F32)],
        scratch_shapes=[pltpu.VMEM((8, N_EXPERTS), F32)],
        compiler_params=_cparams(("arbitrary",)),
        name="mix",
    )(x, yf, yb, g, bonus, yconv, *consts)


def _slots_kernel(idx_ref, rank_ref, start_ref, dest_ref):
    tm = idx_ref.shape[0]
    lane_e = lax.broadcasted_iota(jnp.int32, (tm, N_EXPERTS), 1)
    lane_o = lax.broadcasted_iota(jnp.int32, (tm, LANES), 1)
    idx = idx_ref[...]
    start = start_ref[...]
    acc = jnp.zeros((tm, LANES), F32)
    for kk in range(TOP_K):
        st = jnp.sum(jnp.where(lane_e == idx[:, kk:kk + 1], start, 0.0), axis=-1, keepdims=True)
        acc = jnp.where(lane_o == kk, st, acc)
    dest_ref[...] = acc.astype(jnp.int32) + rank_ref[...]


def _slots(idx, rank, pad_start):
    t = idx.shape[0]
    tm = TOK_TILE
    spec = pl.BlockSpec((tm, LANES), lambda i: (i, 0))
    return pl.pallas_call(
        _slots_kernel,
        grid=(t // tm,),
        in_specs=[spec, spec, pl.BlockSpec((1, N_EXPERTS), lambda i: (0, 0))],
        out_specs=spec,
        out_shape=jax.ShapeDtypeStruct((t, LANES), jnp.int32),
        compiler_params=_cparams(("arbitrary",)),
        name="slots",
    )(idx, rank, pad_start)


def _dispatch_kernel(dest_hbm, h_ref, hs_in, hs_out, idx_smem, idx_sem, row_sem):
    del hs_in
    b = pl.program_id(0)
    nb = pl.num_programs(0)
    slot = b % 2
    tm = h_ref.shape[0]

    def idx_copy(blk, s):
        return pltpu.make_async_copy(dest_hbm.at[blk], idx_smem.at[s], idx_sem.at[s])

    def row_copy(t, dst):
        return pltpu.make_async_copy(h_ref.at[pl.ds(t, 1)], hs_out.at[pl.ds(dst, 1)], row_sem.at[0])

    @pl.when(b == 0)
    def _():
        idx_copy(0, 0).start()

    idx_copy(b, slot).wait()

    @pl.when(b + 1 < nb)
    def _():
        idx_copy(b + 1, 1 - slot).start()

    def issue(t, carry):
        for kk in range(TOP_K):
            row_copy(t, idx_smem[slot, t * TOP_K + kk]).start()
        return carry
    lax.fori_loop(0, tm, issue, 0)

    def drain(t, carry):
        for kk in range(TOP_K):
            row_copy(t, 0).wait()
        return carry
    lax.fori_loop(0, tm, drain, 0)


def _dispatch(dest, h2, n_slots):
    t = h2.shape[0]
    tm = COMB_TILE
    nt = t // tm
    hs0 = jnp.zeros((n_slots, D_MODEL), F32)
    return pl.pallas_call(
        _dispatch_kernel,
        grid=(nt,),
        in_specs=[pl.BlockSpec(memory_space=pl.ANY),
                  pl.BlockSpec((tm, D_MODEL), lambda i: (i, 0)),
                  pl.BlockSpec(memory_space=pl.ANY)],
        out_specs=pl.BlockSpec(memory_space=pl.ANY),
        out_shape=jax.ShapeDtypeStruct((n_slots, D_MODEL), F32),
        scratch_shapes=[pltpu.SMEM((2, tm * TOP_K), jnp.int32),
                        pltpu.SemaphoreType.DMA((2,)),
                        pltpu.SemaphoreType.DMA((1,))],
        input_output_aliases={2: 0},
        compiler_params=pltpu.CompilerParams(dimension_semantics=("arbitrary",), vmem_limit_bytes=VMEM_LIMIT,
                                             disable_bounds_checks=True),
        name="dispatch",
    )(dest.reshape(nt, tm * TOP_K), h2, hs0)


def _experts_kernel(be_ref, nu_ref, hs_ref, w1_ref, w3_ref, w2_ref, o_ref):
    del be_ref

    @pl.when(pl.program_id(0) < nu_ref[0])
    def _():
        hb = hs_ref[...].astype(BF16)
        a1 = _dg(hb, w1_ref[...].astype(BF16))
        a3 = _dg(hb, w3_ref[...].astype(BF16))
        act = (_silu(a1) * a3).astype(BF16)
        o_ref[...] = _dg(act, w2_ref[...].astype(BF16))

    @pl.when(pl.program_id(0) >= nu_ref[0])
    def _():
        o_ref[...] = jnp.zeros(o_ref.shape, F32)


def _experts(block_e, n_used, hs, w1, w3, w2):
    nblk = block_e.shape[0]
    sb = SLOT_BLOCK
    row_map = lambda b, be, nu: (jnp.minimum(b, nu[0] - 1), 0)
    w_map = lambda b, be, nu: (be[b], 0, 0)
    grid_spec = pltpu.PrefetchScalarGridSpec(
        num_scalar_prefetch=2,
        grid=(nblk,),
        in_specs=[pl.BlockSpec((sb, D_MODEL), row_map),
                  pl.BlockSpec((None, D_MODEL, D_EXPERT), w_map),
                  pl.BlockSpec((None, D_MODEL, D_EXPERT), w_map),
                  pl.BlockSpec((None, D_EXPERT, D_MODEL), w_map)],
        out_specs=pl.BlockSpec((sb, D_MODEL), lambda b, be, nu: (b, 0)),
    )
    return pl.pallas_call(
        _experts_kernel,
        grid_spec=grid_spec,
        out_shape=jax.ShapeDtypeStruct((nblk * sb, D_MODEL), F32),
        compiler_params=_cparams(("arbitrary",)),
        name="experts",
    )(block_e, n_used, hs, w1, w3, w2)


def _combine_kernel(dest_hbm, yb_hbm, base_ref, gate_ref, mod_ref, fg_ref, o_ref,
                    idx_smem, buf, idx_sem, row_sem):
    b = pl.program_id(0)
    nb = pl.num_programs(0)
    slot = b % 2
    nslot = 1 - slot
    tm = base_ref.shape[0]
    nrow = tm * TOP_K

    def idx_copy(blk, s):
        return pltpu.make_async_copy(dest_hbm.at[blk], idx_smem.at[s], idx_sem.at[s])

    def row_copy(s, t, kk, src):
        return pltpu.make_async_copy(yb_hbm.at[pl.ds(src, 1)], buf.at[s, kk, pl.ds(t, 1)], row_sem.at[s])

    def issue_rows(s):
        def body(t, carry):
            for kk in range(TOP_K):
                row_copy(s, t, kk, idx_smem[s, t * TOP_K + kk]).start()
            return carry
        lax.fori_loop(0, tm, body, 0)

    def wait_rows(s):
        def body(t, carry):
            for kk in range(TOP_K):
                row_copy(s, t, kk, 0).wait()
            return carry
        lax.fori_loop(0, tm, body, 0)

    @pl.when(b == 0)
    def _():
        idx_copy(0, 0).start()
        idx_copy(0, 0).wait()
        issue_rows(0)

        @pl.when(nb > 1)
        def _():
            idx_copy(1, 1).start()

    @pl.when(b + 1 < nb)
    def _():
        idx_copy(b + 1, nslot).wait()
        issue_rows(nslot)

    @pl.when(b + 2 < nb)
    def _():
        idx_copy(b + 2, slot).start()

    wait_rows(slot)
    gate = gate_ref[...]
    acc = gate[:, 0:1] * buf[slot, 0]
    for kk in range(1, TOP_K):
        acc = acc + gate[:, kk:kk + 1] * buf[slot, kk]
    g2 = mod_ref[0:1, 5 * D_MODEL:6 * D_MODEL]
    o_ref[...] = _rms(base_ref[...] + g2 * acc, fg_ref[...])


def _combine(dest, yb, base, gate, mod, fg):
    t = base.shape[0]
    tm = COMB_TILE
    nt = t // tm
    full = lambda arr: pl.BlockSpec(arr.shape, lambda i: (0,) * arr.ndim)
    return pl.pallas_call(
        _combine_kernel,
        grid=(nt,),
        in_specs=[pl.BlockSpec(memory_space=pl.ANY),
                  pl.BlockSpec(memory_space=pl.ANY),
                  pl.BlockSpec((tm, D_MODEL), lambda i: (i, 0)),
                  pl.BlockSpec((tm, LANES), lambda i: (i, 0)),
                  full(mod), full(fg)],
        out_specs=pl.BlockSpec((tm, D_MODEL), lambda i: (i, 0)),
        out_shape=jax.ShapeDtypeStruct((t, D_MODEL), F32),
        scratch_shapes=[pltpu.SMEM((2, tm * TOP_K), jnp.int32),
                        pltpu.VMEM((2, TOP_K, tm, D_MODEL), F32),
                        pltpu.SemaphoreType.DMA((2,)),
                        pltpu.SemaphoreType.DMA((2,))],
        compiler_params=pltpu.CompilerParams(dimension_semantics=("arbitrary",), vmem_limit_bytes=VMEM_LIMIT,
                                             disable_bounds_checks=True),
        name="combine",
    )(dest.reshape(nt, tm * TOP_K), yb, base, gate, mod, fg)


def _block_tables(counts, n_blocks):
    sb = SLOT_BLOCK
    padded = (counts + sb - 1) // sb * sb
    pad_end = jnp.cumsum(padded)
    pad_start = pad_end - padded
    n_used = pad_end[-1] // sb
    blk = jnp.minimum(jnp.arange(n_blocks, dtype=jnp.int32), n_used - 1)
    block_e = jnp.minimum(jnp.searchsorted(pad_end, blk * sb, side='right'), N_EXPERTS - 1).astype(jnp.int32)
    return block_e, n_used.reshape(1).astype(jnp.int32), pad_start


def _blockdiag2(w):
    z = jnp.zeros_like(w[0])
    return jnp.concatenate([jnp.concatenate([w[0], z], axis=1), jnp.concatenate([z, w[1]], axis=1)], axis=0)


def _pad_rows(w, n=8):
    return jnp.concatenate([w, jnp.zeros((n - w.shape[0],) + w.shape[1:], w.dtype)], axis=0)


def kernel(x, c, ctx, c_ctx, ada_w, ada_b, norm1_g, norm2_g, w_in, conv_w, shift_mu, decay_w0, decay_up, iclr_a0, iclr_up, key_xi, key_alpha, bonus_rho, gate_up, gn_w, gn_b, w_out, router_w, router_bias, exp_w1, exp_w3, exp_w2, sh_w1, sh_w3, sh_w2, final_g):
    assert x.shape[0] == 1 and ada_w.shape[0] == 1
    assert ctx.shape[1] == TOK_TILE and x.shape[1] % TOK_TILE == 0
    l = 0
    xs = x[0]
    row = lambda w: w.reshape(1, -1)

    cc = _pad_rows(jnp.stack([c[0], c_ctx], axis=0))
    mod = _ada(cc, ada_w[l], row(ada_b[l]))

    hid = lax.broadcasted_iota(jnp.int32, (D_RWKV, D_RWKV), 0) // HEAD_DIM
    bsum = (hid == hid.T).astype(BF16)

    xcat = jnp.concatenate([ctx[0], xs], axis=0)
    r4, v4, g, bonus, yconv, lw, kh, kt, a = _proj(
        xcat, mod, row(norm1_g[l]), w_in[l].astype(BF16), _pad_rows(conv_w[l]), _pad_rows(shift_mu[l]),
        row(decay_w0[l]), _blockdiag2(decay_up[l]), row(iclr_a0[l]), _blockdiag2(iclr_up[l]),
        row(key_xi[l]), row(key_alpha[l]), row(bonus_rho[l]), gate_up[l], bsum)

    yf, yb = _carry(*_chunks(r4, v4, lw, kh, kt, a))

    base, h2, idx, gate, rank, cnt = _mix(
        xs, yf, yb, g, bonus, yconv, mod, row(gn_w[l]), row(gn_b[l]), bsum, w_out[l].astype(BF16),
        row(norm2_g[l]), router_w[l], row(router_bias[l]),
        sh_w1[l].astype(BF16), sh_w3[l].astype(BF16), sh_w2[l].astype(BF16))

    n_blocks = xs.shape[0] * TOP_K // SLOT_BLOCK + N_EXPERTS
    block_e, n_used, pad_start = _block_tables(cnt[0].astype(jnp.int32), n_blocks)
    dest = _slots(idx, rank, pad_start.astype(F32).reshape(1, N_EXPERTS))[:, :TOP_K]
    hs = _dispatch(dest, h2, n_blocks * SLOT_BLOCK)
    ye = _experts(block_e, n_used, hs, exp_w1[l], exp_w3[l], exp_w2[l])
    out = _combine(dest, ye, base, gate, mod, row(final_g))
    return out[None]
```

```python
import functools

import jax
import jax.numpy as jnp
from jax import lax
from jax.experimental import pallas as pl
from jax.experimental.pallas import tpu as pltpu

F32 = jnp.float32
BF16 = jnp.bfloat16

D_MODEL = 1024
D_CONV = 512
D_RWKV = 512
HEAD_DIM = 64
N_HEADS = D_RWKV // HEAD_DIM
N_PAIRS = N_HEADS // 2
LORA = 128
P_RWKV = 3 * D_RWKV + 3 * LORA
P_IN = 3 * D_CONV + P_RWKV
GRID_W = 64
N_EXPERTS = 256
TOP_K = 8
D_EXPERT = 256
ROUTED_SCALE = 2.5
RMS_EPS = 1e-6
GN_EPS = 64e-5
DECAY_SCALE = 0.6065306597126334
NORM_EPS = 1e-12

TOK_TILE = 256
SCAN_BLOCK = 256
CHUNK = 64
CHUNK_PAIRS = 2
SLOT_BLOCK = 256
COMB_TILE = 128
LANES = 128
VMEM_LIMIT = 56 * 1024 * 1024

NN = ((1,), (0,))
NT = ((1,), (1,))
TN = ((0,), (0,))


def _dg(a, b, dims=NN):
    return lax.dot_general(a, b, (dims, ((), ())), preferred_element_type=F32)


def _split2(a):
    hi = a.astype(BF16)
    lo = (a - hi.astype(F32)).astype(BF16)
    return hi, lo


def _split3(a):
    hi = a.astype(BF16)
    r1 = a - hi.astype(F32)
    mid = r1.astype(BF16)
    lo = (r1 - mid.astype(F32)).astype(BF16)
    return hi, mid, lo


def _mm1(a, b, dims=NN):
    return _dg(a.astype(BF16), b.astype(BF16), dims)


def _mm3(a, b, dims=NN):
    ah, al = _split2(a)
    bh, bl = _split2(b)
    return _dg(ah, bh, dims) + (_dg(ah, bl, dims) + _dg(al, bh, dims))


def _mm_ones_rhs(a, ones_bf16, dims=NN):
    h, m, l = _split3(a)
    return _dg(h, ones_bf16, dims) + (_dg(m, ones_bf16, dims) + _dg(l, ones_bf16, dims))


def _mm_ones_lhs(ones_bf16, b, dims=NN):
    h, m, l = _split3(b)
    return _dg(ones_bf16, h, dims) + (_dg(ones_bf16, m, dims) + _dg(ones_bf16, l, dims))


HALF = D_MODEL // 2
U32 = jnp.uint32


def _pack_bf16_pairs(x):
    u = lax.bitcast_convert_type(x, U32)
    return (u[:, 0:HALF] >> 16) | (u[:, HALF:] & jnp.uint32(0xFFFF0000))


def _unpack_bf16_pairs(w):
    lo = lax.bitcast_convert_type(w << 16, F32)
    hi = lax.bitcast_convert_type(w & jnp.uint32(0xFFFF0000), F32)
    return lo, hi


def _sigmoid(x):
    return 1.0 / (1.0 + jnp.exp(-x))


def _silu(x):
    return x * _sigmoid(x)


def _rms(xv, g):
    ms = jnp.mean(xv * xv, axis=-1, keepdims=True)
    return xv * lax.rsqrt(ms + RMS_EPS) * g


def _cparams(sem):
    return pltpu.CompilerParams(dimension_semantics=sem, vmem_limit_bytes=VMEM_LIMIT)


def _ada_kernel(c_ref, w_ref, b_ref, o_ref):
    o_ref[...] = _mm3(_silu(c_ref[...]), w_ref[...]) + b_ref[...]


def _ada(cc, ada_w, ada_b):
    n = ada_w.shape[1]
    tn = 1024
    return pl.pallas_call(
        _ada_kernel,
        grid=(n // tn,),
        in_specs=[pl.BlockSpec((8, D_MODEL), lambda j: (0, 0)),
                  pl.BlockSpec((D_MODEL, tn), lambda j: (0, j)),
                  pl.BlockSpec((1, tn), lambda j: (0, j))],
        out_specs=pl.BlockSpec((8, tn), lambda j: (0, j)),
        out_shape=jax.ShapeDtypeStruct((8, n), F32),
        compiler_params=_cparams(("arbitrary",)),
        name="ada",
    )(cc, ada_w, ada_b)


def _proj_kernel(x_ref, xp_ref, xn_ref, mod_ref, n1g_ref, win_ref, convw_ref, mu_ref,
                 dw0_ref, dup_ref, ia0_ref, iup_ref, xi_ref, kal_ref, rho_ref, gup_ref, bsum_ref,
                 r_ref, v_ref, g_ref, bonus_ref, yconv_ref, lw_ref, kh_ref, kt_ref, a_ref):
    i = pl.program_id(0)
    nt = pl.num_programs(0)
    tm = x_ref.shape[0]
    is_ctx = i == 0
    sh = jnp.where(is_ctx, mod_ref[1:2, 0:D_MODEL], mod_ref[0:1, 0:D_MODEL])
    sc = jnp.where(is_ctx, mod_ref[1:2, D_MODEL:2 * D_MODEL], mod_ref[0:1, D_MODEL:2 * D_MODEL])
    n1g = n1g_ref[...]

    def norm_mod(xv):
        return _rms(xv, n1g) * (1.0 + sc) + sh

    h = norm_mod(x_ref[...]).astype(BF16)
    p = _dg(h, win_ref[...])
    hh = norm_mod(jnp.concatenate([xp_ref[...], xn_ref[...]], axis=0)).astype(BF16)
    ph = _dg(hh, win_ref[:, 3 * D_CONV:])

    rows = lax.broadcasted_iota(jnp.int32, (tm, 1), 0)

    bg = p[:, 0:D_CONV]
    z = p[:, D_CONV:2 * D_CONV] * p[:, 2 * D_CONV:3 * D_CONV]
    col = rows % GRID_W
    zp = jnp.where(col == 0, 0.0, pltpu.roll(z, 1, 0))
    zn = jnp.where(col == GRID_W - 1, 0.0, pltpu.roll(z, tm - 1, 0))
    yconv = bg * (convw_ref[0:1, :] * zp + convw_ref[1:2, :] * z + convw_ref[2:3, :] * zn)
    yconv_ref[...] = yconv.astype(BF16)

    prev_ok = jnp.logical_and(i != 0, i != 1).astype(F32)
    next_ok = jnp.logical_and(i != 0, i != nt - 1).astype(F32)
    cur = p[:, 3 * D_CONV:]
    prev = jnp.where(rows == 0, ph[7:8, :] * prev_ok, pltpu.roll(cur, 1, 0))
    nxt = jnp.where(rows == tm - 1, ph[8:9, :] * next_ok, pltpu.roll(cur, tm - 1, 0))
    ps = cur + mu_ref[0:1, :] * (prev - cur) + mu_ref[1:2, :] * (nxt - cur)

    r = ps[:, 0:D_RWKV]
    k = ps[:, D_RWKV:2 * D_RWKV]
    v = ps[:, 2 * D_RWKV:3 * D_RWKV]
    o = 3 * D_RWKV
    wlo = ps[:, o:o + LORA]
    alo = ps[:, o + LORA:o + 2 * LORA]
    glo = ps[:, o + 2 * LORA:o + 3 * LORA]

    dd = dw0_ref[...] + _mm3(jnp.tanh(wlo), dup_ref[...])
    lw = -DECAY_SCALE * _sigmoid(dd)
    a = _sigmoid(ia0_ref[...] + _mm3(alo, iup_ref[...]))
    g = _mm3(_sigmoid(glo), gup_ref[...])
    k2 = jnp.concatenate([k, k], axis=1)
    kap = k2 * xi_ref[...]
    kap2 = kap * kap
    bs = bsum_ref[...]
    ss = jnp.concatenate([_mm_ones_rhs(kap2[:, 0:D_RWKV], bs),
                          _mm_ones_rhs(kap2[:, D_RWKV:], bs)], axis=1)
    kh = kap * lax.rsqrt(ss + NORM_EPS)
    kt = k2 * (1.0 + (a - 1.0) * kal_ref[...])
    bon = _mm_ones_rhs(r * rho_ref[...] * (kt[:, 0:D_RWKV] + kt[:, D_RWKV:]), bs)

    g_ref[...] = g
    bonus_ref[...] = bon * v
    for pr in range(N_PAIRS):
        ls = slice(pr * LANES, (pr + 1) * LANES)
        r_ref[pr] = r[:, ls]
        v_ref[pr] = v[:, ls]
        for d in range(2):
            ld = slice(d * D_RWKV + pr * LANES, d * D_RWKV + (pr + 1) * LANES)
            lw_ref[d, pr] = lw[:, ld]
            kh_ref[d, pr] = kh[:, ld]
            kt_ref[d, pr] = kt[:, ld]
            a_ref[d, pr] = a[:, ld]


def _proj(xcat, mod, n1g, win, convw, mu, dw0, dup, ia0, iup, xi, kal, rho, gup, bsum):
    t = xcat.shape[0]
    tm = TOK_TILE
    nt = t // tm
    nb8 = t // 8
    full = lambda arr: pl.BlockSpec(arr.shape, lambda i: (0,) * arr.ndim)
    pair_spec = pl.BlockSpec((N_PAIRS, tm, LANES), lambda i: (0, i, 0))
    dpair_spec = pl.BlockSpec((2, N_PAIRS, tm, LANES), lambda i: (0, 0, i, 0))
    row_spec = pl.BlockSpec((tm, D_RWKV), lambda i: (i, 0))
    pair_shape = jax.ShapeDtypeStruct((N_PAIRS, t, LANES), F32)
    dpair_shape = jax.ShapeDtypeStruct((2, N_PAIRS, t, LANES), F32)
    consts = (mod, n1g, win, convw, mu, dw0, dup, ia0, iup, xi, kal, rho, gup, bsum)
    return pl.pallas_call(
        _proj_kernel,
        grid=(nt,),
        in_specs=[pl.BlockSpec((tm, D_MODEL), lambda i: (i, 0)),
                  pl.BlockSpec((8, D_MODEL), lambda i: (jnp.maximum(i * (tm // 8) - 1, 0), 0)),
                  pl.BlockSpec((8, D_MODEL), lambda i: (jnp.minimum((i + 1) * (tm // 8), nb8 - 1), 0))]
                 + [full(c) for c in consts],
        out_specs=[pair_spec, pair_spec, row_spec, row_spec, row_spec,
                   dpair_spec, dpair_spec, dpair_spec, dpair_spec],
        out_shape=[pair_shape, pair_shape,
                   jax.ShapeDtypeStruct((t, D_RWKV), F32), jax.ShapeDtypeStruct((t, D_RWKV), F32),
                   jax.ShapeDtypeStruct((t, D_CONV), BF16),
                   dpair_shape, dpair_shape, dpair_shape, dpair_shape],
        compiler_params=_cparams(("arbitrary",)),
        name="proj",
    )(xcat, xcat, xcat, *consts)


def _chunk_kernel(r_ref, v_ref, lw_ref, kh_ref, kt_ref, a_ref, rh_ref, yh_ref, mt_ref, dt_ref):
    d = pl.program_id(0)
    rev = d == 1
    c = CHUNK
    npair = r_ref.shape[0]
    nsub = r_ref.shape[1] // c

    row = lax.broadcasted_iota(jnp.int32, (c, c), 0)
    col = lax.broadcasted_iota(jnp.int32, (c, c), 1)
    strict = (col - row) * (1 - 2 * d) < 0
    eye = row == col
    incl = jnp.logical_or(strict, eye)
    incl_bf = jnp.where(incl, 1.0, 0.0).astype(BF16)
    eye_f = jnp.where(eye, 1.0, 0.0)
    head0 = lax.broadcasted_iota(jnp.int32, (c, LANES), 1) < HEAD_DIM
    row2 = lax.broadcasted_iota(jnp.int32, (LANES, LANES), 0)
    col2 = lax.broadcasted_iota(jnp.int32, (LANES, LANES), 1)
    same_head = (row2 < HEAD_DIM) == (col2 < HEAD_DIM)
    eye2 = row2 == col2

    subs = range(npair * nsub)
    units = [(s, h) for s in subs for h in range(2)]
    sls = [(pp, slice(s * c, (s + 1) * c), slice(None)) for pp in range(npair) for s in range(nsub)]
    sls2 = [(pp, slice(2 * s * c, 2 * (s + 1) * c), slice(None)) for pp in range(npair) for s in range(nsub)]
    v = [v_ref[sl] for sl in sls]
    lw = [lw_ref[sl] for sl in sls]
    cum = [_mm_ones_lhs(incl_bf, lw[s]) for s in subs]
    al, be, kk, rr, bew, kkw, wc = [], [], [], [], [], [], []
    for s in subs:
        cum_last = jnp.where(rev, cum[s][0:1, :], cum[s][c - 1:c, :])
        e_end = jnp.exp(cum_last - cum[s])
        e_neg = jnp.exp(-cum[s])
        kh = kh_ref[sls[s]]
        kt = kt_ref[sls[s]]
        kha = kh * a_ref[sls[s]]
        al.append(kh * jnp.exp(cum[s] - lw[s]))
        be.append(-(kha * e_neg))
        kk.append(kt * e_neg)
        rr.append(r_ref[sls[s]] * jnp.exp(cum[s]))
        bew.append(-(kha * e_end))
        kkw.append(kt * e_end)
        wc.append(jnp.exp(cum_last))

    lhs = {}
    for s, h in units:
        hm = head0 if h == 0 else jnp.logical_not(head0)
        lhs[s, h] = jnp.concatenate([jnp.where(hm, al[s], 0.0), jnp.where(hm, rr[s], 0.0)], axis=0)
    gb = {u: _mm1(lhs[u], be[u[0]], NT) for u in units}
    gk = {u: _mm1(lhs[u], kk[u[0]], NT) for u in units}
    a_ab = {u: jnp.where(strict, gb[u][0:c], 0.0) for u in units}
    a_ak = {u: jnp.where(strict, gk[u][0:c], 0.0) for u in units}
    a_rb = {u: jnp.where(incl, gb[u][c:2 * c], 0.0) for u in units}
    a_rk = {u: jnp.where(incl, gk[u][c:2 * c], 0.0) for u in units}

    pw = dict(a_ab)
    tinv = {u: eye_f + a_ab[u] for u in units}
    avk = {u: _mm1(a_ak[u], v[u[0]]) for u in units}
    for _ in range(c.bit_length() - 2):
        pw = {u: _mm1(pw[u], pw[u]) for u in units}
        tinv = {u: tinv[u] + _mm1(tinv[u], pw[u]) for u in units}
    tu = {u: _mm1(tinv[u], jnp.concatenate([al[u[0]], avk[u]], axis=1)) for u in units}
    at = [jnp.where(head0, tu[s, 0][:, 0:LANES], tu[s, 1][:, 0:LANES]) for s in subs]
    ut = [jnp.where(head0, tu[s, 0][:, LANES:], tu[s, 1][:, LANES:]) for s in subs]

    rb_at = {u: _mm1(a_rb[u], at[u[0]]) for u in units}
    rb_ut = {u: _mm1(a_rb[u], ut[u[0]]) for u in units}
    rk_v = {u: _mm1(a_rk[u], v[u[0]]) for u in units}
    m_off = [_mm1(bew[s], at[s], TN) for s in subs]
    d_u = [_mm1(bew[s], ut[s], TN) for s in subs]
    d_v = [_mm1(kkw[s], v[s], TN) for s in subs]
    for s in subs:
        rh_ref[sls[s]] = rr[s] + jnp.where(head0, rb_at[s, 0], rb_at[s, 1])
        yh_ref[sls[s]] = jnp.where(head0, rb_ut[s, 0] + rk_v[s, 0], rb_ut[s, 1] + rk_v[s, 1])
        mt_ref[sls2[s]] = jnp.where(eye2, wc[s], 0.0) + jnp.where(same_head, m_off[s], 0.0)
        dt_ref[sls2[s]] = jnp.where(same_head, d_u[s] + d_v[s], 0.0)


def _chunks(r4, v4, lw, kh, kt, a):
    t = r4.shape[1]
    tb = SCAN_BLOCK
    nb = t // tb
    pp = CHUNK_PAIRS
    shared = pl.BlockSpec((pp, tb, LANES), lambda d, p, j: (p, j, 0))
    perdir = pl.BlockSpec((None, pp, tb, LANES), lambda d, p, j: (d, p, j, 0))
    perdir2 = pl.BlockSpec((None, pp, 2 * tb, LANES), lambda d, p, j: (d, p, j, 0))
    shp = jax.ShapeDtypeStruct((2, N_PAIRS, t, LANES), F32)
    shp2 = jax.ShapeDtypeStruct((2, N_PAIRS, 2 * t, LANES), F32)
    return pl.pallas_call(
        _chunk_kernel,
        grid=(2, N_PAIRS // pp, nb),
        in_specs=[shared, shared, perdir, perdir, perdir, perdir],
        out_specs=[perdir, perdir, perdir2, perdir2],
        out_shape=[shp, shp, shp2, shp2],
        compiler_params=_cparams(("arbitrary", "arbitrary", "arbitrary")),
        name="chunks",
    )(r4, v4, lw, kh, kt, a)


def _carry_kernel(rhf_ref, yhf_ref, mtf_ref, dtf_ref, rhb_ref, yhb_ref, mtb_ref, dtb_ref,
                  yf_ref, yb_ref, q_ref):
    j = pl.program_id(0)
    c = CHUNK
    nsub = rhf_ref.shape[1] // c

    @pl.when(j == 0)
    def _():
        q_ref[...] = jnp.zeros(q_ref.shape, F32)

    dirs = ((rhf_ref, yhf_ref, mtf_ref, dtf_ref, yf_ref), (rhb_ref, yhb_ref, mtb_ref, dtb_ref, yb_ref))
    chains = [(dirn, p) for dirn in range(2) for p in range(N_PAIRS)]
    q = {ch: q_ref[ch[0], ch[1]] for ch in chains}
    for s in range(nsub):
        ys, qn = {}, {}
        for dirn, p in chains:
            rh_ref, yh_ref, mt_ref, dt_ref, _ = dirs[dirn]
            cs = s if dirn == 0 else nsub - 1 - s
            ys[dirn, p] = _mm3(rh_ref[p, cs * c:(cs + 1) * c, :], q[dirn, p])
            qn[dirn, p] = _mm3(mt_ref[p, 2 * cs * c:2 * (cs + 1) * c, :], q[dirn, p])
        for dirn, p in chains:
            _, yh_ref, _, dt_ref, y_ref = dirs[dirn]
            cs = s if dirn == 0 else nsub - 1 - s
            y_ref[p, cs * c:(cs + 1) * c, :] = ys[dirn, p] + yh_ref[p, cs * c:(cs + 1) * c, :]
            q[dirn, p] = qn[dirn, p] + dt_ref[p, 2 * cs * c:2 * (cs + 1) * c, :]
    for dirn, p in chains:
        q_ref[dirn, p] = q[dirn, p]


def _carry(rh, yh, mt, dt):
    t = rh.shape[2]
    tb = SCAN_BLOCK
    nb = t // tb
    bwd = lambda j: jnp.where(j == 0, 0, nb - j)
    f1 = pl.BlockSpec((None, N_PAIRS, tb, LANES), lambda j: (0, 0, j, 0))
    f2 = pl.BlockSpec((None, N_PAIRS, 2 * tb, LANES), lambda j: (0, 0, j, 0))
    b1 = pl.BlockSpec((None, N_PAIRS, tb, LANES), lambda j: (1, 0, bwd(j), 0))
    b2 = pl.BlockSpec((None, N_PAIRS, 2 * tb, LANES), lambda j: (1, 0, bwd(j), 0))
    yshape = jax.ShapeDtypeStruct((N_PAIRS, t, LANES), F32)
    return pl.pallas_call(
        _carry_kernel,
        grid=(nb,),
        in_specs=[f1, f1, f2, f2, b1, b1, b2, b2],
        out_specs=[pl.BlockSpec((N_PAIRS, tb, LANES), lambda j: (0, j, 0)),
                   pl.BlockSpec((N_PAIRS, tb, LANES), lambda j: (0, bwd(j), 0))],
        out_shape=[yshape, yshape],
        scratch_shapes=[pltpu.VMEM((2, N_PAIRS, LANES, LANES), F32)],
        compiler_params=_cparams(("arbitrary",)),
        name="carry",
    )(rh, yh, mt, dt, rh, yh, mt, dt)


def _mix_kernel(x_ref, yf_ref, yb_ref, g_ref, bonus_ref, yconv_ref, mod_ref, gnw_ref, gnb_ref, bsum_ref,
                wout_ref, n2g_ref, rw_ref, rb_ref, sw1_ref, sw3_ref, sw2_ref,
                base_ref, h2_ref, idx_ref, gate_ref, rank_ref, cnt_ref, run_ref):
    tm = x_ref.shape[0]
    g1 = mod_ref[0:1, 2 * D_MODEL:3 * D_MODEL]
    sh2 = mod_ref[0:1, 3 * D_MODEL:4 * D_MODEL]
    sc2 = mod_ref[0:1, 4 * D_MODEL:5 * D_MODEL]
    g2 = mod_ref[0:1, 5 * D_MODEL:6 * D_MODEL]

    @pl.when(pl.program_id(0) == 0)
    def _():
        run_ref[...] = jnp.zeros(run_ref.shape, F32)

    y = jnp.concatenate([yf_ref[pr] + yb_ref[pr] for pr in range(N_PAIRS)], axis=1)
    bs = bsum_ref[...]
    mu = _mm_ones_rhs(y, bs) * (1.0 / HEAD_DIM)
    yc = y - mu
    var = _mm_ones_rhs(yc * yc, bs) * (1.0 / HEAD_DIM)
    yn = yc * lax.rsqrt(var + GN_EPS) * gnw_ref[...] + gnb_ref[...]
    yrw = ((yn + bonus_ref[...]) * g_ref[...]).astype(BF16)
    mix = _dg(yconv_ref[...], wout_ref[0:D_CONV, :]) + _dg(yrw, wout_ref[D_CONV:, :])
    x1 = x_ref[...] + g1 * mix

    h2 = _rms(x1, n2g_ref[...]) * (1.0 + sc2) + sh2
    hb = h2.astype(BF16)
    h2_ref[...] = _pack_bf16_pairs(hb.astype(F32))

    scores = _sigmoid(_mm3(h2, rw_ref[...]))
    work = scores + rb_ref[...]
    lane_e = lax.broadcasted_iota(jnp.int32, (tm, N_EXPERTS), 1).astype(F32)
    lane_o = lax.broadcasted_iota(jnp.int32, (tm, LANES), 1)
    idx_acc = jnp.zeros((tm, LANES), F32)
    gate_acc = jnp.zeros((tm, LANES), F32)
    gsum = jnp.zeros((tm, 1), F32)
    chosen = jnp.zeros((tm, N_EXPERTS), F32)
    sels = []
    for kk in range(TOP_K):
        m = jnp.max(work, axis=-1, keepdims=True)
        sel = jnp.min(jnp.where(work == m, lane_e, float(N_EXPERTS)), axis=-1, keepdims=True)
        hit = lane_e == sel
        sk = jnp.sum(jnp.where(hit, scores, 0.0), axis=-1, keepdims=True)
        idx_acc = jnp.where(lane_o == kk, sel, idx_acc)
        gate_acc = jnp.where(lane_o == kk, sk, gate_acc)
        gsum = gsum + sk
        work = jnp.where(hit, -jnp.inf, work)
        chosen = jnp.where(hit, 1.0, chosen)
        sels.append(sel)
    idx_ref[...] = idx_acc.astype(jnp.int32)
    gate_ref[...] = gate_acc / gsum * ROUTED_SCALE

    trow = lax.broadcasted_iota(jnp.int32, (tm, tm), 0)
    tcol = lax.broadcasted_iota(jnp.int32, (tm, tm), 1)
    earlier = jnp.where(tcol < trow, 1.0, 0.0).astype(BF16)
    before = _dg(earlier, chosen.astype(BF16)) + run_ref[0:1, :]
    rank_acc = jnp.zeros((tm, LANES), F32)
    for kk in range(TOP_K):
        rk = jnp.sum(jnp.where(lane_e == sels[kk], before, 0.0), axis=-1, keepdims=True)
        rank_acc = jnp.where(lane_o == kk, rk, rank_acc)
    rank_ref[...] = rank_acc.astype(jnp.int32)
    run_ref[...] = run_ref[...] + jnp.sum(chosen, axis=0, keepdims=True)
    cnt_ref[...] = run_ref[...]

    act = (_silu(_dg(hb, sw1_ref[...])) * _dg(hb, sw3_ref[...])).astype(BF16)
    base_ref[...] = x1 + g2 * _dg(act, sw2_ref[...])


def _mix(x, yf, yb, g, bonus, yconv, mod, gnw, gnb, bsum, wout, n2g, rw, rb, sw1, sw3, sw2):
    t = x.shape[0]
    tm = TOK_TILE
    off = (yf.shape[1] - t) // tm
    full = lambda arr: pl.BlockSpec(arr.shape, lambda i: (0,) * arr.ndim)
    consts = (mod, gnw, gnb, bsum, wout, n2g, rw, rb, sw1, sw3, sw2)
    cat_spec = lambda w: pl.BlockSpec((tm, w), lambda i: (i + off, 0))
    tok_spec = lambda w: pl.BlockSpec((tm, w), lambda i: (i, 0))
    y_spec = pl.BlockSpec((N_PAIRS, tm, LANES), lambda i: (0, i + off, 0))
    return pl.pallas_call(
        _mix_kernel,
        grid=(t // tm,),
        in_specs=[tok_spec(D_MODEL), y_spec, y_spec,
                  cat_spec(D_RWKV), cat_spec(D_RWKV), cat_spec(D_CONV)]
                 + [full(c) for c in consts],
        out_specs=[tok_spec(D_MODEL), tok_spec(HALF), tok_spec(LANES), tok_spec(LANES), tok_spec(LANES),
                   pl.BlockSpec((8, N_EXPERTS), lambda i: (0, 0))],
        out_shape=[jax.ShapeDtypeStruct((t, D_MODEL), F32), jax.ShapeDtypeStruct((t, HALF), U32),
                   jax.ShapeDtypeStruct((t, LANES), jnp.int32), jax.ShapeDtypeStruct((t, LANES), F32),
                   jax.ShapeDtypeStruct((t, LANES), jnp.int32), jax.ShapeDtypeStruct((8, N_EXPERTS), F32)],
        scratch_shapes=[pltpu.VMEM((8, N_EXPERTS), F32)],
        compiler_params=_cparams(("arbitrary",)),
        name="mix",
    )(x, yf, yb, g, bonus, yconv, *consts)


def _slots_kernel(idx_ref, rank_ref, start_ref, dest_ref):
    tm = idx_ref.shape[0]
    lane_e = lax.broadcasted_iota(jnp.int32, (tm, N_EXPERTS), 1)
    lane_o = lax.broadcasted_iota(jnp.int32, (tm, LANES), 1)
    idx = idx_ref[...]
    start = start_ref[...]
    acc = jnp.zeros((tm, LANES), F32)
    for kk in range(TOP_K):
        st = jnp.sum(jnp.where(lane_e == idx[:, kk:kk + 1], start, 0.0), axis=-1, keepdims=True)
        acc = jnp.where(lane_o == kk, st, acc)
    dest_ref[...] = acc.astype(jnp.int32) + rank_ref[...]


def _slots(idx, rank, pad_start):
    t = idx.shape[0]
    tm = TOK_TILE
    spec = pl.BlockSpec((tm, LANES), lambda i: (i, 0))
    return pl.pallas_call(
        _slots_kernel,
        grid=(t // tm,),
        in_specs=[spec, spec, pl.BlockSpec((1, N_EXPERTS), lambda i: (0, 0))],
        out_specs=spec,
        out_shape=jax.ShapeDtypeStruct((t, LANES), jnp.int32),
        compiler_params=_cparams(("arbitrary",)),
        name="slots",
    )(idx, rank, pad_start)


def _dispatch_kernel(dest_hbm, h_ref, hs_in, hs_out, idx_smem, idx_sem, row_sem):
    del hs_in
    b = pl.program_id(0)
    nb = pl.num_programs(0)
    slot = b % 2
    tm = h_ref.shape[0]

    def idx_copy(blk, s):
        return pltpu.make_async_copy(dest_hbm.at[blk], idx_smem.at[s], idx_sem.at[s])

    def row_copy(t, dst):
        return pltpu.make_async_copy(h_ref.at[pl.ds(t, 1)], hs_out.at[pl.ds(dst, 1)], row_sem.at[0])

    @pl.when(b == 0)
    def _():
        idx_copy(0, 0).start()

    idx_copy(b, slot).wait()

    @pl.when(b + 1 < nb)
    def _():
        idx_copy(b + 1, 1 - slot).start()

    def issue(t, carry):
        for kk in range(TOP_K):
            row_copy(t, idx_smem[slot, t * TOP_K + kk]).start()
        return carry
    lax.fori_loop(0, tm, issue, 0)

    def drain(t, carry):
        for kk in range(TOP_K):
            row_copy(t, 0).wait()
        return carry
    lax.fori_loop(0, tm, drain, 0)


def _dispatch(dest, h2, n_slots):
    t = h2.shape[0]
    tm = COMB_TILE
    nt = t // tm
    hs0 = jnp.zeros((n_slots, HALF), U32)
    return pl.pallas_call(
        _dispatch_kernel,
        grid=(nt,),
        in_specs=[pl.BlockSpec(memory_space=pl.ANY),
                  pl.BlockSpec((tm, HALF), lambda i: (i, 0)),
                  pl.BlockSpec(memory_space=pl.ANY)],
        out_specs=pl.BlockSpec(memory_space=pl.ANY),
        out_shape=jax.ShapeDtypeStruct((n_slots, HALF), U32),
        scratch_shapes=[pltpu.SMEM((2, tm * TOP_K), jnp.int32),
                        pltpu.SemaphoreType.DMA((2,)),
                        pltpu.SemaphoreType.DMA((1,))],
        input_output_aliases={2: 0},
        compiler_params=pltpu.CompilerParams(dimension_semantics=("arbitrary",), vmem_limit_bytes=VMEM_LIMIT,
                                             disable_bounds_checks=True),
        name="dispatch",
    )(dest.reshape(nt, tm * TOP_K), h2, hs0)


def _experts_kernel(be_ref, nu_ref, hs_ref, w1_ref, w3_ref, w2_ref, o_ref, w1c, w3c, w2c):
    b = pl.program_id(0)
    new_expert = jnp.logical_or(b == 0, be_ref[b] != be_ref[jnp.maximum(b - 1, 0)])

    @pl.when(new_expert)
    def _():
        w1c[...] = w1_ref[...].astype(BF16)
        w3c[...] = w3_ref[...].astype(BF16)
        w2c[...] = w2_ref[...].astype(BF16)

    @pl.when(b < nu_ref[0])
    def _():
        lo, hi = _unpack_bf16_pairs(hs_ref[...])
        lo = lo.astype(BF16)
        hi = hi.astype(BF16)
        a1 = _dg(lo, w1c[0:HALF, :]) + _dg(hi, w1c[HALF:, :])
        a3 = _dg(lo, w3c[0:HALF, :]) + _dg(hi, w3c[HALF:, :])
        act = (_silu(a1) * a3).astype(BF16)
        y = _dg(act, w2c[...]).astype(BF16).astype(F32)
        o_ref[...] = _pack_bf16_pairs(y)

    @pl.when(b >= nu_ref[0])
    def _():
        o_ref[...] = jnp.zeros(o_ref.shape, U32)


def _experts(block_e, n_used, hs, w1, w3, w2):
    nblk = block_e.shape[0]
    sb = SLOT_BLOCK
    row_map = lambda b, be, nu: (jnp.minimum(b, nu[0] - 1), 0)
    w_map = lambda b, be, nu: (be[b], 0, 0)
    grid_spec = pltpu.PrefetchScalarGridSpec(
        num_scalar_prefetch=2,
        grid=(nblk,),
        in_specs=[pl.BlockSpec((sb, HALF), row_map),
                  pl.BlockSpec((None, D_MODEL, D_EXPERT), w_map),
                  pl.BlockSpec((None, D_MODEL, D_EXPERT), w_map),
                  pl.BlockSpec((None, D_EXPERT, D_MODEL), w_map)],
        out_specs=pl.BlockSpec((sb, HALF), lambda b, be, nu: (b, 0)),
        scratch_shapes=[pltpu.VMEM((D_MODEL, D_EXPERT), BF16), pltpu.VMEM((D_MODEL, D_EXPERT), BF16),
                        pltpu.VMEM((D_EXPERT, D_MODEL), BF16)],
    )
    return pl.pallas_call(
        _experts_kernel,
        grid_spec=grid_spec,
        out_shape=jax.ShapeDtypeStruct((nblk * sb, HALF), U32),
        compiler_params=_cparams(("arbitrary",)),
        name="experts",
    )(block_e, n_used, hs, w1, w3, w2)


def _combine_kernel(dest_hbm, yb_hbm, base_ref, gate_ref, mod_ref, fg_ref, o_ref,
                    idx_smem, buf, idx_sem, row_sem):
    b = pl.program_id(0)
    nb = pl.num_programs(0)
    slot = b % 2
    nslot = 1 - slot
    tm = base_ref.shape[0]
    nrow = tm * TOP_K

    def idx_copy(blk, s):
        return pltpu.make_async_copy(dest_hbm.at[blk], idx_smem.at[s], idx_sem.at[s])

    def row_copy(s, t, kk, src):
        return pltpu.make_async_copy(yb_hbm.at[pl.ds(src, 1)], buf.at[s, kk, pl.ds(t, 1)], row_sem.at[s])

    def issue_rows(s):
        def body(t, carry):
            for kk in range(TOP_K):
                row_copy(s, t, kk, idx_smem[s, t * TOP_K + kk]).start()
            return carry
        lax.fori_loop(0, tm, body, 0)

    def wait_rows(s):
        def body(t, carry):
            for kk in range(TOP_K):
                row_copy(s, t, kk, 0).wait()
            return carry
        lax.fori_loop(0, tm, body, 0)

    @pl.when(b == 0)
    def _():
        idx_copy(0, 0).start()
        idx_copy(0, 0).wait()
        issue_rows(0)

        @pl.when(nb > 1)
        def _():
            idx_copy(1, 1).start()

    @pl.when(b + 1 < nb)
    def _():
        idx_copy(b + 1, nslot).wait()
        issue_rows(nslot)

    @pl.when(b + 2 < nb)
    def _():
        idx_copy(b + 2, slot).start()

    wait_rows(slot)
    gate = gate_ref[...]
    acc_lo = jnp.zeros((tm, HALF), F32)
    acc_hi = jnp.zeros((tm, HALF), F32)
    for kk in range(TOP_K):
        lo, hi = _unpack_bf16_pairs(buf[slot, kk])
        acc_lo = acc_lo + gate[:, kk:kk + 1] * lo
        acc_hi = acc_hi + gate[:, kk:kk + 1] * hi
    acc = jnp.concatenate([acc_lo, acc_hi], axis=1)
    g2 = mod_ref[0:1, 5 * D_MODEL:6 * D_MODEL]
    o_ref[...] = _rms(base_ref[...] + g2 * acc, fg_ref[...])


def _combine(dest, yb, base, gate, mod, fg):
    t = base.shape[0]
    tm = COMB_TILE
    nt = t // tm
    full = lambda arr: pl.BlockSpec(arr.shape, lambda i: (0,) * arr.ndim)
    return pl.pallas_call(
        _combine_kernel,
        grid=(nt,),
        in_specs=[pl.BlockSpec(memory_space=pl.ANY),
                  pl.BlockSpec(memory_space=pl.ANY),
                  pl.BlockSpec((tm, D_MODEL), lambda i: (i, 0)),
                  pl.BlockSpec((tm, LANES), lambda i: (i, 0)),
                  full(mod), full(fg)],
        out_specs=pl.BlockSpec((tm, D_MODEL), lambda i: (i, 0)),
        out_shape=jax.ShapeDtypeStruct((t, D_MODEL), F32),
        scratch_shapes=[pltpu.SMEM((2, tm * TOP_K), jnp.int32),
                        pltpu.VMEM((2, TOP_K, tm, HALF), U32),
                        pltpu.SemaphoreType.DMA((2,)),
                        pltpu.SemaphoreType.DMA((2,))],
        compiler_params=pltpu.CompilerParams(dimension_semantics=("arbitrary",), vmem_limit_bytes=VMEM_LIMIT,
                                             disable_bounds_checks=True),
        name="combine",
    )(dest.reshape(nt, tm * TOP_K), yb, base, gate, mod, fg)


def _block_tables(counts, n_blocks):
    sb = SLOT_BLOCK
    padded = (counts + sb - 1) // sb * sb
    pad_end = jnp.cumsum(padded)
    pad_start = pad_end - padded
    n_used = pad_end[-1] // sb
    blk = jnp.minimum(jnp.arange(n_blocks, dtype=jnp.int32), n_used - 1)
    block_e = jnp.minimum(jnp.sum(pad_end[None, :] <= (blk * sb)[:, None], axis=1), N_EXPERTS - 1).astype(jnp.int32)
    return block_e, n_used.reshape(1).astype(jnp.int32), pad_start


def _blockdiag2(w):
    z = jnp.zeros_like(w[0])
    return jnp.concatenate([jnp.concatenate([w[0], z], axis=1), jnp.concatenate([z, w[1]], axis=1)], axis=0)


def _pad_rows(w, n=8):
    return jnp.concatenate([w, jnp.zeros((n - w.shape[0],) + w.shape[1:], w.dtype)], axis=0)


def kernel(x, c, ctx, c_ctx, ada_w, ada_b, norm1_g, norm2_g, w_in, conv_w, shift_mu, decay_w0, decay_up, iclr_a0, iclr_up, key_xi, key_alpha, bonus_rho, gate_up, gn_w, gn_b, w_out, router_w, router_bias, exp_w1, exp_w3, exp_w2, sh_w1, sh_w3, sh_w2, final_g):
    assert x.shape[0] == 1 and ada_w.shape[0] == 1
    assert ctx.shape[1] == TOK_TILE and x.shape[1] % TOK_TILE == 0
    l = 0
    xs = x[0]
    row = lambda w: w.reshape(1, -1)

    cc = _pad_rows(jnp.stack([c[0], c_ctx], axis=0))
    mod = _ada(cc, ada_w[l], row(ada_b[l]))

    hid = lax.broadcasted_iota(jnp.int32, (D_RWKV, D_RWKV), 0) // HEAD_DIM
    bsum = (hid == hid.T).astype(BF16)

    xcat = jnp.concatenate([ctx[0], xs], axis=0)
    r4, v4, g, bonus, yconv, lw, kh, kt, a = _proj(
        xcat, mod, row(norm1_g[l]), w_in[l].astype(BF16), _pad_rows(conv_w[l]), _pad_rows(shift_mu[l]),
        row(decay_w0[l]), _blockdiag2(decay_up[l]), row(iclr_a0[l]), _blockdiag2(iclr_up[l]),
        row(key_xi[l]), row(key_alpha[l]), row(bonus_rho[l]), gate_up[l], bsum)

    yf, yb = _carry(*_chunks(r4, v4, lw, kh, kt, a))

    base, h2, idx, gate, rank, cnt = _mix(
        xs, yf, yb, g, bonus, yconv, mod, row(gn_w[l]), row(gn_b[l]), bsum, w_out[l].astype(BF16),
        row(norm2_g[l]), router_w[l], row(router_bias[l]),
        sh_w1[l].astype(BF16), sh_w3[l].astype(BF16), sh_w2[l].astype(BF16))

    n_blocks = xs.shape[0] * TOP_K // SLOT_BLOCK + N_EXPERTS
    block_e, n_used, pad_start = _block_tables(cnt[0].astype(jnp.int32), n_blocks)
    dest = _slots(idx, rank, pad_start.astype(F32).reshape(1, N_EXPERTS))[:, :TOP_K]
    hs = _dispatch(dest, h2, n_blocks * SLOT_BLOCK)
    ye = _experts(block_e, n_used, hs, exp_w1[l], exp_w3[l], exp_w2[l])
    out = _combine(dest, ye, base, gate, mod, row(final_g))
    return out[None]
```

```python
import functools

import jax
import jax.numpy as jnp
from jax import lax
from jax.experimental import pallas as pl
from jax.experimental.pallas import tpu as pltpu
from jax.experimental.pallas import tpu_sc as plsc

F32 = jnp.float32
BF16 = jnp.bfloat16

D_MODEL = 1024
D_CONV = 512
D_RWKV = 512
HEAD_DIM = 64
N_HEADS = D_RWKV // HEAD_DIM
N_PAIRS = N_HEADS // 2
LORA = 128
P_RWKV = 3 * D_RWKV + 3 * LORA
P_IN = 3 * D_CONV + P_RWKV
GRID_W = 64
N_EXPERTS = 256
TOP_K = 8
D_EXPERT = 256
ROUTED_SCALE = 2.5
RMS_EPS = 1e-6
GN_EPS = 64e-5
DECAY_SCALE = 0.6065306597126334
NORM_EPS = 1e-12

TOK_TILE = 256
SCAN_BLOCK = 256
CHUNK = 64
CHUNK_PAIRS = 2
SLOT_BLOCK = 256
COMB_TILE = 128
SC_CHUNK = 64
LANES = 128
VMEM_LIMIT = 56 * 1024 * 1024

NN = ((1,), (0,))
NT = ((1,), (1,))
TN = ((0,), (0,))


def _dg(a, b, dims=NN):
    return lax.dot_general(a, b, (dims, ((), ())), preferred_element_type=F32)


def _split2(a):
    hi = a.astype(BF16)
    lo = (a - hi.astype(F32)).astype(BF16)
    return hi, lo


def _split3(a):
    hi = a.astype(BF16)
    r1 = a - hi.astype(F32)
    mid = r1.astype(BF16)
    lo = (r1 - mid.astype(F32)).astype(BF16)
    return hi, mid, lo


def _mm1(a, b, dims=NN):
    return _dg(a.astype(BF16), b.astype(BF16), dims)


def _mm3(a, b, dims=NN):
    ah, al = _split2(a)
    bh, bl = _split2(b)
    return _dg(ah, bh, dims) + (_dg(ah, bl, dims) + _dg(al, bh, dims))


def _mm_ones_rhs(a, ones_bf16, dims=NN):
    h, m, l = _split3(a)
    return _dg(h, ones_bf16, dims) + (_dg(m, ones_bf16, dims) + _dg(l, ones_bf16, dims))


def _mm_ones_lhs(ones_bf16, b, dims=NN):
    h, m, l = _split3(b)
    return _dg(ones_bf16, h, dims) + (_dg(ones_bf16, m, dims) + _dg(ones_bf16, l, dims))


HALF = D_MODEL // 2
U32 = jnp.uint32


def _pack_bf16_pairs(x):
    u = lax.bitcast_convert_type(x, U32)
    return (u[:, 0:HALF] >> 16) | (u[:, HALF:] & jnp.uint32(0xFFFF0000))


def _unpack_bf16_pairs(w):
    lo = lax.bitcast_convert_type(w << 16, F32)
    hi = lax.bitcast_convert_type(w & jnp.uint32(0xFFFF0000), F32)
    return lo, hi


def _sigmoid(x):
    return 1.0 / (1.0 + jnp.exp(-x))


def _silu(x):
    return x * _sigmoid(x)


def _rms(xv, g):
    ms = jnp.mean(xv * xv, axis=-1, keepdims=True)
    return xv * lax.rsqrt(ms + RMS_EPS) * g


def _cparams(sem):
    return pltpu.CompilerParams(dimension_semantics=sem, vmem_limit_bytes=VMEM_LIMIT)


def _ada_kernel(c_ref, w_ref, b_ref, o_ref):
    o_ref[...] = _mm3(_silu(c_ref[...]), w_ref[...]) + b_ref[...]


def _ada(cc, ada_w, ada_b):
    n = ada_w.shape[1]
    tn = 1024
    return pl.pallas_call(
        _ada_kernel,
        grid=(n // tn,),
        in_specs=[pl.BlockSpec((8, D_MODEL), lambda j: (0, 0)),
                  pl.BlockSpec((D_MODEL, tn), lambda j: (0, j)),
                  pl.BlockSpec((1, tn), lambda j: (0, j))],
        out_specs=pl.BlockSpec((8, tn), lambda j: (0, j)),
        out_shape=jax.ShapeDtypeStruct((8, n), F32),
        compiler_params=_cparams(("arbitrary",)),
        name="ada",
    )(cc, ada_w, ada_b)


def _proj_kernel(x_ref, xp_ref, xn_ref, mod_ref, n1g_ref, win_ref, convw_ref, mu_ref,
                 dw0_ref, dup_ref, ia0_ref, iup_ref, xi_ref, kal_ref, rho_ref, gup_ref, bsum_ref,
                 r_ref, v_ref, g_ref, bonus_ref, yconv_ref, lw_ref, kh_ref, kt_ref, a_ref):
    i = pl.program_id(0)
    nt = pl.num_programs(0)
    tm = x_ref.shape[0]
    is_ctx = i == 0
    sh = jnp.where(is_ctx, mod_ref[1:2, 0:D_MODEL], mod_ref[0:1, 0:D_MODEL])
    sc = jnp.where(is_ctx, mod_ref[1:2, D_MODEL:2 * D_MODEL], mod_ref[0:1, D_MODEL:2 * D_MODEL])
    n1g = n1g_ref[...]

    def norm_mod(xv):
        return _rms(xv, n1g) * (1.0 + sc) + sh

    h = norm_mod(x_ref[...]).astype(BF16)
    p = _dg(h, win_ref[...])
    hh = norm_mod(jnp.concatenate([xp_ref[...], xn_ref[...]], axis=0)).astype(BF16)
    ph = _dg(hh, win_ref[:, 3 * D_CONV:])

    rows = lax.broadcasted_iota(jnp.int32, (tm, 1), 0)

    bg = p[:, 0:D_CONV]
    z = p[:, D_CONV:2 * D_CONV] * p[:, 2 * D_CONV:3 * D_CONV]
    col = rows % GRID_W
    zp = jnp.where(col == 0, 0.0, pltpu.roll(z, 1, 0))
    zn = jnp.where(col == GRID_W - 1, 0.0, pltpu.roll(z, tm - 1, 0))
    yconv = bg * (convw_ref[0:1, :] * zp + convw_ref[1:2, :] * z + convw_ref[2:3, :] * zn)
    yconv_ref[...] = yconv.astype(BF16)

    prev_ok = jnp.logical_and(i != 0, i != 1).astype(F32)
    next_ok = jnp.logical_and(i != 0, i != nt - 1).astype(F32)
    cur = p[:, 3 * D_CONV:]
    prev = jnp.where(rows == 0, ph[7:8, :] * prev_ok, pltpu.roll(cur, 1, 0))
    nxt = jnp.where(rows == tm - 1, ph[8:9, :] * next_ok, pltpu.roll(cur, tm - 1, 0))
    ps = cur + mu_ref[0:1, :] * (prev - cur) + mu_ref[1:2, :] * (nxt - cur)

    r = ps[:, 0:D_RWKV]
    k = ps[:, D_RWKV:2 * D_RWKV]
    v = ps[:, 2 * D_RWKV:3 * D_RWKV]
    o = 3 * D_RWKV
    wlo = ps[:, o:o + LORA]
    alo = ps[:, o + LORA:o + 2 * LORA]
    glo = ps[:, o + 2 * LORA:o + 3 * LORA]

    dd = dw0_ref[...] + _mm3(jnp.tanh(wlo), dup_ref[...])
    lw = -DECAY_SCALE * _sigmoid(dd)
    a = _sigmoid(ia0_ref[...] + _mm3(alo, iup_ref[...]))
    g = _mm3(_sigmoid(glo), gup_ref[...])
    k2 = jnp.concatenate([k, k], axis=1)
    kap = k2 * xi_ref[...]
    kap2 = kap * kap
    bs = bsum_ref[...]
    ss = jnp.concatenate([_mm_ones_rhs(kap2[:, 0:D_RWKV], bs),
                          _mm_ones_rhs(kap2[:, D_RWKV:], bs)], axis=1)
    kh = kap * lax.rsqrt(ss + NORM_EPS)
    kt = k2 * (1.0 + (a - 1.0) * kal_ref[...])
    bon = _mm_ones_rhs(r * rho_ref[...] * (kt[:, 0:D_RWKV] + kt[:, D_RWKV:]), bs)

    g_ref[...] = g
    bonus_ref[...] = bon * v
    for pr in range(N_PAIRS):
        ls = slice(pr * LANES, (pr + 1) * LANES)
        r_ref[pr] = r[:, ls]
        v_ref[pr] = v[:, ls]
        for d in range(2):
            ld = slice(d * D_RWKV + pr * LANES, d * D_RWKV + (pr + 1) * LANES)
            lw_ref[d, pr] = lw[:, ld]
            kh_ref[d, pr] = kh[:, ld]
            kt_ref[d, pr] = kt[:, ld]
            a_ref[d, pr] = a[:, ld]


def _proj(xcat, mod, n1g, win, convw, mu, dw0, dup, ia0, iup, xi, kal, rho, gup, bsum):
    t = xcat.shape[0]
    tm = TOK_TILE
    nt = t // tm
    nb8 = t // 8
    full = lambda arr: pl.BlockSpec(arr.shape, lambda i: (0,) * arr.ndim)
    pair_spec = pl.BlockSpec((N_PAIRS, tm, LANES), lambda i: (0, i, 0))
    dpair_spec = pl.BlockSpec((2, N_PAIRS, tm, LANES), lambda i: (0, 0, i, 0))
    row_spec = pl.BlockSpec((tm, D_RWKV), lambda i: (i, 0))
    pair_shape = jax.ShapeDtypeStruct((N_PAIRS, t, LANES), F32)
    dpair_shape = jax.ShapeDtypeStruct((2, N_PAIRS, t, LANES), F32)
    consts = (mod, n1g, win, convw, mu, dw0, dup, ia0, iup, xi, kal, rho, gup, bsum)
    return pl.pallas_call(
        _proj_kernel,
        grid=(nt,),
        in_specs=[pl.BlockSpec((tm, D_MODEL), lambda i: (i, 0)),
                  pl.BlockSpec((8, D_MODEL), lambda i: (jnp.maximum(i * (tm // 8) - 1, 0), 0)),
                  pl.BlockSpec((8, D_MODEL), lambda i: (jnp.minimum((i + 1) * (tm // 8), nb8 - 1), 0))]
                 + [full(c) for c in consts],
        out_specs=[pair_spec, pair_spec, row_spec, row_spec, row_spec,
                   dpair_spec, dpair_spec, dpair_spec, dpair_spec],
        out_shape=[pair_shape, pair_shape,
                   jax.ShapeDtypeStruct((t, D_RWKV), F32), jax.ShapeDtypeStruct((t, D_RWKV), F32),
                   jax.ShapeDtypeStruct((t, D_CONV), BF16),
                   dpair_shape, dpair_shape, dpair_shape, dpair_shape],
        compiler_params=_cparams(("arbitrary",)),
        name="proj",
    )(xcat, xcat, xcat, *consts)


def _chunk_kernel(r_ref, v_ref, lw_ref, kh_ref, kt_ref, a_ref, rh_ref, yh_ref, mt_ref, dt_ref):
    d = pl.program_id(0)
    rev = d == 1
    c = CHUNK
    npair = r_ref.shape[0]
    nsub = r_ref.shape[1] // c

    row = lax.broadcasted_iota(jnp.int32, (c, c), 0)
    col = lax.broadcasted_iota(jnp.int32, (c, c), 1)
    strict = (col - row) * (1 - 2 * d) < 0
    eye = row == col
    incl = jnp.logical_or(strict, eye)
    incl_bf = jnp.where(incl, 1.0, 0.0).astype(BF16)
    eye_f = jnp.where(eye, 1.0, 0.0)
    head0 = lax.broadcasted_iota(jnp.int32, (c, LANES), 1) < HEAD_DIM
    row2 = lax.broadcasted_iota(jnp.int32, (LANES, LANES), 0)
    col2 = lax.broadcasted_iota(jnp.int32, (LANES, LANES), 1)
    same_head = (row2 < HEAD_DIM) == (col2 < HEAD_DIM)
    eye2 = row2 == col2

    subs = range(npair * nsub)
    units = [(s, h) for s in subs for h in range(2)]
    sls = [(pp, slice(s * c, (s + 1) * c), slice(None)) for pp in range(npair) for s in range(nsub)]
    sls2 = [(pp, slice(2 * s * c, 2 * (s + 1) * c), slice(None)) for pp in range(npair) for s in range(nsub)]
    v = [v_ref[sl] for sl in sls]
    lw = [lw_ref[sl] for sl in sls]
    cum = [_mm_ones_lhs(incl_bf, lw[s]) for s in subs]
    al, be, kk, rr, bew, kkw, wc = [], [], [], [], [], [], []
    for s in subs:
        cum_last = jnp.where(rev, cum[s][0:1, :], cum[s][c - 1:c, :])
        e_end = jnp.exp(cum_last - cum[s])
        e_neg = jnp.exp(-cum[s])
        kh = kh_ref[sls[s]]
        kt = kt_ref[sls[s]]
        kha = kh * a_ref[sls[s]]
        al.append(kh * jnp.exp(cum[s] - lw[s]))
        be.append(-(kha * e_neg))
        kk.append(kt * e_neg)
        rr.append(r_ref[sls[s]] * jnp.exp(cum[s]))
        bew.append(-(kha * e_end))
        kkw.append(kt * e_end)
        wc.append(jnp.exp(cum_last))

    lhs = {}
    for s, h in units:
        hm = head0 if h == 0 else jnp.logical_not(head0)
        lhs[s, h] = jnp.concatenate([jnp.where(hm, al[s], 0.0), jnp.where(hm, rr[s], 0.0)], axis=0)
    gb = {u: _mm1(lhs[u], be[u[0]], NT) for u in units}
    gk = {u: _mm1(lhs[u], kk[u[0]], NT) for u in units}
    a_ab = {u: jnp.where(strict, gb[u][0:c], 0.0) for u in units}
    a_ak = {u: jnp.where(strict, gk[u][0:c], 0.0) for u in units}
    a_rb = {u: jnp.where(incl, gb[u][c:2 * c], 0.0) for u in units}
    a_rk = {u: jnp.where(incl, gk[u][c:2 * c], 0.0) for u in units}

    pw = dict(a_ab)
    tinv = {u: eye_f + a_ab[u] for u in units}
    avk = {u: _mm1(a_ak[u], v[u[0]]) for u in units}
    for _ in range(c.bit_length() - 2):
        pw = {u: _mm1(pw[u], pw[u]) for u in units}
        tinv = {u: tinv[u] + _mm1(tinv[u], pw[u]) for u in units}
    tu = {u: _mm1(tinv[u], jnp.concatenate([al[u[0]], avk[u]], axis=1)) for u in units}
    at = [jnp.where(head0, tu[s, 0][:, 0:LANES], tu[s, 1][:, 0:LANES]) for s in subs]
    ut = [jnp.where(head0, tu[s, 0][:, LANES:], tu[s, 1][:, LANES:]) for s in subs]

    rb_at = {u: _mm1(a_rb[u], at[u[0]]) for u in units}
    rb_ut = {u: _mm1(a_rb[u], ut[u[0]]) for u in units}
    rk_v = {u: _mm1(a_rk[u], v[u[0]]) for u in units}
    m_off = [_mm1(bew[s], at[s], TN) for s in subs]
    d_u = [_mm1(bew[s], ut[s], TN) for s in subs]
    d_v = [_mm1(kkw[s], v[s], TN) for s in subs]
    for s in subs:
        rh_ref[sls[s]] = rr[s] + jnp.where(head0, rb_at[s, 0], rb_at[s, 1])
        yh_ref[sls[s]] = jnp.where(head0, rb_ut[s, 0] + rk_v[s, 0], rb_ut[s, 1] + rk_v[s, 1])
        mt_ref[sls2[s]] = jnp.where(eye2, wc[s], 0.0) + jnp.where(same_head, m_off[s], 0.0)
        dt_ref[sls2[s]] = jnp.where(same_head, d_u[s] + d_v[s], 0.0)


def _chunks(r4, v4, lw, kh, kt, a):
    t = r4.shape[1]
    tb = SCAN_BLOCK
    nb = t // tb
    pp = CHUNK_PAIRS
    shared = pl.BlockSpec((pp, tb, LANES), lambda d, p, j: (p, j, 0))
    perdir = pl.BlockSpec((None, pp, tb, LANES), lambda d, p, j: (d, p, j, 0))
    perdir2 = pl.BlockSpec((None, pp, 2 * tb, LANES), lambda d, p, j: (d, p, j, 0))
    shp = jax.ShapeDtypeStruct((2, N_PAIRS, t, LANES), F32)
    shp2 = jax.ShapeDtypeStruct((2, N_PAIRS, 2 * t, LANES), F32)
    return pl.pallas_call(
        _chunk_kernel,
        grid=(2, N_PAIRS // pp, nb),
        in_specs=[shared, shared, perdir, perdir, perdir, perdir],
        out_specs=[perdir, perdir, perdir2, perdir2],
        out_shape=[shp, shp, shp2, shp2],
        compiler_params=_cparams(("arbitrary", "arbitrary", "arbitrary")),
        name="chunks",
    )(r4, v4, lw, kh, kt, a)


def _carry_kernel(rhf_ref, yhf_ref, mtf_ref, dtf_ref, rhb_ref, yhb_ref, mtb_ref, dtb_ref,
                  yf_ref, yb_ref, q_ref):
    j = pl.program_id(0)
    c = CHUNK
    nsub = rhf_ref.shape[1] // c

    @pl.when(j == 0)
    def _():
        q_ref[...] = jnp.zeros(q_ref.shape, F32)

    dirs = ((rhf_ref, yhf_ref, mtf_ref, dtf_ref, yf_ref), (rhb_ref, yhb_ref, mtb_ref, dtb_ref, yb_ref))
    chains = [(dirn, p) for dirn in range(2) for p in range(N_PAIRS)]
    q = {ch: q_ref[ch[0], ch[1]] for ch in chains}
    for s in range(nsub):
        ys, qn = {}, {}
        for dirn, p in chains:
            rh_ref, yh_ref, mt_ref, dt_ref, _ = dirs[dirn]
            cs = s if dirn == 0 else nsub - 1 - s
            ys[dirn, p] = _mm3(rh_ref[p, cs * c:(cs + 1) * c, :], q[dirn, p])
            qn[dirn, p] = _mm3(mt_ref[p, 2 * cs * c:2 * (cs + 1) * c, :], q[dirn, p])
        for dirn, p in chains:
            _, yh_ref, _, dt_ref, y_ref = dirs[dirn]
            cs = s if dirn == 0 else nsub - 1 - s
            y_ref[p, cs * c:(cs + 1) * c, :] = ys[dirn, p] + yh_ref[p, cs * c:(cs + 1) * c, :]
            q[dirn, p] = qn[dirn, p] + dt_ref[p, 2 * cs * c:2 * (cs + 1) * c, :]
    for dirn, p in chains:
        q_ref[dirn, p] = q[dirn, p]


def _carry(rh, yh, mt, dt):
    t = rh.shape[2]
    tb = SCAN_BLOCK
    nb = t // tb
    bwd = lambda j: jnp.where(j == 0, 0, nb - j)
    f1 = pl.BlockSpec((None, N_PAIRS, tb, LANES), lambda j: (0, 0, j, 0))
    f2 = pl.BlockSpec((None, N_PAIRS, 2 * tb, LANES), lambda j: (0, 0, j, 0))
    b1 = pl.BlockSpec((None, N_PAIRS, tb, LANES), lambda j: (1, 0, bwd(j), 0))
    b2 = pl.BlockSpec((None, N_PAIRS, 2 * tb, LANES), lambda j: (1, 0, bwd(j), 0))
    yshape = jax.ShapeDtypeStruct((N_PAIRS, t, LANES), F32)
    return pl.pallas_call(
        _carry_kernel,
        grid=(nb,),
        in_specs=[f1, f1, f2, f2, b1, b1, b2, b2],
        out_specs=[pl.BlockSpec((N_PAIRS, tb, LANES), lambda j: (0, j, 0)),
                   pl.BlockSpec((N_PAIRS, tb, LANES), lambda j: (0, bwd(j), 0))],
        out_shape=[yshape, yshape],
        scratch_shapes=[pltpu.VMEM((2, N_PAIRS, LANES, LANES), F32)],
        compiler_params=_cparams(("arbitrary",)),
        name="carry",
    )(rh, yh, mt, dt, rh, yh, mt, dt)


def _mix_kernel(x_ref, yf_ref, yb_ref, g_ref, bonus_ref, yconv_ref, mod_ref, gnw_ref, gnb_ref, bsum_ref,
                wout_ref, n2g_ref, rw_ref, rb_ref, sw1_ref, sw3_ref, sw2_ref,
                base_ref, h2_ref, idx_ref, gate_ref, rank_ref, cnt_ref, run_ref):
    tm = x_ref.shape[0]
    g1 = mod_ref[0:1, 2 * D_MODEL:3 * D_MODEL]
    sh2 = mod_ref[0:1, 3 * D_MODEL:4 * D_MODEL]
    sc2 = mod_ref[0:1, 4 * D_MODEL:5 * D_MODEL]
    g2 = mod_ref[0:1, 5 * D_MODEL:6 * D_MODEL]

    @pl.when(pl.program_id(0) == 0)
    def _():
        run_ref[...] = jnp.zeros(run_ref.shape, F32)

    y = jnp.concatenate([yf_ref[pr] + yb_ref[pr] for pr in range(N_PAIRS)], axis=1)
    bs = bsum_ref[...]
    mu = _mm_ones_rhs(y, bs) * (1.0 / HEAD_DIM)
    yc = y - mu
    var = _mm_ones_rhs(yc * yc, bs) * (1.0 / HEAD_DIM)
    yn = yc * lax.rsqrt(var + GN_EPS) * gnw_ref[...] + gnb_ref[...]
    yrw = ((yn + bonus_ref[...]) * g_ref[...]).astype(BF16)
    mix = _dg(yconv_ref[...], wout_ref[0:D_CONV, :]) + _dg(yrw, wout_ref[D_CONV:, :])
    x1 = x_ref[...] + g1 * mix

    h2 = _rms(x1, n2g_ref[...]) * (1.0 + sc2) + sh2
    hb = h2.astype(BF16)
    h2_ref[...] = _pack_bf16_pairs(hb.astype(F32))

    scores = _sigmoid(_mm3(h2, rw_ref[...]))
    work = scores + rb_ref[...]
    lane_e = lax.broadcasted_iota(jnp.int32, (tm, N_EXPERTS), 1).astype(F32)
    lane_o = lax.broadcasted_iota(jnp.int32, (tm, LANES), 1)
    idx_acc = jnp.zeros((tm, LANES), F32)
    gate_acc = jnp.zeros((tm, LANES), F32)
    gsum = jnp.zeros((tm, 1), F32)
    chosen = jnp.zeros((tm, N_EXPERTS), F32)
    sels = []
    for kk in range(TOP_K):
        m = jnp.max(work, axis=-1, keepdims=True)
        sel = jnp.min(jnp.where(work == m, lane_e, float(N_EXPERTS)), axis=-1, keepdims=True)
        hit = lane_e == sel
        sk = jnp.sum(jnp.where(hit, scores, 0.0), axis=-1, keepdims=True)
        idx_acc = jnp.where(lane_o == kk, sel, idx_acc)
        gate_acc = jnp.where(lane_o == kk, sk, gate_acc)
        gsum = gsum + sk
        work = jnp.where(hit, -jnp.inf, work)
        chosen = jnp.where(hit, 1.0, chosen)
        sels.append(sel)
    idx_ref[...] = idx_acc.astype(jnp.int32)
    gate_ref[...] = gate_acc / gsum * ROUTED_SCALE

    trow = lax.broadcasted_iota(jnp.int32, (tm, tm), 0)
    tcol = lax.broadcasted_iota(jnp.int32, (tm, tm), 1)
    earlier = jnp.where(tcol < trow, 1.0, 0.0).astype(BF16)
    before = _dg(earlier, chosen.astype(BF16)) + run_ref[0:1, :]
    rank_acc = jnp.zeros((tm, LANES), F32)
    for kk in range(TOP_K):
        rk = jnp.sum(jnp.where(lane_e == sels[kk], before, 0.0), axis=-1, keepdims=True)
        rank_acc = jnp.where(lane_o == kk, rk, rank_acc)
    rank_ref[...] = rank_acc.astype(jnp.int32)
    run_ref[...] = run_ref[...] + jnp.sum(chosen, axis=0, keepdims=True)
    cnt_ref[...] = run_ref[...]

    act = (_silu(_dg(hb, sw1_ref[...])) * _dg(hb, sw3_ref[...])).astype(BF16)
    base_ref[...] = x1 + g2 * _dg(act, sw2_ref[...])


def _mix(x, yf, yb, g, bonus, yconv, mod, gnw, gnb, bsum, wout, n2g, rw, rb, sw1, sw3, sw2):
    t = x.shape[0]
    tm = TOK_TILE
    off = (yf.shape[1] - t) // tm
    full = lambda arr: pl.BlockSpec(arr.shape, lambda i: (0,) * arr.ndim)
    consts = (mod, gnw, gnb, bsum, wout, n2g, rw, rb, sw1, sw3, sw2)
    cat_spec = lambda w: pl.BlockSpec((tm, w), lambda i: (i + off, 0))
    tok_spec = lambda w: pl.BlockSpec((tm, w), lambda i: (i, 0))
    y_spec = pl.BlockSpec((N_PAIRS, tm, LANES), lambda i: (0, i + off, 0))
    return pl.pallas_call(
        _mix_kernel,
        grid=(t // tm,),
        in_specs=[tok_spec(D_MODEL), y_spec, y_spec,
                  cat_spec(D_RWKV), cat_spec(D_RWKV), cat_spec(D_CONV)]
                 + [full(c) for c in consts],
        out_specs=[tok_spec(D_MODEL), tok_spec(HALF), tok_spec(LANES), tok_spec(LANES), tok_spec(LANES),
                   pl.BlockSpec((8, N_EXPERTS), lambda i: (0, 0))],
        out_shape=[jax.ShapeDtypeStruct((t, D_MODEL), F32), jax.ShapeDtypeStruct((t, HALF), U32),
                   jax.ShapeDtypeStruct((t, LANES), jnp.int32), jax.ShapeDtypeStruct((t, LANES), F32),
                   jax.ShapeDtypeStruct((t, LANES), jnp.int32), jax.ShapeDtypeStruct((8, N_EXPERTS), F32)],
        scratch_shapes=[pltpu.VMEM((8, N_EXPERTS), F32)],
        compiler_params=_cparams(("arbitrary",)),
        name="mix",
    )(x, yf, yb, g, bonus, yconv, *consts)


def _slots_kernel(idx_ref, rank_ref, start_ref, dest_ref):
    tm = idx_ref.shape[0]
    lane_e = lax.broadcasted_iota(jnp.int32, (tm, N_EXPERTS), 1)
    lane_o = lax.broadcasted_iota(jnp.int32, (tm, LANES), 1)
    idx = idx_ref[...]
    start = start_ref[...]
    acc = jnp.zeros((tm, LANES), F32)
    for kk in range(TOP_K):
        st = jnp.sum(jnp.where(lane_e == idx[:, kk:kk + 1], start, 0.0), axis=-1, keepdims=True)
        acc = jnp.where(lane_o == kk, st, acc)
    dest_ref[...] = acc.astype(jnp.int32) + rank_ref[...]


def _slots(idx, rank, pad_start):
    t = idx.shape[0]
    tm = TOK_TILE
    spec = pl.BlockSpec((tm, LANES), lambda i: (i, 0))
    return pl.pallas_call(
        _slots_kernel,
        grid=(t // tm,),
        in_specs=[spec, spec, pl.BlockSpec((1, N_EXPERTS), lambda i: (0, 0))],
        out_specs=spec,
        out_shape=jax.ShapeDtypeStruct((t, LANES), jnp.int32),
        compiler_params=_cparams(("arbitrary",)),
        name="slots",
    )(idx, rank, pad_start)


def _dispatch_kernel(dest_hbm, h_ref, hs_in, hs_out, idx_smem, idx_sem, row_sem):
    del hs_in
    b = pl.program_id(0)
    nb = pl.num_programs(0)
    slot = b % 2
    tm = h_ref.shape[0]

    def idx_copy(blk, s):
        return pltpu.make_async_copy(dest_hbm.at[blk], idx_smem.at[s], idx_sem.at[s])

    def row_copy(t, dst):
        return pltpu.make_async_copy(h_ref.at[pl.ds(t, 1)], hs_out.at[pl.ds(dst, 1)], row_sem.at[0])

    @pl.when(b == 0)
    def _():
        idx_copy(0, 0).start()

    idx_copy(b, slot).wait()

    @pl.when(b + 1 < nb)
    def _():
        idx_copy(b + 1, 1 - slot).start()

    def issue(t, carry):
        for kk in range(TOP_K):
            row_copy(t, idx_smem[slot, t * TOP_K + kk]).start()
        return carry
    lax.fori_loop(0, tm, issue, 0)

    def drain(t, carry):
        for kk in range(TOP_K):
            row_copy(t, 0).wait()
        return carry
    lax.fori_loop(0, tm, drain, 0)


def _dispatch(dest, h2, n_slots):
    t = h2.shape[0]
    tm = COMB_TILE
    nt = t // tm
    hs0 = jnp.zeros((n_slots, HALF), U32)
    return pl.pallas_call(
        _dispatch_kernel,
        grid=(nt,),
        in_specs=[pl.BlockSpec(memory_space=pl.ANY),
                  pl.BlockSpec((tm, HALF), lambda i: (i, 0)),
                  pl.BlockSpec(memory_space=pl.ANY)],
        out_specs=pl.BlockSpec(memory_space=pl.ANY),
        out_shape=jax.ShapeDtypeStruct((n_slots, HALF), U32),
        scratch_shapes=[pltpu.SMEM((2, tm * TOP_K), jnp.int32),
                        pltpu.SemaphoreType.DMA((2,)),
                        pltpu.SemaphoreType.DMA((1,))],
        input_output_aliases={2: 0},
        compiler_params=pltpu.CompilerParams(dimension_semantics=("arbitrary",), vmem_limit_bytes=VMEM_LIMIT,
                                             disable_bounds_checks=True),
        name="dispatch",
    )(dest.reshape(nt, tm * TOP_K), h2, hs0)


def _experts_kernel(be_ref, nu_ref, hs_ref, w1_ref, w3_ref, w2_ref, o_ref, w1c, w3c, w2c):
    b = pl.program_id(0)
    new_expert = jnp.logical_or(b == 0, be_ref[b] != be_ref[jnp.maximum(b - 1, 0)])

    @pl.when(new_expert)
    def _():
        w1c[...] = w1_ref[...].astype(BF16)
        w3c[...] = w3_ref[...].astype(BF16)
        w2c[...] = w2_ref[...].astype(BF16)

    @pl.when(b < nu_ref[0])
    def _():
        lo, hi = _unpack_bf16_pairs(hs_ref[...])
        lo = lo.astype(BF16)
        hi = hi.astype(BF16)
        a1 = _dg(lo, w1c[0:HALF, :]) + _dg(hi, w1c[HALF:, :])
        a3 = _dg(lo, w3c[0:HALF, :]) + _dg(hi, w3c[HALF:, :])
        act = (_silu(a1) * a3).astype(BF16)
        y = _dg(act, w2c[...]).astype(BF16).astype(F32)
        o_ref[...] = _pack_bf16_pairs(y)

    @pl.when(b >= nu_ref[0])
    def _():
        o_ref[...] = jnp.zeros(o_ref.shape, U32)


def _experts(block_e, n_used, hs, w1, w3, w2):
    nblk = block_e.shape[0]
    sb = SLOT_BLOCK
    row_map = lambda b, be, nu: (jnp.minimum(b, nu[0] - 1), 0)
    w_map = lambda b, be, nu: (be[b], 0, 0)
    grid_spec = pltpu.PrefetchScalarGridSpec(
        num_scalar_prefetch=2,
        grid=(nblk,),
        in_specs=[pl.BlockSpec((sb, HALF), row_map),
                  pl.BlockSpec((None, D_MODEL, D_EXPERT), w_map),
                  pl.BlockSpec((None, D_MODEL, D_EXPERT), w_map),
                  pl.BlockSpec((None, D_EXPERT, D_MODEL), w_map)],
        out_specs=pl.BlockSpec((sb, HALF), lambda b, be, nu: (b, 0)),
        scratch_shapes=[pltpu.VMEM((D_MODEL, D_EXPERT), BF16), pltpu.VMEM((D_MODEL, D_EXPERT), BF16),
                        pltpu.VMEM((D_EXPERT, D_MODEL), BF16)],
    )
    return pl.pallas_call(
        _experts_kernel,
        grid_spec=grid_spec,
        out_shape=jax.ShapeDtypeStruct((nblk * sb, HALF), U32),
        compiler_params=_cparams(("arbitrary",)),
        name="experts",
    )(block_e, n_used, hs, w1, w3, w2)


def _sc_gather_rows(table, idx):
    info = plsc.get_sparse_core_info()
    nc, ns = info.num_cores, info.num_subcores
    nw = nc * ns
    n = idx.shape[0]
    d = table.shape[1]
    ch = SC_CHUNK
    per_w = n // nw
    n_ch = per_w // ch
    assert n % nw == 0 and per_w % (2 * ch) == 0
    mesh = plsc.VectorSubcoreMesh(core_axis_name="c", subcore_axis_name="s")

    @functools.partial(
        pl.kernel, mesh=mesh,
        out_type=jax.ShapeDtypeStruct((n, d), table.dtype),
        scratch_types=[pltpu.VMEM((n_ch, ch), jnp.int32),
                       pltpu.VMEM((2, ch, d), table.dtype),
                       pltpu.SemaphoreType.DMA((2,))],
    )
    def gather_kernel(table_hbm, idx_hbm, out_hbm, idx_v, rows_v, sem):
        wid = lax.axis_index("s") * nc + lax.axis_index("c")
        base = wid * per_w
        pltpu.sync_copy(idx_hbm.at[wid], idx_v)

        def gather(j, buf):
            return pltpu.make_async_copy(table_hbm.at[idx_v.at[j]], rows_v.at[buf], sem.at[buf])

        gather(0, 0).start()

        @pl.loop(0, n_ch, step=2)
        def _(j):
            for buf in range(2):
                jj = j + buf
                gather(jj, buf).wait()

                @pl.when(jj + 1 < n_ch)
                def _():
                    gather(jj + 1, 1 - buf).start()

                pltpu.sync_copy(rows_v.at[buf], out_hbm.at[pl.ds(base + jj * ch, ch)])

    return gather_kernel(table, idx.reshape(nw, n_ch, ch))


def _combine_kernel(rows_ref, base_ref, gate_ref, mod_ref, fg_ref, o_ref):
    tm = base_ref.shape[0]
    gate = gate_ref[...]
    acc_lo = jnp.zeros((tm, HALF), F32)
    acc_hi = jnp.zeros((tm, HALF), F32)
    for kk in range(TOP_K):
        lo, hi = _unpack_bf16_pairs(rows_ref[kk])
        acc_lo = acc_lo + gate[:, kk:kk + 1] * lo
        acc_hi = acc_hi + gate[:, kk:kk + 1] * hi
    acc = jnp.concatenate([acc_lo, acc_hi], axis=1)
    g2 = mod_ref[0:1, 5 * D_MODEL:6 * D_MODEL]
    o_ref[...] = _rms(base_ref[...] + g2 * acc, fg_ref[...])


def _combine(dest, ye, base, gate, mod, fg):
    t = base.shape[0]
    tm = TOK_TILE
    rows = _sc_gather_rows(lax.bitcast_convert_type(ye, jnp.int32), dest.T.reshape(-1))
    rows = lax.bitcast_convert_type(rows, U32).reshape(TOP_K, t, HALF)
    full = lambda arr: pl.BlockSpec(arr.shape, lambda i: (0,) * arr.ndim)
    return pl.pallas_call(
        _combine_kernel,
        grid=(t // tm,),
        in_specs=[pl.BlockSpec((TOP_K, tm, HALF), lambda i: (0, i, 0)),
                  pl.BlockSpec((tm, D_MODEL), lambda i: (i, 0)),
                  pl.BlockSpec((tm, LANES), lambda i: (i, 0)),
                  full(mod), full(fg)],
        out_specs=pl.BlockSpec((tm, D_MODEL), lambda i: (i, 0)),
        out_shape=jax.ShapeDtypeStruct((t, D_MODEL), F32),
        compiler_params=_cparams(("arbitrary",)),
        name="combine",
    )(rows, base, gate, mod, fg)


def _block_tables(counts, n_blocks):
    sb = SLOT_BLOCK
    padded = (counts + sb - 1) // sb * sb
    pad_end = jnp.cumsum(padded)
    pad_start = pad_end - padded
    n_used = pad_end[-1] // sb
    blk = jnp.minimum(jnp.arange(n_blocks, dtype=jnp.int32), n_used - 1)
    block_e = jnp.minimum(jnp.sum(pad_end[None, :] <= (blk * sb)[:, None], axis=1), N_EXPERTS - 1).astype(jnp.int32)
    return block_e, n_used.reshape(1).astype(jnp.int32), pad_start


def _blockdiag2(w):
    z = jnp.zeros_like(w[0])
    return jnp.concatenate([jnp.concatenate([w[0], z], axis=1), jnp.concatenate([z, w[1]], axis=1)], axis=0)


def _pad_rows(w, n=8):
    return jnp.concatenate([w, jnp.zeros((n - w.shape[0],) + w.shape[1:], w.dtype)], axis=0)


def kernel(x, c, ctx, c_ctx, ada_w, ada_b, norm1_g, norm2_g, w_in, conv_w, shift_mu, decay_w0, decay_up, iclr_a0, iclr_up, key_xi, key_alpha, bonus_rho, gate_up, gn_w, gn_b, w_out, router_w, router_bias, exp_w1, exp_w3, exp_w2, sh_w1, sh_w3, sh_w2, final_g):
    assert x.shape[0] == 1 and ada_w.shape[0] == 1
    assert ctx.shape[1] == TOK_TILE and x.shape[1] % TOK_TILE == 0
    l = 0
    xs = x[0]
    row = lambda w: w.reshape(1, -1)

    cc = _pad_rows(jnp.stack([c[0], c_ctx], axis=0))
    mod = _ada(cc, ada_w[l], row(ada_b[l]))

    hid = lax.broadcasted_iota(jnp.int32, (D_RWKV, D_RWKV), 0) // HEAD_DIM
    bsum = (hid == hid.T).astype(BF16)

    xcat = jnp.concatenate([ctx[0], xs], axis=0)
    r4, v4, g, bonus, yconv, lw, kh, kt, a = _proj(
        xcat, mod, row(norm1_g[l]), w_in[l].astype(BF16), _pad_rows(conv_w[l]), _pad_rows(shift_mu[l]),
        row(decay_w0[l]), _blockdiag2(decay_up[l]), row(iclr_a0[l]), _blockdiag2(iclr_up[l]),
        row(key_xi[l]), row(key_alpha[l]), row(bonus_rho[l]), gate_up[l], bsum)

    yf, yb = _carry(*_chunks(r4, v4, lw, kh, kt, a))

    base, h2, idx, gate, rank, cnt = _mix(
        xs, yf, yb, g, bonus, yconv, mod, row(gn_w[l]), row(gn_b[l]), bsum, w_out[l].astype(BF16),
        row(norm2_g[l]), router_w[l], row(router_bias[l]),
        sh_w1[l].astype(BF16), sh_w3[l].astype(BF16), sh_w2[l].astype(BF16))

    n_blocks = xs.shape[0] * TOP_K // SLOT_BLOCK + N_EXPERTS
    block_e, n_used, pad_start = _block_tables(cnt[0].astype(jnp.int32), n_blocks)
    dest = _slots(idx, rank, pad_start.astype(F32).reshape(1, N_EXPERTS))[:, :TOP_K]
    hs = _dispatch(dest, h2, n_blocks * SLOT_BLOCK)
    ye = _experts(block_e, n_used, hs, exp_w1[l], exp_w3[l], exp_w2[l])
    out = _combine(dest, ye, base, gate, mod, row(final_g))
    return out[None]
```

```python
import functools

import jax
import jax.numpy as jnp
from jax import lax
from jax.experimental import pallas as pl
from jax.experimental.pallas import tpu as pltpu
from jax.experimental.pallas import tpu_sc as plsc

F32 = jnp.float32
BF16 = jnp.bfloat16

D_MODEL = 1024
D_CONV = 512
D_RWKV = 512
HEAD_DIM = 64
N_HEADS = D_RWKV // HEAD_DIM
N_PAIRS = N_HEADS // 2
LORA = 128
P_RWKV = 3 * D_RWKV + 3 * LORA
P_IN = 3 * D_CONV + P_RWKV
GRID_W = 64
N_EXPERTS = 256
TOP_K = 8
D_EXPERT = 256
ROUTED_SCALE = 2.5
RMS_EPS = 1e-6
GN_EPS = 64e-5
DECAY_SCALE = 0.6065306597126334
NORM_EPS = 1e-12

TOK_TILE = 256
SCAN_BLOCK = 256
CHUNK = 64
CHUNK_PAIRS = 2
SLOT_BLOCK = 256
SC_CHUNK = 64
SC_SCATTER_CHUNK = 128
LANES = 128
VMEM_LIMIT = 56 * 1024 * 1024

NN = ((1,), (0,))
NT = ((1,), (1,))
TN = ((0,), (0,))


def _dg(a, b, dims=NN):
    return lax.dot_general(a, b, (dims, ((), ())), preferred_element_type=F32)


def _split2(a):
    hi = a.astype(BF16)
    lo = (a - hi.astype(F32)).astype(BF16)
    return hi, lo


def _split3(a):
    hi = a.astype(BF16)
    r1 = a - hi.astype(F32)
    mid = r1.astype(BF16)
    lo = (r1 - mid.astype(F32)).astype(BF16)
    return hi, mid, lo


def _mm1(a, b, dims=NN):
    return _dg(a.astype(BF16), b.astype(BF16), dims)


def _mm3(a, b, dims=NN):
    ah, al = _split2(a)
    bh, bl = _split2(b)
    return _dg(ah, bh, dims) + (_dg(ah, bl, dims) + _dg(al, bh, dims))


def _mm_ones_rhs(a, ones_bf16, dims=NN):
    h, m, l = _split3(a)
    return _dg(h, ones_bf16, dims) + (_dg(m, ones_bf16, dims) + _dg(l, ones_bf16, dims))


def _mm_ones_lhs(ones_bf16, b, dims=NN):
    h, m, l = _split3(b)
    return _dg(ones_bf16, h, dims) + (_dg(ones_bf16, m, dims) + _dg(ones_bf16, l, dims))


HALF = D_MODEL // 2
I32 = jnp.int32
HI_MASK = -65536


def _pack_bf16_pairs(x):
    u = lax.bitcast_convert_type(x, I32)
    return lax.shift_right_logical(u[:, 0:HALF], jnp.int32(16)) | (u[:, HALF:] & jnp.int32(HI_MASK))


def _unpack_bf16_pairs(w):
    lo = lax.bitcast_convert_type(lax.shift_left(w, jnp.int32(16)), F32)
    hi = lax.bitcast_convert_type(w & jnp.int32(HI_MASK), F32)
    return lo, hi


def _sigmoid(x):
    return 1.0 / (1.0 + jnp.exp(-x))


def _silu(x):
    return x * _sigmoid(x)


def _rms(xv, g):
    ms = jnp.mean(xv * xv, axis=-1, keepdims=True)
    return xv * lax.rsqrt(ms + RMS_EPS) * g


def _cparams(sem):
    return pltpu.CompilerParams(dimension_semantics=sem, vmem_limit_bytes=VMEM_LIMIT)


def _ada_kernel(c_ref, w_ref, b_ref, o_ref):
    o_ref[...] = _mm3(_silu(c_ref[...]), w_ref[...]) + b_ref[...]


def _ada(cc, ada_w, ada_b):
    n = ada_w.shape[1]
    tn = 1024
    return pl.pallas_call(
        _ada_kernel,
        grid=(n // tn,),
        in_specs=[pl.BlockSpec((8, D_MODEL), lambda j: (0, 0)),
                  pl.BlockSpec((D_MODEL, tn), lambda j: (0, j)),
                  pl.BlockSpec((1, tn), lambda j: (0, j))],
        out_specs=pl.BlockSpec((8, tn), lambda j: (0, j)),
        out_shape=jax.ShapeDtypeStruct((8, n), F32),
        compiler_params=_cparams(("arbitrary",)),
        name="ada",
    )(cc, ada_w, ada_b)


def _proj_kernel(x_ref, xp_ref, xn_ref, mod_ref, n1g_ref, win_ref, convw_ref, mu_ref,
                 dw0_ref, dup_ref, ia0_ref, iup_ref, xi_ref, kal_ref, rho_ref, gup_ref, bsum_ref,
                 r_ref, v_ref, g_ref, bonus_ref, yconv_ref, lw_ref, kh_ref, kt_ref, a_ref):
    i = pl.program_id(0)
    nt = pl.num_programs(0)
    tm = x_ref.shape[0]
    is_ctx = i == 0
    sh = jnp.where(is_ctx, mod_ref[1:2, 0:D_MODEL], mod_ref[0:1, 0:D_MODEL])
    sc = jnp.where(is_ctx, mod_ref[1:2, D_MODEL:2 * D_MODEL], mod_ref[0:1, D_MODEL:2 * D_MODEL])
    n1g = n1g_ref[...]

    def norm_mod(xv):
        return _rms(xv, n1g) * (1.0 + sc) + sh

    h = norm_mod(x_ref[...]).astype(BF16)
    p = _dg(h, win_ref[...])
    hh = norm_mod(jnp.concatenate([xp_ref[...], xn_ref[...]], axis=0)).astype(BF16)
    ph = _dg(hh, win_ref[:, 3 * D_CONV:])

    rows = lax.broadcasted_iota(jnp.int32, (tm, 1), 0)

    bg = p[:, 0:D_CONV]
    z = p[:, D_CONV:2 * D_CONV] * p[:, 2 * D_CONV:3 * D_CONV]
    col = rows % GRID_W
    zp = jnp.where(col == 0, 0.0, pltpu.roll(z, 1, 0))
    zn = jnp.where(col == GRID_W - 1, 0.0, pltpu.roll(z, tm - 1, 0))
    yconv = bg * (convw_ref[0:1, :] * zp + convw_ref[1:2, :] * z + convw_ref[2:3, :] * zn)
    yconv_ref[...] = yconv.astype(BF16)

    prev_ok = jnp.logical_and(i != 0, i != 1).astype(F32)
    next_ok = jnp.logical_and(i != 0, i != nt - 1).astype(F32)
    cur = p[:, 3 * D_CONV:]
    prev = jnp.where(rows == 0, ph[7:8, :] * prev_ok, pltpu.roll(cur, 1, 0))
    nxt = jnp.where(rows == tm - 1, ph[8:9, :] * next_ok, pltpu.roll(cur, tm - 1, 0))
    ps = cur + mu_ref[0:1, :] * (prev - cur) + mu_ref[1:2, :] * (nxt - cur)

    r = ps[:, 0:D_RWKV]
    k = ps[:, D_RWKV:2 * D_RWKV]
    v = ps[:, 2 * D_RWKV:3 * D_RWKV]
    o = 3 * D_RWKV
    wlo = ps[:, o:o + LORA]
    alo = ps[:, o + LORA:o + 2 * LORA]
    glo = ps[:, o + 2 * LORA:o + 3 * LORA]

    dd = dw0_ref[...] + _mm3(jnp.tanh(wlo), dup_ref[...])
    lw = -DECAY_SCALE * _sigmoid(dd)
    a = _sigmoid(ia0_ref[...] + _mm3(alo, iup_ref[...]))
    g = _mm3(_sigmoid(glo), gup_ref[...])
    k2 = jnp.concatenate([k, k], axis=1)
    kap = k2 * xi_ref[...]
    kap2 = kap * kap
    bs = bsum_ref[...]
    ss = jnp.concatenate([_mm_ones_rhs(kap2[:, 0:D_RWKV], bs),
                          _mm_ones_rhs(kap2[:, D_RWKV:], bs)], axis=1)
    kh = kap * lax.rsqrt(ss + NORM_EPS)
    kt = k2 * (1.0 + (a - 1.0) * kal_ref[...])
    bon = _mm_ones_rhs(r * rho_ref[...] * (kt[:, 0:D_RWKV] + kt[:, D_RWKV:]), bs)

    g_ref[...] = g
    bonus_ref[...] = bon * v
    for pr in range(N_PAIRS):
        ls = slice(pr * LANES, (pr + 1) * LANES)
        r_ref[pr] = r[:, ls]
        v_ref[pr] = v[:, ls]
        for d in range(2):
            ld = slice(d * D_RWKV + pr * LANES, d * D_RWKV + (pr + 1) * LANES)
            lw_ref[d, pr] = lw[:, ld]
            kh_ref[d, pr] = kh[:, ld]
            kt_ref[d, pr] = kt[:, ld]
            a_ref[d, pr] = a[:, ld]


def _proj(xcat, mod, n1g, win, convw, mu, dw0, dup, ia0, iup, xi, kal, rho, gup, bsum):
    t = xcat.shape[0]
    tm = TOK_TILE
    nt = t // tm
    nb8 = t // 8
    full = lambda arr: pl.BlockSpec(arr.shape, lambda i: (0,) * arr.ndim)
    pair_spec = pl.BlockSpec((N_PAIRS, tm, LANES), lambda i: (0, i, 0))
    dpair_spec = pl.BlockSpec((2, N_PAIRS, tm, LANES), lambda i: (0, 0, i, 0))
    row_spec = pl.BlockSpec((tm, D_RWKV), lambda i: (i, 0))
    pair_shape = jax.ShapeDtypeStruct((N_PAIRS, t, LANES), F32)
    dpair_shape = jax.ShapeDtypeStruct((2, N_PAIRS, t, LANES), F32)
    consts = (mod, n1g, win, convw, mu, dw0, dup, ia0, iup, xi, kal, rho, gup, bsum)
    return pl.pallas_call(
        _proj_kernel,
        grid=(nt,),
        in_specs=[pl.BlockSpec((tm, D_MODEL), lambda i: (i, 0)),
                  pl.BlockSpec((8, D_MODEL), lambda i: (jnp.maximum(i * (tm // 8) - 1, 0), 0)),
                  pl.BlockSpec((8, D_MODEL), lambda i: (jnp.minimum((i + 1) * (tm // 8), nb8 - 1), 0))]
                 + [full(c) for c in consts],
        out_specs=[pair_spec, pair_spec, row_spec, row_spec, row_spec,
                   dpair_spec, dpair_spec, dpair_spec, dpair_spec],
        out_shape=[pair_shape, pair_shape,
                   jax.ShapeDtypeStruct((t, D_RWKV), F32), jax.ShapeDtypeStruct((t, D_RWKV), F32),
                   jax.ShapeDtypeStruct((t, D_CONV), BF16),
                   dpair_shape, dpair_shape, dpair_shape, dpair_shape],
        compiler_params=_cparams(("arbitrary",)),
        name="proj",
    )(xcat, xcat, xcat, *consts)


def _chunk_kernel(r_ref, v_ref, lw_ref, kh_ref, kt_ref, a_ref, rh_ref, yh_ref, mt_ref, dt_ref):
    d = pl.program_id(0)
    rev = d == 1
    c = CHUNK
    npair = r_ref.shape[0]
    nsub = r_ref.shape[1] // c

    row = lax.broadcasted_iota(jnp.int32, (c, c), 0)
    col = lax.broadcasted_iota(jnp.int32, (c, c), 1)
    strict = (col - row) * (1 - 2 * d) < 0
    eye = row == col
    incl = jnp.logical_or(strict, eye)
    incl_bf = jnp.where(incl, 1.0, 0.0).astype(BF16)
    eye_f = jnp.where(eye, 1.0, 0.0)
    head0 = lax.broadcasted_iota(jnp.int32, (c, LANES), 1) < HEAD_DIM
    row2 = lax.broadcasted_iota(jnp.int32, (LANES, LANES), 0)
    col2 = lax.broadcasted_iota(jnp.int32, (LANES, LANES), 1)
    same_head = (row2 < HEAD_DIM) == (col2 < HEAD_DIM)
    eye2 = row2 == col2

    subs = range(npair * nsub)
    units = [(s, h) for s in subs for h in range(2)]
    sls = [(pp, slice(s * c, (s + 1) * c), slice(None)) for pp in range(npair) for s in range(nsub)]
    sls2 = [(pp, slice(2 * s * c, 2 * (s + 1) * c), slice(None)) for pp in range(npair) for s in range(nsub)]
    v = [v_ref[sl] for sl in sls]
    lw = [lw_ref[sl] for sl in sls]
    cum = [_mm_ones_lhs(incl_bf, lw[s]) for s in subs]
    al, be, kk, rr, bew, kkw, wc = [], [], [], [], [], [], []
    for s in subs:
        cum_last = jnp.where(rev, cum[s][0:1, :], cum[s][c - 1:c, :])
        e_end = jnp.exp(cum_last - cum[s])
        e_neg = jnp.exp(-cum[s])
        kh = kh_ref[sls[s]]
        kt = kt_ref[sls[s]]
        kha = kh * a_ref[sls[s]]
        al.append(kh * jnp.exp(cum[s] - lw[s]))
        be.append(-(kha * e_neg))
        kk.append(kt * e_neg)
        rr.append(r_ref[sls[s]] * jnp.exp(cum[s]))
        bew.append(-(kha * e_end))
        kkw.append(kt * e_end)
        wc.append(jnp.exp(cum_last))

    lhs = {}
    for s, h in units:
        hm = head0 if h == 0 else jnp.logical_not(head0)
        lhs[s, h] = jnp.concatenate([jnp.where(hm, al[s], 0.0), jnp.where(hm, rr[s], 0.0)], axis=0)
    gb = {u: _mm1(lhs[u], be[u[0]], NT) for u in units}
    gk = {u: _mm1(lhs[u], kk[u[0]], NT) for u in units}
    a_ab = {u: jnp.where(strict, gb[u][0:c], 0.0) for u in units}
    a_ak = {u: jnp.where(strict, gk[u][0:c], 0.0) for u in units}
    a_rb = {u: jnp.where(incl, gb[u][c:2 * c], 0.0) for u in units}
    a_rk = {u: jnp.where(incl, gk[u][c:2 * c], 0.0) for u in units}

    pw = dict(a_ab)
    tinv = {u: eye_f + a_ab[u] for u in units}
    avk = {u: _mm1(a_ak[u], v[u[0]]) for u in units}
    for _ in range(c.bit_length() - 2):
        pw = {u: _mm1(pw[u], pw[u]) for u in units}
        tinv = {u: tinv[u] + _mm1(tinv[u], pw[u]) for u in units}
    tu = {u: _mm1(tinv[u], jnp.concatenate([al[u[0]], avk[u]], axis=1)) for u in units}
    at = [jnp.where(head0, tu[s, 0][:, 0:LANES], tu[s, 1][:, 0:LANES]) for s in subs]
    ut = [jnp.where(head0, tu[s, 0][:, LANES:], tu[s, 1][:, LANES:]) for s in subs]

    rb_at = {u: _mm1(a_rb[u], at[u[0]]) for u in units}
    rb_ut = {u: _mm1(a_rb[u], ut[u[0]]) for u in units}
    rk_v = {u: _mm1(a_rk[u], v[u[0]]) for u in units}
    m_off = [_mm1(bew[s], at[s], TN) for s in subs]
    d_u = [_mm1(bew[s], ut[s], TN) for s in subs]
    d_v = [_mm1(kkw[s], v[s], TN) for s in subs]
    for s in subs:
        rh_ref[sls[s]] = rr[s] + jnp.where(head0, rb_at[s, 0], rb_at[s, 1])
        yh_ref[sls[s]] = jnp.where(head0, rb_ut[s, 0] + rk_v[s, 0], rb_ut[s, 1] + rk_v[s, 1])
        mt_ref[sls2[s]] = jnp.where(eye2, wc[s], 0.0) + jnp.where(same_head, m_off[s], 0.0)
        dt_ref[sls2[s]] = jnp.where(same_head, d_u[s] + d_v[s], 0.0)


def _chunks(r4, v4, lw, kh, kt, a):
    t = r4.shape[1]
    tb = SCAN_BLOCK
    nb = t // tb
    pp = CHUNK_PAIRS
    shared = pl.BlockSpec((pp, tb, LANES), lambda d, p, j: (p, j, 0))
    perdir = pl.BlockSpec((None, pp, tb, LANES), lambda d, p, j: (d, p, j, 0))
    perdir2 = pl.BlockSpec((None, pp, 2 * tb, LANES), lambda d, p, j: (d, p, j, 0))
    shp = jax.ShapeDtypeStruct((2, N_PAIRS, t, LANES), F32)
    shp2 = jax.ShapeDtypeStruct((2, N_PAIRS, 2 * t, LANES), F32)
    return pl.pallas_call(
        _chunk_kernel,
        grid=(2, N_PAIRS // pp, nb),
        in_specs=[shared, shared, perdir, perdir, perdir, perdir],
        out_specs=[perdir, perdir, perdir2, perdir2],
        out_shape=[shp, shp, shp2, shp2],
        compiler_params=_cparams(("arbitrary", "arbitrary", "arbitrary")),
        name="chunks",
    )(r4, v4, lw, kh, kt, a)


def _carry_kernel(rhf_ref, yhf_ref, mtf_ref, dtf_ref, rhb_ref, yhb_ref, mtb_ref, dtb_ref,
                  yf_ref, yb_ref, q_ref):
    j = pl.program_id(0)
    c = CHUNK
    nsub = rhf_ref.shape[1] // c

    @pl.when(j == 0)
    def _():
        q_ref[...] = jnp.zeros(q_ref.shape, F32)

    dirs = ((rhf_ref, yhf_ref, mtf_ref, dtf_ref, yf_ref), (rhb_ref, yhb_ref, mtb_ref, dtb_ref, yb_ref))
    chains = [(dirn, p) for dirn in range(2) for p in range(N_PAIRS)]
    q = {ch: q_ref[ch[0], ch[1]] for ch in chains}
    for s in range(nsub):
        ys, qn = {}, {}
        for dirn, p in chains:
            rh_ref, yh_ref, mt_ref, dt_ref, _ = dirs[dirn]
            cs = s if dirn == 0 else nsub - 1 - s
            ys[dirn, p] = _mm3(rh_ref[p, cs * c:(cs + 1) * c, :], q[dirn, p])
            qn[dirn, p] = _mm3(mt_ref[p, 2 * cs * c:2 * (cs + 1) * c, :], q[dirn, p])
        for dirn, p in chains:
            _, yh_ref, _, dt_ref, y_ref = dirs[dirn]
            cs = s if dirn == 0 else nsub - 1 - s
            y_ref[p, cs * c:(cs + 1) * c, :] = ys[dirn, p] + yh_ref[p, cs * c:(cs + 1) * c, :]
            q[dirn, p] = qn[dirn, p] + dt_ref[p, 2 * cs * c:2 * (cs + 1) * c, :]
    for dirn, p in chains:
        q_ref[dirn, p] = q[dirn, p]


def _carry(rh, yh, mt, dt):
    t = rh.shape[2]
    tb = SCAN_BLOCK
    nb = t // tb
    bwd = lambda j: jnp.where(j == 0, 0, nb - j)
    f1 = pl.BlockSpec((None, N_PAIRS, tb, LANES), lambda j: (0, 0, j, 0))
    f2 = pl.BlockSpec((None, N_PAIRS, 2 * tb, LANES), lambda j: (0, 0, j, 0))
    b1 = pl.BlockSpec((None, N_PAIRS, tb, LANES), lambda j: (1, 0, bwd(j), 0))
    b2 = pl.BlockSpec((None, N_PAIRS, 2 * tb, LANES), lambda j: (1, 0, bwd(j), 0))
    yshape = jax.ShapeDtypeStruct((N_PAIRS, t, LANES), F32)
    return pl.pallas_call(
        _carry_kernel,
        grid=(nb,),
        in_specs=[f1, f1, f2, f2, b1, b1, b2, b2],
        out_specs=[pl.BlockSpec((N_PAIRS, tb, LANES), lambda j: (0, j, 0)),
                   pl.BlockSpec((N_PAIRS, tb, LANES), lambda j: (0, bwd(j), 0))],
        out_shape=[yshape, yshape],
        scratch_shapes=[pltpu.VMEM((2, N_PAIRS, LANES, LANES), F32)],
        compiler_params=_cparams(("arbitrary",)),
        name="carry",
    )(rh, yh, mt, dt, rh, yh, mt, dt)


def _mix_kernel(x_ref, yf_ref, yb_ref, g_ref, bonus_ref, yconv_ref, mod_ref, gnw_ref, gnb_ref, bsum_ref,
                wout_ref, n2g_ref, rw_ref, rb_ref, sw1_ref, sw3_ref, sw2_ref,
                base_ref, h2_ref, idx_ref, gate_ref, rank_ref, cnt_ref, run_ref):
    tm = x_ref.shape[0]
    g1 = mod_ref[0:1, 2 * D_MODEL:3 * D_MODEL]
    sh2 = mod_ref[0:1, 3 * D_MODEL:4 * D_MODEL]
    sc2 = mod_ref[0:1, 4 * D_MODEL:5 * D_MODEL]
    g2 = mod_ref[0:1, 5 * D_MODEL:6 * D_MODEL]

    @pl.when(pl.program_id(0) == 0)
    def _():
        run_ref[...] = jnp.zeros(run_ref.shape, F32)

    y = jnp.concatenate([yf_ref[pr] + yb_ref[pr] for pr in range(N_PAIRS)], axis=1)
    bs = bsum_ref[...]
    mu = _mm_ones_rhs(y, bs) * (1.0 / HEAD_DIM)
    yc = y - mu
    var = _mm_ones_rhs(yc * yc, bs) * (1.0 / HEAD_DIM)
    yn = yc * lax.rsqrt(var + GN_EPS) * gnw_ref[...] + gnb_ref[...]
    yrw = ((yn + bonus_ref[...]) * g_ref[...]).astype(BF16)
    mix = _dg(yconv_ref[...], wout_ref[0:D_CONV, :]) + _dg(yrw, wout_ref[D_CONV:, :])
    x1 = x_ref[...] + g1 * mix

    h2 = _rms(x1, n2g_ref[...]) * (1.0 + sc2) + sh2
    hb = h2.astype(BF16)
    h2_ref[...] = _pack_bf16_pairs(hb.astype(F32))

    scores = _sigmoid(_mm3(h2, rw_ref[...]))
    work = scores + rb_ref[...]
    lane_e = lax.broadcasted_iota(jnp.int32, (tm, N_EXPERTS), 1).astype(F32)
    lane_o = lax.broadcasted_iota(jnp.int32, (tm, LANES), 1)
    idx_acc = jnp.zeros((tm, LANES), F32)
    gate_acc = jnp.zeros((tm, LANES), F32)
    gsum = jnp.zeros((tm, 1), F32)
    chosen = jnp.zeros((tm, N_EXPERTS), F32)
    sels = []
    for kk in range(TOP_K):
        m = jnp.max(work, axis=-1, keepdims=True)
        sel = jnp.min(jnp.where(work == m, lane_e, float(N_EXPERTS)), axis=-1, keepdims=True)
        hit = lane_e == sel
        sk = jnp.sum(jnp.where(hit, scores, 0.0), axis=-1, keepdims=True)
        idx_acc = jnp.where(lane_o == kk, sel, idx_acc)
        gate_acc = jnp.where(lane_o == kk, sk, gate_acc)
        gsum = gsum + sk
        work = jnp.where(hit, -jnp.inf, work)
        chosen = jnp.where(hit, 1.0, chosen)
        sels.append(sel)
    idx_ref[...] = idx_acc.astype(jnp.int32)
    gate_ref[...] = gate_acc / gsum * ROUTED_SCALE

    trow = lax.broadcasted_iota(jnp.int32, (tm, tm), 0)
    tcol = lax.broadcasted_iota(jnp.int32, (tm, tm), 1)
    earlier = jnp.where(tcol < trow, 1.0, 0.0).astype(BF16)
    before = _dg(earlier, chosen.astype(BF16)) + run_ref[0:1, :]
    rank_acc = jnp.zeros((tm, LANES), F32)
    for kk in range(TOP_K):
        rk = jnp.sum(jnp.where(lane_e == sels[kk], before, 0.0), axis=-1, keepdims=True)
        rank_acc = jnp.where(lane_o == kk, rk, rank_acc)
    rank_ref[...] = rank_acc.astype(jnp.int32)
    run_ref[...] = run_ref[...] + jnp.sum(chosen, axis=0, keepdims=True)
    cnt_ref[...] = run_ref[...]

    act = (_silu(_dg(hb, sw1_ref[...])) * _dg(hb, sw3_ref[...])).astype(BF16)
    base_ref[...] = x1 + g2 * _dg(act, sw2_ref[...])


def _mix(x, yf, yb, g, bonus, yconv, mod, gnw, gnb, bsum, wout, n2g, rw, rb, sw1, sw3, sw2):
    t = x.shape[0]
    tm = TOK_TILE
    off = (yf.shape[1] - t) // tm
    full = lambda arr: pl.BlockSpec(arr.shape, lambda i: (0,) * arr.ndim)
    consts = (mod, gnw, gnb, bsum, wout, n2g, rw, rb, sw1, sw3, sw2)
    cat_spec = lambda w: pl.BlockSpec((tm, w), lambda i: (i + off, 0))
    tok_spec = lambda w: pl.BlockSpec((tm, w), lambda i: (i, 0))
    y_spec = pl.BlockSpec((N_PAIRS, tm, LANES), lambda i: (0, i + off, 0))
    return pl.pallas_call(
        _mix_kernel,
        grid=(t // tm,),
        in_specs=[tok_spec(D_MODEL), y_spec, y_spec,
                  cat_spec(D_RWKV), cat_spec(D_RWKV), cat_spec(D_CONV)]
                 + [full(c) for c in consts],
        out_specs=[tok_spec(D_MODEL), tok_spec(HALF), tok_spec(LANES), tok_spec(LANES), tok_spec(LANES),
                   pl.BlockSpec((8, N_EXPERTS), lambda i: (0, 0))],
        out_shape=[jax.ShapeDtypeStruct((t, D_MODEL), F32), jax.ShapeDtypeStruct((t, HALF), I32),
                   jax.ShapeDtypeStruct((t, LANES), jnp.int32), jax.ShapeDtypeStruct((t, LANES), F32),
                   jax.ShapeDtypeStruct((t, LANES), jnp.int32), jax.ShapeDtypeStruct((8, N_EXPERTS), F32)],
        scratch_shapes=[pltpu.VMEM((8, N_EXPERTS), F32)],
        compiler_params=_cparams(("arbitrary",)),
        name="mix",
    )(x, yf, yb, g, bonus, yconv, *consts)


def _slots_kernel(idx_ref, rank_ref, start_ref, dest_ref):
    tm = idx_ref.shape[0]
    lane_e = lax.broadcasted_iota(jnp.int32, (tm, N_EXPERTS), 1)
    lane_o = lax.broadcasted_iota(jnp.int32, (tm, LANES), 1)
    idx = idx_ref[...]
    start = start_ref[...]
    acc = jnp.zeros((tm, LANES), F32)
    for kk in range(TOP_K):
        st = jnp.sum(jnp.where(lane_e == idx[:, kk:kk + 1], start, 0.0), axis=-1, keepdims=True)
        acc = jnp.where(lane_o == kk, st, acc)
    dest_ref[...] = acc.astype(jnp.int32) + rank_ref[...]


def _slots(idx, rank, pad_start):
    t = idx.shape[0]
    tm = TOK_TILE
    spec = pl.BlockSpec((tm, LANES), lambda i: (i, 0))
    return pl.pallas_call(
        _slots_kernel,
        grid=(t // tm,),
        in_specs=[spec, spec, pl.BlockSpec((1, N_EXPERTS), lambda i: (0, 0))],
        out_specs=spec,
        out_shape=jax.ShapeDtypeStruct((t, LANES), jnp.int32),
        compiler_params=_cparams(("arbitrary",)),
        name="slots",
    )(idx, rank, pad_start)


def _sc_scatter_rows(rows, dest_km, n_slots):
    info = plsc.get_sparse_core_info()
    nc, ns = info.num_cores, info.num_subcores
    nw = nc * ns
    t, d = rows.shape
    nk = dest_km.shape[0]
    ch = SC_SCATTER_CHUNK
    per_w = t // nw
    n_ch = per_w // ch
    assert t % nw == 0 and per_w % ch == 0
    mesh = plsc.VectorSubcoreMesh(core_axis_name="c", subcore_axis_name="s")
    idx = dest_km.reshape(nk, nw, n_ch, ch).transpose(1, 2, 0, 3).reshape(nw, n_ch * nk, ch)

    @functools.partial(
        pl.kernel, mesh=mesh,
        out_type=jax.ShapeDtypeStruct((n_slots, d), rows.dtype),
        scratch_types=[pltpu.VMEM((n_ch * nk, ch), jnp.int32),
                       pltpu.VMEM((ch, d), rows.dtype),
                       pltpu.SemaphoreType.DMA],
    )
    def scatter_kernel(rows_hbm, idx_hbm, out_hbm, idx_v, rows_v, sem):
        wid = lax.axis_index("s") * nc + lax.axis_index("c")
        base = wid * per_w
        pltpu.sync_copy(idx_hbm.at[wid], idx_v)

        @pl.loop(0, n_ch)
        def _(j):
            pltpu.sync_copy(rows_hbm.at[pl.ds(base + j * ch, ch)], rows_v)
            copies = [pltpu.make_async_copy(rows_v, out_hbm.at[idx_v.at[j * nk + kk]], sem) for kk in range(nk)]
            for cp in copies:
                cp.start()
            for cp in copies:
                cp.wait()

    return scatter_kernel(rows, idx)


def _experts_kernel(be_ref, nu_ref, nv_ref, hs_ref, w1_ref, w3_ref, w2_ref, o_ref, w1c, w3c, w2c):
    b = pl.program_id(0)
    new_expert = jnp.logical_or(b == 0, be_ref[b] != be_ref[jnp.maximum(b - 1, 0)])

    @pl.when(new_expert)
    def _():
        w1c[...] = w1_ref[...].astype(BF16)
        w3c[...] = w3_ref[...].astype(BF16)
        w2c[...] = w2_ref[...].astype(BF16)

    @pl.when(b < nu_ref[0])
    def _():
        live = lax.broadcasted_iota(jnp.int32, (hs_ref.shape[0], 1), 0) < nv_ref[b]
        lo, hi = _unpack_bf16_pairs(jnp.where(live, hs_ref[...], 0))
        lo = lo.astype(BF16)
        hi = hi.astype(BF16)
        a1 = _dg(lo, w1c[0:HALF, :]) + _dg(hi, w1c[HALF:, :])
        a3 = _dg(lo, w3c[0:HALF, :]) + _dg(hi, w3c[HALF:, :])
        act = (_silu(a1) * a3).astype(BF16)
        y = _dg(act, w2c[...]).astype(BF16).astype(F32)
        o_ref[...] = _pack_bf16_pairs(y)

    @pl.when(b >= nu_ref[0])
    def _():
        o_ref[...] = jnp.zeros(o_ref.shape, I32)


def _experts(block_e, n_used, n_valid, hs, w1, w3, w2):
    nblk = block_e.shape[0]
    sb = SLOT_BLOCK
    row_map = lambda b, be, nu, nv: (jnp.minimum(b, nu[0] - 1), 0)
    w_map = lambda b, be, nu, nv: (be[b], 0, 0)
    grid_spec = pltpu.PrefetchScalarGridSpec(
        num_scalar_prefetch=3,
        grid=(nblk,),
        in_specs=[pl.BlockSpec((sb, HALF), row_map),
                  pl.BlockSpec((None, D_MODEL, D_EXPERT), w_map),
                  pl.BlockSpec((None, D_MODEL, D_EXPERT), w_map),
                  pl.BlockSpec((None, D_EXPERT, D_MODEL), w_map)],
        out_specs=pl.BlockSpec((sb, HALF), lambda b, be, nu, nv: (b, 0)),
        scratch_shapes=[pltpu.VMEM((D_MODEL, D_EXPERT), BF16), pltpu.VMEM((D_MODEL, D_EXPERT), BF16),
                        pltpu.VMEM((D_EXPERT, D_MODEL), BF16)],
    )
    return pl.pallas_call(
        _experts_kernel,
        grid_spec=grid_spec,
        out_shape=jax.ShapeDtypeStruct((nblk * sb, HALF), I32),
        compiler_params=_cparams(("arbitrary",)),
        name="experts",
    )(block_e, n_used, n_valid, hs, w1, w3, w2)


def _sc_gather_rows(table, idx):
    info = plsc.get_sparse_core_info()
    nc, ns = info.num_cores, info.num_subcores
    nw = nc * ns
    n = idx.shape[0]
    d = table.shape[1]
    ch = SC_CHUNK
    per_w = n // nw
    n_ch = per_w // ch
    assert n % nw == 0 and per_w % (2 * ch) == 0
    mesh = plsc.VectorSubcoreMesh(core_axis_name="c", subcore_axis_name="s")

    @functools.partial(
        pl.kernel, mesh=mesh,
        out_type=jax.ShapeDtypeStruct((n, d), table.dtype),
        scratch_types=[pltpu.VMEM((n_ch, ch), jnp.int32),
                       pltpu.VMEM((2, ch, d), table.dtype),
                       pltpu.SemaphoreType.DMA((2,))],
    )
    def gather_kernel(table_hbm, idx_hbm, out_hbm, idx_v, rows_v, sem):
        wid = lax.axis_index("s") * nc + lax.axis_index("c")
        base = wid * per_w
        pltpu.sync_copy(idx_hbm.at[wid], idx_v)

        def gather(j, buf):
            return pltpu.make_async_copy(table_hbm.at[idx_v.at[j]], rows_v.at[buf], sem.at[buf])

        gather(0, 0).start()

        @pl.loop(0, n_ch, step=2)
        def _(j):
            for buf in range(2):
                jj = j + buf
                gather(jj, buf).wait()

                @pl.when(jj + 1 < n_ch)
                def _():
                    gather(jj + 1, 1 - buf).start()

                pltpu.sync_copy(rows_v.at[buf], out_hbm.at[pl.ds(base + jj * ch, ch)])

    return gather_kernel(table, idx.reshape(nw, n_ch, ch))


def _combine_kernel(rows_ref, base_ref, gate_ref, mod_ref, fg_ref, o_ref):
    tm = base_ref.shape[0]
    gate = gate_ref[...]
    acc_lo = jnp.zeros((tm, HALF), F32)
    acc_hi = jnp.zeros((tm, HALF), F32)
    for kk in range(TOP_K):
        lo, hi = _unpack_bf16_pairs(rows_ref[kk])
        acc_lo = acc_lo + gate[:, kk:kk + 1] * lo
        acc_hi = acc_hi + gate[:, kk:kk + 1] * hi
    acc = jnp.concatenate([acc_lo, acc_hi], axis=1)
    g2 = mod_ref[0:1, 5 * D_MODEL:6 * D_MODEL]
    o_ref[...] = _rms(base_ref[...] + g2 * acc, fg_ref[...])


def _combine(dest_km, ye, base, gate, mod, fg):
    t = base.shape[0]
    tm = TOK_TILE
    rows = _sc_gather_rows(ye, dest_km.reshape(-1)).reshape(TOP_K, t, HALF)
    full = lambda arr: pl.BlockSpec(arr.shape, lambda i: (0,) * arr.ndim)
    return pl.pallas_call(
        _combine_kernel,
        grid=(t // tm,),
        in_specs=[pl.BlockSpec((TOP_K, tm, HALF), lambda i: (0, i, 0)),
                  pl.BlockSpec((tm, D_MODEL), lambda i: (i, 0)),
                  pl.BlockSpec((tm, LANES), lambda i: (i, 0)),
                  full(mod), full(fg)],
        out_specs=pl.BlockSpec((tm, D_MODEL), lambda i: (i, 0)),
        out_shape=jax.ShapeDtypeStruct((t, D_MODEL), F32),
        compiler_params=_cparams(("arbitrary",)),
        name="combine",
    )(rows, base, gate, mod, fg)


def _block_tables(counts, n_blocks):
    sb = SLOT_BLOCK
    padded = (counts + sb - 1) // sb * sb
    pad_end = jnp.cumsum(padded)
    pad_start = pad_end - padded
    n_used = pad_end[-1] // sb
    blk = jnp.minimum(jnp.arange(n_blocks, dtype=jnp.int32), n_used - 1)
    block_e = jnp.minimum(jnp.sum(pad_end[None, :] <= (blk * sb)[:, None], axis=1), N_EXPERTS - 1).astype(jnp.int32)
    onehot = block_e[:, None] == jnp.arange(N_EXPERTS, dtype=jnp.int32)[None, :]
    live_end = jnp.sum(jnp.where(onehot, (pad_start + counts)[None, :], 0), axis=1)
    n_valid = jnp.clip(live_end - blk * sb, 0, sb).astype(jnp.int32)
    return block_e, n_used.reshape(1).astype(jnp.int32), n_valid, pad_start


def _blockdiag2(w):
    z = jnp.zeros_like(w[0])
    return jnp.concatenate([jnp.concatenate([w[0], z], axis=1), jnp.concatenate([z, w[1]], axis=1)], axis=0)


def _pad_rows(w, n=8):
    return jnp.concatenate([w, jnp.zeros((n - w.shape[0],) + w.shape[1:], w.dtype)], axis=0)


def kernel(x, c, ctx, c_ctx, ada_w, ada_b, norm1_g, norm2_g, w_in, conv_w, shift_mu, decay_w0, decay_up, iclr_a0, iclr_up, key_xi, key_alpha, bonus_rho, gate_up, gn_w, gn_b, w_out, router_w, router_bias, exp_w1, exp_w3, exp_w2, sh_w1, sh_w3, sh_w2, final_g):
    assert x.shape[0] == 1 and ada_w.shape[0] == 1
    assert ctx.shape[1] == TOK_TILE and x.shape[1] % TOK_TILE == 0
    l = 0
    xs = x[0]
    row = lambda w: w.reshape(1, -1)

    cc = _pad_rows(jnp.stack([c[0], c_ctx], axis=0))
    mod = _ada(cc, ada_w[l], row(ada_b[l]))

    hid = lax.broadcasted_iota(jnp.int32, (D_RWKV, D_RWKV), 0) // HEAD_DIM
    bsum = (hid == hid.T).astype(BF16)

    xcat = jnp.concatenate([ctx[0], xs], axis=0)
    r4, v4, g, bonus, yconv, lw, kh, kt, a = _proj(
        xcat, mod, row(norm1_g[l]), w_in[l].astype(BF16), _pad_rows(conv_w[l]), _pad_rows(shift_mu[l]),
        row(decay_w0[l]), _blockdiag2(decay_up[l]), row(iclr_a0[l]), _blockdiag2(iclr_up[l]),
        row(key_xi[l]), row(key_alpha[l]), row(bonus_rho[l]), gate_up[l], bsum)

    yf, yb = _carry(*_chunks(r4, v4, lw, kh, kt, a))

    base, h2, idx, gate, rank, cnt = _mix(
        xs, yf, yb, g, bonus, yconv, mod, row(gn_w[l]), row(gn_b[l]), bsum, w_out[l].astype(BF16),
        row(norm2_g[l]), router_w[l], row(router_bias[l]),
        sh_w1[l].astype(BF16), sh_w3[l].astype(BF16), sh_w2[l].astype(BF16))

    n_blocks = xs.shape[0] * TOP_K // SLOT_BLOCK + N_EXPERTS
    block_e, n_used, n_valid, pad_start = _block_tables(cnt[0].astype(jnp.int32), n_blocks)
    dest_km = _slots(idx, rank, pad_start.astype(F32).reshape(1, N_EXPERTS))[:, :TOP_K].T
    hs = _sc_scatter_rows(h2, dest_km, n_blocks * SLOT_BLOCK)
    ye = _experts(block_e, n_used, n_valid, hs, exp_w1[l], exp_w3[l], exp_w2[l])
    out = _combine(dest_km, ye, base, gate, mod, row(final_g))
    return out[None]
```

```python
import functools

import jax
import jax.numpy as jnp
from jax import lax
from jax.experimental import pallas as pl
from jax.experimental.pallas import tpu as pltpu
from jax.experimental.pallas import tpu_sc as plsc

F32 = jnp.float32
BF16 = jnp.bfloat16

D_MODEL = 1024
D_CONV = 512
D_RWKV = 512
HEAD_DIM = 64
N_HEADS = D_RWKV // HEAD_DIM
N_PAIRS = N_HEADS // 2
LORA = 128
P_RWKV = 3 * D_RWKV + 3 * LORA
P_IN = 3 * D_CONV + P_RWKV
GRID_W = 64
N_EXPERTS = 256
TOP_K = 8
D_EXPERT = 256
ROUTED_SCALE = 2.5
RMS_EPS = 1e-6
GN_EPS = 64e-5
DECAY_SCALE = 0.6065306597126334
NORM_EPS = 1e-12

TOK_TILE = 256
SCAN_BLOCK = 256
CHUNK = 64
CHUNK_PAIRS = 4
SLOT_BLOCK = 512
SC_CHUNK = 64
SC_SCATTER_CHUNK = 128
LANES = 128
VMEM_LIMIT = 56 * 1024 * 1024

NN = ((1,), (0,))
NT = ((1,), (1,))
TN = ((0,), (0,))


def _dg(a, b, dims=NN):
    return lax.dot_general(a, b, (dims, ((), ())), preferred_element_type=F32)


def _split2(a):
    hi = a.astype(BF16)
    lo = (a - hi.astype(F32)).astype(BF16)
    return hi, lo


def _split3(a):
    hi = a.astype(BF16)
    r1 = a - hi.astype(F32)
    mid = r1.astype(BF16)
    lo = (r1 - mid.astype(F32)).astype(BF16)
    return hi, mid, lo


def _mm1(a, b, dims=NN):
    return _dg(a.astype(BF16), b.astype(BF16), dims)


def _mm3(a, b, dims=NN):
    ah, al = _split2(a)
    bh, bl = _split2(b)
    return _dg(ah, bh, dims) + (_dg(ah, bl, dims) + _dg(al, bh, dims))


def _mm_ones_rhs(a, ones_bf16, dims=NN):
    h, l = _split2(a)
    return _dg(h, ones_bf16, dims) + _dg(l, ones_bf16, dims)


def _mm_ones_lhs(ones_bf16, b, dims=NN):
    h, m, l = _split3(b)
    return _dg(ones_bf16, h, dims) + (_dg(ones_bf16, m, dims) + _dg(ones_bf16, l, dims))


HALF = D_MODEL // 2
I32 = jnp.int32
HI_MASK = -65536


def _pack_bf16_pairs(x):
    u = lax.bitcast_convert_type(x, I32)
    return lax.shift_right_logical(u[:, 0:HALF], jnp.int32(16)) | (u[:, HALF:] & jnp.int32(HI_MASK))


def _unpack_bf16_pairs(w):
    lo = lax.bitcast_convert_type(lax.shift_left(w, jnp.int32(16)), F32)
    hi = lax.bitcast_convert_type(w & jnp.int32(HI_MASK), F32)
    return lo, hi


def _sigmoid(x):
    return 1.0 / (1.0 + jnp.exp(-x))


def _silu(x):
    return x * _sigmoid(x)


def _rms(xv, g):
    ms = jnp.mean(xv * xv, axis=-1, keepdims=True)
    return xv * lax.rsqrt(ms + RMS_EPS) * g


def _cparams(sem):
    return pltpu.CompilerParams(dimension_semantics=sem, vmem_limit_bytes=VMEM_LIMIT)


def _ada_kernel(c_ref, w_ref, b_ref, o_ref):
    o_ref[...] = _mm3(_silu(c_ref[...]), w_ref[...]) + b_ref[...]


def _ada(cc, ada_w, ada_b):
    n = ada_w.shape[1]
    tn = 1024
    return pl.pallas_call(
        _ada_kernel,
        grid=(n // tn,),
        in_specs=[pl.BlockSpec((8, D_MODEL), lambda j: (0, 0)),
                  pl.BlockSpec((D_MODEL, tn), lambda j: (0, j)),
                  pl.BlockSpec((1, tn), lambda j: (0, j))],
        out_specs=pl.BlockSpec((8, tn), lambda j: (0, j)),
        out_shape=jax.ShapeDtypeStruct((8, n), F32),
        compiler_params=_cparams(("arbitrary",)),
        name="ada",
    )(cc, ada_w, ada_b)


def _proj_kernel(x_ref, xp_ref, xn_ref, mod_ref, n1g_ref, win_ref, convw_ref, mu_ref,
                 dw0_ref, dup_ref, ia0_ref, iup_ref, xi_ref, kal_ref, rho_ref, gup_ref, bsum_ref,
                 r_ref, v_ref, g_ref, bonus_ref, yconv_ref, lw_ref, kh_ref, kt_ref, a_ref):
    i = pl.program_id(0)
    nt = pl.num_programs(0)
    tm = x_ref.shape[0]
    is_ctx = i == 0
    sh = jnp.where(is_ctx, mod_ref[1:2, 0:D_MODEL], mod_ref[0:1, 0:D_MODEL])
    sc = jnp.where(is_ctx, mod_ref[1:2, D_MODEL:2 * D_MODEL], mod_ref[0:1, D_MODEL:2 * D_MODEL])
    n1g = n1g_ref[...]

    def norm_mod(xv):
        return _rms(xv, n1g) * (1.0 + sc) + sh

    h = norm_mod(x_ref[...]).astype(BF16)
    p = _dg(h, win_ref[...])
    hh = norm_mod(jnp.concatenate([xp_ref[...], xn_ref[...]], axis=0)).astype(BF16)
    ph = _dg(hh, win_ref[:, 3 * D_CONV:])

    rows = lax.broadcasted_iota(jnp.int32, (tm, 1), 0)

    bg = p[:, 0:D_CONV]
    z = p[:, D_CONV:2 * D_CONV] * p[:, 2 * D_CONV:3 * D_CONV]
    col = rows % GRID_W
    zp = jnp.where(col == 0, 0.0, pltpu.roll(z, 1, 0))
    zn = jnp.where(col == GRID_W - 1, 0.0, pltpu.roll(z, tm - 1, 0))
    yconv = bg * (convw_ref[0:1, :] * zp + convw_ref[1:2, :] * z + convw_ref[2:3, :] * zn)
    yconv_ref[...] = yconv.astype(BF16)

    prev_ok = jnp.logical_and(i != 0, i != 1).astype(F32)
    next_ok = jnp.logical_and(i != 0, i != nt - 1).astype(F32)
    cur = p[:, 3 * D_CONV:]
    prev = jnp.where(rows == 0, ph[7:8, :] * prev_ok, pltpu.roll(cur, 1, 0))
    nxt = jnp.where(rows == tm - 1, ph[8:9, :] * next_ok, pltpu.roll(cur, tm - 1, 0))
    ps = cur + mu_ref[0:1, :] * (prev - cur) + mu_ref[1:2, :] * (nxt - cur)

    r = ps[:, 0:D_RWKV]
    k = ps[:, D_RWKV:2 * D_RWKV]
    v = ps[:, 2 * D_RWKV:3 * D_RWKV]
    o = 3 * D_RWKV
    wlo = ps[:, o:o + LORA]
    alo = ps[:, o + LORA:o + 2 * LORA]
    glo = ps[:, o + 2 * LORA:o + 3 * LORA]

    dd = dw0_ref[...] + _mm3(jnp.tanh(wlo), dup_ref[...])
    lw = -DECAY_SCALE * _sigmoid(dd)
    a = _sigmoid(ia0_ref[...] + _mm3(alo, iup_ref[...]))
    g = _mm3(_sigmoid(glo), gup_ref[...])
    k2 = jnp.concatenate([k, k], axis=1)
    kap = k2 * xi_ref[...]
    kap2 = kap * kap
    bs = bsum_ref[...]
    ss = jnp.concatenate([_mm_ones_rhs(kap2[:, 0:D_RWKV], bs),
                          _mm_ones_rhs(kap2[:, D_RWKV:], bs)], axis=1)
    kh = kap * lax.rsqrt(ss + NORM_EPS)
    kt = k2 * (1.0 + (a - 1.0) * kal_ref[...])
    bon = _mm_ones_rhs(r * rho_ref[...] * (kt[:, 0:D_RWKV] + kt[:, D_RWKV:]), bs)

    g_ref[...] = g
    bonus_ref[...] = bon * v
    for pr in range(N_PAIRS):
        ls = slice(pr * LANES, (pr + 1) * LANES)
        r_ref[pr] = r[:, ls]
        v_ref[pr] = v[:, ls]
        for d in range(2):
            ld = slice(d * D_RWKV + pr * LANES, d * D_RWKV + (pr + 1) * LANES)
            lw_ref[d, pr] = lw[:, ld]
            kh_ref[d, pr] = kh[:, ld]
            kt_ref[d, pr] = kt[:, ld]
            a_ref[d, pr] = a[:, ld]


def _proj(xcat, mod, n1g, win, convw, mu, dw0, dup, ia0, iup, xi, kal, rho, gup, bsum):
    t = xcat.shape[0]
    tm = TOK_TILE
    nt = t // tm
    nb8 = t // 8
    full = lambda arr: pl.BlockSpec(arr.shape, lambda i: (0,) * arr.ndim)
    pair_spec = pl.BlockSpec((N_PAIRS, tm, LANES), lambda i: (0, i, 0))
    dpair_spec = pl.BlockSpec((2, N_PAIRS, tm, LANES), lambda i: (0, 0, i, 0))
    row_spec = pl.BlockSpec((tm, D_RWKV), lambda i: (i, 0))
    pair_shape = jax.ShapeDtypeStruct((N_PAIRS, t, LANES), F32)
    dpair_shape = jax.ShapeDtypeStruct((2, N_PAIRS, t, LANES), F32)
    consts = (mod, n1g, win, convw, mu, dw0, dup, ia0, iup, xi, kal, rho, gup, bsum)
    return pl.pallas_call(
        _proj_kernel,
        grid=(nt,),
        in_specs=[pl.BlockSpec((tm, D_MODEL), lambda i: (i, 0)),
                  pl.BlockSpec((8, D_MODEL), lambda i: (jnp.maximum(i * (tm // 8) - 1, 0), 0)),
                  pl.BlockSpec((8, D_MODEL), lambda i: (jnp.minimum((i + 1) * (tm // 8), nb8 - 1), 0))]
                 + [full(c) for c in consts],
        out_specs=[pair_spec, pair_spec, row_spec, row_spec, row_spec,
                   dpair_spec, dpair_spec, dpair_spec, dpair_spec],
        out_shape=[pair_shape, pair_shape,
                   jax.ShapeDtypeStruct((t, D_RWKV), F32), jax.ShapeDtypeStruct((t, D_RWKV), F32),
                   jax.ShapeDtypeStruct((t, D_CONV), BF16),
                   dpair_shape, dpair_shape, dpair_shape, dpair_shape],
        compiler_params=_cparams(("arbitrary",)),
        name="proj",
    )(xcat, xcat, xcat, *consts)


def _chunk_kernel(r_ref, v_ref, lw_ref, kh_ref, kt_ref, a_ref, rh_ref, yh_ref, mt_ref, dt_ref):
    d = pl.program_id(0)
    rev = d == 1
    c = CHUNK
    npair = r_ref.shape[0]
    nsub = r_ref.shape[1] // c

    row = lax.broadcasted_iota(jnp.int32, (c, c), 0)
    col = lax.broadcasted_iota(jnp.int32, (c, c), 1)
    strict = (col - row) * (1 - 2 * d) < 0
    eye = row == col
    incl = jnp.logical_or(strict, eye)
    incl_bf = jnp.where(incl, 1.0, 0.0).astype(BF16)
    eye_f = jnp.where(eye, 1.0, 0.0)
    head0 = lax.broadcasted_iota(jnp.int32, (c, LANES), 1) < HEAD_DIM
    row2 = lax.broadcasted_iota(jnp.int32, (LANES, LANES), 0)
    col2 = lax.broadcasted_iota(jnp.int32, (LANES, LANES), 1)
    same_head = (row2 < HEAD_DIM) == (col2 < HEAD_DIM)
    eye2 = row2 == col2

    subs = range(npair * nsub)
    units = [(s, h) for s in subs for h in range(2)]
    sls = [(pp, slice(s * c, (s + 1) * c), slice(None)) for pp in range(npair) for s in range(nsub)]
    sls2 = [(pp, slice(2 * s * c, 2 * (s + 1) * c), slice(None)) for pp in range(npair) for s in range(nsub)]
    v = [v_ref[sl] for sl in sls]
    lw = [lw_ref[sl] for sl in sls]
    cum = [_mm_ones_lhs(incl_bf, lw[s]) for s in subs]
    al, be, kk, rr, bew, kkw, wc = [], [], [], [], [], [], []
    for s in subs:
        cum_last = jnp.where(rev, cum[s][0:1, :], cum[s][c - 1:c, :])
        e_end = jnp.exp(cum_last - cum[s])
        e_neg = jnp.exp(-cum[s])
        kh = kh_ref[sls[s]]
        kt = kt_ref[sls[s]]
        kha = kh * a_ref[sls[s]]
        al.append(kh * jnp.exp(cum[s] - lw[s]))
        be.append(-(kha * e_neg))
        kk.append(kt * e_neg)
        rr.append(r_ref[sls[s]] * jnp.exp(cum[s]))
        bew.append(-(kha * e_end))
        kkw.append(kt * e_end)
        wc.append(jnp.exp(cum_last))

    lhs = {}
    for s, h in units:
        hm = head0 if h == 0 else jnp.logical_not(head0)
        lhs[s, h] = jnp.concatenate([jnp.where(hm, al[s], 0.0), jnp.where(hm, rr[s], 0.0)], axis=0)
    gb = {u: _mm1(lhs[u], be[u[0]], NT) for u in units}
    gk = {u: _mm1(lhs[u], kk[u[0]], NT) for u in units}
    a_ab = {u: jnp.where(strict, gb[u][0:c], 0.0) for u in units}
    a_ak = {u: jnp.where(strict, gk[u][0:c], 0.0) for u in units}
    a_rb = {u: jnp.where(incl, gb[u][c:2 * c], 0.0) for u in units}
    a_rk = {u: jnp.where(incl, gk[u][c:2 * c], 0.0) for u in units}

    pw = dict(a_ab)
    tinv = {u: eye_f + a_ab[u] for u in units}
    akrk_v = {u: _mm1(jnp.concatenate([a_ak[u], a_rk[u]], axis=0), v[u[0]]) for u in units}
    avk = {u: akrk_v[u][0:c] for u in units}
    rk_v = {u: akrk_v[u][c:2 * c] for u in units}
    for _ in range(c.bit_length() - 2):
        pw = {u: _mm1(pw[u], pw[u]) for u in units}
        tinv = {u: tinv[u] + _mm1(tinv[u], pw[u]) for u in units}
    tu = {u: _mm1(tinv[u], jnp.concatenate([al[u[0]], avk[u]], axis=1)) for u in units}
    at = [jnp.where(head0, tu[s, 0][:, 0:LANES], tu[s, 1][:, 0:LANES]) for s in subs]
    ut = [jnp.where(head0, tu[s, 0][:, LANES:], tu[s, 1][:, LANES:]) for s in subs]

    rb = {u: _mm1(a_rb[u], jnp.concatenate([at[u[0]], ut[u[0]]], axis=1)) for u in units}
    rb_at = {u: rb[u][:, 0:LANES] for u in units}
    rb_ut = {u: rb[u][:, LANES:] for u in units}
    md = [_mm1(bew[s], jnp.concatenate([at[s], ut[s]], axis=1), TN) for s in subs]
    m_off = [md[s][:, 0:LANES] for s in subs]
    d_u = [md[s][:, LANES:] for s in subs]
    d_v = [_mm1(kkw[s], v[s], TN) for s in subs]
    for s in subs:
        rh_ref[sls[s]] = rr[s] + jnp.where(head0, rb_at[s, 0], rb_at[s, 1])
        yh_ref[sls[s]] = jnp.where(head0, rb_ut[s, 0] + rk_v[s, 0], rb_ut[s, 1] + rk_v[s, 1])
        mt_ref[sls2[s]] = jnp.where(eye2, wc[s], 0.0) + jnp.where(same_head, m_off[s], 0.0)
        dt_ref[sls2[s]] = jnp.where(same_head, d_u[s] + d_v[s], 0.0)


def _chunks(r4, v4, lw, kh, kt, a):
    t = r4.shape[1]
    tb = SCAN_BLOCK
    nb = t // tb
    pp = CHUNK_PAIRS
    shared = pl.BlockSpec((pp, tb, LANES), lambda d, p, j: (p, j, 0))
    perdir = pl.BlockSpec((None, pp, tb, LANES), lambda d, p, j: (d, p, j, 0))
    perdir2 = pl.BlockSpec((None, pp, 2 * tb, LANES), lambda d, p, j: (d, p, j, 0))
    shp = jax.ShapeDtypeStruct((2, N_PAIRS, t, LANES), F32)
    shp2 = jax.ShapeDtypeStruct((2, N_PAIRS, 2 * t, LANES), F32)
    return pl.pallas_call(
        _chunk_kernel,
        grid=(2, N_PAIRS // pp, nb),
        in_specs=[shared, shared, perdir, perdir, perdir, perdir],
        out_specs=[perdir, perdir, perdir2, perdir2],
        out_shape=[shp, shp, shp2, shp2],
        compiler_params=_cparams(("arbitrary", "arbitrary", "arbitrary")),
        name="chunks",
    )(r4, v4, lw, kh, kt, a)


def _carry_kernel(rhf_ref, yhf_ref, mtf_ref, dtf_ref, rhb_ref, yhb_ref, mtb_ref, dtb_ref,
                  yf_ref, yb_ref, q_ref):
    j = pl.program_id(0)
    c = CHUNK
    nsub = rhf_ref.shape[1] // c

    @pl.when(j == 0)
    def _():
        q_ref[...] = jnp.zeros(q_ref.shape, F32)

    dirs = ((rhf_ref, yhf_ref, mtf_ref, dtf_ref, yf_ref), (rhb_ref, yhb_ref, mtb_ref, dtb_ref, yb_ref))
    chains = [(dirn, p) for dirn in range(2) for p in range(N_PAIRS)]
    q = {ch: q_ref[ch[0], ch[1]] for ch in chains}
    for s in range(nsub):
        ys, qn = {}, {}
        for dirn, p in chains:
            rh_ref, yh_ref, mt_ref, dt_ref, _ = dirs[dirn]
            cs = s if dirn == 0 else nsub - 1 - s
            ys[dirn, p] = _mm3(rh_ref[p, cs * c:(cs + 1) * c, :], q[dirn, p])
            qn[dirn, p] = _mm3(mt_ref[p, 2 * cs * c:2 * (cs + 1) * c, :], q[dirn, p])
        for dirn, p in chains:
            _, yh_ref, _, dt_ref, y_ref = dirs[dirn]
            cs = s if dirn == 0 else nsub - 1 - s
            y_ref[p, cs * c:(cs + 1) * c, :] = ys[dirn, p] + yh_ref[p, cs * c:(cs + 1) * c, :]
            q[dirn, p] = qn[dirn, p] + dt_ref[p, 2 * cs * c:2 * (cs + 1) * c, :]
    for dirn, p in chains:
        q_ref[dirn, p] = q[dirn, p]


def _carry(rh, yh, mt, dt):
    t = rh.shape[2]
    tb = SCAN_BLOCK
    nb = t // tb
    bwd = lambda j: jnp.where(j == 0, 0, nb - j)
    f1 = pl.BlockSpec((None, N_PAIRS, tb, LANES), lambda j: (0, 0, j, 0))
    f2 = pl.BlockSpec((None, N_PAIRS, 2 * tb, LANES), lambda j: (0, 0, j, 0))
    b1 = pl.BlockSpec((None, N_PAIRS, tb, LANES), lambda j: (1, 0, bwd(j), 0))
    b2 = pl.BlockSpec((None, N_PAIRS, 2 * tb, LANES), lambda j: (1, 0, bwd(j), 0))
    yshape = jax.ShapeDtypeStruct((N_PAIRS, t, LANES), F32)
    return pl.pallas_call(
        _carry_kernel,
        grid=(nb,),
        in_specs=[f1, f1, f2, f2, b1, b1, b2, b2],
        out_specs=[pl.BlockSpec((N_PAIRS, tb, LANES), lambda j: (0, j, 0)),
                   pl.BlockSpec((N_PAIRS, tb, LANES), lambda j: (0, bwd(j), 0))],
        out_shape=[yshape, yshape],
        scratch_shapes=[pltpu.VMEM((2, N_PAIRS, LANES, LANES), F32)],
        compiler_params=_cparams(("arbitrary",)),
        name="carry",
    )(rh, yh, mt, dt, rh, yh, mt, dt)


def _mix_kernel(x_ref, yf_ref, yb_ref, g_ref, bonus_ref, yconv_ref, mod_ref, gnw_ref, gnb_ref, bsum_ref,
                wout_ref, n2g_ref, rw_ref, rb_ref, sw1_ref, sw3_ref, sw2_ref,
                base_ref, h2_ref, idx_ref, gate_ref, rank_ref, cnt_ref, run_ref):
    tm = x_ref.shape[0]
    g1 = mod_ref[0:1, 2 * D_MODEL:3 * D_MODEL]
    sh2 = mod_ref[0:1, 3 * D_MODEL:4 * D_MODEL]
    sc2 = mod_ref[0:1, 4 * D_MODEL:5 * D_MODEL]
    g2 = mod_ref[0:1, 5 * D_MODEL:6 * D_MODEL]

    @pl.when(pl.program_id(0) == 0)
    def _():
        run_ref[...] = jnp.zeros(run_ref.shape, F32)

    y = jnp.concatenate([yf_ref[pr] + yb_ref[pr] for pr in range(N_PAIRS)], axis=1)
    bs = bsum_ref[...]
    mu = _mm_ones_rhs(y, bs) * (1.0 / HEAD_DIM)
    yc = y - mu
    var = _mm_ones_rhs(yc * yc, bs) * (1.0 / HEAD_DIM)
    yn = yc * lax.rsqrt(var + GN_EPS) * gnw_ref[...] + gnb_ref[...]
    yrw = ((yn + bonus_ref[...]) * g_ref[...]).astype(BF16)
    mix = _dg(yconv_ref[...], wout_ref[0:D_CONV, :]) + _dg(yrw, wout_ref[D_CONV:, :])
    x1 = x_ref[...] + g1 * mix

    h2 = _rms(x1, n2g_ref[...]) * (1.0 + sc2) + sh2
    hb = h2.astype(BF16)
    h2_ref[...] = _pack_bf16_pairs(hb.astype(F32))

    scores = _sigmoid(_mm3(h2, rw_ref[...]))
    work = scores + rb_ref[...]
    lane_e = lax.broadcasted_iota(jnp.int32, (tm, N_EXPERTS), 1).astype(F32)
    lane_o = lax.broadcasted_iota(jnp.int32, (tm, LANES), 1)
    idx_acc = jnp.zeros((tm, LANES), F32)
    gate_acc = jnp.zeros((tm, LANES), F32)
    gsum = jnp.zeros((tm, 1), F32)
    chosen = jnp.zeros((tm, N_EXPERTS), F32)
    sels = []
    for kk in range(TOP_K):
        m = jnp.max(work, axis=-1, keepdims=True)
        sel = jnp.min(jnp.where(work == m, lane_e, float(N_EXPERTS)), axis=-1, keepdims=True)
        hit = lane_e == sel
        sk = jnp.sum(jnp.where(hit, scores, 0.0), axis=-1, keepdims=True)
        idx_acc = jnp.where(lane_o == kk, sel, idx_acc)
        gate_acc = jnp.where(lane_o == kk, sk, gate_acc)
        gsum = gsum + sk
        work = jnp.where(hit, -jnp.inf, work)
        chosen = jnp.where(hit, 1.0, chosen)
        sels.append(sel)
    idx_ref[...] = idx_acc.astype(jnp.int32)
    gate_ref[...] = gate_acc / gsum * ROUTED_SCALE

    trow = lax.broadcasted_iota(jnp.int32, (tm, tm), 0)
    tcol = lax.broadcasted_iota(jnp.int32, (tm, tm), 1)
    earlier = jnp.where(tcol < trow, 1.0, 0.0).astype(BF16)
    before = _dg(earlier, chosen.astype(BF16)) + run_ref[0:1, :]
    rank_acc = jnp.zeros((tm, LANES), F32)
    for kk in range(TOP_K):
        rk = jnp.sum(jnp.where(lane_e == sels[kk], before, 0.0), axis=-1, keepdims=True)
        rank_acc = jnp.where(lane_o == kk, rk, rank_acc)
    rank_ref[...] = rank_acc.astype(jnp.int32)
    run_ref[...] = run_ref[...] + jnp.sum(chosen, axis=0, keepdims=True)
    cnt_ref[...] = run_ref[...]

    act = (_silu(_dg(hb, sw1_ref[...])) * _dg(hb, sw3_ref[...])).astype(BF16)
    base_ref[...] = x1 + g2 * _dg(act, sw2_ref[...])


def _mix(x, yf, yb, g, bonus, yconv, mod, gnw, gnb, bsum, wout, n2g, rw, rb, sw1, sw3, sw2):
    t = x.shape[0]
    tm = TOK_TILE
    off = (yf.shape[1] - t) // tm
    full = lambda arr: pl.BlockSpec(arr.shape, lambda i: (0,) * arr.ndim)
    consts = (mod, gnw, gnb, bsum, wout, n2g, rw, rb, sw1, sw3, sw2)
    cat_spec = lambda w: pl.BlockSpec((tm, w), lambda i: (i + off, 0))
    tok_spec = lambda w: pl.BlockSpec((tm, w), lambda i: (i, 0))
    y_spec = pl.BlockSpec((N_PAIRS, tm, LANES), lambda i: (0, i + off, 0))
    return pl.pallas_call(
        _mix_kernel,
        grid=(t // tm,),
        in_specs=[tok_spec(D_MODEL), y_spec, y_spec,
                  cat_spec(D_RWKV), cat_spec(D_RWKV), cat_spec(D_CONV)]
                 + [full(c) for c in consts],
        out_specs=[tok_spec(D_MODEL), tok_spec(HALF), tok_spec(LANES), tok_spec(LANES), tok_spec(LANES),
                   pl.BlockSpec((8, N_EXPERTS), lambda i: (0, 0))],
        out_shape=[jax.ShapeDtypeStruct((t, D_MODEL), F32), jax.ShapeDtypeStruct((t, HALF), I32),
                   jax.ShapeDtypeStruct((t, LANES), jnp.int32), jax.ShapeDtypeStruct((t, LANES), F32),
                   jax.ShapeDtypeStruct((t, LANES), jnp.int32), jax.ShapeDtypeStruct((8, N_EXPERTS), F32)],
        scratch_shapes=[pltpu.VMEM((8, N_EXPERTS), F32)],
        compiler_params=_cparams(("arbitrary",)),
        name="mix",
    )(x, yf, yb, g, bonus, yconv, *consts)


def _slots_kernel(idx_ref, rank_ref, start_ref, dest_ref):
    tm = idx_ref.shape[0]
    lane_e = lax.broadcasted_iota(jnp.int32, (tm, N_EXPERTS), 1)
    lane_o = lax.broadcasted_iota(jnp.int32, (tm, LANES), 1)
    idx = idx_ref[...]
    start = start_ref[...]
    acc = jnp.zeros((tm, LANES), F32)
    for kk in range(TOP_K):
        st = jnp.sum(jnp.where(lane_e == idx[:, kk:kk + 1], start, 0.0), axis=-1, keepdims=True)
        acc = jnp.where(lane_o == kk, st, acc)
    dest_ref[...] = acc.astype(jnp.int32) + rank_ref[...]


def _slots(idx, rank, pad_start):
    t = idx.shape[0]
    tm = TOK_TILE
    spec = pl.BlockSpec((tm, LANES), lambda i: (i, 0))
    return pl.pallas_call(
        _slots_kernel,
        grid=(t // tm,),
        in_specs=[spec, spec, pl.BlockSpec((1, N_EXPERTS), lambda i: (0, 0))],
        out_specs=spec,
        out_shape=jax.ShapeDtypeStruct((t, LANES), jnp.int32),
        compiler_params=_cparams(("arbitrary",)),
        name="slots",
    )(idx, rank, pad_start)


def _sc_scatter_rows(rows, dest_km, n_slots):
    info = plsc.get_sparse_core_info()
    nc, ns = info.num_cores, info.num_subcores
    nw = nc * ns
    t, d = rows.shape
    nk = dest_km.shape[0]
    ch = SC_SCATTER_CHUNK
    per_w = t // nw
    n_ch = per_w // ch
    assert t % nw == 0 and per_w % ch == 0
    mesh = plsc.VectorSubcoreMesh(core_axis_name="c", subcore_axis_name="s")
    idx = dest_km.reshape(nk, nw, n_ch, ch).transpose(1, 2, 0, 3).reshape(nw, n_ch * nk, ch)

    @functools.partial(
        pl.kernel, mesh=mesh,
        out_type=jax.ShapeDtypeStruct((n_slots, d), rows.dtype),
        scratch_types=[pltpu.VMEM((n_ch * nk, ch), jnp.int32),
                       pltpu.VMEM((ch, d), rows.dtype),
                       pltpu.SemaphoreType.DMA],
    )
    def scatter_kernel(rows_hbm, idx_hbm, out_hbm, idx_v, rows_v, sem):
        wid = lax.axis_index("s") * nc + lax.axis_index("c")
        base = wid * per_w
        pltpu.sync_copy(idx_hbm.at[wid], idx_v)

        @pl.loop(0, n_ch)
        def _(j):
            pltpu.sync_copy(rows_hbm.at[pl.ds(base + j * ch, ch)], rows_v)
            copies = [pltpu.make_async_copy(rows_v, out_hbm.at[idx_v.at[j * nk + kk]], sem) for kk in range(nk)]
            for cp in copies:
                cp.start()
            for cp in copies:
                cp.wait()

    return scatter_kernel(rows, idx)


def _experts_kernel(be_ref, nu_ref, nv_ref, hs_ref, w1_ref, w3_ref, w2_ref, o_ref, w1c, w3c, w2c):
    b = pl.program_id(0)
    new_expert = jnp.logical_or(b == 0, be_ref[b] != be_ref[jnp.maximum(b - 1, 0)])

    @pl.when(new_expert)
    def _():
        w1c[...] = w1_ref[...].astype(BF16)
        w3c[...] = w3_ref[...].astype(BF16)
        w2c[...] = w2_ref[...].astype(BF16)

    @pl.when(b < nu_ref[0])
    def _():
        live = lax.broadcasted_iota(jnp.int32, (hs_ref.shape[0], 1), 0) < nv_ref[b]
        lo, hi = _unpack_bf16_pairs(jnp.where(live, hs_ref[...], 0))
        lo = lo.astype(BF16)
        hi = hi.astype(BF16)
        a1 = _dg(lo, w1c[0:HALF, :]) + _dg(hi, w1c[HALF:, :])
        a3 = _dg(lo, w3c[0:HALF, :]) + _dg(hi, w3c[HALF:, :])
        act = (_silu(a1) * a3).astype(BF16)
        y = _dg(act, w2c[...]).astype(BF16).astype(F32)
        o_ref[...] = _pack_bf16_pairs(y)

    @pl.when(b >= nu_ref[0])
    def _():
        o_ref[...] = jnp.zeros(o_ref.shape, I32)


def _experts(block_e, n_used, n_valid, hs, w1, w3, w2):
    nblk = block_e.shape[0]
    sb = SLOT_BLOCK
    row_map = lambda b, be, nu, nv: (jnp.minimum(b, nu[0] - 1), 0)
    w_map = lambda b, be, nu, nv: (be[b], 0, 0)
    grid_spec = pltpu.PrefetchScalarGridSpec(
        num_scalar_prefetch=3,
        grid=(nblk,),
        in_specs=[pl.BlockSpec((sb, HALF), row_map),
                  pl.BlockSpec((None, D_MODEL, D_EXPERT), w_map),
                  pl.BlockSpec((None, D_MODEL, D_EXPERT), w_map),
                  pl.BlockSpec((None, D_EXPERT, D_MODEL), w_map)],
        out_specs=pl.BlockSpec((sb, HALF), lambda b, be, nu, nv: (b, 0)),
        scratch_shapes=[pltpu.VMEM((D_MODEL, D_EXPERT), BF16), pltpu.VMEM((D_MODEL, D_EXPERT), BF16),
                        pltpu.VMEM((D_EXPERT, D_MODEL), BF16)],
    )
    return pl.pallas_call(
        _experts_kernel,
        grid_spec=grid_spec,
        out_shape=jax.ShapeDtypeStruct((nblk * sb, HALF), I32),
        compiler_params=_cparams(("arbitrary",)),
        name="experts",
    )(block_e, n_used, n_valid, hs, w1, w3, w2)


def _sc_gather_rows(table, idx):
    info = plsc.get_sparse_core_info()
    nc, ns = info.num_cores, info.num_subcores
    nw = nc * ns
    n = idx.shape[0]
    d = table.shape[1]
    ch = SC_CHUNK
    per_w = n // nw
    n_ch = per_w // ch
    assert n % nw == 0 and per_w % (2 * ch) == 0
    mesh = plsc.VectorSubcoreMesh(core_axis_name="c", subcore_axis_name="s")

    @functools.partial(
        pl.kernel, mesh=mesh,
        out_type=jax.ShapeDtypeStruct((n, d), table.dtype),
        scratch_types=[pltpu.VMEM((n_ch, ch), jnp.int32),
                       pltpu.VMEM((2, ch, d), table.dtype),
                       pltpu.SemaphoreType.DMA((2,))],
    )
    def gather_kernel(table_hbm, idx_hbm, out_hbm, idx_v, rows_v, sem):
        wid = lax.axis_index("s") * nc + lax.axis_index("c")
        base = wid * per_w
        pltpu.sync_copy(idx_hbm.at[wid], idx_v)

        def gather(j, buf):
            return pltpu.make_async_copy(table_hbm.at[idx_v.at[j]], rows_v.at[buf], sem.at[buf])

        gather(0, 0).start()

        @pl.loop(0, n_ch, step=2)
        def _(j):
            for buf in range(2):
                jj = j + buf
                gather(jj, buf).wait()

                @pl.when(jj + 1 < n_ch)
                def _():
                    gather(jj + 1, 1 - buf).start()

                pltpu.sync_copy(rows_v.at[buf], out_hbm.at[pl.ds(base + jj * ch, ch)])

    return gather_kernel(table, idx.reshape(nw, n_ch, ch))


def _combine_kernel(rows_ref, base_ref, gate_ref, mod_ref, fg_ref, o_ref):
    tm = base_ref.shape[0]
    gate = gate_ref[...]
    acc_lo = jnp.zeros((tm, HALF), F32)
    acc_hi = jnp.zeros((tm, HALF), F32)
    for kk in range(TOP_K):
        lo, hi = _unpack_bf16_pairs(rows_ref[kk])
        acc_lo = acc_lo + gate[:, kk:kk + 1] * lo
        acc_hi = acc_hi + gate[:, kk:kk + 1] * hi
    acc = jnp.concatenate([acc_lo, acc_hi], axis=1)
    g2 = mod_ref[0:1, 5 * D_MODEL:6 * D_MODEL]
    o_ref[...] = _rms(base_ref[...] + g2 * acc, fg_ref[...])


def _combine(dest_km, ye, base, gate, mod, fg):
    t = base.shape[0]
    tm = TOK_TILE
    rows = _sc_gather_rows(ye, dest_km.reshape(-1)).reshape(TOP_K, t, HALF)
    full = lambda arr: pl.BlockSpec(arr.shape, lambda i: (0,) * arr.ndim)
    return pl.pallas_call(
        _combine_kernel,
        grid=(t // tm,),
        in_specs=[pl.BlockSpec((TOP_K, tm, HALF), lambda i: (0, i, 0)),
                  pl.BlockSpec((tm, D_MODEL), lambda i: (i, 0)),
                  pl.BlockSpec((tm, LANES), lambda i: (i, 0)),
                  full(mod), full(fg)],
        out_specs=pl.BlockSpec((tm, D_MODEL), lambda i: (i, 0)),
        out_shape=jax.ShapeDtypeStruct((t, D_MODEL), F32),
        compiler_params=_cparams(("arbitrary",)),
        name="combine",
    )(rows, base, gate, mod, fg)


def _block_tables(counts, n_blocks):
    sb = SLOT_BLOCK
    padded = (counts + sb - 1) // sb * sb
    pad_end = jnp.cumsum(padded)
    pad_start = pad_end - padded
    n_used = pad_end[-1] // sb
    blk = jnp.minimum(jnp.arange(n_blocks, dtype=jnp.int32), n_used - 1)
    block_e = jnp.minimum(jnp.sum(pad_end[None, :] <= (blk * sb)[:, None], axis=1), N_EXPERTS - 1).astype(jnp.int32)
    onehot = block_e[:, None] == jnp.arange(N_EXPERTS, dtype=jnp.int32)[None, :]
    live_end = jnp.sum(jnp.where(onehot, (pad_start + counts)[None, :], 0), axis=1)
    n_valid = jnp.clip(live_end - blk * sb, 0, sb).astype(jnp.int32)
    return block_e, n_used.reshape(1).astype(jnp.int32), n_valid, pad_start


def _blockdiag2(w):
    z = jnp.zeros_like(w[0])
    return jnp.concatenate([jnp.concatenate([w[0], z], axis=1), jnp.concatenate([z, w[1]], axis=1)], axis=0)


def _pad_rows(w, n=8):
    return jnp.concatenate([w, jnp.zeros((n - w.shape[0],) + w.shape[1:], w.dtype)], axis=0)


def kernel(x, c, ctx, c_ctx, ada_w, ada_b, norm1_g, norm2_g, w_in, conv_w, shift_mu, decay_w0, decay_up, iclr_a0, iclr_up, key_xi, key_alpha, bonus_rho, gate_up, gn_w, gn_b, w_out, router_w, router_bias, exp_w1, exp_w3, exp_w2, sh_w1, sh_w3, sh_w2, final_g):
    assert x.shape[0] == 1 and ada_w.shape[0] == 1
    assert ctx.shape[1] == TOK_TILE and x.shape[1] % TOK_TILE == 0
    l = 0
    xs = x[0]
    row = lambda w: w.reshape(1, -1)

    cc = _pad_rows(jnp.stack([c[0], c_ctx], axis=0))
    mod = _ada(cc, ada_w[l], row(ada_b[l]))

    hid = lax.broadcasted_iota(jnp.int32, (D_RWKV, D_RWKV), 0) // HEAD_DIM
    bsum = (hid == hid.T).astype(BF16)

    xcat = jnp.concatenate([ctx[0], xs], axis=0)
    r4, v4, g, bonus, yconv, lw, kh, kt, a = _proj(
        xcat, mod, row(norm1_g[l]), w_in[l].astype(BF16), _pad_rows(conv_w[l]), _pad_rows(shift_mu[l]),
        row(decay_w0[l]), _blockdiag2(decay_up[l]), row(iclr_a0[l]), _blockdiag2(iclr_up[l]),
        row(key_xi[l]), row(key_alpha[l]), row(bonus_rho[l]), gate_up[l], bsum)

    yf, yb = _carry(*_chunks(r4, v4, lw, kh, kt, a))

    base, h2, idx, gate, rank, cnt = _mix(
        xs, yf, yb, g, bonus, yconv, mod, row(gn_w[l]), row(gn_b[l]), bsum, w_out[l].astype(BF16),
        row(norm2_g[l]), router_w[l], row(router_bias[l]),
        sh_w1[l].astype(BF16), sh_w3[l].astype(BF16), sh_w2[l].astype(BF16))

    n_blocks = xs.shape[0] * TOP_K // SLOT_BLOCK + N_EXPERTS
    block_e, n_used, n_valid, pad_start = _block_tables(cnt[0].astype(jnp.int32), n_blocks)
    dest_km = _slots(idx, rank, pad_start.astype(F32).reshape(1, N_EXPERTS))[:, :TOP_K].T
    hs = _sc_scatter_rows(h2, dest_km, n_blocks * SLOT_BLOCK)
    ye = _experts(block_e, n_used, n_valid, hs, exp_w1[l], exp_w3[l], exp_w2[l])
    out = _combine(dest_km, ye, base, gate, mod, row(final_g))
    return out[None]
```

```python
import functools

import jax
import jax.numpy as jnp
from jax import lax
from jax.experimental import pallas as pl
from jax.experimental.pallas import tpu as pltpu
from jax.experimental.pallas import tpu_sc as plsc

F32 = jnp.float32
BF16 = jnp.bfloat16

D_MODEL = 1024
D_CONV = 512
D_RWKV = 512
HEAD_DIM = 64
N_HEADS = D_RWKV // HEAD_DIM
N_PAIRS = N_HEADS // 2
LORA = 128
P_RWKV = 3 * D_RWKV + 3 * LORA
P_IN = 3 * D_CONV + P_RWKV
GRID_W = 64
N_EXPERTS = 256
TOP_K = 8
D_EXPERT = 256
ROUTED_SCALE = 2.5
RMS_EPS = 1e-6
GN_EPS = 64e-5
DECAY_SCALE = 0.6065306597126334
NORM_EPS = 1e-12

TOK_TILE = 256
SLOTS_TILE = 1024
SCAN_BLOCK = 256
CHUNK = 64
CHUNK_PAIRS = 4
SLOT_BLOCK = 512
SC_CHUNK = 64
SC_SCATTER_CHUNK = 128
LANES = 128
VMEM_LIMIT = 56 * 1024 * 1024

NN = ((1,), (0,))
NT = ((1,), (1,))
TN = ((0,), (0,))


def _dg(a, b, dims=NN):
    return lax.dot_general(a, b, (dims, ((), ())), preferred_element_type=F32)


def _split2(a):
    hi = a.astype(BF16)
    lo = (a - hi.astype(F32)).astype(BF16)
    return hi, lo


def _split3(a):
    hi = a.astype(BF16)
    r1 = a - hi.astype(F32)
    mid = r1.astype(BF16)
    lo = (r1 - mid.astype(F32)).astype(BF16)
    return hi, mid, lo


def _mm1(a, b, dims=NN):
    return _dg(a.astype(BF16), b.astype(BF16), dims)


def _mm3(a, b, dims=NN):
    ah, al = _split2(a)
    bh, bl = _split2(b)
    return _dg(ah, bh, dims) + (_dg(ah, bl, dims) + _dg(al, bh, dims))


def _mm_split_lhs(a, b_bf16, dims=NN):
    h, l = _split2(a)
    return _dg(h, b_bf16, dims) + _dg(l, b_bf16, dims)


_mm_ones_rhs = _mm_split_lhs


def _mm_ones_lhs(ones_bf16, b, dims=NN):
    h, m, l = _split3(b)
    return _dg(ones_bf16, h, dims) + (_dg(ones_bf16, m, dims) + _dg(ones_bf16, l, dims))


HALF = D_MODEL // 2
I32 = jnp.int32
HI_MASK = -65536


def _pack_bf16_pairs(x):
    u = lax.bitcast_convert_type(x, I32)
    return lax.shift_right_logical(u[:, 0:HALF], jnp.int32(16)) | (u[:, HALF:] & jnp.int32(HI_MASK))


def _unpack_bf16_pairs(w):
    lo = lax.bitcast_convert_type(lax.shift_left(w, jnp.int32(16)), F32)
    hi = lax.bitcast_convert_type(w & jnp.int32(HI_MASK), F32)
    return lo, hi


def _sigmoid(x):
    return 1.0 / (1.0 + jnp.exp(-x))


def _silu(x):
    return x * _sigmoid(x)


def _rms(xv, g):
    ms = jnp.mean(xv * xv, axis=-1, keepdims=True)
    return xv * lax.rsqrt(ms + RMS_EPS) * g


def _cparams(sem):
    return pltpu.CompilerParams(dimension_semantics=sem, vmem_limit_bytes=VMEM_LIMIT)


def _ada_kernel(c_ref, w_ref, b_ref, o_ref):
    o_ref[...] = _mm3(_silu(c_ref[...]), w_ref[...]) + b_ref[...]


def _ada(cc, ada_w, ada_b):
    n = ada_w.shape[1]
    tn = 1024
    return pl.pallas_call(
        _ada_kernel,
        grid=(n // tn,),
        in_specs=[pl.BlockSpec((8, D_MODEL), lambda j: (0, 0)),
                  pl.BlockSpec((D_MODEL, tn), lambda j: (0, j)),
                  pl.BlockSpec((1, tn), lambda j: (0, j))],
        out_specs=pl.BlockSpec((8, tn), lambda j: (0, j)),
        out_shape=jax.ShapeDtypeStruct((8, n), F32),
        compiler_params=_cparams(("arbitrary",)),
        name="ada",
    )(cc, ada_w, ada_b)


def _proj_kernel(ctx_ref, x_ref, xp_ref, xn_ref, mod_ref, n1g_ref, win_ref, convw_ref, mu_ref,
                 dw0_ref, dup_ref, ia0_ref, iup_ref, xi_ref, kal_ref, rho_ref, gup_ref, bsum_ref,
                 r_ref, v_ref, g_ref, bonus_ref, yconv_ref, lw_ref, kh_ref, kt_ref, a_ref):
    i = pl.program_id(0)
    nt = pl.num_programs(0)
    tm = x_ref.shape[0]
    is_ctx = i == 0
    sh = jnp.where(is_ctx, mod_ref[1:2, 0:D_MODEL], mod_ref[0:1, 0:D_MODEL])
    sc = jnp.where(is_ctx, mod_ref[1:2, D_MODEL:2 * D_MODEL], mod_ref[0:1, D_MODEL:2 * D_MODEL])
    n1g = n1g_ref[...]

    def norm_mod(xv):
        return _rms(xv, n1g) * (1.0 + sc) + sh

    xt = jnp.where(is_ctx, ctx_ref[...], x_ref[...])
    h = norm_mod(jnp.concatenate([xt, xp_ref[...], xn_ref[...]], axis=0)).astype(BF16)
    p_all = _dg(h, win_ref[...])
    p = p_all[0:tm]
    ph = p_all[tm:tm + 16, 3 * D_CONV:]

    rows = lax.broadcasted_iota(jnp.int32, (tm, 1), 0)

    bg = p[:, 0:D_CONV]
    z = p[:, D_CONV:2 * D_CONV] * p[:, 2 * D_CONV:3 * D_CONV]
    col = rows % GRID_W
    zp = jnp.where(col == 0, 0.0, pltpu.roll(z, 1, 0))
    zn = jnp.where(col == GRID_W - 1, 0.0, pltpu.roll(z, tm - 1, 0))
    yconv = bg * (convw_ref[0:1, :] * zp + convw_ref[1:2, :] * z + convw_ref[2:3, :] * zn)
    yconv_ref[...] = yconv.astype(BF16)

    prev_ok = jnp.logical_and(i != 0, i != 1).astype(F32)
    next_ok = jnp.logical_and(i != 0, i != nt - 1).astype(F32)
    cur = p[:, 3 * D_CONV:]
    prev = jnp.where(rows == 0, ph[7:8, :] * prev_ok, pltpu.roll(cur, 1, 0))
    nxt = jnp.where(rows == tm - 1, ph[8:9, :] * next_ok, pltpu.roll(cur, tm - 1, 0))
    ps = cur + mu_ref[0:1, :] * (prev - cur) + mu_ref[1:2, :] * (nxt - cur)

    r = ps[:, 0:D_RWKV]
    k = ps[:, D_RWKV:2 * D_RWKV]
    v = ps[:, 2 * D_RWKV:3 * D_RWKV]
    o = 3 * D_RWKV
    wlo = ps[:, o:o + LORA]
    alo = ps[:, o + LORA:o + 2 * LORA]
    glo = ps[:, o + 2 * LORA:o + 3 * LORA]

    dd = dw0_ref[...] + _mm_split_lhs(jnp.tanh(wlo), dup_ref[...])
    lw = -DECAY_SCALE * _sigmoid(dd)
    a = _sigmoid(ia0_ref[...] + _mm_split_lhs(alo, iup_ref[...]))
    g = _dg(_sigmoid(glo).astype(BF16), gup_ref[...])
    k2 = jnp.concatenate([k, k], axis=1)
    kap = k2 * xi_ref[...]
    kap2 = kap * kap
    bs = bsum_ref[...]
    ss = jnp.concatenate([_mm_ones_rhs(kap2[:, 0:D_RWKV], bs),
                          _mm_ones_rhs(kap2[:, D_RWKV:], bs)], axis=1)
    kh = kap * lax.rsqrt(ss + NORM_EPS)
    kt = k2 * (1.0 + (a - 1.0) * kal_ref[...])
    bon = _mm_ones_rhs(r * rho_ref[...] * (kt[:, 0:D_RWKV] + kt[:, D_RWKV:]), bs)

    g_ref[...] = g
    bonus_ref[...] = bon * v
    for pr in range(N_PAIRS):
        ls = slice(pr * LANES, (pr + 1) * LANES)
        r_ref[pr] = r[:, ls]
        v_ref[pr] = v[:, ls]
        for d in range(2):
            ld = slice(d * D_RWKV + pr * LANES, d * D_RWKV + (pr + 1) * LANES)
            lw_ref[d, pr] = lw[:, ld]
            kh_ref[d, pr] = kh[:, ld]
            kt_ref[d, pr] = kt[:, ld]
            a_ref[d, pr] = a[:, ld]


def _proj(ctx2, x2, mod, n1g, win, convw, mu, dw0, dup, ia0, iup, xi, kal, rho, gup, bsum):
    tm = TOK_TILE
    t = ctx2.shape[0] + x2.shape[0]
    nt = t // tm
    per = tm // 8
    nb8 = x2.shape[0] // 8
    full = lambda arr: pl.BlockSpec(arr.shape, lambda i: (0,) * arr.ndim)
    pair_spec = pl.BlockSpec((N_PAIRS, tm, LANES), lambda i: (0, i, 0))
    dpair_spec = pl.BlockSpec((2, N_PAIRS, tm, LANES), lambda i: (0, 0, i, 0))
    row_spec = pl.BlockSpec((tm, D_RWKV), lambda i: (i, 0))
    pair_shape = jax.ShapeDtypeStruct((N_PAIRS, t, LANES), F32)
    dpair_shape = jax.ShapeDtypeStruct((2, N_PAIRS, t, LANES), F32)
    consts = (mod, n1g, win, convw, mu, dw0, dup, ia0, iup, xi, kal, rho, gup, bsum)
    return pl.pallas_call(
        _proj_kernel,
        grid=(nt,),
        in_specs=[pl.BlockSpec((tm, D_MODEL), lambda i: (0, 0)),
                  pl.BlockSpec((tm, D_MODEL), lambda i: (jnp.maximum(i - 1, 0), 0)),
                  pl.BlockSpec((8, D_MODEL), lambda i: (jnp.maximum((i - 1) * per - 1, 0), 0)),
                  pl.BlockSpec((8, D_MODEL), lambda i: (jnp.clip(i * per, 0, nb8 - 1), 0))]
                 + [full(c) for c in consts],
        out_specs=[pair_spec, pair_spec, row_spec, row_spec, row_spec,
                   dpair_spec, dpair_spec, dpair_spec, dpair_spec],
        out_shape=[pair_shape, pair_shape,
                   jax.ShapeDtypeStruct((t, D_RWKV), F32), jax.ShapeDtypeStruct((t, D_RWKV), F32),
                   jax.ShapeDtypeStruct((t, D_CONV), BF16),
                   dpair_shape, dpair_shape, dpair_shape, dpair_shape],
        compiler_params=_cparams(("arbitrary",)),
        name="proj",
    )(ctx2, x2, x2, x2, *consts)


def _chunk_kernel(r_ref, v_ref, lw_ref, kh_ref, kt_ref, a_ref, rh_ref, yh_ref, mt_ref, dt_ref):
    d = pl.program_id(0)
    rev = d == 1
    c = CHUNK
    npair = r_ref.shape[0]
    nsub = r_ref.shape[1] // c

    row = lax.broadcasted_iota(jnp.int32, (c, c), 0)
    col = lax.broadcasted_iota(jnp.int32, (c, c), 1)
    strict = (col - row) * (1 - 2 * d) < 0
    eye = row == col
    incl = jnp.logical_or(strict, eye)
    incl_bf = jnp.where(incl, 1.0, 0.0).astype(BF16)
    eye_f = jnp.where(eye, 1.0, 0.0)
    head0 = lax.broadcasted_iota(jnp.int32, (c, LANES), 1) < HEAD_DIM
    row2 = lax.broadcasted_iota(jnp.int32, (LANES, LANES), 0)
    col2 = lax.broadcasted_iota(jnp.int32, (LANES, LANES), 1)
    same_head = (row2 < HEAD_DIM) == (col2 < HEAD_DIM)
    eye2 = row2 == col2

    subs = range(npair * nsub)
    units = [(s, h) for s in subs for h in range(2)]
    sls = [(pp, slice(s * c, (s + 1) * c), slice(None)) for pp in range(npair) for s in range(nsub)]
    sls2 = [(pp, slice(2 * s * c, 2 * (s + 1) * c), slice(None)) for pp in range(npair) for s in range(nsub)]
    v = [v_ref[sl] for sl in sls]
    lw = [lw_ref[sl] for sl in sls]
    cum = [_mm_ones_lhs(incl_bf, lw[s]) for s in subs]
    al, be, kk, rr, bew, kkw, wc = [], [], [], [], [], [], []
    for s in subs:
        cum_last = jnp.where(rev, cum[s][0:1, :], cum[s][c - 1:c, :])
        e_end = jnp.exp(cum_last - cum[s])
        e_neg = jnp.exp(-cum[s])
        kh = kh_ref[sls[s]]
        kt = kt_ref[sls[s]]
        kha = kh * a_ref[sls[s]]
        al.append(kh * jnp.exp(cum[s] - lw[s]))
        be.append(-(kha * e_neg))
        kk.append(kt * e_neg)
        rr.append(r_ref[sls[s]] * jnp.exp(cum[s]))
        bew.append(-(kha * e_end))
        kkw.append(kt * e_end)
        wc.append(jnp.exp(cum_last))

    lhs = {}
    for s, h in units:
        hm = head0 if h == 0 else jnp.logical_not(head0)
        lhs[s, h] = jnp.concatenate([jnp.where(hm, al[s], 0.0), jnp.where(hm, rr[s], 0.0)], axis=0)
    gb = {u: _mm1(lhs[u], be[u[0]], NT) for u in units}
    gk = {u: _mm1(lhs[u], kk[u[0]], NT) for u in units}
    a_ab = {u: jnp.where(strict, gb[u][0:c], 0.0) for u in units}
    a_ak = {u: jnp.where(strict, gk[u][0:c], 0.0) for u in units}
    a_rb = {u: jnp.where(incl, gb[u][c:2 * c], 0.0) for u in units}
    a_rk = {u: jnp.where(incl, gk[u][c:2 * c], 0.0) for u in units}

    pw = dict(a_ab)
    tinv = {u: eye_f + a_ab[u] for u in units}
    akrk_v = {u: _mm1(jnp.concatenate([a_ak[u], a_rk[u]], axis=0), v[u[0]]) for u in units}
    avk = {u: akrk_v[u][0:c] for u in units}
    rk_v = {u: akrk_v[u][c:2 * c] for u in units}
    for _ in range(c.bit_length() - 2):
        pw = {u: _mm1(pw[u], pw[u]) for u in units}
        tinv = {u: tinv[u] + _mm1(tinv[u], pw[u]) for u in units}
    tu = {u: _mm1(tinv[u], jnp.concatenate([al[u[0]], avk[u]], axis=1)) for u in units}
    at = [jnp.where(head0, tu[s, 0][:, 0:LANES], tu[s, 1][:, 0:LANES]) for s in subs]
    ut = [jnp.where(head0, tu[s, 0][:, LANES:], tu[s, 1][:, LANES:]) for s in subs]

    rb = {u: _mm1(a_rb[u], jnp.concatenate([at[u[0]], ut[u[0]]], axis=1)) for u in units}
    rb_at = {u: rb[u][:, 0:LANES] for u in units}
    rb_ut = {u: rb[u][:, LANES:] for u in units}
    md = [_mm1(bew[s], jnp.concatenate([at[s], ut[s]], axis=1), TN) for s in subs]
    m_off = [md[s][:, 0:LANES] for s in subs]
    d_u = [md[s][:, LANES:] for s in subs]
    d_v = [_mm1(kkw[s], v[s], TN) for s in subs]
    for s in subs:
        rh_ref[sls[s]] = rr[s] + jnp.where(head0, rb_at[s, 0], rb_at[s, 1])
        yh_ref[sls[s]] = jnp.where(head0, rb_ut[s, 0] + rk_v[s, 0], rb_ut[s, 1] + rk_v[s, 1])
        mt_ref[sls2[s]] = jnp.where(eye2, wc[s], 0.0) + jnp.where(same_head, m_off[s], 0.0)
        dt_ref[sls2[s]] = jnp.where(same_head, d_u[s] + d_v[s], 0.0)


def _chunks(r4, v4, lw, kh, kt, a):
    t = r4.shape[1]
    tb = SCAN_BLOCK
    nb = t // tb
    pp = CHUNK_PAIRS
    shared = pl.BlockSpec((pp, tb, LANES), lambda d, p, j: (p, j, 0))
    perdir = pl.BlockSpec((None, pp, tb, LANES), lambda d, p, j: (d, p, j, 0))
    perdir2 = pl.BlockSpec((None, pp, 2 * tb, LANES), lambda d, p, j: (d, p, j, 0))
    shp = jax.ShapeDtypeStruct((2, N_PAIRS, t, LANES), F32)
    shp2 = jax.ShapeDtypeStruct((2, N_PAIRS, 2 * t, LANES), F32)
    return pl.pallas_call(
        _chunk_kernel,
        grid=(2, N_PAIRS // pp, nb),
        in_specs=[shared, shared, perdir, perdir, perdir, perdir],
        out_specs=[perdir, perdir, perdir2, perdir2],
        out_shape=[shp, shp, shp2, shp2],
        compiler_params=_cparams(("arbitrary", "arbitrary", "arbitrary")),
        name="chunks",
    )(r4, v4, lw, kh, kt, a)


def _carry_kernel(rhf_ref, yhf_ref, mtf_ref, dtf_ref, rhb_ref, yhb_ref, mtb_ref, dtb_ref,
                  yf_ref, yb_ref, q_ref):
    j = pl.program_id(0)
    c = CHUNK
    nsub = rhf_ref.shape[1] // c

    @pl.when(j == 0)
    def _():
        q_ref[...] = jnp.zeros(q_ref.shape, F32)

    dirs = ((rhf_ref, yhf_ref, mtf_ref, dtf_ref, yf_ref), (rhb_ref, yhb_ref, mtb_ref, dtb_ref, yb_ref))
    chains = [(dirn, p) for dirn in range(2) for p in range(N_PAIRS)]
    q = {ch: q_ref[ch[0], ch[1]] for ch in chains}
    for s in range(nsub):
        ys, qn = {}, {}
        for dirn, p in chains:
            rh_ref, yh_ref, mt_ref, dt_ref, _ = dirs[dirn]
            cs = s if dirn == 0 else nsub - 1 - s
            ys[dirn, p] = _mm3(rh_ref[p, cs * c:(cs + 1) * c, :], q[dirn, p])
            qn[dirn, p] = _mm3(mt_ref[p, 2 * cs * c:2 * (cs + 1) * c, :], q[dirn, p])
        for dirn, p in chains:
            _, yh_ref, _, dt_ref, y_ref = dirs[dirn]
            cs = s if dirn == 0 else nsub - 1 - s
            y_ref[p, cs * c:(cs + 1) * c, :] = ys[dirn, p] + yh_ref[p, cs * c:(cs + 1) * c, :]
            q[dirn, p] = qn[dirn, p] + dt_ref[p, 2 * cs * c:2 * (cs + 1) * c, :]
    for dirn, p in chains:
        q_ref[dirn, p] = q[dirn, p]


def _carry(rh, yh, mt, dt):
    t = rh.shape[2]
    tb = SCAN_BLOCK
    nb = t // tb
    bwd = lambda j: jnp.where(j == 0, 0, nb - j)
    f1 = pl.BlockSpec((None, N_PAIRS, tb, LANES), lambda j: (0, 0, j, 0))
    f2 = pl.BlockSpec((None, N_PAIRS, 2 * tb, LANES), lambda j: (0, 0, j, 0))
    b1 = pl.BlockSpec((None, N_PAIRS, tb, LANES), lambda j: (1, 0, bwd(j), 0))
    b2 = pl.BlockSpec((None, N_PAIRS, 2 * tb, LANES), lambda j: (1, 0, bwd(j), 0))
    yshape = jax.ShapeDtypeStruct((N_PAIRS, t, LANES), F32)
    return pl.pallas_call(
        _carry_kernel,
        grid=(nb,),
        in_specs=[f1, f1, f2, f2, b1, b1, b2, b2],
        out_specs=[pl.BlockSpec((N_PAIRS, tb, LANES), lambda j: (0, j, 0)),
                   pl.BlockSpec((N_PAIRS, tb, LANES), lambda j: (0, bwd(j), 0))],
        out_shape=[yshape, yshape],
        scratch_shapes=[pltpu.VMEM((2, N_PAIRS, LANES, LANES), F32)],
        compiler_params=_cparams(("arbitrary",)),
        name="carry",
    )(rh, yh, mt, dt, rh, yh, mt, dt)


def _mix_kernel(x_ref, yf_ref, yb_ref, g_ref, bonus_ref, yconv_ref, mod_ref, gnw_ref, gnb_ref, bsum_ref,
                wout_ref, n2g_ref, rw_ref, rb_ref, sw1_ref, sw3_ref, sw2_ref,
                base_ref, h2_ref, idx_ref, gate_ref, rank_ref, cnt_ref, run_ref):
    tm = x_ref.shape[0]
    g1 = mod_ref[0:1, 2 * D_MODEL:3 * D_MODEL]
    sh2 = mod_ref[0:1, 3 * D_MODEL:4 * D_MODEL]
    sc2 = mod_ref[0:1, 4 * D_MODEL:5 * D_MODEL]
    g2 = mod_ref[0:1, 5 * D_MODEL:6 * D_MODEL]

    @pl.when(pl.program_id(0) == 0)
    def _():
        run_ref[...] = jnp.zeros(run_ref.shape, F32)

    y = jnp.concatenate([yf_ref[pr] + yb_ref[pr] for pr in range(N_PAIRS)], axis=1)
    bs = bsum_ref[...]
    mu = _mm_ones_rhs(y, bs) * (1.0 / HEAD_DIM)
    yc = y - mu
    var = _mm_ones_rhs(yc * yc, bs) * (1.0 / HEAD_DIM)
    yn = yc * lax.rsqrt(var + GN_EPS) * gnw_ref[...] + gnb_ref[...]
    yrw = ((yn + bonus_ref[...]) * g_ref[...]).astype(BF16)
    mix = _dg(yconv_ref[...], wout_ref[0:D_CONV, :]) + _dg(yrw, wout_ref[D_CONV:, :])
    x1 = x_ref[...] + g1 * mix

    h2 = _rms(x1, n2g_ref[...]) * (1.0 + sc2) + sh2
    hb = h2.astype(BF16)
    h2_ref[...] = _pack_bf16_pairs(hb.astype(F32))

    scores = _sigmoid(_mm3(h2, rw_ref[...]))
    work = scores + rb_ref[...]
    lane_e = lax.broadcasted_iota(jnp.int32, (tm, N_EXPERTS), 1).astype(F32)
    lane_o = lax.broadcasted_iota(jnp.int32, (tm, LANES), 1)
    idx_acc = jnp.zeros((tm, LANES), F32)
    gate_acc = jnp.zeros((tm, LANES), F32)
    gsum = jnp.zeros((tm, 1), F32)
    chosen = jnp.zeros((tm, N_EXPERTS), F32)
    sels = []
    for kk in range(TOP_K):
        m = jnp.max(work, axis=-1, keepdims=True)
        sel = jnp.min(jnp.where(work == m, lane_e, float(N_EXPERTS)), axis=-1, keepdims=True)
        hit = lane_e == sel
        sk = jnp.sum(jnp.where(hit, scores, 0.0), axis=-1, keepdims=True)
        idx_acc = jnp.where(lane_o == kk, sel, idx_acc)
        gate_acc = jnp.where(lane_o == kk, sk, gate_acc)
        gsum = gsum + sk
        work = jnp.where(hit, -jnp.inf, work)
        chosen = jnp.where(hit, 1.0, chosen)
        sels.append(sel)
    idx_ref[...] = idx_acc.astype(jnp.int32)
    gate_ref[...] = gate_acc / gsum * ROUTED_SCALE

    trow = lax.broadcasted_iota(jnp.int32, (tm, tm), 0)
    tcol = lax.broadcasted_iota(jnp.int32, (tm, tm), 1)
    earlier = jnp.where(tcol < trow, 1.0, 0.0).astype(BF16)
    before = _dg(earlier, chosen.astype(BF16)) + run_ref[0:1, :]
    rank_acc = jnp.zeros((tm, LANES), F32)
    for kk in range(TOP_K):
        rk = jnp.sum(jnp.where(lane_e == sels[kk], before, 0.0), axis=-1, keepdims=True)
        rank_acc = jnp.where(lane_o == kk, rk, rank_acc)
    rank_ref[...] = rank_acc.astype(jnp.int32)
    run_ref[...] = run_ref[...] + jnp.sum(chosen, axis=0, keepdims=True)
    cnt_ref[...] = run_ref[...]

    act = (_silu(_dg(hb, sw1_ref[...])) * _dg(hb, sw3_ref[...])).astype(BF16)
    base_ref[...] = x1 + g2 * _dg(act, sw2_ref[...])


def _mix(x, yf, yb, g, bonus, yconv, mod, gnw, gnb, bsum, wout, n2g, rw, rb, sw1, sw3, sw2):
    t = x.shape[0]
    tm = TOK_TILE
    off = (yf.shape[1] - t) // tm
    full = lambda arr: pl.BlockSpec(arr.shape, lambda i: (0,) * arr.ndim)
    consts = (mod, gnw, gnb, bsum, wout, n2g, rw, rb, sw1, sw3, sw2)
    cat_spec = lambda w: pl.BlockSpec((tm, w), lambda i: (i + off, 0))
    tok_spec = lambda w: pl.BlockSpec((tm, w), lambda i: (i, 0))
    y_spec = pl.BlockSpec((N_PAIRS, tm, LANES), lambda i: (0, i + off, 0))
    return pl.pallas_call(
        _mix_kernel,
        grid=(t // tm,),
        in_specs=[tok_spec(D_MODEL), y_spec, y_spec,
                  cat_spec(D_RWKV), cat_spec(D_RWKV), cat_spec(D_CONV)]
                 + [full(c) for c in consts],
        out_specs=[tok_spec(D_MODEL), tok_spec(HALF), tok_spec(LANES), tok_spec(LANES), tok_spec(LANES),
                   pl.BlockSpec((8, N_EXPERTS), lambda i: (0, 0))],
        out_shape=[jax.ShapeDtypeStruct((t, D_MODEL), F32), jax.ShapeDtypeStruct((t, HALF), I32),
                   jax.ShapeDtypeStruct((t, LANES), jnp.int32), jax.ShapeDtypeStruct((t, LANES), F32),
                   jax.ShapeDtypeStruct((t, LANES), jnp.int32), jax.ShapeDtypeStruct((8, N_EXPERTS), F32)],
        scratch_shapes=[pltpu.VMEM((8, N_EXPERTS), F32)],
        compiler_params=_cparams(("arbitrary",)),
        name="mix",
    )(x, yf, yb, g, bonus, yconv, *consts)


def _slots_kernel(idx_ref, rank_ref, start_ref, dest_ref):
    tm = idx_ref.shape[0]
    lane_e = lax.broadcasted_iota(jnp.int32, (tm, N_EXPERTS), 1)
    lane_o = lax.broadcasted_iota(jnp.int32, (tm, LANES), 1)
    idx = idx_ref[...]
    start = start_ref[...]
    acc = jnp.zeros((tm, LANES), F32)
    for kk in range(TOP_K):
        st = jnp.sum(jnp.where(lane_e == idx[:, kk:kk + 1], start, 0.0), axis=-1, keepdims=True)
        acc = jnp.where(lane_o == kk, st, acc)
    dest_ref[...] = acc.astype(jnp.int32) + rank_ref[...]


def _slots(idx, rank, pad_start):
    t = idx.shape[0]
    tm = SLOTS_TILE
    spec = pl.BlockSpec((tm, LANES), lambda i: (i, 0))
    return pl.pallas_call(
        _slots_kernel,
        grid=(t // tm,),
        in_specs=[spec, spec, pl.BlockSpec((1, N_EXPERTS), lambda i: (0, 0))],
        out_specs=spec,
        out_shape=jax.ShapeDtypeStruct((t, LANES), jnp.int32),
        compiler_params=_cparams(("arbitrary",)),
        name="slots",
    )(idx, rank, pad_start)


def _sc_scatter_rows(rows, dest_km, n_slots):
    info = plsc.get_sparse_core_info()
    nc, ns = info.num_cores, info.num_subcores
    nw = nc * ns
    t, d = rows.shape
    nk = dest_km.shape[0]
    ch = SC_SCATTER_CHUNK
    per_w = t // nw
    n_ch = per_w // ch
    assert t % nw == 0 and per_w % ch == 0
    mesh = plsc.VectorSubcoreMesh(core_axis_name="c", subcore_axis_name="s")
    idx = dest_km.reshape(nk, nw, n_ch, ch).transpose(1, 2, 0, 3).reshape(nw, n_ch * nk, ch)

    @functools.partial(
        pl.kernel, mesh=mesh,
        out_type=jax.ShapeDtypeStruct((n_slots, d), rows.dtype),
        scratch_types=[pltpu.VMEM((n_ch * nk, ch), jnp.int32),
                       pltpu.VMEM((ch, d), rows.dtype),
                       pltpu.SemaphoreType.DMA],
    )
    def scatter_kernel(rows_hbm, idx_hbm, out_hbm, idx_v, rows_v, sem):
        wid = lax.axis_index("s") * nc + lax.axis_index("c")
        base = wid * per_w
        pltpu.sync_copy(idx_hbm.at[wid], idx_v)

        @pl.loop(0, n_ch)
        def _(j):
            pltpu.sync_copy(rows_hbm.at[pl.ds(base + j * ch, ch)], rows_v)
            copies = [pltpu.make_async_copy(rows_v, out_hbm.at[idx_v.at[j * nk + kk]], sem) for kk in range(nk)]
            for cp in copies:
                cp.start()
            for cp in copies:
                cp.wait()

    return scatter_kernel(rows, idx)


def _experts_kernel(be_ref, nu_ref, nv_ref, hs_ref, w1_ref, w3_ref, w2_ref, o_ref, w1c, w3c, w2c):
    b = pl.program_id(0)
    new_expert = jnp.logical_or(b == 0, be_ref[b] != be_ref[jnp.maximum(b - 1, 0)])

    @pl.when(new_expert)
    def _():
        w1c[...] = w1_ref[...].astype(BF16)
        w3c[...] = w3_ref[...].astype(BF16)
        w2c[...] = w2_ref[...].astype(BF16)

    @pl.when(b < nu_ref[0])
    def _():
        live = lax.broadcasted_iota(jnp.int32, (hs_ref.shape[0], 1), 0) < nv_ref[b]
        lo, hi = _unpack_bf16_pairs(jnp.where(live, hs_ref[...], 0))
        lo = lo.astype(BF16)
        hi = hi.astype(BF16)
        a1 = _dg(lo, w1c[0:HALF, :]) + _dg(hi, w1c[HALF:, :])
        a3 = _dg(lo, w3c[0:HALF, :]) + _dg(hi, w3c[HALF:, :])
        act = (_silu(a1) * a3).astype(BF16)
        y = _dg(act, w2c[...]).astype(BF16).astype(F32)
        o_ref[...] = _pack_bf16_pairs(y)

    @pl.when(b >= nu_ref[0])
    def _():
        o_ref[...] = jnp.zeros(o_ref.shape, I32)


def _experts(block_e, n_used, n_valid, hs, w1, w3, w2):
    nblk = block_e.shape[0]
    sb = SLOT_BLOCK
    row_map = lambda b, be, nu, nv: (jnp.minimum(b, nu[0] - 1), 0)
    w_map = lambda b, be, nu, nv: (be[b], 0, 0)
    grid_spec = pltpu.PrefetchScalarGridSpec(
        num_scalar_prefetch=3,
        grid=(nblk,),
        in_specs=[pl.BlockSpec((sb, HALF), row_map),
                  pl.BlockSpec((None, D_MODEL, D_EXPERT), w_map),
                  pl.BlockSpec((None, D_MODEL, D_EXPERT), w_map),
                  pl.BlockSpec((None, D_EXPERT, D_MODEL), w_map)],
        out_specs=pl.BlockSpec((sb, HALF), lambda b, be, nu, nv: (b, 0)),
        scratch_shapes=[pltpu.VMEM((D_MODEL, D_EXPERT), BF16), pltpu.VMEM((D_MODEL, D_EXPERT), BF16),
                        pltpu.VMEM((D_EXPERT, D_MODEL), BF16)],
    )
    return pl.pallas_call(
        _experts_kernel,
        grid_spec=grid_spec,
        out_shape=jax.ShapeDtypeStruct((nblk * sb, HALF), I32),
        compiler_params=_cparams(("arbitrary",)),
        name="experts",
    )(block_e, n_used, n_valid, hs, w1, w3, w2)


def _sc_gather_rows(table, idx):
    info = plsc.get_sparse_core_info()
    nc, ns = info.num_cores, info.num_subcores
    nw = nc * ns
    n = idx.shape[0]
    d = table.shape[1]
    ch = SC_CHUNK
    per_w = n // nw
    n_ch = per_w // ch
    assert n % nw == 0 and per_w % (2 * ch) == 0
    mesh = plsc.VectorSubcoreMesh(core_axis_name="c", subcore_axis_name="s")

    @functools.partial(
        pl.kernel, mesh=mesh,
        out_type=jax.ShapeDtypeStruct((n, d), table.dtype),
        scratch_types=[pltpu.VMEM((n_ch, ch), jnp.int32),
                       pltpu.VMEM((2, ch, d), table.dtype),
                       pltpu.SemaphoreType.DMA((2,))],
    )
    def gather_kernel(table_hbm, idx_hbm, out_hbm, idx_v, rows_v, sem):
        wid = lax.axis_index("s") * nc + lax.axis_index("c")
        base = wid * per_w
        pltpu.sync_copy(idx_hbm.at[wid], idx_v)

        def gather(j, buf):
            return pltpu.make_async_copy(table_hbm.at[idx_v.at[j]], rows_v.at[buf], sem.at[buf])

        gather(0, 0).start()

        @pl.loop(0, n_ch, step=2)
        def _(j):
            for buf in range(2):
                jj = j + buf
                gather(jj, buf).wait()

                @pl.when(jj + 1 < n_ch)
                def _():
                    gather(jj + 1, 1 - buf).start()

                pltpu.sync_copy(rows_v.at[buf], out_hbm.at[pl.ds(base + jj * ch, ch)])

    return gather_kernel(table, idx.reshape(nw, n_ch, ch))


def _combine_kernel(rows_ref, base_ref, gate_ref, mod_ref, fg_ref, o_ref):
    tm = base_ref.shape[0]
    gate = gate_ref[...]
    acc_lo = jnp.zeros((tm, HALF), F32)
    acc_hi = jnp.zeros((tm, HALF), F32)
    for kk in range(TOP_K):
        lo, hi = _unpack_bf16_pairs(rows_ref[kk])
        acc_lo = acc_lo + gate[:, kk:kk + 1] * lo
        acc_hi = acc_hi + gate[:, kk:kk + 1] * hi
    acc = jnp.concatenate([acc_lo, acc_hi], axis=1)
    g2 = mod_ref[0:1, 5 * D_MODEL:6 * D_MODEL]
    o_ref[...] = _rms(base_ref[...] + g2 * acc, fg_ref[...])


def _combine(dest_km, ye, base, gate, mod, fg):
    t = base.shape[0]
    tm = TOK_TILE
    rows = _sc_gather_rows(ye, dest_km.reshape(-1)).reshape(TOP_K, t, HALF)
    full = lambda arr: pl.BlockSpec(arr.shape, lambda i: (0,) * arr.ndim)
    return pl.pallas_call(
        _combine_kernel,
        grid=(t // tm,),
        in_specs=[pl.BlockSpec((TOP_K, tm, HALF), lambda i: (0, i, 0)),
                  pl.BlockSpec((tm, D_MODEL), lambda i: (i, 0)),
                  pl.BlockSpec((tm, LANES), lambda i: (i, 0)),
                  full(mod), full(fg)],
        out_specs=pl.BlockSpec((tm, D_MODEL), lambda i: (i, 0)),
        out_shape=jax.ShapeDtypeStruct((t, D_MODEL), F32),
        compiler_params=_cparams(("arbitrary",)),
        name="combine",
    )(rows, base, gate, mod, fg)


def _block_tables(counts, n_blocks):
    sb = SLOT_BLOCK
    padded = (counts + sb - 1) // sb * sb
    pad_end = jnp.cumsum(padded)
    pad_start = pad_end - padded
    n_used = pad_end[-1] // sb
    blk = jnp.minimum(jnp.arange(n_blocks, dtype=jnp.int32), n_used - 1)
    block_e = jnp.minimum(jnp.sum(pad_end[None, :] <= (blk * sb)[:, None], axis=1), N_EXPERTS - 1).astype(jnp.int32)
    onehot = block_e[:, None] == jnp.arange(N_EXPERTS, dtype=jnp.int32)[None, :]
    live_end = jnp.sum(jnp.where(onehot, (pad_start + counts)[None, :], 0), axis=1)
    n_valid = jnp.clip(live_end - blk * sb, 0, sb).astype(jnp.int32)
    return block_e, n_used.reshape(1).astype(jnp.int32), n_valid, pad_start


def _blockdiag2(w):
    z = jnp.zeros_like(w[0])
    return jnp.concatenate([jnp.concatenate([w[0], z], axis=1), jnp.concatenate([z, w[1]], axis=1)], axis=0)


def _pad_rows(w, n=8):
    return jnp.concatenate([w, jnp.zeros((n - w.shape[0],) + w.shape[1:], w.dtype)], axis=0)


def kernel(x, c, ctx, c_ctx, ada_w, ada_b, norm1_g, norm2_g, w_in, conv_w, shift_mu, decay_w0, decay_up, iclr_a0, iclr_up, key_xi, key_alpha, bonus_rho, gate_up, gn_w, gn_b, w_out, router_w, router_bias, exp_w1, exp_w3, exp_w2, sh_w1, sh_w3, sh_w2, final_g):
    assert x.shape[0] == 1 and ada_w.shape[0] == 1
    assert ctx.shape[1] == TOK_TILE and x.shape[1] % TOK_TILE == 0
    l = 0
    xs = x[0]
    row = lambda w: w.reshape(1, -1)

    cc = _pad_rows(jnp.stack([c[0], c_ctx], axis=0))
    mod = _ada(cc, ada_w[l], row(ada_b[l]))

    hid = lax.broadcasted_iota(jnp.int32, (D_RWKV, D_RWKV), 0) // HEAD_DIM
    bsum = (hid == hid.T).astype(BF16)

    r4, v4, g, bonus, yconv, lw, kh, kt, a = _proj(
        ctx[0], xs, mod, row(norm1_g[l]), w_in[l].astype(BF16), _pad_rows(conv_w[l]), _pad_rows(shift_mu[l]),
        row(decay_w0[l]), _blockdiag2(decay_up[l]).astype(BF16), row(iclr_a0[l]), _blockdiag2(iclr_up[l]).astype(BF16),
        row(key_xi[l]), row(key_alpha[l]), row(bonus_rho[l]), gate_up[l].astype(BF16), bsum)

    yf, yb = _carry(*_chunks(r4, v4, lw, kh, kt, a))

    base, h2, idx, gate, rank, cnt = _mix(
        xs, yf, yb, g, bonus, yconv, mod, row(gn_w[l]), row(gn_b[l]), bsum, w_out[l].astype(BF16),
        row(norm2_g[l]), router_w[l], row(router_bias[l]),
        sh_w1[l].astype(BF16), sh_w3[l].astype(BF16), sh_w2[l].astype(BF16))

    n_blocks = xs.shape[0] * TOP_K // SLOT_BLOCK + N_EXPERTS
    block_e, n_used, n_valid, pad_start = _block_tables(cnt[0].astype(jnp.int32), n_blocks)
    dest_km = _slots(idx, rank, pad_start.astype(F32).reshape(1, N_EXPERTS))[:, :TOP_K].T
    hs = _sc_scatter_rows(h2, dest_km, n_blocks * SLOT_BLOCK)
    ye = _experts(block_e, n_used, n_valid, hs, exp_w1[l], exp_w3[l], exp_w2[l])
    out = _combine(dest_km, ye, base, gate, mod, row(final_g))
    return out[None]
```

```python
import functools

import jax
import jax.numpy as jnp
from jax import lax
from jax.experimental import pallas as pl
from jax.experimental.pallas import tpu as pltpu
from jax.experimental.pallas import tpu_sc as plsc

F32 = jnp.float32
BF16 = jnp.bfloat16

D_MODEL = 1024
D_CONV = 512
D_RWKV = 512
HEAD_DIM = 64
N_HEADS = D_RWKV // HEAD_DIM
N_PAIRS = N_HEADS // 2
LORA = 128
P_RWKV = 3 * D_RWKV + 3 * LORA
P_IN = 3 * D_CONV + P_RWKV
GRID_W = 64
N_EXPERTS = 256
TOP_K = 8
D_EXPERT = 256
ROUTED_SCALE = 2.5
RMS_EPS = 1e-6
GN_EPS = 64e-5
DECAY_SCALE = 0.6065306597126334
NORM_EPS = 1e-12

TOK_TILE = 256
SLOTS_TILE = 1024
SCAN_BLOCK = 256
CHUNK = 64
SLOT_BLOCK = 512
SC_CHUNK = 64
SC_SCATTER_CHUNK = 128
LANES = 128
VMEM_LIMIT = 56 * 1024 * 1024

NN = ((1,), (0,))
NT = ((1,), (1,))
TN = ((0,), (0,))


def _dg(a, b, dims=NN):
    return lax.dot_general(a, b, (dims, ((), ())), preferred_element_type=F32)


def _split2(a):
    hi = a.astype(BF16)
    lo = (a - hi.astype(F32)).astype(BF16)
    return hi, lo


def _split3(a):
    hi = a.astype(BF16)
    r1 = a - hi.astype(F32)
    mid = r1.astype(BF16)
    lo = (r1 - mid.astype(F32)).astype(BF16)
    return hi, mid, lo


def _mm1(a, b, dims=NN):
    return _dg(a.astype(BF16), b.astype(BF16), dims)


def _mm3(a, b, dims=NN):
    ah, al = _split2(a)
    bh, bl = _split2(b)
    return _dg(ah, bh, dims) + (_dg(ah, bl, dims) + _dg(al, bh, dims))


def _mm_split_lhs(a, b_bf16, dims=NN):
    h, l = _split2(a)
    return _dg(h, b_bf16, dims) + _dg(l, b_bf16, dims)


def _mm_ones_rhs(a, ones_bf16):
    return _dg(a.astype(BF16), ones_bf16)


def _mm_ones_lhs(ones_bf16, b, dims=NN):
    h, m, l = _split3(b)
    return _dg(ones_bf16, h, dims) + (_dg(ones_bf16, m, dims) + _dg(ones_bf16, l, dims))


HALF = D_MODEL // 2
I32 = jnp.int32
HI_MASK = -65536


def _pack_bf16_pairs(x):
    u = lax.bitcast_convert_type(x, I32)
    return lax.shift_right_logical(u[:, 0:HALF], jnp.int32(16)) | (u[:, HALF:] & jnp.int32(HI_MASK))


def _unpack_bf16_pairs(w):
    lo = lax.bitcast_convert_type(lax.shift_left(w, jnp.int32(16)), F32)
    hi = lax.bitcast_convert_type(w & jnp.int32(HI_MASK), F32)
    return lo, hi


def _sigmoid(x):
    return 1.0 / (1.0 + jnp.exp(-x))


def _silu(x):
    return x * _sigmoid(x)


def _rms(xv, g):
    ms = jnp.mean(xv * xv, axis=-1, keepdims=True)
    return xv * lax.rsqrt(ms + RMS_EPS) * g


def _cparams(sem):
    return pltpu.CompilerParams(dimension_semantics=sem, vmem_limit_bytes=VMEM_LIMIT)


def _ada_kernel(c_ref, w_ref, b_ref, o_ref):
    o_ref[...] = _mm3(_silu(c_ref[...]), w_ref[...]) + b_ref[...]


def _ada(cc, ada_w, ada_b):
    n = ada_w.shape[1]
    tn = 1024
    return pl.pallas_call(
        _ada_kernel,
        grid=(n // tn,),
        in_specs=[pl.BlockSpec((8, D_MODEL), lambda j: (0, 0)),
                  pl.BlockSpec((D_MODEL, tn), lambda j: (0, j)),
                  pl.BlockSpec((1, tn), lambda j: (0, j))],
        out_specs=pl.BlockSpec((8, tn), lambda j: (0, j)),
        out_shape=jax.ShapeDtypeStruct((8, n), F32),
        compiler_params=_cparams(("arbitrary",)),
        name="ada",
    )(cc, ada_w, ada_b)


def _proj_kernel(ctx_ref, x_ref, xp_ref, xn_ref, mod_ref, n1g_ref, win_ref, convw_ref, mu_ref,
                 dw0_ref, dup_ref, ia0_ref, iup_ref, xi_ref, kal_ref, rho_ref, gup_ref, bsum_ref,
                 r_ref, v_ref, g_ref, bonus_ref, yconv_ref, lw_ref, kh_ref, kt_ref, a_ref):
    i = pl.program_id(0)
    nt = pl.num_programs(0)
    tm = x_ref.shape[0]
    is_ctx = i == 0
    sh = jnp.where(is_ctx, mod_ref[1:2, 0:D_MODEL], mod_ref[0:1, 0:D_MODEL])
    sc = jnp.where(is_ctx, mod_ref[1:2, D_MODEL:2 * D_MODEL], mod_ref[0:1, D_MODEL:2 * D_MODEL])
    n1g = n1g_ref[...]

    def norm_mod(xv):
        return _rms(xv, n1g) * (1.0 + sc) + sh

    xt = jnp.where(is_ctx, ctx_ref[...], x_ref[...])
    h = norm_mod(jnp.concatenate([xt, xp_ref[...], xn_ref[...]], axis=0)).astype(BF16)
    p_all = _dg(h, win_ref[...])
    p = p_all[0:tm]
    ph = p_all[tm:tm + 16, 3 * D_CONV:]

    rows = lax.broadcasted_iota(jnp.int32, (tm, 1), 0)

    bg = p[:, 0:D_CONV]
    z = p[:, D_CONV:2 * D_CONV] * p[:, 2 * D_CONV:3 * D_CONV]
    col = rows % GRID_W
    zp = jnp.where(col == 0, 0.0, pltpu.roll(z, 1, 0))
    zn = jnp.where(col == GRID_W - 1, 0.0, pltpu.roll(z, tm - 1, 0))
    yconv = bg * (convw_ref[0:1, :] * zp + convw_ref[1:2, :] * z + convw_ref[2:3, :] * zn)
    yconv_ref[...] = yconv.astype(BF16)

    prev_ok = jnp.logical_and(i != 0, i != 1).astype(F32)
    next_ok = jnp.logical_and(i != 0, i != nt - 1).astype(F32)
    cur = p[:, 3 * D_CONV:]
    prev = jnp.where(rows == 0, ph[7:8, :] * prev_ok, pltpu.roll(cur, 1, 0))
    nxt = jnp.where(rows == tm - 1, ph[8:9, :] * next_ok, pltpu.roll(cur, tm - 1, 0))
    ps = cur + mu_ref[0:1, :] * (prev - cur) + mu_ref[1:2, :] * (nxt - cur)

    r = ps[:, 0:D_RWKV]
    k = ps[:, D_RWKV:2 * D_RWKV]
    v = ps[:, 2 * D_RWKV:3 * D_RWKV]
    o = 3 * D_RWKV
    wlo = ps[:, o:o + LORA]
    alo = ps[:, o + LORA:o + 2 * LORA]
    glo = ps[:, o + 2 * LORA:o + 3 * LORA]

    dd = dw0_ref[...] + _mm_split_lhs(jnp.tanh(wlo), dup_ref[...])
    lw = -DECAY_SCALE * _sigmoid(dd)
    a = _sigmoid(ia0_ref[...] + _mm_split_lhs(alo, iup_ref[...]))
    g = _dg(_sigmoid(glo).astype(BF16), gup_ref[...])
    k2 = jnp.concatenate([k, k], axis=1)
    kap = k2 * xi_ref[...]
    kap2 = kap * kap
    bs = bsum_ref[...]
    ss = jnp.concatenate([_mm_ones_rhs(kap2[:, 0:D_RWKV], bs),
                          _mm_ones_rhs(kap2[:, D_RWKV:], bs)], axis=1)
    kh = kap * lax.rsqrt(ss + NORM_EPS)
    kt = k2 * (1.0 + (a - 1.0) * kal_ref[...])
    bon = _mm_ones_rhs(r * rho_ref[...] * (kt[:, 0:D_RWKV] + kt[:, D_RWKV:]), bs)

    g_ref[...] = g
    bonus_ref[...] = bon * v
    for pr in range(N_PAIRS):
        ls = slice(pr * LANES, (pr + 1) * LANES)
        r_ref[pr] = r[:, ls]
        v_ref[pr] = v[:, ls]
        for d in range(2):
            ld = slice(d * D_RWKV + pr * LANES, d * D_RWKV + (pr + 1) * LANES)
            lw_ref[d, pr] = lw[:, ld]
            kh_ref[d, pr] = kh[:, ld]
            kt_ref[d, pr] = kt[:, ld]
            a_ref[d, pr] = a[:, ld]


def _proj(ctx2, x2, mod, n1g, win, convw, mu, dw0, dup, ia0, iup, xi, kal, rho, gup, bsum):
    tm = TOK_TILE
    t = ctx2.shape[0] + x2.shape[0]
    nt = t // tm
    per = tm // 8
    nb8 = x2.shape[0] // 8
    full = lambda arr: pl.BlockSpec(arr.shape, lambda i: (0,) * arr.ndim)
    pair_spec = pl.BlockSpec((N_PAIRS, tm, LANES), lambda i: (0, i, 0))
    dpair_spec = pl.BlockSpec((2, N_PAIRS, tm, LANES), lambda i: (0, 0, i, 0))
    row_spec = pl.BlockSpec((tm, D_RWKV), lambda i: (i, 0))
    pair_shape = jax.ShapeDtypeStruct((N_PAIRS, t, LANES), F32)
    dpair_shape = jax.ShapeDtypeStruct((2, N_PAIRS, t, LANES), F32)
    consts = (mod, n1g, win, convw, mu, dw0, dup, ia0, iup, xi, kal, rho, gup, bsum)
    return pl.pallas_call(
        _proj_kernel,
        grid=(nt,),
        in_specs=[pl.BlockSpec((tm, D_MODEL), lambda i: (0, 0)),
                  pl.BlockSpec((tm, D_MODEL), lambda i: (jnp.maximum(i - 1, 0), 0)),
                  pl.BlockSpec((8, D_MODEL), lambda i: (jnp.maximum((i - 1) * per - 1, 0), 0)),
                  pl.BlockSpec((8, D_MODEL), lambda i: (jnp.clip(i * per, 0, nb8 - 1), 0))]
                 + [full(c) for c in consts],
        out_specs=[pair_spec, pair_spec, row_spec, row_spec, row_spec,
                   dpair_spec, dpair_spec, dpair_spec, dpair_spec],
        out_shape=[pair_shape, pair_shape,
                   jax.ShapeDtypeStruct((t, D_RWKV), F32), jax.ShapeDtypeStruct((t, D_RWKV), F32),
                   jax.ShapeDtypeStruct((t, D_CONV), BF16),
                   dpair_shape, dpair_shape, dpair_shape, dpair_shape],
        compiler_params=_cparams(("arbitrary",)),
        name="proj",
    )(ctx2, x2, x2, x2, *consts)


def _scan_kernel(r_ref, v_ref, lw_ref, kh_ref, kt_ref, a_ref, y_ref, q_ref):
    d = pl.program_id(0)
    rev = d == 1
    c = CHUNK
    npair = r_ref.shape[0]
    nsub = r_ref.shape[1] // c

    row = lax.broadcasted_iota(jnp.int32, (c, c), 0)
    col = lax.broadcasted_iota(jnp.int32, (c, c), 1)
    strict = (col - row) * (1 - 2 * d) < 0
    eye = row == col
    incl = jnp.logical_or(strict, eye)
    incl_bf = jnp.where(incl, 1.0, 0.0).astype(BF16)
    eye_f = jnp.where(eye, 1.0, 0.0)
    head0 = lax.broadcasted_iota(jnp.int32, (c, LANES), 1) < HEAD_DIM
    row2 = lax.broadcasted_iota(jnp.int32, (LANES, LANES), 0)
    col2 = lax.broadcasted_iota(jnp.int32, (LANES, LANES), 1)
    same_head = (row2 < HEAD_DIM) == (col2 < HEAD_DIM)
    eye2 = row2 == col2

    subs = range(npair * nsub)
    units = [(s, h) for s in subs for h in range(2)]
    off = [pl.multiple_of(jnp.where(rev, nsub - 1 - s, s) * c, c) for s in range(nsub)]
    sls = [(pp, pl.ds(off[s], c), slice(None)) for pp in range(npair) for s in range(nsub)]
    v = [v_ref[sl] for sl in sls]
    lw = [lw_ref[sl] for sl in sls]
    cum = [_mm_ones_lhs(incl_bf, lw[s]) for s in subs]
    al, be, kk, rr, bew, kkw, wc = [], [], [], [], [], [], []
    for s in subs:
        cum_last = jnp.where(rev, cum[s][0:1, :], cum[s][c - 1:c, :])
        e_end = jnp.exp(cum_last - cum[s])
        e_neg = jnp.exp(-cum[s])
        kh = kh_ref[sls[s]]
        kt = kt_ref[sls[s]]
        kha = kh * a_ref[sls[s]]
        al.append(kh * jnp.exp(cum[s] - lw[s]))
        be.append(-(kha * e_neg))
        kk.append(kt * e_neg)
        rr.append(r_ref[sls[s]] * jnp.exp(cum[s]))
        bew.append(-(kha * e_end))
        kkw.append(kt * e_end)
        wc.append(jnp.exp(cum_last))

    lhs = {}
    for s, h in units:
        hm = head0 if h == 0 else jnp.logical_not(head0)
        lhs[s, h] = jnp.concatenate([jnp.where(hm, al[s], 0.0), jnp.where(hm, rr[s], 0.0)], axis=0)
    gb = {u: _mm1(lhs[u], be[u[0]], NT) for u in units}
    gk = {u: _mm1(lhs[u], kk[u[0]], NT) for u in units}
    a_ab = {u: jnp.where(strict, gb[u][0:c], 0.0) for u in units}
    a_ak = {u: jnp.where(strict, gk[u][0:c], 0.0) for u in units}
    a_rb = {u: jnp.where(incl, gb[u][c:2 * c], 0.0) for u in units}
    a_rk = {u: jnp.where(incl, gk[u][c:2 * c], 0.0) for u in units}

    pw = dict(a_ab)
    tinv = {u: eye_f + a_ab[u] for u in units}
    akrk_v = {u: _mm1(jnp.concatenate([a_ak[u], a_rk[u]], axis=0), v[u[0]]) for u in units}
    avk = {u: akrk_v[u][0:c] for u in units}
    rk_v = {u: akrk_v[u][c:2 * c] for u in units}
    for _ in range(c.bit_length() - 2):
        pw = {u: _mm1(pw[u], pw[u]) for u in units}
        tinv = {u: tinv[u] + _mm1(tinv[u], pw[u]) for u in units}
    tu = {u: _mm1(tinv[u], jnp.concatenate([al[u[0]], avk[u]], axis=1)) for u in units}
    at = [jnp.where(head0, tu[s, 0][:, 0:LANES], tu[s, 1][:, 0:LANES]) for s in subs]
    ut = [jnp.where(head0, tu[s, 0][:, LANES:], tu[s, 1][:, LANES:]) for s in subs]

    rb = {u: _mm1(a_rb[u], jnp.concatenate([at[u[0]], ut[u[0]]], axis=1)) for u in units}
    rb_at = {u: rb[u][:, 0:LANES] for u in units}
    rb_ut = {u: rb[u][:, LANES:] for u in units}
    md = [_mm1(bew[s], jnp.concatenate([at[s], ut[s]], axis=1), TN) for s in subs]
    m_off = [md[s][:, 0:LANES] for s in subs]
    d_u = [md[s][:, LANES:] for s in subs]
    d_v = [_mm1(kkw[s], v[s], TN) for s in subs]
    rh = [rr[s] + jnp.where(head0, rb_at[s, 0], rb_at[s, 1]) for s in subs]
    yh = [jnp.where(head0, rb_ut[s, 0] + rk_v[s, 0], rb_ut[s, 1] + rk_v[s, 1]) for s in subs]
    mt = [jnp.where(eye2, wc[s], 0.0) + jnp.where(same_head, m_off[s], 0.0) for s in subs]
    dt = [jnp.where(same_head, d_u[s] + d_v[s], 0.0) for s in subs]

    @pl.when(pl.program_id(1) == 0)
    def _():
        q_ref[...] = jnp.zeros(q_ref.shape, F32)

    q = [q_ref[pp] for pp in range(npair)]
    for si in range(nsub):
        ys = [_mm1(rh[pp * nsub + si], q[pp]) for pp in range(npair)]
        qn = [_mm1(mt[pp * nsub + si], q[pp]) for pp in range(npair)]
        for pp in range(npair):
            y_ref[sls[pp * nsub + si]] = ys[pp] + yh[pp * nsub + si]
            q[pp] = qn[pp] + dt[pp * nsub + si]
    for pp in range(npair):
        q_ref[pp] = q[pp]


def _scan(r4, v4, lw, kh, kt, a):
    t = r4.shape[1]
    tb = SCAN_BLOCK
    nb = t // tb

    def blk(d, j):
        return jnp.where(d == 0, j, jnp.where(j == 0, 0, nb - j))

    shared = pl.BlockSpec((N_PAIRS, tb, LANES), lambda d, j: (0, blk(d, j), 0))
    perdir = pl.BlockSpec((None, N_PAIRS, tb, LANES), lambda d, j: (d, 0, blk(d, j), 0))
    return pl.pallas_call(
        _scan_kernel,
        grid=(2, nb),
        in_specs=[shared, shared, perdir, perdir, perdir, perdir],
        out_specs=perdir,
        out_shape=jax.ShapeDtypeStruct((2, N_PAIRS, t, LANES), F32),
        scratch_shapes=[pltpu.VMEM((N_PAIRS, LANES, LANES), F32)],
        compiler_params=_cparams(("arbitrary", "arbitrary")),
        name="scan",
    )(r4, v4, lw, kh, kt, a)


def _mix_kernel(x_ref, yf_ref, yb_ref, g_ref, bonus_ref, yconv_ref, mod_ref, gnw_ref, gnb_ref, bsum_ref,
                wout_ref, n2g_ref, rw_ref, rb_ref, sw1_ref, sw3_ref, sw2_ref,
                base_ref, h2_ref, idx_ref, gate_ref, rank_ref, cnt_ref, run_ref):
    tm = x_ref.shape[0]
    g1 = mod_ref[0:1, 2 * D_MODEL:3 * D_MODEL]
    sh2 = mod_ref[0:1, 3 * D_MODEL:4 * D_MODEL]
    sc2 = mod_ref[0:1, 4 * D_MODEL:5 * D_MODEL]
    g2 = mod_ref[0:1, 5 * D_MODEL:6 * D_MODEL]

    @pl.when(pl.program_id(0) == 0)
    def _():
        run_ref[...] = jnp.zeros(run_ref.shape, F32)

    y = jnp.concatenate([yf_ref[pr] + yb_ref[pr] for pr in range(N_PAIRS)], axis=1)
    bs = bsum_ref[...]
    mu = _mm_ones_rhs(y, bs) * (1.0 / HEAD_DIM)
    yc = y - mu
    var = _mm_ones_rhs(yc * yc, bs) * (1.0 / HEAD_DIM)
    yn = yc * lax.rsqrt(var + GN_EPS) * gnw_ref[...] + gnb_ref[...]
    yrw = ((yn + bonus_ref[...]) * g_ref[...]).astype(BF16)
    mix = _dg(yconv_ref[...], wout_ref[0:D_CONV, :]) + _dg(yrw, wout_ref[D_CONV:, :])
    x1 = x_ref[...] + g1 * mix

    h2 = _rms(x1, n2g_ref[...]) * (1.0 + sc2) + sh2
    hb = h2.astype(BF16)
    h2_ref[...] = _pack_bf16_pairs(hb.astype(F32))

    scores = _sigmoid(_mm3(h2, rw_ref[...]))
    work = scores + rb_ref[...]
    lane_e = lax.broadcasted_iota(jnp.int32, (tm, N_EXPERTS), 1).astype(F32)
    lane_o = lax.broadcasted_iota(jnp.int32, (tm, LANES), 1)
    idx_acc = jnp.zeros((tm, LANES), F32)
    gate_acc = jnp.zeros((tm, LANES), F32)
    gsum = jnp.zeros((tm, 1), F32)
    chosen = jnp.zeros((tm, N_EXPERTS), F32)
    sels = []
    for kk in range(TOP_K):
        m = jnp.max(work, axis=-1, keepdims=True)
        sel = jnp.min(jnp.where(work == m, lane_e, float(N_EXPERTS)), axis=-1, keepdims=True)
        hit = lane_e == sel
        sk = jnp.sum(jnp.where(hit, scores, 0.0), axis=-1, keepdims=True)
        idx_acc = jnp.where(lane_o == kk, sel, idx_acc)
        gate_acc = jnp.where(lane_o == kk, sk, gate_acc)
        gsum = gsum + sk
        work = jnp.where(hit, -jnp.inf, work)
        chosen = jnp.where(hit, 1.0, chosen)
        sels.append(sel)
    idx_ref[...] = idx_acc.astype(jnp.int32)
    gate_ref[...] = gate_acc / gsum * ROUTED_SCALE

    trow = lax.broadcasted_iota(jnp.int32, (tm, tm), 0)
    tcol = lax.broadcasted_iota(jnp.int32, (tm, tm), 1)
    earlier = jnp.where(tcol < trow, 1.0, 0.0).astype(BF16)
    before = _dg(earlier, chosen.astype(BF16)) + run_ref[0:1, :]
    rank_acc = jnp.zeros((tm, LANES), F32)
    for kk in range(TOP_K):
        rk = jnp.sum(jnp.where(lane_e == sels[kk], before, 0.0), axis=-1, keepdims=True)
        rank_acc = jnp.where(lane_o == kk, rk, rank_acc)
    rank_ref[...] = rank_acc.astype(jnp.int32)
    run_ref[...] = run_ref[...] + jnp.sum(chosen, axis=0, keepdims=True)
    cnt_ref[...] = run_ref[...]

    act = (_silu(_dg(hb, sw1_ref[...])) * _dg(hb, sw3_ref[...])).astype(BF16)
    base_ref[...] = x1 + g2 * _dg(act, sw2_ref[...])


def _mix(x, y, g, bonus, yconv, mod, gnw, gnb, bsum, wout, n2g, rw, rb, sw1, sw3, sw2):
    t = x.shape[0]
    tm = TOK_TILE
    off = (y.shape[2] - t) // tm
    full = lambda arr: pl.BlockSpec(arr.shape, lambda i: (0,) * arr.ndim)
    consts = (mod, gnw, gnb, bsum, wout, n2g, rw, rb, sw1, sw3, sw2)
    cat_spec = lambda w: pl.BlockSpec((tm, w), lambda i: (i + off, 0))
    tok_spec = lambda w: pl.BlockSpec((tm, w), lambda i: (i, 0))
    yf_spec = pl.BlockSpec((None, N_PAIRS, tm, LANES), lambda i: (0, 0, i + off, 0))
    yb_spec = pl.BlockSpec((None, N_PAIRS, tm, LANES), lambda i: (1, 0, i + off, 0))
    return pl.pallas_call(
        _mix_kernel,
        grid=(t // tm,),
        in_specs=[tok_spec(D_MODEL), yf_spec, yb_spec,
                  cat_spec(D_RWKV), cat_spec(D_RWKV), cat_spec(D_CONV)]
                 + [full(c) for c in consts],
        out_specs=[tok_spec(D_MODEL), tok_spec(HALF), tok_spec(LANES), tok_spec(LANES), tok_spec(LANES),
                   pl.BlockSpec((8, N_EXPERTS), lambda i: (0, 0))],
        out_shape=[jax.ShapeDtypeStruct((t, D_MODEL), F32), jax.ShapeDtypeStruct((t, HALF), I32),
                   jax.ShapeDtypeStruct((t, LANES), jnp.int32), jax.ShapeDtypeStruct((t, LANES), F32),
                   jax.ShapeDtypeStruct((t, LANES), jnp.int32), jax.ShapeDtypeStruct((8, N_EXPERTS), F32)],
        scratch_shapes=[pltpu.VMEM((8, N_EXPERTS), F32)],
        compiler_params=_cparams(("arbitrary",)),
        name="mix",
    )(x, y, y, g, bonus, yconv, *consts)


def _slots_kernel(idx_ref, rank_ref, start_ref, dest_ref):
    tm = idx_ref.shape[0]
    lane_e = lax.broadcasted_iota(jnp.int32, (tm, N_EXPERTS), 1)
    lane_o = lax.broadcasted_iota(jnp.int32, (tm, LANES), 1)
    idx = idx_ref[...]
    start = start_ref[...]
    acc = jnp.zeros((tm, LANES), F32)
    for kk in range(TOP_K):
        st = jnp.sum(jnp.where(lane_e == idx[:, kk:kk + 1], start, 0.0), axis=-1, keepdims=True)
        acc = jnp.where(lane_o == kk, st, acc)
    dest_ref[...] = acc.astype(jnp.int32) + rank_ref[...]


def _slots(idx, rank, pad_start):
    t = idx.shape[0]
    tm = SLOTS_TILE
    spec = pl.BlockSpec((tm, LANES), lambda i: (i, 0))
    return pl.pallas_call(
        _slots_kernel,
        grid=(t // tm,),
        in_specs=[spec, spec, pl.BlockSpec((1, N_EXPERTS), lambda i: (0, 0))],
        out_specs=spec,
        out_shape=jax.ShapeDtypeStruct((t, LANES), jnp.int32),
        compiler_params=_cparams(("arbitrary",)),
        name="slots",
    )(idx, rank, pad_start)


def _sc_scatter_rows(rows, dest_km, n_slots):
    info = plsc.get_sparse_core_info()
    nc, ns = info.num_cores, info.num_subcores
    nw = nc * ns
    t, d = rows.shape
    nk = dest_km.shape[0]
    ch = SC_SCATTER_CHUNK
    per_w = t // nw
    n_ch = per_w // ch
    assert t % nw == 0 and per_w % ch == 0
    mesh = plsc.VectorSubcoreMesh(core_axis_name="c", subcore_axis_name="s")
    idx = dest_km.reshape(nk, nw, n_ch, ch).transpose(1, 2, 0, 3).reshape(nw, n_ch * nk, ch)

    @functools.partial(
        pl.kernel, mesh=mesh,
        out_type=jax.ShapeDtypeStruct((n_slots, d), rows.dtype),
        scratch_types=[pltpu.VMEM((n_ch * nk, ch), jnp.int32),
                       pltpu.VMEM((ch, d), rows.dtype),
                       pltpu.SemaphoreType.DMA],
    )
    def scatter_kernel(rows_hbm, idx_hbm, out_hbm, idx_v, rows_v, sem):
        wid = lax.axis_index("s") * nc + lax.axis_index("c")
        base = wid * per_w
        pltpu.sync_copy(idx_hbm.at[wid], idx_v)

        @pl.loop(0, n_ch)
        def _(j):
            pltpu.sync_copy(rows_hbm.at[pl.ds(base + j * ch, ch)], rows_v)
            copies = [pltpu.make_async_copy(rows_v, out_hbm.at[idx_v.at[j * nk + kk]], sem) for kk in range(nk)]
            for cp in copies:
                cp.start()
            for cp in copies:
                cp.wait()

    return scatter_kernel(rows, idx)


def _experts_kernel(be_ref, nu_ref, nv_ref, hs_ref, w1_ref, w3_ref, w2_ref, o_ref, w1c, w3c, w2c):
    b = pl.program_id(0)
    new_expert = jnp.logical_or(b == 0, be_ref[b] != be_ref[jnp.maximum(b - 1, 0)])

    @pl.when(new_expert)
    def _():
        w1c[...] = w1_ref[...].astype(BF16)
        w3c[...] = w3_ref[...].astype(BF16)
        w2c[...] = w2_ref[...].astype(BF16)

    @pl.when(b < nu_ref[0])
    def _():
        live = lax.broadcasted_iota(jnp.int32, (hs_ref.shape[0], 1), 0) < nv_ref[b]
        lo, hi = _unpack_bf16_pairs(jnp.where(live, hs_ref[...], 0))
        lo = lo.astype(BF16)
        hi = hi.astype(BF16)
        a1 = _dg(lo, w1c[0:HALF, :]) + _dg(hi, w1c[HALF:, :])
        a3 = _dg(lo, w3c[0:HALF, :]) + _dg(hi, w3c[HALF:, :])
        act = (_silu(a1) * a3).astype(BF16)
        y = _dg(act, w2c[...]).astype(BF16).astype(F32)
        o_ref[...] = _pack_bf16_pairs(y)

    @pl.when(b >= nu_ref[0])
    def _():
        o_ref[...] = jnp.zeros(o_ref.shape, I32)


def _experts(block_e, n_used, n_valid, hs, w1, w3, w2):
    nblk = block_e.shape[0]
    sb = SLOT_BLOCK
    row_map = lambda b, be, nu, nv: (jnp.minimum(b, nu[0] - 1), 0)
    w_map = lambda b, be, nu, nv: (be[b], 0, 0)
    grid_spec = pltpu.PrefetchScalarGridSpec(
        num_scalar_prefetch=3,
        grid=(nblk,),
        in_specs=[pl.BlockSpec((sb, HALF), row_map),
                  pl.BlockSpec((None, D_MODEL, D_EXPERT), w_map),
                  pl.BlockSpec((None, D_MODEL, D_EXPERT), w_map),
                  pl.BlockSpec((None, D_EXPERT, D_MODEL), w_map)],
        out_specs=pl.BlockSpec((sb, HALF), lambda b, be, nu, nv: (b, 0)),
        scratch_shapes=[pltpu.VMEM((D_MODEL, D_EXPERT), BF16), pltpu.VMEM((D_MODEL, D_EXPERT), BF16),
                        pltpu.VMEM((D_EXPERT, D_MODEL), BF16)],
    )
    return pl.pallas_call(
        _experts_kernel,
        grid_spec=grid_spec,
        out_shape=jax.ShapeDtypeStruct((nblk * sb, HALF), I32),
        compiler_params=_cparams(("arbitrary",)),
        name="experts",
    )(block_e, n_used, n_valid, hs, w1, w3, w2)


def _sc_gather_rows(table, idx):
    info = plsc.get_sparse_core_info()
    nc, ns = info.num_cores, info.num_subcores
    nw = nc * ns
    n = idx.shape[0]
    d = table.shape[1]
    ch = SC_CHUNK
    per_w = n // nw
    n_ch = per_w // ch
    assert n % nw == 0 and per_w % (2 * ch) == 0
    mesh = plsc.VectorSubcoreMesh(core_axis_name="c", subcore_axis_name="s")

    @functools.partial(
        pl.kernel, mesh=mesh,
        out_type=jax.ShapeDtypeStruct((n, d), table.dtype),
        scratch_types=[pltpu.VMEM((n_ch, ch), jnp.int32),
                       pltpu.VMEM((2, ch, d), table.dtype),
                       pltpu.SemaphoreType.DMA((2,))],
    )
    def gather_kernel(table_hbm, idx_hbm, out_hbm, idx_v, rows_v, sem):
        wid = lax.axis_index("s") * nc + lax.axis_index("c")
        base = wid * per_w
        pltpu.sync_copy(idx_hbm.at[wid], idx_v)

        def gather(j, buf):
            return pltpu.make_async_copy(table_hbm.at[idx_v.at[j]], rows_v.at[buf], sem.at[buf])

        gather(0, 0).start()

        @pl.loop(0, n_ch, step=2)
        def _(j):
            for buf in range(2):
                jj = j + buf
                gather(jj, buf).wait()

                @pl.when(jj + 1 < n_ch)
                def _():
                    gather(jj + 1, 1 - buf).start()

                pltpu.sync_copy(rows_v.at[buf], out_hbm.at[pl.ds(base + jj * ch, ch)])

    return gather_kernel(table, idx.reshape(nw, n_ch, ch))


def _combine_kernel(rows_ref, base_ref, gate_ref, mod_ref, fg_ref, o_ref):
    tm = base_ref.shape[0]
    gate = gate_ref[...]
    acc_lo = jnp.zeros((tm, HALF), F32)
    acc_hi = jnp.zeros((tm, HALF), F32)
    for kk in range(TOP_K):
        lo, hi = _unpack_bf16_pairs(rows_ref[kk])
        acc_lo = acc_lo + gate[:, kk:kk + 1] * lo
        acc_hi = acc_hi + gate[:, kk:kk + 1] * hi
    acc = jnp.concatenate([acc_lo, acc_hi], axis=1)
    g2 = mod_ref[0:1, 5 * D_MODEL:6 * D_MODEL]
    o_ref[...] = _rms(base_ref[...] + g2 * acc, fg_ref[...])


def _combine(dest_km, ye, base, gate, mod, fg):
    t = base.shape[0]
    tm = TOK_TILE
    rows = _sc_gather_rows(ye, dest_km.reshape(-1)).reshape(TOP_K, t, HALF)
    full = lambda arr: pl.BlockSpec(arr.shape, lambda i: (0,) * arr.ndim)
    return pl.pallas_call(
        _combine_kernel,
        grid=(t // tm,),
        in_specs=[pl.BlockSpec((TOP_K, tm, HALF), lambda i: (0, i, 0)),
                  pl.BlockSpec((tm, D_MODEL), lambda i: (i, 0)),
                  pl.BlockSpec((tm, LANES), lambda i: (i, 0)),
                  full(mod), full(fg)],
        out_specs=pl.BlockSpec((tm, D_MODEL), lambda i: (i, 0)),
        out_shape=jax.ShapeDtypeStruct((t, D_MODEL), F32),
        compiler_params=_cparams(("arbitrary",)),
        name="combine",
    )(rows, base, gate, mod, fg)


def _block_tables(counts, n_blocks):
    sb = SLOT_BLOCK
    padded = (counts + sb - 1) // sb * sb
    pad_end = jnp.cumsum(padded)
    pad_start = pad_end - padded
    n_used = pad_end[-1] // sb
    blk = jnp.minimum(jnp.arange(n_blocks, dtype=jnp.int32), n_used - 1)
    block_e = jnp.minimum(jnp.sum(pad_end[None, :] <= (blk * sb)[:, None], axis=1), N_EXPERTS - 1).astype(jnp.int32)
    onehot = block_e[:, None] == jnp.arange(N_EXPERTS, dtype=jnp.int32)[None, :]
    live_end = jnp.sum(jnp.where(onehot, (pad_start + counts)[None, :], 0), axis=1)
    n_valid = jnp.clip(live_end - blk * sb, 0, sb).astype(jnp.int32)
    return block_e, n_used.reshape(1).astype(jnp.int32), n_valid, pad_start


def _blockdiag2(w):
    z = jnp.zeros_like(w[0])
    return jnp.concatenate([jnp.concatenate([w[0], z], axis=1), jnp.concatenate([z, w[1]], axis=1)], axis=0)


def _pad_rows(w, n=8):
    return jnp.concatenate([w, jnp.zeros((n - w.shape[0],) + w.shape[1:], w.dtype)], axis=0)


def kernel(x, c, ctx, c_ctx, ada_w, ada_b, norm1_g, norm2_g, w_in, conv_w, shift_mu, decay_w0, decay_up, iclr_a0, iclr_up, key_xi, key_alpha, bonus_rho, gate_up, gn_w, gn_b, w_out, router_w, router_bias, exp_w1, exp_w3, exp_w2, sh_w1, sh_w3, sh_w2, final_g):
    assert x.shape[0] == 1 and ada_w.shape[0] == 1
    assert ctx.shape[1] == TOK_TILE and x.shape[1] % TOK_TILE == 0
    l = 0
    xs = x[0]
    row = lambda w: w.reshape(1, -1)

    cc = _pad_rows(jnp.stack([c[0], c_ctx], axis=0))
    mod = _ada(cc, ada_w[l], row(ada_b[l]))

    hid = lax.broadcasted_iota(jnp.int32, (D_RWKV, D_RWKV), 0) // HEAD_DIM
    bsum = (hid == hid.T).astype(BF16)

    r4, v4, g, bonus, yconv, lw, kh, kt, a = _proj(
        ctx[0], xs, mod, row(norm1_g[l]), w_in[l].astype(BF16), _pad_rows(conv_w[l]), _pad_rows(shift_mu[l]),
        row(decay_w0[l]), _blockdiag2(decay_up[l]).astype(BF16), row(iclr_a0[l]), _blockdiag2(iclr_up[l]).astype(BF16),
        row(key_xi[l]), row(key_alpha[l]), row(bonus_rho[l]), gate_up[l].astype(BF16), bsum)

    y = _scan(r4, v4, lw, kh, kt, a)

    base, h2, idx, gate, rank, cnt = _mix(
        xs, y, g, bonus, yconv, mod, row(gn_w[l]), row(gn_b[l]), bsum, w_out[l].astype(BF16),
        row(norm2_g[l]), router_w[l], row(router_bias[l]),
        sh_w1[l].astype(BF16), sh_w3[l].astype(BF16), sh_w2[l].astype(BF16))

    n_blocks = xs.shape[0] * TOP_K // SLOT_BLOCK + N_EXPERTS
    block_e, n_used, n_valid, pad_start = _block_tables(cnt[0].astype(jnp.int32), n_blocks)
    dest_km = _slots(idx, rank, pad_start.astype(F32).reshape(1, N_EXPERTS))[:, :TOP_K].T
    hs = _sc_scatter_rows(h2, dest_km, n_blocks * SLOT_BLOCK)
    ye = _experts(block_e, n_used, n_valid, hs, exp_w1[l], exp_w3[l], exp_w2[l])
    out = _combine(dest_km, ye, base, gate, mod, row(final_g))
    return out[None]
```

```python
import functools

import jax
import jax.numpy as jnp
from jax import lax
from jax.experimental import pallas as pl
from jax.experimental.pallas import tpu as pltpu
from jax.experimental.pallas import tpu_sc as plsc

F32 = jnp.float32
BF16 = jnp.bfloat16

D_MODEL = 1024
D_CONV = 512
D_RWKV = 512
HEAD_DIM = 64
N_HEADS = D_RWKV // HEAD_DIM
N_PAIRS = N_HEADS // 2
LORA = 128
P_RWKV = 3 * D_RWKV + 3 * LORA
P_IN = 3 * D_CONV + P_RWKV
GRID_W = 64
N_EXPERTS = 256
TOP_K = 8
D_EXPERT = 256
ROUTED_SCALE = 2.5
RMS_EPS = 1e-6
GN_EPS = 64e-5
DECAY_SCALE = 0.6065306597126334
NORM_EPS = 1e-12

TOK_TILE = 256
SLOTS_TILE = 1024
SCAN_BLOCK = 256
CHUNK = 64
SLOT_BLOCK = 512
SC_CHUNK = 64
SC_SCATTER_CHUNK = 128
LANES = 128
VMEM_LIMIT = 56 * 1024 * 1024

NN = ((1,), (0,))
NT = ((1,), (1,))
TN = ((0,), (0,))


def _dg(a, b, dims=NN):
    return lax.dot_general(a, b, (dims, ((), ())), preferred_element_type=F32)


def _split2(a):
    hi = a.astype(BF16)
    lo = (a - hi.astype(F32)).astype(BF16)
    return hi, lo


def _split3(a):
    hi = a.astype(BF16)
    r1 = a - hi.astype(F32)
    mid = r1.astype(BF16)
    lo = (r1 - mid.astype(F32)).astype(BF16)
    return hi, mid, lo


def _mm1(a, b, dims=NN):
    return _dg(a.astype(BF16), b.astype(BF16), dims)


def _mm3(a, b, dims=NN):
    ah, al = _split2(a)
    bh, bl = _split2(b)
    return _dg(ah, bh, dims) + (_dg(ah, bl, dims) + _dg(al, bh, dims))


def _mm_split_lhs(a, b_bf16, dims=NN):
    h, l = _split2(a)
    return _dg(h, b_bf16, dims) + _dg(l, b_bf16, dims)


def _mm_ones_rhs(a, ones_bf16):
    return _dg(a.astype(BF16), ones_bf16)


def _mm_ones_lhs(ones_bf16, b, dims=NN):
    h, m, l = _split3(b)
    return _dg(ones_bf16, h, dims) + (_dg(ones_bf16, m, dims) + _dg(ones_bf16, l, dims))


HALF = D_MODEL // 2
I32 = jnp.int32
HI_MASK = -65536


def _pack_bf16_pairs(x):
    u = lax.bitcast_convert_type(x, I32)
    return lax.shift_right_logical(u[:, 0:HALF], jnp.int32(16)) | (u[:, HALF:] & jnp.int32(HI_MASK))


def _unpack_bf16_pairs(w):
    lo = lax.bitcast_convert_type(lax.shift_left(w, jnp.int32(16)), F32)
    hi = lax.bitcast_convert_type(w & jnp.int32(HI_MASK), F32)
    return lo, hi


def _sigmoid(x):
    return 1.0 / (1.0 + jnp.exp(-x))


def _silu(x):
    return x * _sigmoid(x)


def _rms(xv, g):
    ms = jnp.mean(xv * xv, axis=-1, keepdims=True)
    return xv * lax.rsqrt(ms + RMS_EPS) * g


def _cparams(sem):
    return pltpu.CompilerParams(dimension_semantics=sem, vmem_limit_bytes=VMEM_LIMIT)


def _ada_kernel(c_ref, w_ref, b_ref, o_ref):
    o_ref[...] = _mm3(_silu(c_ref[...]), w_ref[...]) + b_ref[...]


def _ada(cc, ada_w, ada_b):
    n = ada_w.shape[1]
    tn = 1024
    return pl.pallas_call(
        _ada_kernel,
        grid=(n // tn,),
        in_specs=[pl.BlockSpec((8, D_MODEL), lambda j: (0, 0)),
                  pl.BlockSpec((D_MODEL, tn), lambda j: (0, j)),
                  pl.BlockSpec((1, tn), lambda j: (0, j))],
        out_specs=pl.BlockSpec((8, tn), lambda j: (0, j)),
        out_shape=jax.ShapeDtypeStruct((8, n), F32),
        compiler_params=_cparams(("arbitrary",)),
        name="ada",
    )(cc, ada_w, ada_b)


def _proj_kernel(ctx_ref, x_ref, xp_ref, xn_ref, mod_ref, n1g_ref, win_ref, convw_ref, mu_ref,
                 dw0_ref, dup_ref, ia0_ref, iup_ref, xi_ref, kal_ref, rho_ref, gup_ref, bsum_ref,
                 r_ref, v_ref, g_ref, bonus_ref, yconv_ref, lw_ref, kh_ref, kt_ref, a_ref):
    i = pl.program_id(0)
    nt = pl.num_programs(0)
    tm = x_ref.shape[0]
    is_ctx = i == 0
    sh = jnp.where(is_ctx, mod_ref[1:2, 0:D_MODEL], mod_ref[0:1, 0:D_MODEL])
    sc = jnp.where(is_ctx, mod_ref[1:2, D_MODEL:2 * D_MODEL], mod_ref[0:1, D_MODEL:2 * D_MODEL])
    n1g = n1g_ref[...]

    def norm_mod(xv):
        return _rms(xv, n1g) * (1.0 + sc) + sh

    xt = jnp.where(is_ctx, ctx_ref[...], x_ref[...])
    h = norm_mod(jnp.concatenate([xt, xp_ref[...], xn_ref[...]], axis=0)).astype(BF16)
    p_all = _dg(h, win_ref[...])
    p = p_all[0:tm]
    ph = p_all[tm:tm + 16, 3 * D_CONV:]

    rows = lax.broadcasted_iota(jnp.int32, (tm, 1), 0)

    bg = p[:, 0:D_CONV]
    z = p[:, D_CONV:2 * D_CONV] * p[:, 2 * D_CONV:3 * D_CONV]
    col = rows % GRID_W
    zp = jnp.where(col == 0, 0.0, pltpu.roll(z, 1, 0))
    zn = jnp.where(col == GRID_W - 1, 0.0, pltpu.roll(z, tm - 1, 0))
    yconv = bg * (convw_ref[0:1, :] * zp + convw_ref[1:2, :] * z + convw_ref[2:3, :] * zn)
    yconv_ref[...] = yconv.astype(BF16)

    prev_ok = jnp.logical_and(i != 0, i != 1).astype(F32)
    next_ok = jnp.logical_and(i != 0, i != nt - 1).astype(F32)
    cur = p[:, 3 * D_CONV:]
    prev = jnp.where(rows == 0, ph[7:8, :] * prev_ok, pltpu.roll(cur, 1, 0))
    nxt = jnp.where(rows == tm - 1, ph[8:9, :] * next_ok, pltpu.roll(cur, tm - 1, 0))
    ps = cur + mu_ref[0:1, :] * (prev - cur) + mu_ref[1:2, :] * (nxt - cur)

    r = ps[:, 0:D_RWKV]
    k = ps[:, D_RWKV:2 * D_RWKV]
    v = ps[:, 2 * D_RWKV:3 * D_RWKV]
    o = 3 * D_RWKV
    wlo = ps[:, o:o + LORA]
    alo = ps[:, o + LORA:o + 2 * LORA]
    glo = ps[:, o + 2 * LORA:o + 3 * LORA]

    dd = dw0_ref[...] + _mm_split_lhs(jnp.tanh(wlo), dup_ref[...])
    lw = -DECAY_SCALE * _sigmoid(dd)
    a = _sigmoid(ia0_ref[...] + _mm_split_lhs(alo, iup_ref[...]))
    g = _dg(_sigmoid(glo).astype(BF16), gup_ref[...])
    k2 = jnp.concatenate([k, k], axis=1)
    kap = k2 * xi_ref[...]
    kap2 = kap * kap
    bs = bsum_ref[...]
    ss = jnp.concatenate([_mm_ones_rhs(kap2[:, 0:D_RWKV], bs),
                          _mm_ones_rhs(kap2[:, D_RWKV:], bs)], axis=1)
    kh = kap * lax.rsqrt(ss + NORM_EPS)
    kt = k2 * (1.0 + (a - 1.0) * kal_ref[...])
    bon = _mm_ones_rhs(r * rho_ref[...] * (kt[:, 0:D_RWKV] + kt[:, D_RWKV:]), bs)

    g_ref[...] = g
    bonus_ref[...] = bon * v
    for pr in range(N_PAIRS):
        ls = slice(pr * LANES, (pr + 1) * LANES)
        r_ref[pr] = r[:, ls]
        v_ref[pr] = v[:, ls]
        for d in range(2):
            ld = slice(d * D_RWKV + pr * LANES, d * D_RWKV + (pr + 1) * LANES)
            lw_ref[d, pr] = lw[:, ld]
            kh_ref[d, pr] = kh[:, ld]
            kt_ref[d, pr] = kt[:, ld]
            a_ref[d, pr] = a[:, ld]


def _proj(ctx2, x2, mod, n1g, win, convw, mu, dw0, dup, ia0, iup, xi, kal, rho, gup, bsum):
    tm = TOK_TILE
    t = ctx2.shape[0] + x2.shape[0]
    nt = t // tm
    per = tm // 8
    nb8 = x2.shape[0] // 8
    full = lambda arr: pl.BlockSpec(arr.shape, lambda i: (0,) * arr.ndim)
    pair_spec = pl.BlockSpec((N_PAIRS, tm, LANES), lambda i: (0, i, 0))
    dpair_spec = pl.BlockSpec((2, N_PAIRS, tm, LANES), lambda i: (0, 0, i, 0))
    row_spec = pl.BlockSpec((tm, D_RWKV), lambda i: (i, 0))
    pair_shape = jax.ShapeDtypeStruct((N_PAIRS, t, LANES), F32)
    dpair_shape = jax.ShapeDtypeStruct((2, N_PAIRS, t, LANES), F32)
    consts = (mod, n1g, win, convw, mu, dw0, dup, ia0, iup, xi, kal, rho, gup, bsum)
    return pl.pallas_call(
        _proj_kernel,
        grid=(nt,),
        in_specs=[pl.BlockSpec((tm, D_MODEL), lambda i: (0, 0)),
                  pl.BlockSpec((tm, D_MODEL), lambda i: (jnp.maximum(i - 1, 0), 0)),
                  pl.BlockSpec((8, D_MODEL), lambda i: (jnp.maximum((i - 1) * per - 1, 0), 0)),
                  pl.BlockSpec((8, D_MODEL), lambda i: (jnp.clip(i * per, 0, nb8 - 1), 0))]
                 + [full(c) for c in consts],
        out_specs=[pair_spec, pair_spec, row_spec, row_spec, row_spec,
                   dpair_spec, dpair_spec, dpair_spec, dpair_spec],
        out_shape=[pair_shape, pair_shape,
                   jax.ShapeDtypeStruct((t, D_RWKV), F32), jax.ShapeDtypeStruct((t, D_RWKV), F32),
                   jax.ShapeDtypeStruct((t, D_CONV), BF16),
                   dpair_shape, dpair_shape, dpair_shape, dpair_shape],
        compiler_params=_cparams(("arbitrary",)),
        name="proj",
    )(ctx2, x2, x2, x2, *consts)


def _scan_kernel(r_ref, v_ref, lw_ref, kh_ref, kt_ref, a_ref, w1_ref, w3_ref, w2_ref,
                 y_ref, w1b_ref, w3b_ref, w2b_ref, q_ref):
    w1b_ref[...] = w1_ref[...].astype(BF16)
    w3b_ref[...] = w3_ref[...].astype(BF16)
    w2b_ref[...] = w2_ref[...].astype(BF16)

    d = pl.program_id(0)
    rev = d == 1
    c = CHUNK
    npair = r_ref.shape[0]
    nsub = r_ref.shape[1] // c

    row = lax.broadcasted_iota(jnp.int32, (c, c), 0)
    col = lax.broadcasted_iota(jnp.int32, (c, c), 1)
    strict = (col - row) * (1 - 2 * d) < 0
    eye = row == col
    incl = jnp.logical_or(strict, eye)
    incl_bf = jnp.where(incl, 1.0, 0.0).astype(BF16)
    eye_f = jnp.where(eye, 1.0, 0.0)
    head0 = lax.broadcasted_iota(jnp.int32, (c, LANES), 1) < HEAD_DIM
    row2 = lax.broadcasted_iota(jnp.int32, (LANES, LANES), 0)
    col2 = lax.broadcasted_iota(jnp.int32, (LANES, LANES), 1)
    same_head = (row2 < HEAD_DIM) == (col2 < HEAD_DIM)
    eye2 = row2 == col2

    subs = range(npair * nsub)
    units = [(s, h) for s in subs for h in range(2)]
    off = [pl.multiple_of(jnp.where(rev, nsub - 1 - s, s) * c, c) for s in range(nsub)]
    sls = [(pp, pl.ds(off[s], c), slice(None)) for pp in range(npair) for s in range(nsub)]
    v = [v_ref[sl] for sl in sls]
    lw = [lw_ref[sl] for sl in sls]
    cum = [_mm_ones_lhs(incl_bf, lw[s]) for s in subs]
    al, be, kk, rr, bew, kkw, wc = [], [], [], [], [], [], []
    for s in subs:
        cum_last = jnp.where(rev, cum[s][0:1, :], cum[s][c - 1:c, :])
        e_end = jnp.exp(cum_last - cum[s])
        e_neg = jnp.exp(-cum[s])
        kh = kh_ref[sls[s]]
        kt = kt_ref[sls[s]]
        kha = kh * a_ref[sls[s]]
        al.append(kh * jnp.exp(cum[s] - lw[s]))
        be.append(-(kha * e_neg))
        kk.append(kt * e_neg)
        rr.append(r_ref[sls[s]] * jnp.exp(cum[s]))
        bew.append(-(kha * e_end))
        kkw.append(kt * e_end)
        wc.append(jnp.exp(cum_last))

    lhs = {}
    for s, h in units:
        hm = head0 if h == 0 else jnp.logical_not(head0)
        lhs[s, h] = jnp.concatenate([jnp.where(hm, al[s], 0.0), jnp.where(hm, rr[s], 0.0)], axis=0)
    gb = {u: _mm1(lhs[u], be[u[0]], NT) for u in units}
    gk = {u: _mm1(lhs[u], kk[u[0]], NT) for u in units}
    a_ab = {u: jnp.where(strict, gb[u][0:c], 0.0) for u in units}
    a_ak = {u: jnp.where(strict, gk[u][0:c], 0.0) for u in units}
    a_rb = {u: jnp.where(incl, gb[u][c:2 * c], 0.0) for u in units}
    a_rk = {u: jnp.where(incl, gk[u][c:2 * c], 0.0) for u in units}

    pw = dict(a_ab)
    tinv = {u: eye_f + a_ab[u] for u in units}
    akrk_v = {u: _mm1(jnp.concatenate([a_ak[u], a_rk[u]], axis=0), v[u[0]]) for u in units}
    avk = {u: akrk_v[u][0:c] for u in units}
    rk_v = {u: akrk_v[u][c:2 * c] for u in units}
    for _ in range(c.bit_length() - 2):
        pw = {u: _mm1(pw[u], pw[u]) for u in units}
        tinv = {u: tinv[u] + _mm1(tinv[u], pw[u]) for u in units}
    tu = {u: _mm1(tinv[u], jnp.concatenate([al[u[0]], avk[u]], axis=1)) for u in units}
    at = [jnp.where(head0, tu[s, 0][:, 0:LANES], tu[s, 1][:, 0:LANES]) for s in subs]
    ut = [jnp.where(head0, tu[s, 0][:, LANES:], tu[s, 1][:, LANES:]) for s in subs]

    rb = {u: _mm1(a_rb[u], jnp.concatenate([at[u[0]], ut[u[0]]], axis=1)) for u in units}
    rb_at = {u: rb[u][:, 0:LANES] for u in units}
    rb_ut = {u: rb[u][:, LANES:] for u in units}
    md = [_mm1(bew[s], jnp.concatenate([at[s], ut[s]], axis=1), TN) for s in subs]
    m_off = [md[s][:, 0:LANES] for s in subs]
    d_u = [md[s][:, LANES:] for s in subs]
    d_v = [_mm1(kkw[s], v[s], TN) for s in subs]
    rh = [rr[s] + jnp.where(head0, rb_at[s, 0], rb_at[s, 1]) for s in subs]
    yh = [jnp.where(head0, rb_ut[s, 0] + rk_v[s, 0], rb_ut[s, 1] + rk_v[s, 1]) for s in subs]
    mt = [jnp.where(eye2, wc[s], 0.0) + jnp.where(same_head, m_off[s], 0.0) for s in subs]
    dt = [jnp.where(same_head, d_u[s] + d_v[s], 0.0) for s in subs]

    @pl.when(pl.program_id(1) == 0)
    def _():
        q_ref[...] = jnp.zeros(q_ref.shape, F32)

    q = [q_ref[pp] for pp in range(npair)]
    for si in range(nsub):
        ys = [_mm1(rh[pp * nsub + si], q[pp]) for pp in range(npair)]
        qn = [_mm1(mt[pp * nsub + si], q[pp]) for pp in range(npair)]
        for pp in range(npair):
            y_ref[sls[pp * nsub + si]] = ys[pp] + yh[pp * nsub + si]
            q[pp] = qn[pp] + dt[pp * nsub + si]
    for pp in range(npair):
        q_ref[pp] = q[pp]


def _scan(r4, v4, lw, kh, kt, a, w1, w3, w2):
    t = r4.shape[1]
    tb = SCAN_BLOCK
    nb = t // tb
    ne = w1.shape[0]
    epb = -(-ne // (2 * nb))
    assert ne % epb == 0
    last = ne // epb - 1

    def blk(d, j):
        return jnp.where(d == 0, j, jnp.where(j == 0, 0, nb - j))

    def wblk(d, j):
        return (jnp.minimum(d * nb + j, last), 0, 0)

    shared = pl.BlockSpec((N_PAIRS, tb, LANES), lambda d, j: (0, blk(d, j), 0))
    perdir = pl.BlockSpec((None, N_PAIRS, tb, LANES), lambda d, j: (d, 0, blk(d, j), 0))
    w13 = pl.BlockSpec((epb, D_MODEL, D_EXPERT), wblk)
    w2s = pl.BlockSpec((epb, D_EXPERT, D_MODEL), wblk)
    return pl.pallas_call(
        _scan_kernel,
        grid=(2, nb),
        in_specs=[shared, shared, perdir, perdir, perdir, perdir, w13, w13, w2s],
        out_specs=[perdir, w13, w13, w2s],
        out_shape=[jax.ShapeDtypeStruct((2, N_PAIRS, t, LANES), F32),
                   jax.ShapeDtypeStruct(w1.shape, BF16), jax.ShapeDtypeStruct(w3.shape, BF16),
                   jax.ShapeDtypeStruct(w2.shape, BF16)],
        scratch_shapes=[pltpu.VMEM((N_PAIRS, LANES, LANES), F32)],
        compiler_params=_cparams(("arbitrary", "arbitrary")),
        name="scan",
    )(r4, v4, lw, kh, kt, a, w1, w3, w2)


def _mix_kernel(x_ref, yf_ref, yb_ref, g_ref, bonus_ref, yconv_ref, mod_ref, gnw_ref, gnb_ref, bsum_ref,
                wout_ref, n2g_ref, rw_ref, rb_ref, sw1_ref, sw3_ref, sw2_ref,
                base_ref, h2_ref, idx_ref, gate_ref, rank_ref, cnt_ref, run_ref):
    tm = x_ref.shape[0]
    g1 = mod_ref[0:1, 2 * D_MODEL:3 * D_MODEL]
    sh2 = mod_ref[0:1, 3 * D_MODEL:4 * D_MODEL]
    sc2 = mod_ref[0:1, 4 * D_MODEL:5 * D_MODEL]
    g2 = mod_ref[0:1, 5 * D_MODEL:6 * D_MODEL]

    @pl.when(pl.program_id(0) == 0)
    def _():
        run_ref[...] = jnp.zeros(run_ref.shape, F32)

    y = jnp.concatenate([yf_ref[pr] + yb_ref[pr] for pr in range(N_PAIRS)], axis=1)
    bs = bsum_ref[...]
    mu = _mm_ones_rhs(y, bs) * (1.0 / HEAD_DIM)
    yc = y - mu
    var = _mm_ones_rhs(yc * yc, bs) * (1.0 / HEAD_DIM)
    yn = yc * lax.rsqrt(var + GN_EPS) * gnw_ref[...] + gnb_ref[...]
    yrw = ((yn + bonus_ref[...]) * g_ref[...]).astype(BF16)
    mix = _dg(yconv_ref[...], wout_ref[0:D_CONV, :]) + _dg(yrw, wout_ref[D_CONV:, :])
    x1 = x_ref[...] + g1 * mix

    h2 = _rms(x1, n2g_ref[...]) * (1.0 + sc2) + sh2
    hb = h2.astype(BF16)
    h2_ref[...] = _pack_bf16_pairs(hb.astype(F32))

    scores = _sigmoid(_mm3(h2, rw_ref[...]))
    work = scores + rb_ref[...]
    lane_e = lax.broadcasted_iota(jnp.int32, (tm, N_EXPERTS), 1).astype(F32)
    lane_o = lax.broadcasted_iota(jnp.int32, (tm, LANES), 1)
    idx_acc = jnp.zeros((tm, LANES), F32)
    gate_acc = jnp.zeros((tm, LANES), F32)
    gsum = jnp.zeros((tm, 1), F32)
    chosen = jnp.zeros((tm, N_EXPERTS), F32)
    sels = []
    for kk in range(TOP_K):
        m = jnp.max(work, axis=-1, keepdims=True)
        sel = jnp.min(jnp.where(work == m, lane_e, float(N_EXPERTS)), axis=-1, keepdims=True)
        hit = lane_e == sel
        sk = jnp.sum(jnp.where(hit, scores, 0.0), axis=-1, keepdims=True)
        idx_acc = jnp.where(lane_o == kk, sel, idx_acc)
        gate_acc = jnp.where(lane_o == kk, sk, gate_acc)
        gsum = gsum + sk
        work = jnp.where(hit, -jnp.inf, work)
        chosen = jnp.where(hit, 1.0, chosen)
        sels.append(sel)
    idx_ref[...] = idx_acc.astype(jnp.int32)
    gate_ref[...] = gate_acc / gsum * ROUTED_SCALE

    trow = lax.broadcasted_iota(jnp.int32, (tm, tm), 0)
    tcol = lax.broadcasted_iota(jnp.int32, (tm, tm), 1)
    earlier = jnp.where(tcol < trow, 1.0, 0.0).astype(BF16)
    before = _dg(earlier, chosen.astype(BF16)) + run_ref[0:1, :]
    rank_acc = jnp.zeros((tm, LANES), F32)
    for kk in range(TOP_K):
        rk = jnp.sum(jnp.where(lane_e == sels[kk], before, 0.0), axis=-1, keepdims=True)
        rank_acc = jnp.where(lane_o == kk, rk, rank_acc)
    rank_ref[...] = rank_acc.astype(jnp.int32)
    run_ref[...] = run_ref[...] + jnp.sum(chosen, axis=0, keepdims=True)
    cnt_ref[...] = run_ref[...]

    act = (_silu(_dg(hb, sw1_ref[...])) * _dg(hb, sw3_ref[...])).astype(BF16)
    base_ref[...] = x1 + g2 * _dg(act, sw2_ref[...])


def _mix(x, y, g, bonus, yconv, mod, gnw, gnb, bsum, wout, n2g, rw, rb, sw1, sw3, sw2):
    t = x.shape[0]
    tm = TOK_TILE
    off = (y.shape[2] - t) // tm
    full = lambda arr: pl.BlockSpec(arr.shape, lambda i: (0,) * arr.ndim)
    consts = (mod, gnw, gnb, bsum, wout, n2g, rw, rb, sw1, sw3, sw2)
    cat_spec = lambda w: pl.BlockSpec((tm, w), lambda i: (i + off, 0))
    tok_spec = lambda w: pl.BlockSpec((tm, w), lambda i: (i, 0))
    yf_spec = pl.BlockSpec((None, N_PAIRS, tm, LANES), lambda i: (0, 0, i + off, 0))
    yb_spec = pl.BlockSpec((None, N_PAIRS, tm, LANES), lambda i: (1, 0, i + off, 0))
    return pl.pallas_call(
        _mix_kernel,
        grid=(t // tm,),
        in_specs=[tok_spec(D_MODEL), yf_spec, yb_spec,
                  cat_spec(D_RWKV), cat_spec(D_RWKV), cat_spec(D_CONV)]
                 + [full(c) for c in consts],
        out_specs=[tok_spec(D_MODEL), tok_spec(HALF), tok_spec(LANES), tok_spec(LANES), tok_spec(LANES),
                   pl.BlockSpec((8, N_EXPERTS), lambda i: (0, 0))],
        out_shape=[jax.ShapeDtypeStruct((t, D_MODEL), F32), jax.ShapeDtypeStruct((t, HALF), I32),
                   jax.ShapeDtypeStruct((t, LANES), jnp.int32), jax.ShapeDtypeStruct((t, LANES), F32),
                   jax.ShapeDtypeStruct((t, LANES), jnp.int32), jax.ShapeDtypeStruct((8, N_EXPERTS), F32)],
        scratch_shapes=[pltpu.VMEM((8, N_EXPERTS), F32)],
        compiler_params=_cparams(("arbitrary",)),
        name="mix",
    )(x, y, y, g, bonus, yconv, *consts)


def _slots_kernel(idx_ref, rank_ref, start_ref, dest_ref):
    tm = idx_ref.shape[0]
    lane_e = lax.broadcasted_iota(jnp.int32, (tm, N_EXPERTS), 1)
    lane_o = lax.broadcasted_iota(jnp.int32, (tm, LANES), 1)
    idx = idx_ref[...]
    start = start_ref[...]
    acc = jnp.zeros((tm, LANES), F32)
    for kk in range(TOP_K):
        st = jnp.sum(jnp.where(lane_e == idx[:, kk:kk + 1], start, 0.0), axis=-1, keepdims=True)
        acc = jnp.where(lane_o == kk, st, acc)
    dest_ref[...] = acc.astype(jnp.int32) + rank_ref[...]


def _slots(idx, rank, pad_start):
    t = idx.shape[0]
    tm = SLOTS_TILE
    spec = pl.BlockSpec((tm, LANES), lambda i: (i, 0))
    return pl.pallas_call(
        _slots_kernel,
        grid=(t // tm,),
        in_specs=[spec, spec, pl.BlockSpec((1, N_EXPERTS), lambda i: (0, 0))],
        out_specs=spec,
        out_shape=jax.ShapeDtypeStruct((t, LANES), jnp.int32),
        compiler_params=_cparams(("arbitrary",)),
        name="slots",
    )(idx, rank, pad_start)


def _sc_scatter_rows(rows, dest_km, n_slots):
    info = plsc.get_sparse_core_info()
    nc, ns = info.num_cores, info.num_subcores
    nw = nc * ns
    t, d = rows.shape
    nk = dest_km.shape[0]
    ch = SC_SCATTER_CHUNK
    per_w = t // nw
    n_ch = per_w // ch
    assert t % nw == 0 and per_w % ch == 0
    mesh = plsc.VectorSubcoreMesh(core_axis_name="c", subcore_axis_name="s")
    idx = dest_km.reshape(nk, nw, n_ch, ch).transpose(1, 2, 0, 3).reshape(nw, n_ch * nk, ch)

    @functools.partial(
        pl.kernel, mesh=mesh,
        out_type=jax.ShapeDtypeStruct((n_slots, d), rows.dtype),
        scratch_types=[pltpu.VMEM((n_ch * nk, ch), jnp.int32),
                       pltpu.VMEM((ch, d), rows.dtype),
                       pltpu.SemaphoreType.DMA],
    )
    def scatter_kernel(rows_hbm, idx_hbm, out_hbm, idx_v, rows_v, sem):
        wid = lax.axis_index("s") * nc + lax.axis_index("c")
        base = wid * per_w
        pltpu.sync_copy(idx_hbm.at[wid], idx_v)

        @pl.loop(0, n_ch)
        def _(j):
            pltpu.sync_copy(rows_hbm.at[pl.ds(base + j * ch, ch)], rows_v)
            copies = [pltpu.make_async_copy(rows_v, out_hbm.at[idx_v.at[j * nk + kk]], sem) for kk in range(nk)]
            for cp in copies:
                cp.start()
            for cp in copies:
                cp.wait()

    return scatter_kernel(rows, idx)


def _experts_kernel(be_ref, nu_ref, nv_ref, hs_ref, w1c, w3c, w2c, o_ref):
    del be_ref
    b = pl.program_id(0)

    @pl.when(b < nu_ref[0])
    def _():
        live = lax.broadcasted_iota(jnp.int32, (hs_ref.shape[0], 1), 0) < nv_ref[b]
        lo, hi = _unpack_bf16_pairs(jnp.where(live, hs_ref[...], 0))
        lo = lo.astype(BF16)
        hi = hi.astype(BF16)
        a1 = _dg(lo, w1c[0:HALF, :]) + _dg(hi, w1c[HALF:, :])
        a3 = _dg(lo, w3c[0:HALF, :]) + _dg(hi, w3c[HALF:, :])
        act = (_silu(a1) * a3).astype(BF16)
        y = _dg(act, w2c[...]).astype(BF16).astype(F32)
        o_ref[...] = _pack_bf16_pairs(y)

    @pl.when(b >= nu_ref[0])
    def _():
        o_ref[...] = jnp.zeros(o_ref.shape, I32)


def _experts(block_e, n_used, n_valid, hs, w1, w3, w2):
    nblk = block_e.shape[0]
    sb = SLOT_BLOCK
    row_map = lambda b, be, nu, nv: (jnp.minimum(b, nu[0] - 1), 0)
    w_map = lambda b, be, nu, nv: (be[b], 0, 0)
    grid_spec = pltpu.PrefetchScalarGridSpec(
        num_scalar_prefetch=3,
        grid=(nblk,),
        in_specs=[pl.BlockSpec((sb, HALF), row_map),
                  pl.BlockSpec((None, D_MODEL, D_EXPERT), w_map),
                  pl.BlockSpec((None, D_MODEL, D_EXPERT), w_map),
                  pl.BlockSpec((None, D_EXPERT, D_MODEL), w_map)],
        out_specs=pl.BlockSpec((sb, HALF), lambda b, be, nu, nv: (b, 0)),
    )
    return pl.pallas_call(
        _experts_kernel,
        grid_spec=grid_spec,
        out_shape=jax.ShapeDtypeStruct((nblk * sb, HALF), I32),
        compiler_params=_cparams(("arbitrary",)),
        name="experts",
    )(block_e, n_used, n_valid, hs, w1, w3, w2)


def _sc_gather_rows(table, idx):
    info = plsc.get_sparse_core_info()
    nc, ns = info.num_cores, info.num_subcores
    nw = nc * ns
    n = idx.shape[0]
    d = table.shape[1]
    ch = SC_CHUNK
    per_w = n // nw
    n_ch = per_w // ch
    assert n % nw == 0 and per_w % (2 * ch) == 0
    mesh = plsc.VectorSubcoreMesh(core_axis_name="c", subcore_axis_name="s")

    @functools.partial(
        pl.kernel, mesh=mesh,
        out_type=jax.ShapeDtypeStruct((n, d), table.dtype),
        scratch_types=[pltpu.VMEM((n_ch, ch), jnp.int32),
                       pltpu.VMEM((2, ch, d), table.dtype),
                       pltpu.SemaphoreType.DMA((2,))],
    )
    def gather_kernel(table_hbm, idx_hbm, out_hbm, idx_v, rows_v, sem):
        wid = lax.axis_index("s") * nc + lax.axis_index("c")
        base = wid * per_w
        pltpu.sync_copy(idx_hbm.at[wid], idx_v)

        def gather(j, buf):
            return pltpu.make_async_copy(table_hbm.at[idx_v.at[j]], rows_v.at[buf], sem.at[buf])

        gather(0, 0).start()

        @pl.loop(0, n_ch, step=2)
        def _(j):
            for buf in range(2):
                jj = j + buf
                gather(jj, buf).wait()

                @pl.when(jj + 1 < n_ch)
                def _():
                    gather(jj + 1, 1 - buf).start()

                pltpu.sync_copy(rows_v.at[buf], out_hbm.at[pl.ds(base + jj * ch, ch)])

    return gather_kernel(table, idx.reshape(nw, n_ch, ch))


def _combine_kernel(rows_ref, base_ref, gate_ref, mod_ref, fg_ref, o_ref):
    tm = base_ref.shape[0]
    gate = gate_ref[...]
    acc_lo = jnp.zeros((tm, HALF), F32)
    acc_hi = jnp.zeros((tm, HALF), F32)
    for kk in range(TOP_K):
        lo, hi = _unpack_bf16_pairs(rows_ref[kk])
        acc_lo = acc_lo + gate[:, kk:kk + 1] * lo
        acc_hi = acc_hi + gate[:, kk:kk + 1] * hi
    acc = jnp.concatenate([acc_lo, acc_hi], axis=1)
    g2 = mod_ref[0:1, 5 * D_MODEL:6 * D_MODEL]
    o_ref[...] = _rms(base_ref[...] + g2 * acc, fg_ref[...])


def _combine(dest_km, ye, base, gate, mod, fg):
    t = base.shape[0]
    tm = TOK_TILE
    rows = _sc_gather_rows(ye, dest_km.reshape(-1)).reshape(TOP_K, t, HALF)
    full = lambda arr: pl.BlockSpec(arr.shape, lambda i: (0,) * arr.ndim)
    return pl.pallas_call(
        _combine_kernel,
        grid=(t // tm,),
        in_specs=[pl.BlockSpec((TOP_K, tm, HALF), lambda i: (0, i, 0)),
                  pl.BlockSpec((tm, D_MODEL), lambda i: (i, 0)),
                  pl.BlockSpec((tm, LANES), lambda i: (i, 0)),
                  full(mod), full(fg)],
        out_specs=pl.BlockSpec((tm, D_MODEL), lambda i: (i, 0)),
        out_shape=jax.ShapeDtypeStruct((t, D_MODEL), F32),
        compiler_params=_cparams(("arbitrary",)),
        name="combine",
    )(rows, base, gate, mod, fg)


def _block_tables(counts, n_blocks):
    sb = SLOT_BLOCK
    padded = (counts + sb - 1) // sb * sb
    pad_end = jnp.cumsum(padded)
    pad_start = pad_end - padded
    n_used = pad_end[-1] // sb
    blk = jnp.minimum(jnp.arange(n_blocks, dtype=jnp.int32), n_used - 1)
    block_e = jnp.minimum(jnp.sum(pad_end[None, :] <= (blk * sb)[:, None], axis=1), N_EXPERTS - 1).astype(jnp.int32)
    onehot = block_e[:, None] == jnp.arange(N_EXPERTS, dtype=jnp.int32)[None, :]
    live_end = jnp.sum(jnp.where(onehot, (pad_start + counts)[None, :], 0), axis=1)
    n_valid = jnp.clip(live_end - blk * sb, 0, sb).astype(jnp.int32)
    return block_e, n_used.reshape(1).astype(jnp.int32), n_valid, pad_start


def _blockdiag2(w):
    z = jnp.zeros_like(w[0])
    return jnp.concatenate([jnp.concatenate([w[0], z], axis=1), jnp.concatenate([z, w[1]], axis=1)], axis=0)


def _pad_rows(w, n=8):
    return jnp.concatenate([w, jnp.zeros((n - w.shape[0],) + w.shape[1:], w.dtype)], axis=0)


def kernel(x, c, ctx, c_ctx, ada_w, ada_b, norm1_g, norm2_g, w_in, conv_w, shift_mu, decay_w0, decay_up, iclr_a0, iclr_up, key_xi, key_alpha, bonus_rho, gate_up, gn_w, gn_b, w_out, router_w, router_bias, exp_w1, exp_w3, exp_w2, sh_w1, sh_w3, sh_w2, final_g):
    assert x.shape[0] == 1 and ada_w.shape[0] == 1
    assert ctx.shape[1] == TOK_TILE and x.shape[1] % TOK_TILE == 0
    l = 0
    xs = x[0]
    row = lambda w: w.reshape(1, -1)

    cc = _pad_rows(jnp.stack([c[0], c_ctx], axis=0))
    mod = _ada(cc, ada_w[l], row(ada_b[l]))

    hid = lax.broadcasted_iota(jnp.int32, (D_RWKV, D_RWKV), 0) // HEAD_DIM
    bsum = (hid == hid.T).astype(BF16)

    r4, v4, g, bonus, yconv, lw, kh, kt, a = _proj(
        ctx[0], xs, mod, row(norm1_g[l]), w_in[l].astype(BF16), _pad_rows(conv_w[l]), _pad_rows(shift_mu[l]),
        row(decay_w0[l]), _blockdiag2(decay_up[l]).astype(BF16), row(iclr_a0[l]), _blockdiag2(iclr_up[l]).astype(BF16),
        row(key_xi[l]), row(key_alpha[l]), row(bonus_rho[l]), gate_up[l].astype(BF16), bsum)

    y, w1b, w3b, w2b = _scan(r4, v4, lw, kh, kt, a, exp_w1[l], exp_w3[l], exp_w2[l])

    base, h2, idx, gate, rank, cnt = _mix(
        xs, y, g, bonus, yconv, mod, row(gn_w[l]), row(gn_b[l]), bsum, w_out[l].astype(BF16),
        row(norm2_g[l]), router_w[l], row(router_bias[l]),
        sh_w1[l].astype(BF16), sh_w3[l].astype(BF16), sh_w2[l].astype(BF16))

    n_blocks = xs.shape[0] * TOP_K // SLOT_BLOCK + N_EXPERTS
    block_e, n_used, n_valid, pad_start = _block_tables(cnt[0].astype(jnp.int32), n_blocks)
    dest_km = _slots(idx, rank, pad_start.astype(F32).reshape(1, N_EXPERTS))[:, :TOP_K].T
    hs = _sc_scatter_rows(h2, dest_km, n_blocks * SLOT_BLOCK)
    ye = _experts(block_e, n_used, n_valid, hs, w1b, w3b, w2b)
    out = _combine(dest_km, ye, base, gate, mod, row(final_g))
    return out[None]
```

```python
import functools

import jax
import jax.numpy as jnp
from jax import lax
from jax.experimental import pallas as pl
from jax.experimental.pallas import tpu as pltpu
from jax.experimental.pallas import tpu_sc as plsc

F32 = jnp.float32
BF16 = jnp.bfloat16

D_MODEL = 1024
D_CONV = 512
D_RWKV = 512
HEAD_DIM = 64
N_HEADS = D_RWKV // HEAD_DIM
N_PAIRS = N_HEADS // 2
LORA = 128
P_RWKV = 3 * D_RWKV + 3 * LORA
P_IN = 3 * D_CONV + P_RWKV
GRID_W = 64
N_EXPERTS = 256
TOP_K = 8
D_EXPERT = 256
ROUTED_SCALE = 2.5
RMS_EPS = 1e-6
GN_EPS = 64e-5
DECAY_SCALE = 0.6065306597126334
NORM_EPS = 1e-12

TOK_TILE = 256
SLOTS_TILE = 1024
SCAN_BLOCK = 256
CHUNK = 64
SLOT_BLOCK = 512
COMBINE_PARTS = 2
SC_CHUNK = 64
SC_SCATTER_CHUNK = 128
LANES = 128
VMEM_LIMIT = 56 * 1024 * 1024

NN = ((1,), (0,))
NT = ((1,), (1,))
TN = ((0,), (0,))


def _dg(a, b, dims=NN):
    return lax.dot_general(a, b, (dims, ((), ())), preferred_element_type=F32)


def _split2(a):
    hi = a.astype(BF16)
    lo = (a - hi.astype(F32)).astype(BF16)
    return hi, lo


def _split3(a):
    hi = a.astype(BF16)
    r1 = a - hi.astype(F32)
    mid = r1.astype(BF16)
    lo = (r1 - mid.astype(F32)).astype(BF16)
    return hi, mid, lo


def _mm1(a, b, dims=NN):
    return _dg(a.astype(BF16), b.astype(BF16), dims)


def _mm3(a, b, dims=NN):
    ah, al = _split2(a)
    bh, bl = _split2(b)
    return _dg(ah, bh, dims) + (_dg(ah, bl, dims) + _dg(al, bh, dims))


def _mm_split_lhs(a, b_bf16, dims=NN):
    h, l = _split2(a)
    return _dg(h, b_bf16, dims) + _dg(l, b_bf16, dims)


def _mm_ones_rhs(a, ones_bf16):
    return _dg(a.astype(BF16), ones_bf16)


def _mm_ones_lhs(ones_bf16, b, dims=NN):
    h, m, l = _split3(b)
    return _dg(ones_bf16, h, dims) + (_dg(ones_bf16, m, dims) + _dg(ones_bf16, l, dims))


HALF = D_MODEL // 2
I32 = jnp.int32


def _pack_bf16_pairs(x):
    return pltpu.pack_elementwise([x[:, 0:HALF], x[:, HALF:]], packed_dtype=BF16)


def _unpack_bf16_pairs(w):
    lo = pltpu.unpack_elementwise(w, index=0, packed_dtype=BF16, unpacked_dtype=F32)
    hi = pltpu.unpack_elementwise(w, index=1, packed_dtype=BF16, unpacked_dtype=F32)
    return lo, hi


def _sigmoid(x):
    return 1.0 / (1.0 + jnp.exp(-x))


def _silu(x):
    return x * _sigmoid(x)


def _rms(xv, g):
    ms = jnp.mean(xv * xv, axis=-1, keepdims=True)
    return xv * lax.rsqrt(ms + RMS_EPS) * g


def _cparams(sem):
    return pltpu.CompilerParams(dimension_semantics=sem, vmem_limit_bytes=VMEM_LIMIT)


def _ada_kernel(c_ref, w_ref, b_ref, o_ref):
    o_ref[...] = _mm3(_silu(c_ref[...]), w_ref[...]) + b_ref[...]


def _ada(cc, ada_w, ada_b):
    n = ada_w.shape[1]
    tn = 1024
    return pl.pallas_call(
        _ada_kernel,
        grid=(n // tn,),
        in_specs=[pl.BlockSpec((8, D_MODEL), lambda j: (0, 0)),
                  pl.BlockSpec((D_MODEL, tn), lambda j: (0, j)),
                  pl.BlockSpec((1, tn), lambda j: (0, j))],
        out_specs=pl.BlockSpec((8, tn), lambda j: (0, j)),
        out_shape=jax.ShapeDtypeStruct((8, n), F32),
        compiler_params=_cparams(("arbitrary",)),
        name="ada",
    )(cc, ada_w, ada_b)


def _proj_kernel(ctx_ref, x_ref, xp_ref, xn_ref, mod_ref, n1g_ref, win_ref, convw_ref, mu_ref,
                 dw0_ref, dup_ref, ia0_ref, iup_ref, xi_ref, kal_ref, rho_ref, gup_ref, bsum_ref,
                 r_ref, v_ref, g_ref, bonus_ref, yconv_ref, lw_ref, kh_ref, kt_ref, a_ref):
    i = pl.program_id(0)
    nt = pl.num_programs(0)
    tm = x_ref.shape[0]
    is_ctx = i == 0
    sh = jnp.where(is_ctx, mod_ref[1:2, 0:D_MODEL], mod_ref[0:1, 0:D_MODEL])
    sc = jnp.where(is_ctx, mod_ref[1:2, D_MODEL:2 * D_MODEL], mod_ref[0:1, D_MODEL:2 * D_MODEL])
    n1g = n1g_ref[...]

    def norm_mod(xv):
        return _rms(xv, n1g) * (1.0 + sc) + sh

    xt = jnp.where(is_ctx, ctx_ref[...], x_ref[...])
    h = norm_mod(jnp.concatenate([xt, xp_ref[...], xn_ref[...]], axis=0)).astype(BF16)
    p_all = _dg(h, win_ref[...])
    p = p_all[0:tm]
    ph = p_all[tm:tm + 16, 3 * D_CONV:]

    rows = lax.broadcasted_iota(jnp.int32, (tm, 1), 0)

    bg = p[:, 0:D_CONV]
    z = p[:, D_CONV:2 * D_CONV] * p[:, 2 * D_CONV:3 * D_CONV]
    col = rows % GRID_W
    zp = jnp.where(col == 0, 0.0, pltpu.roll(z, 1, 0))
    zn = jnp.where(col == GRID_W - 1, 0.0, pltpu.roll(z, tm - 1, 0))
    yconv = bg * (convw_ref[0:1, :] * zp + convw_ref[1:2, :] * z + convw_ref[2:3, :] * zn)
    yconv_ref[...] = yconv.astype(BF16)

    prev_ok = jnp.logical_and(i != 0, i != 1).astype(F32)
    next_ok = jnp.logical_and(i != 0, i != nt - 1).astype(F32)
    cur = p[:, 3 * D_CONV:]
    prev = jnp.where(rows == 0, ph[7:8, :] * prev_ok, pltpu.roll(cur, 1, 0))
    nxt = jnp.where(rows == tm - 1, ph[8:9, :] * next_ok, pltpu.roll(cur, tm - 1, 0))
    ps = cur + mu_ref[0:1, :] * (prev - cur) + mu_ref[1:2, :] * (nxt - cur)

    r = ps[:, 0:D_RWKV]
    k = ps[:, D_RWKV:2 * D_RWKV]
    v = ps[:, 2 * D_RWKV:3 * D_RWKV]
    o = 3 * D_RWKV
    wlo = ps[:, o:o + LORA]
    alo = ps[:, o + LORA:o + 2 * LORA]
    glo = ps[:, o + 2 * LORA:o + 3 * LORA]

    dd = dw0_ref[...] + _mm_split_lhs(jnp.tanh(wlo), dup_ref[...])
    lw = -DECAY_SCALE * _sigmoid(dd)
    a = _sigmoid(ia0_ref[...] + _mm_split_lhs(alo, iup_ref[...]))
    g = _dg(_sigmoid(glo).astype(BF16), gup_ref[...])
    k2 = jnp.concatenate([k, k], axis=1)
    kap = k2 * xi_ref[...]
    kap2 = kap * kap
    bs = bsum_ref[...]
    ss = jnp.concatenate([_mm_ones_rhs(kap2[:, 0:D_RWKV], bs),
                          _mm_ones_rhs(kap2[:, D_RWKV:], bs)], axis=1)
    kh = kap * lax.rsqrt(ss + NORM_EPS)
    kt = k2 * (1.0 + (a - 1.0) * kal_ref[...])
    bon = _mm_ones_rhs(r * rho_ref[...] * (kt[:, 0:D_RWKV] + kt[:, D_RWKV:]), bs)

    g_ref[...] = g
    bonus_ref[...] = bon * v
    for pr in range(N_PAIRS):
        ls = slice(pr * LANES, (pr + 1) * LANES)
        r_ref[pr] = r[:, ls]
        v_ref[pr] = v[:, ls]
        for d in range(2):
            ld = slice(d * D_RWKV + pr * LANES, d * D_RWKV + (pr + 1) * LANES)
            lw_ref[d, pr] = lw[:, ld]
            kh_ref[d, pr] = kh[:, ld]
            kt_ref[d, pr] = kt[:, ld]
            a_ref[d, pr] = a[:, ld]


def _proj(ctx2, x2, mod, n1g, win, convw, mu, dw0, dup, ia0, iup, xi, kal, rho, gup, bsum):
    tm = TOK_TILE
    t = ctx2.shape[0] + x2.shape[0]
    nt = t // tm
    per = tm // 8
    nb8 = x2.shape[0] // 8
    full = lambda arr: pl.BlockSpec(arr.shape, lambda i: (0,) * arr.ndim)
    pair_spec = pl.BlockSpec((N_PAIRS, tm, LANES), lambda i: (0, i, 0))
    dpair_spec = pl.BlockSpec((2, N_PAIRS, tm, LANES), lambda i: (0, 0, i, 0))
    row_spec = pl.BlockSpec((tm, D_RWKV), lambda i: (i, 0))
    pair_shape = jax.ShapeDtypeStruct((N_PAIRS, t, LANES), F32)
    dpair_shape = jax.ShapeDtypeStruct((2, N_PAIRS, t, LANES), F32)
    consts = (mod, n1g, win, convw, mu, dw0, dup, ia0, iup, xi, kal, rho, gup, bsum)
    return pl.pallas_call(
        _proj_kernel,
        grid=(nt,),
        in_specs=[pl.BlockSpec((tm, D_MODEL), lambda i: (0, 0)),
                  pl.BlockSpec((tm, D_MODEL), lambda i: (jnp.maximum(i - 1, 0), 0)),
                  pl.BlockSpec((8, D_MODEL), lambda i: (jnp.maximum((i - 1) * per - 1, 0), 0)),
                  pl.BlockSpec((8, D_MODEL), lambda i: (jnp.clip(i * per, 0, nb8 - 1), 0))]
                 + [full(c) for c in consts],
        out_specs=[pair_spec, pair_spec, row_spec, row_spec, row_spec,
                   dpair_spec, dpair_spec, dpair_spec, dpair_spec],
        out_shape=[pair_shape, pair_shape,
                   jax.ShapeDtypeStruct((t, D_RWKV), F32), jax.ShapeDtypeStruct((t, D_RWKV), F32),
                   jax.ShapeDtypeStruct((t, D_CONV), BF16),
                   dpair_shape, dpair_shape, dpair_shape, dpair_shape],
        compiler_params=_cparams(("arbitrary",)),
        name="proj",
    )(ctx2, x2, x2, x2, *consts)


def _scan_kernel(r_ref, v_ref, lw_ref, kh_ref, kt_ref, a_ref, w1_ref, w3_ref, w2_ref,
                 y_ref, w1b_ref, w3b_ref, w2b_ref, q_ref):
    w1b_ref[...] = w1_ref[...].astype(BF16)
    w3b_ref[...] = w3_ref[...].astype(BF16)
    w2b_ref[...] = w2_ref[...].astype(BF16)

    d = pl.program_id(0)
    rev = d == 1
    c = CHUNK
    npair = r_ref.shape[0]
    nsub = r_ref.shape[1] // c

    row = lax.broadcasted_iota(jnp.int32, (c, c), 0)
    col = lax.broadcasted_iota(jnp.int32, (c, c), 1)
    strict = (col - row) * (1 - 2 * d) < 0
    eye = row == col
    incl = jnp.logical_or(strict, eye)
    incl_bf = jnp.where(incl, 1.0, 0.0).astype(BF16)
    eye_f = jnp.where(eye, 1.0, 0.0)
    head0 = lax.broadcasted_iota(jnp.int32, (c, LANES), 1) < HEAD_DIM
    row2 = lax.broadcasted_iota(jnp.int32, (LANES, LANES), 0)
    col2 = lax.broadcasted_iota(jnp.int32, (LANES, LANES), 1)
    same_head = (row2 < HEAD_DIM) == (col2 < HEAD_DIM)
    eye2 = row2 == col2

    subs = range(npair * nsub)
    units = [(s, h) for s in subs for h in range(2)]
    off = [pl.multiple_of(jnp.where(rev, nsub - 1 - s, s) * c, c) for s in range(nsub)]
    sls = [(pp, pl.ds(off[s], c), slice(None)) for pp in range(npair) for s in range(nsub)]
    v = [v_ref[sl] for sl in sls]
    lw = [lw_ref[sl] for sl in sls]
    cum = [_mm_ones_lhs(incl_bf, lw[s]) for s in subs]
    al, be, kk, rr, bew, kkw, wc = [], [], [], [], [], [], []
    for s in subs:
        cum_last = jnp.where(rev, cum[s][0:1, :], cum[s][c - 1:c, :])
        e_end = jnp.exp(cum_last - cum[s])
        e_neg = jnp.exp(-cum[s])
        kh = kh_ref[sls[s]]
        kt = kt_ref[sls[s]]
        kha = kh * a_ref[sls[s]]
        al.append(kh * jnp.exp(cum[s] - lw[s]))
        be.append(-(kha * e_neg))
        kk.append(kt * e_neg)
        rr.append(r_ref[sls[s]] * jnp.exp(cum[s]))
        bew.append(-(kha * e_end))
        kkw.append(kt * e_end)
        wc.append(jnp.exp(cum_last))

    lhs = {}
    for s, h in units:
        hm = head0 if h == 0 else jnp.logical_not(head0)
        lhs[s, h] = jnp.concatenate([jnp.where(hm, al[s], 0.0), jnp.where(hm, rr[s], 0.0)], axis=0)
    gb = {u: _mm1(lhs[u], be[u[0]], NT) for u in units}
    gk = {u: _mm1(lhs[u], kk[u[0]], NT) for u in units}
    a_ab = {u: jnp.where(strict, gb[u][0:c], 0.0) for u in units}
    a_ak = {u: jnp.where(strict, gk[u][0:c], 0.0) for u in units}
    a_rb = {u: jnp.where(incl, gb[u][c:2 * c], 0.0) for u in units}
    a_rk = {u: jnp.where(incl, gk[u][c:2 * c], 0.0) for u in units}

    pw = dict(a_ab)
    tinv = {u: eye_f + a_ab[u] for u in units}
    akrk_v = {u: _mm1(jnp.concatenate([a_ak[u], a_rk[u]], axis=0), v[u[0]]) for u in units}
    avk = {u: akrk_v[u][0:c] for u in units}
    rk_v = {u: akrk_v[u][c:2 * c] for u in units}
    for _ in range(c.bit_length() - 2):
        pw = {u: _mm1(pw[u], pw[u]) for u in units}
        tinv = {u: tinv[u] + _mm1(tinv[u], pw[u]) for u in units}
    tu = {u: _mm1(tinv[u], jnp.concatenate([al[u[0]], avk[u]], axis=1)) for u in units}
    at = [jnp.where(head0, tu[s, 0][:, 0:LANES], tu[s, 1][:, 0:LANES]) for s in subs]
    ut = [jnp.where(head0, tu[s, 0][:, LANES:], tu[s, 1][:, LANES:]) for s in subs]

    rb = {u: _mm1(a_rb[u], jnp.concatenate([at[u[0]], ut[u[0]]], axis=1)) for u in units}
    rb_at = {u: rb[u][:, 0:LANES] for u in units}
    rb_ut = {u: rb[u][:, LANES:] for u in units}
    md = [_mm1(bew[s], jnp.concatenate([at[s], ut[s]], axis=1), TN) for s in subs]
    m_off = [md[s][:, 0:LANES] for s in subs]
    d_u = [md[s][:, LANES:] for s in subs]
    d_v = [_mm1(kkw[s], v[s], TN) for s in subs]
    rh = [rr[s] + jnp.where(head0, rb_at[s, 0], rb_at[s, 1]) for s in subs]
    yh = [jnp.where(head0, rb_ut[s, 0] + rk_v[s, 0], rb_ut[s, 1] + rk_v[s, 1]) for s in subs]
    mt = [jnp.where(eye2, wc[s], 0.0) + jnp.where(same_head, m_off[s], 0.0) for s in subs]
    dt = [jnp.where(same_head, d_u[s] + d_v[s], 0.0) for s in subs]

    @pl.when(pl.program_id(1) == 0)
    def _():
        q_ref[...] = jnp.zeros(q_ref.shape, F32)

    q = [q_ref[pp] for pp in range(npair)]
    for si in range(nsub):
        ys = [_mm1(rh[pp * nsub + si], q[pp]) for pp in range(npair)]
        qn = [_mm1(mt[pp * nsub + si], q[pp]) for pp in range(npair)]
        for pp in range(npair):
            y_ref[sls[pp * nsub + si]] = ys[pp] + yh[pp * nsub + si]
            q[pp] = qn[pp] + dt[pp * nsub + si]
    for pp in range(npair):
        q_ref[pp] = q[pp]


def _scan(r4, v4, lw, kh, kt, a, w1, w3, w2):
    t = r4.shape[1]
    tb = SCAN_BLOCK
    nb = t // tb
    ne = w1.shape[0]
    epb = -(-ne // (2 * nb))
    assert ne % epb == 0
    last = ne // epb - 1

    def blk(d, j):
        return jnp.where(d == 0, j, jnp.where(j == 0, 0, nb - j))

    def wblk(d, j):
        return (jnp.minimum(d * nb + j, last), 0, 0)

    shared = pl.BlockSpec((N_PAIRS, tb, LANES), lambda d, j: (0, blk(d, j), 0))
    perdir = pl.BlockSpec((None, N_PAIRS, tb, LANES), lambda d, j: (d, 0, blk(d, j), 0))
    w13 = pl.BlockSpec((epb, D_MODEL, D_EXPERT), wblk)
    w2s = pl.BlockSpec((epb, D_EXPERT, D_MODEL), wblk)
    return pl.pallas_call(
        _scan_kernel,
        grid=(2, nb),
        in_specs=[shared, shared, perdir, perdir, perdir, perdir, w13, w13, w2s],
        out_specs=[perdir, w13, w13, w2s],
        out_shape=[jax.ShapeDtypeStruct((2, N_PAIRS, t, LANES), F32),
                   jax.ShapeDtypeStruct(w1.shape, BF16), jax.ShapeDtypeStruct(w3.shape, BF16),
                   jax.ShapeDtypeStruct(w2.shape, BF16)],
        scratch_shapes=[pltpu.VMEM((N_PAIRS, LANES, LANES), F32)],
        compiler_params=_cparams(("arbitrary", "arbitrary")),
        name="scan",
    )(r4, v4, lw, kh, kt, a, w1, w3, w2)


def _mix_kernel(x_ref, yf_ref, yb_ref, g_ref, bonus_ref, yconv_ref, mod_ref, gnw_ref, gnb_ref, bsum_ref,
                wout_ref, n2g_ref, rw_ref, rb_ref, sw1_ref, sw3_ref, sw2_ref,
                base_ref, h2_ref, idx_ref, gate_ref, rank_ref, cnt_ref, run_ref):
    tm = x_ref.shape[0]
    g1 = mod_ref[0:1, 2 * D_MODEL:3 * D_MODEL]
    sh2 = mod_ref[0:1, 3 * D_MODEL:4 * D_MODEL]
    sc2 = mod_ref[0:1, 4 * D_MODEL:5 * D_MODEL]
    g2 = mod_ref[0:1, 5 * D_MODEL:6 * D_MODEL]

    @pl.when(pl.program_id(0) == 0)
    def _():
        run_ref[...] = jnp.zeros(run_ref.shape, F32)

    y = jnp.concatenate([yf_ref[pr] + yb_ref[pr] for pr in range(N_PAIRS)], axis=1)
    bs = bsum_ref[...]
    mu = _mm_ones_rhs(y, bs) * (1.0 / HEAD_DIM)
    yc = y - mu
    var = _mm_ones_rhs(yc * yc, bs) * (1.0 / HEAD_DIM)
    yn = yc * lax.rsqrt(var + GN_EPS) * gnw_ref[...] + gnb_ref[...]
    yrw = ((yn + bonus_ref[...]) * g_ref[...]).astype(BF16)
    mix = _dg(yconv_ref[...], wout_ref[0:D_CONV, :]) + _dg(yrw, wout_ref[D_CONV:, :])
    x1 = x_ref[...] + g1 * mix

    h2 = _rms(x1, n2g_ref[...]) * (1.0 + sc2) + sh2
    hb = h2.astype(BF16)
    h2_ref[...] = _pack_bf16_pairs(h2)

    scores = _sigmoid(_mm3(h2, rw_ref[...]))
    work = scores + rb_ref[...]
    lane_e = lax.broadcasted_iota(jnp.int32, (tm, N_EXPERTS), 1).astype(F32)
    lane_o = lax.broadcasted_iota(jnp.int32, (tm, LANES), 1)
    idx_acc = jnp.zeros((tm, LANES), F32)
    gate_acc = jnp.zeros((tm, LANES), F32)
    gsum = jnp.zeros((tm, 1), F32)
    chosen = jnp.zeros((tm, N_EXPERTS), F32)
    sels = []
    for kk in range(TOP_K):
        m = jnp.max(work, axis=-1, keepdims=True)
        sel = jnp.min(jnp.where(work == m, lane_e, float(N_EXPERTS)), axis=-1, keepdims=True)
        hit = lane_e == sel
        sk = jnp.sum(jnp.where(hit, scores, 0.0), axis=-1, keepdims=True)
        idx_acc = jnp.where(lane_o == kk, sel, idx_acc)
        gate_acc = jnp.where(lane_o == kk, sk, gate_acc)
        gsum = gsum + sk
        work = jnp.where(hit, -jnp.inf, work)
        sels.append(sel)
    chosen = jnp.where(work == -jnp.inf, 1.0, 0.0)
    idx_ref[...] = idx_acc.astype(jnp.int32)
    gate_ref[...] = gate_acc / gsum * ROUTED_SCALE

    trow = lax.broadcasted_iota(jnp.int32, (tm, tm), 0)
    tcol = lax.broadcasted_iota(jnp.int32, (tm, tm), 1)
    earlier = jnp.where(tcol < trow, 1.0, 0.0).astype(BF16)
    before = _dg(earlier, chosen.astype(BF16)) + run_ref[0:1, :]
    rank_acc = jnp.zeros((tm, LANES), F32)
    for kk in range(TOP_K):
        rk = jnp.sum(jnp.where(lane_e == sels[kk], before, 0.0), axis=-1, keepdims=True)
        rank_acc = jnp.where(lane_o == kk, rk, rank_acc)
    rank_ref[...] = rank_acc.astype(jnp.int32)
    run_ref[...] = run_ref[...] + jnp.sum(chosen, axis=0, keepdims=True)
    cnt_ref[...] = run_ref[...]

    act = (_silu(_dg(hb, sw1_ref[...])) * _dg(hb, sw3_ref[...])).astype(BF16)
    base_ref[...] = x1 + g2 * _dg(act, sw2_ref[...])


def _mix(x, y, g, bonus, yconv, mod, gnw, gnb, bsum, wout, n2g, rw, rb, sw1, sw3, sw2):
    t = x.shape[0]
    tm = TOK_TILE
    off = (y.shape[2] - t) // tm
    full = lambda arr: pl.BlockSpec(arr.shape, lambda i: (0,) * arr.ndim)
    consts = (mod, gnw, gnb, bsum, wout, n2g, rw, rb, sw1, sw3, sw2)
    cat_spec = lambda w: pl.BlockSpec((tm, w), lambda i: (i + off, 0))
    tok_spec = lambda w: pl.BlockSpec((tm, w), lambda i: (i, 0))
    yf_spec = pl.BlockSpec((None, N_PAIRS, tm, LANES), lambda i: (0, 0, i + off, 0))
    yb_spec = pl.BlockSpec((None, N_PAIRS, tm, LANES), lambda i: (1, 0, i + off, 0))
    return pl.pallas_call(
        _mix_kernel,
        grid=(t // tm,),
        in_specs=[tok_spec(D_MODEL), yf_spec, yb_spec,
                  cat_spec(D_RWKV), cat_spec(D_RWKV), cat_spec(D_CONV)]
                 + [full(c) for c in consts],
        out_specs=[tok_spec(D_MODEL), tok_spec(HALF), tok_spec(LANES), tok_spec(LANES), tok_spec(LANES),
                   pl.BlockSpec((8, N_EXPERTS), lambda i: (0, 0))],
        out_shape=[jax.ShapeDtypeStruct((t, D_MODEL), F32), jax.ShapeDtypeStruct((t, HALF), I32),
                   jax.ShapeDtypeStruct((t, LANES), jnp.int32), jax.ShapeDtypeStruct((t, LANES), F32),
                   jax.ShapeDtypeStruct((t, LANES), jnp.int32), jax.ShapeDtypeStruct((8, N_EXPERTS), F32)],
        scratch_shapes=[pltpu.VMEM((8, N_EXPERTS), F32)],
        compiler_params=_cparams(("arbitrary",)),
        name="mix",
    )(x, y, y, g, bonus, yconv, *consts)


def _slots_kernel(idx_ref, rank_ref, start_ref, dest_ref):
    tm = idx_ref.shape[0]
    lane_e = lax.broadcasted_iota(jnp.int32, (tm, N_EXPERTS), 1)
    lane_o = lax.broadcasted_iota(jnp.int32, (tm, LANES), 1)
    idx = idx_ref[...]
    start = start_ref[...]
    acc = jnp.zeros((tm, LANES), F32)
    for kk in range(TOP_K):
        st = jnp.sum(jnp.where(lane_e == idx[:, kk:kk + 1], start, 0.0), axis=-1, keepdims=True)
        acc = jnp.where(lane_o == kk, st, acc)
    dest_ref[...] = acc.astype(jnp.int32) + rank_ref[...]


def _slots(idx, rank, pad_start):
    t = idx.shape[0]
    tm = SLOTS_TILE
    spec = pl.BlockSpec((tm, LANES), lambda i: (i, 0))
    return pl.pallas_call(
        _slots_kernel,
        grid=(t // tm,),
        in_specs=[spec, spec, pl.BlockSpec((1, N_EXPERTS), lambda i: (0, 0))],
        out_specs=spec,
        out_shape=jax.ShapeDtypeStruct((t, LANES), jnp.int32),
        compiler_params=_cparams(("arbitrary",)),
        name="slots",
    )(idx, rank, pad_start)


def _sc_scatter_rows(rows, dest_km, n_slots):
    info = plsc.get_sparse_core_info()
    nc, ns = info.num_cores, info.num_subcores
    nw = nc * ns
    t, d = rows.shape
    nk = dest_km.shape[0]
    ch = SC_SCATTER_CHUNK
    per_w = t // nw
    n_ch = per_w // ch
    assert t % nw == 0 and per_w % ch == 0
    mesh = plsc.VectorSubcoreMesh(core_axis_name="c", subcore_axis_name="s")
    idx = dest_km.reshape(nk, nw, n_ch, ch).transpose(1, 2, 0, 3).reshape(nw, n_ch * nk, ch)

    @functools.partial(
        pl.kernel, mesh=mesh,
        out_type=jax.ShapeDtypeStruct((n_slots, d), rows.dtype),
        scratch_types=[pltpu.VMEM((n_ch * nk, ch), jnp.int32),
                       pltpu.VMEM((ch, d), rows.dtype),
                       pltpu.SemaphoreType.DMA],
    )
    def scatter_kernel(rows_hbm, idx_hbm, out_hbm, idx_v, rows_v, sem):
        wid = lax.axis_index("s") * nc + lax.axis_index("c")
        base = wid * per_w
        pltpu.sync_copy(idx_hbm.at[wid], idx_v)

        @pl.loop(0, n_ch)
        def _(j):
            pltpu.sync_copy(rows_hbm.at[pl.ds(base + j * ch, ch)], rows_v)
            copies = [pltpu.make_async_copy(rows_v, out_hbm.at[idx_v.at[j * nk + kk]], sem) for kk in range(nk)]
            for cp in copies:
                cp.start()
            for cp in copies:
                cp.wait()

    return scatter_kernel(rows, idx)


def _experts_kernel(be_ref, nu_ref, nv_ref, hs_ref, w1c, w3c, w2c, o_ref):
    del be_ref
    b = pl.program_id(0)

    @pl.when(b < nu_ref[0])
    def _():
        live = lax.broadcasted_iota(jnp.int32, (hs_ref.shape[0], 1), 0) < nv_ref[b]
        lo, hi = _unpack_bf16_pairs(jnp.where(live, hs_ref[...], 0))
        lo = lo.astype(BF16)
        hi = hi.astype(BF16)
        a1 = _dg(lo, w1c[0:HALF, :]) + _dg(hi, w1c[HALF:, :])
        a3 = _dg(lo, w3c[0:HALF, :]) + _dg(hi, w3c[HALF:, :])
        act = (_silu(a1) * a3).astype(BF16)
        o_ref[...] = _pack_bf16_pairs(_dg(act, w2c[...]))

    @pl.when(b >= nu_ref[0])
    def _():
        o_ref[...] = jnp.zeros(o_ref.shape, I32)


def _experts(block_e, n_used, n_valid, hs, w1, w3, w2):
    nblk = block_e.shape[0]
    sb = SLOT_BLOCK
    row_map = lambda b, be, nu, nv: (jnp.minimum(b, nu[0] - 1), 0)
    w_map = lambda b, be, nu, nv: (be[b], 0, 0)
    grid_spec = pltpu.PrefetchScalarGridSpec(
        num_scalar_prefetch=3,
        grid=(nblk,),
        in_specs=[pl.BlockSpec((sb, HALF), row_map),
                  pl.BlockSpec((None, D_MODEL, D_EXPERT), w_map),
                  pl.BlockSpec((None, D_MODEL, D_EXPERT), w_map),
                  pl.BlockSpec((None, D_EXPERT, D_MODEL), w_map)],
        out_specs=pl.BlockSpec((sb, HALF), lambda b, be, nu, nv: (b, 0)),
    )
    return pl.pallas_call(
        _experts_kernel,
        grid_spec=grid_spec,
        out_shape=jax.ShapeDtypeStruct((nblk * sb, HALF), I32),
        compiler_params=_cparams(("arbitrary",)),
        name="experts",
    )(block_e, n_used, n_valid, hs, w1, w3, w2)


def _sc_gather_rows(table, idx):
    info = plsc.get_sparse_core_info()
    nc, ns = info.num_cores, info.num_subcores
    nw = nc * ns
    n = idx.shape[0]
    d = table.shape[1]
    ch = SC_CHUNK
    per_w = n // nw
    n_ch = per_w // ch
    assert n % nw == 0 and per_w % (2 * ch) == 0
    mesh = plsc.VectorSubcoreMesh(core_axis_name="c", subcore_axis_name="s")

    @functools.partial(
        pl.kernel, mesh=mesh,
        out_type=jax.ShapeDtypeStruct((n, d), table.dtype),
        scratch_types=[pltpu.VMEM((n_ch, ch), jnp.int32),
                       pltpu.VMEM((2, ch, d), table.dtype),
                       pltpu.SemaphoreType.DMA((2,))],
    )
    def gather_kernel(table_hbm, idx_hbm, out_hbm, idx_v, rows_v, sem):
        wid = lax.axis_index("s") * nc + lax.axis_index("c")
        base = wid * per_w
        pltpu.sync_copy(idx_hbm.at[wid], idx_v)

        def gather(j, buf):
            return pltpu.make_async_copy(table_hbm.at[idx_v.at[j]], rows_v.at[buf], sem.at[buf])

        gather(0, 0).start()

        @pl.loop(0, n_ch, step=2)
        def _(j):
            for buf in range(2):
                jj = j + buf
                gather(jj, buf).wait()

                @pl.when(jj + 1 < n_ch)
                def _():
                    gather(jj + 1, 1 - buf).start()

                pltpu.sync_copy(rows_v.at[buf], out_hbm.at[pl.ds(base + jj * ch, ch)])

    return gather_kernel(table, idx.reshape(nw, n_ch, ch))


def _combine_kernel(rows_ref, base_ref, gate_ref, mod_ref, fg_ref, *rest):
    o_ref = rest[-1]
    tm = base_ref.shape[0]
    gate = gate_ref[...]
    acc_lo = jnp.zeros((tm, HALF), F32)
    acc_hi = jnp.zeros((tm, HALF), F32)
    for kk in range(TOP_K):
        lo, hi = _unpack_bf16_pairs(rows_ref[kk])
        acc_lo = acc_lo + gate[:, kk:kk + 1] * lo
        acc_hi = acc_hi + gate[:, kk:kk + 1] * hi
    acc = jnp.concatenate([acc_lo, acc_hi], axis=1)
    g2 = mod_ref[0:1, 5 * D_MODEL:6 * D_MODEL]
    o_ref[...] = _rms(base_ref[...] + g2 * acc, fg_ref[...])


def _combine(dest_km, ye, base, gate, mod, fg):
    t = base.shape[0]
    tm = TOK_TILE
    full = lambda arr: pl.BlockSpec(arr.shape, lambda i: (0,) * arr.ndim)
    tp = t // COMBINE_PARTS
    nper = tp // tm
    out = None
    for part in range(COMBINE_PARTS):
        rows = _sc_gather_rows(ye, dest_km[:, part * tp:(part + 1) * tp].reshape(-1)).reshape(TOP_K, tp, HALF)
        tok_map = lambda i, part=part: (i + part * nper, 0)
        in_specs = [pl.BlockSpec((TOP_K, tm, HALF), lambda i: (0, i, 0)),
                    pl.BlockSpec((tm, D_MODEL), tok_map),
                    pl.BlockSpec((tm, LANES), tok_map),
                    full(mod), full(fg)]
        args = [rows, base, gate, mod, fg]
        aliases = {}
        if out is not None:
            in_specs.append(pl.BlockSpec(memory_space=pl.ANY))
            args.append(out)
            aliases = {len(args) - 1: 0}
        out = pl.pallas_call(
            _combine_kernel,
            grid=(nper,),
            in_specs=in_specs,
            out_specs=pl.BlockSpec((tm, D_MODEL), tok_map),
            out_shape=jax.ShapeDtypeStruct((t, D_MODEL), F32),
            input_output_aliases=aliases,
            compiler_params=_cparams(("arbitrary",)),
            name="combine",
        )(*args)
    return out


def _block_tables(counts, n_blocks):
    sb = SLOT_BLOCK
    padded = (counts + sb - 1) // sb * sb
    pad_end = jnp.cumsum(padded)
    pad_start = pad_end - padded
    n_used = pad_end[-1] // sb
    blk = jnp.minimum(jnp.arange(n_blocks, dtype=jnp.int32), n_used - 1)
    block_e = jnp.minimum(jnp.sum(pad_end[None, :] <= (blk * sb)[:, None], axis=1), N_EXPERTS - 1).astype(jnp.int32)
    onehot = block_e[:, None] == jnp.arange(N_EXPERTS, dtype=jnp.int32)[None, :]
    live_end = jnp.sum(jnp.where(onehot, (pad_start + counts)[None, :], 0), axis=1)
    n_valid = jnp.clip(live_end - blk * sb, 0, sb).astype(jnp.int32)
    return block_e, n_used.reshape(1).astype(jnp.int32), n_valid, pad_start


def _blockdiag2(w):
    z = jnp.zeros_like(w[0])
    return jnp.concatenate([jnp.concatenate([w[0], z], axis=1), jnp.concatenate([z, w[1]], axis=1)], axis=0)


def _pad_rows(w, n=8):
    return jnp.concatenate([w, jnp.zeros((n - w.shape[0],) + w.shape[1:], w.dtype)], axis=0)


def kernel(x, c, ctx, c_ctx, ada_w, ada_b, norm1_g, norm2_g, w_in, conv_w, shift_mu, decay_w0, decay_up, iclr_a0, iclr_up, key_xi, key_alpha, bonus_rho, gate_up, gn_w, gn_b, w_out, router_w, router_bias, exp_w1, exp_w3, exp_w2, sh_w1, sh_w3, sh_w2, final_g):
    assert x.shape[0] == 1 and ada_w.shape[0] == 1
    assert ctx.shape[1] == TOK_TILE and x.shape[1] % TOK_TILE == 0
    l = 0
    xs = x[0]
    row = lambda w: w.reshape(1, -1)

    cc = _pad_rows(jnp.stack([c[0], c_ctx], axis=0))
    mod = _ada(cc, ada_w[l], row(ada_b[l]))

    hid = lax.broadcasted_iota(jnp.int32, (D_RWKV, D_RWKV), 0) // HEAD_DIM
    bsum = (hid == hid.T).astype(BF16)

    r4, v4, g, bonus, yconv, lw, kh, kt, a = _proj(
        ctx[0], xs, mod, row(norm1_g[l]), w_in[l].astype(BF16), _pad_rows(conv_w[l]), _pad_rows(shift_mu[l]),
        row(decay_w0[l]), _blockdiag2(decay_up[l]).astype(BF16), row(iclr_a0[l]), _blockdiag2(iclr_up[l]).astype(BF16),
        row(key_xi[l]), row(key_alpha[l]), row(bonus_rho[l]), gate_up[l].astype(BF16), bsum)

    y, w1b, w3b, w2b = _scan(r4, v4, lw, kh, kt, a, exp_w1[l], exp_w3[l], exp_w2[l])

    base, h2, idx, gate, rank, cnt = _mix(
        xs, y, g, bonus, yconv, mod, row(gn_w[l]), row(gn_b[l]), bsum, w_out[l].astype(BF16),
        row(norm2_g[l]), router_w[l], row(router_bias[l]),
        sh_w1[l].astype(BF16), sh_w3[l].astype(BF16), sh_w2[l].astype(BF16))

    n_blocks = xs.shape[0] * TOP_K // SLOT_BLOCK + N_EXPERTS
    block_e, n_used, n_valid, pad_start = _block_tables(cnt[0].astype(jnp.int32), n_blocks)
    dest_km = _slots(idx, rank, pad_start.astype(F32).reshape(1, N_EXPERTS))[:, :TOP_K].T
    hs = _sc_scatter_rows(h2, dest_km, n_blocks * SLOT_BLOCK)
    ye = _experts(block_e, n_used, n_valid, hs, w1b, w3b, w2b)
    out = _combine(dest_km, ye, base, gate, mod, row(final_g))
    return out[None]
```

```python
import functools

import jax
import jax.numpy as jnp
from jax import lax
from jax.experimental import pallas as pl
from jax.experimental.pallas import tpu as pltpu
from jax.experimental.pallas import tpu_sc as plsc

F32 = jnp.float32
BF16 = jnp.bfloat16

D_MODEL = 1024
D_CONV = 512
D_RWKV = 512
HEAD_DIM = 64
N_HEADS = D_RWKV // HEAD_DIM
N_PAIRS = N_HEADS // 2
LORA = 128
P_RWKV = 3 * D_RWKV + 3 * LORA
P_IN = 3 * D_CONV + P_RWKV
GRID_W = 64
N_EXPERTS = 256
TOP_K = 8
D_EXPERT = 256
ROUTED_SCALE = 2.5
RMS_EPS = 1e-6
GN_EPS = 64e-5
DECAY_SCALE = 0.6065306597126334
NORM_EPS = 1e-12

TOK_TILE = 256
SLOTS_TILE = 1024
SCAN_BLOCK = 256
CHUNK = 64
SLOT_BLOCK = 512
SC_CHUNK = 64
SC_SCATTER_CHUNK = 128
LANES = 128
VMEM_LIMIT = 56 * 1024 * 1024

NN = ((1,), (0,))
NT = ((1,), (1,))
TN = ((0,), (0,))


def _dg(a, b, dims=NN):
    return lax.dot_general(a, b, (dims, ((), ())), preferred_element_type=F32)


def _split2(a):
    hi = a.astype(BF16)
    lo = (a - hi.astype(F32)).astype(BF16)
    return hi, lo


def _split3(a):
    hi = a.astype(BF16)
    r1 = a - hi.astype(F32)
    mid = r1.astype(BF16)
    lo = (r1 - mid.astype(F32)).astype(BF16)
    return hi, mid, lo


def _mm1(a, b, dims=NN):
    return _dg(a.astype(BF16), b.astype(BF16), dims)


def _mm3(a, b, dims=NN):
    ah, al = _split2(a)
    bh, bl = _split2(b)
    return _dg(ah, bh, dims) + (_dg(ah, bl, dims) + _dg(al, bh, dims))


def _mm_split_lhs(a, b_bf16, dims=NN):
    h, l = _split2(a)
    return _dg(h, b_bf16, dims) + _dg(l, b_bf16, dims)


def _mm_ones_rhs(a, ones_bf16):
    return _dg(a.astype(BF16), ones_bf16)


def _mm_ones_lhs(ones_bf16, b, dims=NN):
    h, m, l = _split3(b)
    return _dg(ones_bf16, h, dims) + (_dg(ones_bf16, m, dims) + _dg(ones_bf16, l, dims))


HALF = D_MODEL // 2
I32 = jnp.int32


def _pack_bf16_pairs(x):
    return pltpu.pack_elementwise([x[:, 0:HALF], x[:, HALF:]], packed_dtype=BF16)


def _unpack_bf16_pairs(w):
    lo = pltpu.unpack_elementwise(w, index=0, packed_dtype=BF16, unpacked_dtype=F32)
    hi = pltpu.unpack_elementwise(w, index=1, packed_dtype=BF16, unpacked_dtype=F32)
    return lo, hi


def _sigmoid(x):
    return 1.0 / (1.0 + jnp.exp(-x))


def _silu(x):
    return x * _sigmoid(x)


def _rms(xv, g):
    ms = jnp.mean(xv * xv, axis=-1, keepdims=True)
    return xv * lax.rsqrt(ms + RMS_EPS) * g


def _cparams(sem):
    return pltpu.CompilerParams(dimension_semantics=sem, vmem_limit_bytes=VMEM_LIMIT)


def _ada_kernel(c_ref, w_ref, b_ref, o_ref):
    o_ref[...] = _mm3(_silu(c_ref[...]), w_ref[...]) + b_ref[...]


def _ada(cc, ada_w, ada_b):
    n = ada_w.shape[1]
    tn = 1024
    return pl.pallas_call(
        _ada_kernel,
        grid=(n // tn,),
        in_specs=[pl.BlockSpec((8, D_MODEL), lambda j: (0, 0)),
                  pl.BlockSpec((D_MODEL, tn), lambda j: (0, j)),
                  pl.BlockSpec((1, tn), lambda j: (0, j))],
        out_specs=pl.BlockSpec((8, tn), lambda j: (0, j)),
        out_shape=jax.ShapeDtypeStruct((8, n), F32),
        compiler_params=_cparams(("arbitrary",)),
        name="ada",
    )(cc, ada_w, ada_b)


def _proj_kernel(ctx_ref, x_ref, xp_ref, xn_ref, mod_ref, n1g_ref, win_ref, convw_ref, mu_ref,
                 dw0_ref, dup_ref, ia0_ref, iup_ref, xi_ref, kal_ref, rho_ref, gup_ref, bsum_ref,
                 r_ref, v_ref, g_ref, bonus_ref, yconv_ref, lw_ref, kh_ref, kt_ref, a_ref):
    i = pl.program_id(0)
    nt = pl.num_programs(0)
    tm = x_ref.shape[0]
    is_ctx = i == 0
    sh = jnp.where(is_ctx, mod_ref[1:2, 0:D_MODEL], mod_ref[0:1, 0:D_MODEL])
    sc = jnp.where(is_ctx, mod_ref[1:2, D_MODEL:2 * D_MODEL], mod_ref[0:1, D_MODEL:2 * D_MODEL])
    n1g = n1g_ref[...]

    def norm_mod(xv):
        return _rms(xv, n1g) * (1.0 + sc) + sh

    xt = jnp.where(is_ctx, ctx_ref[...], x_ref[...])
    h = norm_mod(jnp.concatenate([xt, xp_ref[...], xn_ref[...]], axis=0)).astype(BF16)
    p_all = _dg(h, win_ref[...])
    p = p_all[0:tm]
    ph = p_all[tm:tm + 16, 3 * D_CONV:]

    rows = lax.broadcasted_iota(jnp.int32, (tm, 1), 0)

    bg = p[:, 0:D_CONV]
    z = p[:, D_CONV:2 * D_CONV] * p[:, 2 * D_CONV:3 * D_CONV]
    col = rows % GRID_W
    zp = jnp.where(col == 0, 0.0, pltpu.roll(z, 1, 0))
    zn = jnp.where(col == GRID_W - 1, 0.0, pltpu.roll(z, tm - 1, 0))
    yconv = bg * (convw_ref[0:1, :] * zp + convw_ref[1:2, :] * z + convw_ref[2:3, :] * zn)
    yconv_ref[...] = yconv.astype(BF16)

    prev_ok = jnp.logical_and(i != 0, i != 1).astype(F32)
    next_ok = jnp.logical_and(i != 0, i != nt - 1).astype(F32)
    cur = p[:, 3 * D_CONV:]
    prev = jnp.where(rows == 0, ph[7:8, :] * prev_ok, pltpu.roll(cur, 1, 0))
    nxt = jnp.where(rows == tm - 1, ph[8:9, :] * next_ok, pltpu.roll(cur, tm - 1, 0))
    ps = cur + mu_ref[0:1, :] * (prev - cur) + mu_ref[1:2, :] * (nxt - cur)

    r = ps[:, 0:D_RWKV]
    k = ps[:, D_RWKV:2 * D_RWKV]
    v = ps[:, 2 * D_RWKV:3 * D_RWKV]
    o = 3 * D_RWKV
    wlo = ps[:, o:o + LORA]
    alo = ps[:, o + LORA:o + 2 * LORA]
    glo = ps[:, o + 2 * LORA:o + 3 * LORA]

    dd = dw0_ref[...] + _mm_split_lhs(jnp.tanh(wlo), dup_ref[...])
    lw = -DECAY_SCALE * _sigmoid(dd)
    a = _sigmoid(ia0_ref[...] + _mm_split_lhs(alo, iup_ref[...]))
    g = _dg(_sigmoid(glo).astype(BF16), gup_ref[...])
    k2 = jnp.concatenate([k, k], axis=1)
    kap = k2 * xi_ref[...]
    kap2 = kap * kap
    bs = bsum_ref[...]
    ss = jnp.concatenate([_mm_ones_rhs(kap2[:, 0:D_RWKV], bs),
                          _mm_ones_rhs(kap2[:, D_RWKV:], bs)], axis=1)
    kh = kap * lax.rsqrt(ss + NORM_EPS)
    kt = k2 * (1.0 + (a - 1.0) * kal_ref[...])
    bon = _mm_ones_rhs(r * rho_ref[...] * (kt[:, 0:D_RWKV] + kt[:, D_RWKV:]), bs)

    g_ref[...] = g
    bonus_ref[...] = bon * v
    for pr in range(N_PAIRS):
        ls = slice(pr * LANES, (pr + 1) * LANES)
        r_ref[pr] = r[:, ls]
        v_ref[pr] = v[:, ls]
        for d in range(2):
            ld = slice(d * D_RWKV + pr * LANES, d * D_RWKV + (pr + 1) * LANES)
            lw_ref[d, pr] = lw[:, ld]
            kh_ref[d, pr] = kh[:, ld]
            kt_ref[d, pr] = kt[:, ld]
            a_ref[d, pr] = a[:, ld]


def _proj(ctx2, x2, mod, n1g, win, convw, mu, dw0, dup, ia0, iup, xi, kal, rho, gup, bsum):
    tm = TOK_TILE
    t = ctx2.shape[0] + x2.shape[0]
    nt = t // tm
    per = tm // 8
    nb8 = x2.shape[0] // 8
    full = lambda arr: pl.BlockSpec(arr.shape, lambda i: (0,) * arr.ndim)
    pair_spec = pl.BlockSpec((N_PAIRS, tm, LANES), lambda i: (0, i, 0))
    dpair_spec = pl.BlockSpec((2, N_PAIRS, tm, LANES), lambda i: (0, 0, i, 0))
    row_spec = pl.BlockSpec((tm, D_RWKV), lambda i: (i, 0))
    pair_shape = jax.ShapeDtypeStruct((N_PAIRS, t, LANES), F32)
    dpair_shape = jax.ShapeDtypeStruct((2, N_PAIRS, t, LANES), F32)
    consts = (mod, n1g, win, convw, mu, dw0, dup, ia0, iup, xi, kal, rho, gup, bsum)
    return pl.pallas_call(
        _proj_kernel,
        grid=(nt,),
        in_specs=[pl.BlockSpec((tm, D_MODEL), lambda i: (0, 0)),
                  pl.BlockSpec((tm, D_MODEL), lambda i: (jnp.maximum(i - 1, 0), 0)),
                  pl.BlockSpec((8, D_MODEL), lambda i: (jnp.maximum((i - 1) * per - 1, 0), 0)),
                  pl.BlockSpec((8, D_MODEL), lambda i: (jnp.clip(i * per, 0, nb8 - 1), 0))]
                 + [full(c) for c in consts],
        out_specs=[pair_spec, pair_spec, row_spec, row_spec, row_spec,
                   dpair_spec, dpair_spec, dpair_spec, dpair_spec],
        out_shape=[pair_shape, pair_shape,
                   jax.ShapeDtypeStruct((t, D_RWKV), F32), jax.ShapeDtypeStruct((t, D_RWKV), F32),
                   jax.ShapeDtypeStruct((t, D_CONV), BF16),
                   dpair_shape, dpair_shape, dpair_shape, dpair_shape],
        compiler_params=_cparams(("arbitrary",)),
        name="proj",
    )(ctx2, x2, x2, x2, *consts)


def _scan_kernel(r_ref, v_ref, lw_ref, kh_ref, kt_ref, a_ref, w1_ref, w3_ref, w2_ref,
                 y_ref, w1b_ref, w3b_ref, w2b_ref, q_ref):
    w1b_ref[...] = w1_ref[...].astype(BF16)
    w3b_ref[...] = w3_ref[...].astype(BF16)
    w2b_ref[...] = w2_ref[...].astype(BF16)

    d = pl.program_id(0)
    rev = d == 1
    c = CHUNK
    npair = r_ref.shape[0]
    nsub = r_ref.shape[1] // c

    row = lax.broadcasted_iota(jnp.int32, (c, c), 0)
    col = lax.broadcasted_iota(jnp.int32, (c, c), 1)
    strict = (col - row) * (1 - 2 * d) < 0
    eye = row == col
    incl = jnp.logical_or(strict, eye)
    incl_bf = jnp.where(incl, 1.0, 0.0).astype(BF16)
    eye_f = jnp.where(eye, 1.0, 0.0)
    head0 = lax.broadcasted_iota(jnp.int32, (c, LANES), 1) < HEAD_DIM
    row2 = lax.broadcasted_iota(jnp.int32, (LANES, LANES), 0)
    col2 = lax.broadcasted_iota(jnp.int32, (LANES, LANES), 1)
    same_head = (row2 < HEAD_DIM) == (col2 < HEAD_DIM)
    eye2 = row2 == col2

    subs = range(npair * nsub)
    units = [(s, h) for s in subs for h in range(2)]
    off = [pl.multiple_of(jnp.where(rev, nsub - 1 - s, s) * c, c) for s in range(nsub)]
    sls = [(pp, pl.ds(off[s], c), slice(None)) for pp in range(npair) for s in range(nsub)]
    v = [v_ref[sl] for sl in sls]
    lw = [lw_ref[sl] for sl in sls]
    cum = [_mm_ones_lhs(incl_bf, lw[s]) for s in subs]
    al, be, kk, rr, bew, kkw, wc = [], [], [], [], [], [], []
    for s in subs:
        cum_last = jnp.where(rev, cum[s][0:1, :], cum[s][c - 1:c, :])
        e_end = jnp.exp(cum_last - cum[s])
        e_neg = jnp.exp(-cum[s])
        kh = kh_ref[sls[s]]
        kt = kt_ref[sls[s]]
        kha = kh * a_ref[sls[s]]
        al.append(kh * jnp.exp(cum[s] - lw[s]))
        be.append(-(kha * e_neg))
        kk.append(kt * e_neg)
        rr.append(r_ref[sls[s]] * jnp.exp(cum[s]))
        bew.append(-(kha * e_end))
        kkw.append(kt * e_end)
        wc.append(jnp.exp(cum_last))

    lhs = {}
    for s, h in units:
        hm = head0 if h == 0 else jnp.logical_not(head0)
        lhs[s, h] = jnp.concatenate([jnp.where(hm, al[s], 0.0), jnp.where(hm, rr[s], 0.0)], axis=0)
    gb = {u: _mm1(lhs[u], be[u[0]], NT) for u in units}
    gk = {u: _mm1(lhs[u], kk[u[0]], NT) for u in units}
    a_ab = {u: jnp.where(strict, gb[u][0:c], 0.0) for u in units}
    a_ak = {u: jnp.where(strict, gk[u][0:c], 0.0) for u in units}
    a_rb = {u: jnp.where(incl, gb[u][c:2 * c], 0.0) for u in units}
    a_rk = {u: jnp.where(incl, gk[u][c:2 * c], 0.0) for u in units}

    pw = dict(a_ab)
    tinv = {u: eye_f + a_ab[u] for u in units}
    akrk_v = {u: _mm1(jnp.concatenate([a_ak[u], a_rk[u]], axis=0), v[u[0]]) for u in units}
    avk = {u: akrk_v[u][0:c] for u in units}
    rk_v = {u: akrk_v[u][c:2 * c] for u in units}
    for _ in range(c.bit_length() - 2):
        pw = {u: _mm1(pw[u], pw[u]) for u in units}
        tinv = {u: tinv[u] + _mm1(tinv[u], pw[u]) for u in units}
    tu = {u: _mm1(tinv[u], jnp.concatenate([al[u[0]], avk[u]], axis=1)) for u in units}
    at = [jnp.where(head0, tu[s, 0][:, 0:LANES], tu[s, 1][:, 0:LANES]) for s in subs]
    ut = [jnp.where(head0, tu[s, 0][:, LANES:], tu[s, 1][:, LANES:]) for s in subs]

    rb = {u: _mm1(a_rb[u], jnp.concatenate([at[u[0]], ut[u[0]]], axis=1)) for u in units}
    rb_at = {u: rb[u][:, 0:LANES] for u in units}
    rb_ut = {u: rb[u][:, LANES:] for u in units}
    md = [_mm1(bew[s], jnp.concatenate([at[s], ut[s]], axis=1), TN) for s in subs]
    m_off = [md[s][:, 0:LANES] for s in subs]
    d_u = [md[s][:, LANES:] for s in subs]
    d_v = [_mm1(kkw[s], v[s], TN) for s in subs]
    rh = [rr[s] + jnp.where(head0, rb_at[s, 0], rb_at[s, 1]) for s in subs]
    yh = [jnp.where(head0, rb_ut[s, 0] + rk_v[s, 0], rb_ut[s, 1] + rk_v[s, 1]) for s in subs]
    mt = [jnp.where(eye2, wc[s], 0.0) + jnp.where(same_head, m_off[s], 0.0) for s in subs]
    dt = [jnp.where(same_head, d_u[s] + d_v[s], 0.0) for s in subs]

    @pl.when(pl.program_id(1) == 0)
    def _():
        q_ref[...] = jnp.zeros(q_ref.shape, F32)

    q = [q_ref[pp] for pp in range(npair)]
    for si in range(nsub):
        ys = [_mm1(rh[pp * nsub + si], q[pp]) for pp in range(npair)]
        qn = [_mm1(mt[pp * nsub + si], q[pp]) for pp in range(npair)]
        for pp in range(npair):
            y_ref[sls[pp * nsub + si]] = ys[pp] + yh[pp * nsub + si]
            q[pp] = qn[pp] + dt[pp * nsub + si]
    for pp in range(npair):
        q_ref[pp] = q[pp]


def _scan(r4, v4, lw, kh, kt, a, w1, w3, w2):
    t = r4.shape[1]
    tb = SCAN_BLOCK
    nb = t // tb
    ne = w1.shape[0]
    epb = -(-ne // (2 * nb))
    assert ne % epb == 0
    last = ne // epb - 1

    def blk(d, j):
        return jnp.where(d == 0, j, jnp.where(j == 0, 0, nb - j))

    def wblk(d, j):
        return (jnp.minimum(d * nb + j, last), 0, 0)

    shared = pl.BlockSpec((N_PAIRS, tb, LANES), lambda d, j: (0, blk(d, j), 0))
    perdir = pl.BlockSpec((None, N_PAIRS, tb, LANES), lambda d, j: (d, 0, blk(d, j), 0))
    w13 = pl.BlockSpec((epb, D_MODEL, D_EXPERT), wblk)
    w2s = pl.BlockSpec((epb, D_EXPERT, D_MODEL), wblk)
    return pl.pallas_call(
        _scan_kernel,
        grid=(2, nb),
        in_specs=[shared, shared, perdir, perdir, perdir, perdir, w13, w13, w2s],
        out_specs=[perdir, w13, w13, w2s],
        out_shape=[jax.ShapeDtypeStruct((2, N_PAIRS, t, LANES), F32),
                   jax.ShapeDtypeStruct(w1.shape, BF16), jax.ShapeDtypeStruct(w3.shape, BF16),
                   jax.ShapeDtypeStruct(w2.shape, BF16)],
        scratch_shapes=[pltpu.VMEM((N_PAIRS, LANES, LANES), F32)],
        compiler_params=_cparams(("arbitrary", "arbitrary")),
        name="scan",
    )(r4, v4, lw, kh, kt, a, w1, w3, w2)


def _mix_kernel(x_ref, yf_ref, yb_ref, g_ref, bonus_ref, yconv_ref, mod_ref, gnw_ref, gnb_ref, bsum_ref,
                wout_ref, n2g_ref, rw_ref, rb_ref, sw1_ref, sw3_ref, sw2_ref,
                base_ref, h2_ref, idx_ref, gate_ref, rank_ref, cnt_ref, run_ref):
    tm = x_ref.shape[0]
    g1 = mod_ref[0:1, 2 * D_MODEL:3 * D_MODEL]
    sh2 = mod_ref[0:1, 3 * D_MODEL:4 * D_MODEL]
    sc2 = mod_ref[0:1, 4 * D_MODEL:5 * D_MODEL]
    g2 = mod_ref[0:1, 5 * D_MODEL:6 * D_MODEL]

    @pl.when(pl.program_id(0) == 0)
    def _():
        run_ref[...] = jnp.zeros(run_ref.shape, F32)

    y = jnp.concatenate([yf_ref[pr] + yb_ref[pr] for pr in range(N_PAIRS)], axis=1)
    bs = bsum_ref[...]
    mu = _mm_ones_rhs(y, bs) * (1.0 / HEAD_DIM)
    yc = y - mu
    var = _mm_ones_rhs(yc * yc, bs) * (1.0 / HEAD_DIM)
    yn = yc * lax.rsqrt(var + GN_EPS) * gnw_ref[...] + gnb_ref[...]
    yrw = ((yn + bonus_ref[...]) * g_ref[...]).astype(BF16)
    mix = _dg(yconv_ref[...], wout_ref[0:D_CONV, :]) + _dg(yrw, wout_ref[D_CONV:, :])
    x1 = x_ref[...] + g1 * mix

    h2 = _rms(x1, n2g_ref[...]) * (1.0 + sc2) + sh2
    hb = h2.astype(BF16)
    h2_ref[...] = _pack_bf16_pairs(h2)

    scores = _sigmoid(_mm3(h2, rw_ref[...]))
    work = scores + rb_ref[...]
    lane_e = lax.broadcasted_iota(jnp.int32, (tm, N_EXPERTS), 1).astype(F32)
    lane_o = lax.broadcasted_iota(jnp.int32, (tm, LANES), 1)
    idx_acc = jnp.zeros((tm, LANES), F32)
    gate_acc = jnp.zeros((tm, LANES), F32)
    gsum = jnp.zeros((tm, 1), F32)
    chosen = jnp.zeros((tm, N_EXPERTS), F32)
    sels = []
    for kk in range(TOP_K):
        m = jnp.max(work, axis=-1, keepdims=True)
        sel = jnp.min(jnp.where(work == m, lane_e, float(N_EXPERTS)), axis=-1, keepdims=True)
        hit = lane_e == sel
        sk = jnp.sum(jnp.where(hit, scores, 0.0), axis=-1, keepdims=True)
        idx_acc = jnp.where(lane_o == kk, sel, idx_acc)
        gate_acc = jnp.where(lane_o == kk, sk, gate_acc)
        gsum = gsum + sk
        work = jnp.where(hit, -jnp.inf, work)
        sels.append(sel)
    chosen = jnp.where(work == -jnp.inf, 1.0, 0.0)
    idx_ref[...] = idx_acc.astype(jnp.int32)
    gate_ref[...] = gate_acc / gsum * ROUTED_SCALE

    trow = lax.broadcasted_iota(jnp.int32, (tm, tm), 0)
    tcol = lax.broadcasted_iota(jnp.int32, (tm, tm), 1)
    earlier = jnp.where(tcol < trow, 1.0, 0.0).astype(BF16)
    before = _dg(earlier, chosen.astype(BF16)) + run_ref[0:1, :]
    rank_acc = jnp.zeros((tm, LANES), F32)
    for kk in range(TOP_K):
        rk = jnp.sum(jnp.where(lane_e == sels[kk], before, 0.0), axis=-1, keepdims=True)
        rank_acc = jnp.where(lane_o == kk, rk, rank_acc)
    rank_ref[...] = rank_acc.astype(jnp.int32)
    run_ref[...] = run_ref[...] + jnp.sum(chosen, axis=0, keepdims=True)
    cnt_ref[...] = run_ref[...]

    act = (_silu(_dg(hb, sw1_ref[...])) * _dg(hb, sw3_ref[...])).astype(BF16)
    base_ref[...] = x1 + g2 * _dg(act, sw2_ref[...])


def _mix(x, y, g, bonus, yconv, mod, gnw, gnb, bsum, wout, n2g, rw, rb, sw1, sw3, sw2):
    t = x.shape[0]
    tm = TOK_TILE
    off = (y.shape[2] - t) // tm
    full = lambda arr: pl.BlockSpec(arr.shape, lambda i: (0,) * arr.ndim)
    consts = (mod, gnw, gnb, bsum, wout, n2g, rw, rb, sw1, sw3, sw2)
    cat_spec = lambda w: pl.BlockSpec((tm, w), lambda i: (i + off, 0))
    tok_spec = lambda w: pl.BlockSpec((tm, w), lambda i: (i, 0))
    yf_spec = pl.BlockSpec((None, N_PAIRS, tm, LANES), lambda i: (0, 0, i + off, 0))
    yb_spec = pl.BlockSpec((None, N_PAIRS, tm, LANES), lambda i: (1, 0, i + off, 0))
    return pl.pallas_call(
        _mix_kernel,
        grid=(t // tm,),
        in_specs=[tok_spec(D_MODEL), yf_spec, yb_spec,
                  cat_spec(D_RWKV), cat_spec(D_RWKV), cat_spec(D_CONV)]
                 + [full(c) for c in consts],
        out_specs=[tok_spec(D_MODEL), tok_spec(HALF), tok_spec(LANES), tok_spec(LANES), tok_spec(LANES),
                   pl.BlockSpec((8, N_EXPERTS), lambda i: (0, 0))],
        out_shape=[jax.ShapeDtypeStruct((t, D_MODEL), F32), jax.ShapeDtypeStruct((t, HALF), I32),
                   jax.ShapeDtypeStruct((t, LANES), jnp.int32), jax.ShapeDtypeStruct((t, LANES), F32),
                   jax.ShapeDtypeStruct((t, LANES), jnp.int32), jax.ShapeDtypeStruct((8, N_EXPERTS), F32)],
        scratch_shapes=[pltpu.VMEM((8, N_EXPERTS), F32)],
        compiler_params=_cparams(("arbitrary",)),
        name="mix",
    )(x, y, y, g, bonus, yconv, *consts)


def _slots_kernel(idx_ref, rank_ref, start_ref, dest_ref):
    tm = idx_ref.shape[0]
    lane_e = lax.broadcasted_iota(jnp.int32, (tm, N_EXPERTS), 1)
    lane_o = lax.broadcasted_iota(jnp.int32, (tm, LANES), 1)
    idx = idx_ref[...]
    start = start_ref[...]
    acc = jnp.zeros((tm, LANES), F32)
    for kk in range(TOP_K):
        st = jnp.sum(jnp.where(lane_e == idx[:, kk:kk + 1], start, 0.0), axis=-1, keepdims=True)
        acc = jnp.where(lane_o == kk, st, acc)
    dest_ref[...] = acc.astype(jnp.int32) + rank_ref[...]


def _slots(idx, rank, pad_start):
    t = idx.shape[0]
    tm = SLOTS_TILE
    spec = pl.BlockSpec((tm, LANES), lambda i: (i, 0))
    return pl.pallas_call(
        _slots_kernel,
        grid=(t // tm,),
        in_specs=[spec, spec, pl.BlockSpec((1, N_EXPERTS), lambda i: (0, 0))],
        out_specs=spec,
        out_shape=jax.ShapeDtypeStruct((t, LANES), jnp.int32),
        compiler_params=_cparams(("arbitrary",)),
        name="slots",
    )(idx, rank, pad_start)


def _sc_scatter_rows(rows, dest_km, n_slots):
    info = plsc.get_sparse_core_info()
    nc, ns = info.num_cores, info.num_subcores
    nw = nc * ns
    t, d = rows.shape
    nk = dest_km.shape[0]
    ch = SC_SCATTER_CHUNK
    per_w = t // nw
    n_ch = per_w // ch
    assert t % nw == 0 and per_w % ch == 0
    mesh = plsc.VectorSubcoreMesh(core_axis_name="c", subcore_axis_name="s")
    idx = dest_km.reshape(nk, nw, n_ch, ch).transpose(1, 2, 0, 3).reshape(nw, n_ch * nk, ch)

    @functools.partial(
        pl.kernel, mesh=mesh,
        out_type=jax.ShapeDtypeStruct((n_slots, d), rows.dtype),
        scratch_types=[pltpu.VMEM((n_ch * nk, ch), jnp.int32),
                       pltpu.VMEM((ch, d), rows.dtype),
                       pltpu.SemaphoreType.DMA],
    )
    def scatter_kernel(rows_hbm, idx_hbm, out_hbm, idx_v, rows_v, sem):
        wid = lax.axis_index("s") * nc + lax.axis_index("c")
        base = wid * per_w
        pltpu.sync_copy(idx_hbm.at[wid], idx_v)

        @pl.loop(0, n_ch)
        def _(j):
            pltpu.sync_copy(rows_hbm.at[pl.ds(base + j * ch, ch)], rows_v)
            copies = [pltpu.make_async_copy(rows_v, out_hbm.at[idx_v.at[j * nk + kk]], sem) for kk in range(nk)]
            for cp in copies:
                cp.start()
            for cp in copies:
                cp.wait()

    return scatter_kernel(rows, idx)


def _experts_kernel(be_ref, nu_ref, nv_ref, hs_ref, w1c, w3c, w2c, o_ref):
    del be_ref
    b = pl.program_id(0)

    @pl.when(b < nu_ref[0])
    def _():
        live = lax.broadcasted_iota(jnp.int32, (hs_ref.shape[0], 1), 0) < nv_ref[b]
        lo, hi = _unpack_bf16_pairs(jnp.where(live, hs_ref[...], 0))
        lo = lo.astype(BF16)
        hi = hi.astype(BF16)
        a1 = _dg(lo, w1c[0:HALF, :]) + _dg(hi, w1c[HALF:, :])
        a3 = _dg(lo, w3c[0:HALF, :]) + _dg(hi, w3c[HALF:, :])
        act = (_silu(a1) * a3).astype(BF16)
        o_ref[...] = _pack_bf16_pairs(_dg(act, w2c[...]))

    @pl.when(b >= nu_ref[0])
    def _():
        o_ref[...] = jnp.zeros(o_ref.shape, I32)


def _experts(block_e, n_used, n_valid, hs, w1, w3, w2):
    nblk = block_e.shape[0]
    sb = SLOT_BLOCK
    row_map = lambda b, be, nu, nv: (jnp.minimum(b, nu[0] - 1), 0)
    w_map = lambda b, be, nu, nv: (be[b], 0, 0)
    grid_spec = pltpu.PrefetchScalarGridSpec(
        num_scalar_prefetch=3,
        grid=(nblk,),
        in_specs=[pl.BlockSpec((sb, HALF), row_map),
                  pl.BlockSpec((None, D_MODEL, D_EXPERT), w_map),
                  pl.BlockSpec((None, D_MODEL, D_EXPERT), w_map),
                  pl.BlockSpec((None, D_EXPERT, D_MODEL), w_map)],
        out_specs=pl.BlockSpec((sb, HALF), lambda b, be, nu, nv: (b, 0)),
    )
    return pl.pallas_call(
        _experts_kernel,
        grid_spec=grid_spec,
        out_shape=jax.ShapeDtypeStruct((nblk * sb, HALF), I32),
        compiler_params=_cparams(("arbitrary",)),
        name="experts",
    )(block_e, n_used, n_valid, hs, w1, w3, w2)


def _sc_gather_rows(table, idx):
    info = plsc.get_sparse_core_info()
    nc, ns = info.num_cores, info.num_subcores
    nw = nc * ns
    n = idx.shape[0]
    d = table.shape[1]
    ch = SC_CHUNK
    per_w = n // nw
    n_ch = per_w // ch
    assert n % nw == 0 and per_w % (2 * ch) == 0
    mesh = plsc.VectorSubcoreMesh(core_axis_name="c", subcore_axis_name="s")

    @functools.partial(
        pl.kernel, mesh=mesh,
        out_type=jax.ShapeDtypeStruct((n, d), table.dtype),
        scratch_types=[pltpu.VMEM((n_ch, ch), jnp.int32),
                       pltpu.VMEM((2, ch, d), table.dtype),
                       pltpu.SemaphoreType.DMA((2,))],
    )
    def gather_kernel(table_hbm, idx_hbm, out_hbm, idx_v, rows_v, sem):
        wid = lax.axis_index("s") * nc + lax.axis_index("c")
        base = wid * per_w
        pltpu.sync_copy(idx_hbm.at[wid], idx_v)

        def gather(j, buf):
            return pltpu.make_async_copy(table_hbm.at[idx_v.at[j]], rows_v.at[buf], sem.at[buf])

        gather(0, 0).start()

        @pl.loop(0, n_ch, step=2)
        def _(j):
            for buf in range(2):
                jj = j + buf
                gather(jj, buf).wait()

                @pl.when(jj + 1 < n_ch)
                def _():
                    gather(jj + 1, 1 - buf).start()

                pltpu.sync_copy(rows_v.at[buf], out_hbm.at[pl.ds(base + jj * ch, ch)])

    return gather_kernel(table, idx.reshape(nw, n_ch, ch))


def _combine_kernel(rows_ref, base_ref, gate_ref, mod_ref, fg_ref, o_ref):
    tm = base_ref.shape[0]
    gate = gate_ref[...]
    acc_lo = jnp.zeros((tm, HALF), F32)
    acc_hi = jnp.zeros((tm, HALF), F32)
    for kk in range(TOP_K):
        lo, hi = _unpack_bf16_pairs(rows_ref[kk])
        acc_lo = acc_lo + gate[:, kk:kk + 1] * lo
        acc_hi = acc_hi + gate[:, kk:kk + 1] * hi
    acc = jnp.concatenate([acc_lo, acc_hi], axis=1)
    g2 = mod_ref[0:1, 5 * D_MODEL:6 * D_MODEL]
    o_ref[...] = _rms(base_ref[...] + g2 * acc, fg_ref[...])


def _combine(dest_km, ye, base, gate, mod, fg):
    t = base.shape[0]
    tm = TOK_TILE
    full = lambda arr: pl.BlockSpec(arr.shape, lambda i: (0,) * arr.ndim)
    rows = _sc_gather_rows(ye, dest_km.reshape(-1)).reshape(TOP_K, t, HALF)
    return pl.pallas_call(
        _combine_kernel,
        grid=(t // tm,),
        in_specs=[pl.BlockSpec((TOP_K, tm, HALF), lambda i: (0, i, 0)),
                  pl.BlockSpec((tm, D_MODEL), lambda i: (i, 0)),
                  pl.BlockSpec((tm, LANES), lambda i: (i, 0)),
                  full(mod), full(fg)],
        out_specs=pl.BlockSpec((tm, D_MODEL), lambda i: (i, 0)),
        out_shape=jax.ShapeDtypeStruct((t, D_MODEL), F32),
        compiler_params=_cparams(("arbitrary",)),
        name="combine",
    )(rows, base, gate, mod, fg)


def _block_tables(counts, n_blocks):
    sb = SLOT_BLOCK
    padded = (counts + sb - 1) // sb * sb
    pad_end = jnp.cumsum(padded)
    pad_start = pad_end - padded
    n_used = pad_end[-1] // sb
    blk = jnp.minimum(jnp.arange(n_blocks, dtype=jnp.int32), n_used - 1)
    block_e = jnp.minimum(jnp.sum(pad_end[None, :] <= (blk * sb)[:, None], axis=1), N_EXPERTS - 1).astype(jnp.int32)
    onehot = block_e[:, None] == jnp.arange(N_EXPERTS, dtype=jnp.int32)[None, :]
    live_end = jnp.sum(jnp.where(onehot, (pad_start + counts)[None, :], 0), axis=1)
    n_valid = jnp.clip(live_end - blk * sb, 0, sb).astype(jnp.int32)
    return block_e, n_used.reshape(1).astype(jnp.int32), n_valid, pad_start


def _blockdiag2(w):
    z = jnp.zeros_like(w[0])
    return jnp.concatenate([jnp.concatenate([w[0], z], axis=1), jnp.concatenate([z, w[1]], axis=1)], axis=0)


def _pad_rows(w, n=8):
    return jnp.concatenate([w, jnp.zeros((n - w.shape[0],) + w.shape[1:], w.dtype)], axis=0)


def kernel(x, c, ctx, c_ctx, ada_w, ada_b, norm1_g, norm2_g, w_in, conv_w, shift_mu, decay_w0, decay_up, iclr_a0, iclr_up, key_xi, key_alpha, bonus_rho, gate_up, gn_w, gn_b, w_out, router_w, router_bias, exp_w1, exp_w3, exp_w2, sh_w1, sh_w3, sh_w2, final_g):
    assert x.shape[0] == 1 and ada_w.shape[0] == 1
    assert ctx.shape[1] == TOK_TILE and x.shape[1] % TOK_TILE == 0
    l = 0
    xs = x[0]
    row = lambda w: w.reshape(1, -1)

    cc = _pad_rows(jnp.stack([c[0], c_ctx], axis=0))
    mod = _ada(cc, ada_w[l], row(ada_b[l]))

    hid = lax.broadcasted_iota(jnp.int32, (D_RWKV, D_RWKV), 0) // HEAD_DIM
    bsum = (hid == hid.T).astype(BF16)

    r4, v4, g, bonus, yconv, lw, kh, kt, a = _proj(
        ctx[0], xs, mod, row(norm1_g[l]), w_in[l].astype(BF16), _pad_rows(conv_w[l]), _pad_rows(shift_mu[l]),
        row(decay_w0[l]), _blockdiag2(decay_up[l]).astype(BF16), row(iclr_a0[l]), _blockdiag2(iclr_up[l]).astype(BF16),
        row(key_xi[l]), row(key_alpha[l]), row(bonus_rho[l]), gate_up[l].astype(BF16), bsum)

    y, w1b, w3b, w2b = _scan(r4, v4, lw, kh, kt, a, exp_w1[l], exp_w3[l], exp_w2[l])

    base, h2, idx, gate, rank, cnt = _mix(
        xs, y, g, bonus, yconv, mod, row(gn_w[l]), row(gn_b[l]), bsum, w_out[l].astype(BF16),
        row(norm2_g[l]), router_w[l], row(router_bias[l]),
        sh_w1[l].astype(BF16), sh_w3[l].astype(BF16), sh_w2[l].astype(BF16))

    n_blocks = xs.shape[0] * TOP_K // SLOT_BLOCK + N_EXPERTS
    block_e, n_used, n_valid, pad_start = _block_tables(cnt[0].astype(jnp.int32), n_blocks)
    dest_km = _slots(idx, rank, pad_start.astype(F32).reshape(1, N_EXPERTS))[:, :TOP_K].T
    hs = _sc_scatter_rows(h2, dest_km, n_blocks * SLOT_BLOCK)
    ye = _experts(block_e, n_used, n_valid, hs, w1b, w3b, w2b)
    out = _combine(dest_km, ye, base, gate, mod, row(final_g))
    return out[None]
```

```python
import functools

import jax
import jax.numpy as jnp
from jax import lax
from jax.experimental import pallas as pl
from jax.experimental.pallas import tpu as pltpu
from jax.experimental.pallas import tpu_sc as plsc

F32 = jnp.float32
BF16 = jnp.bfloat16

D_MODEL = 1024
D_CONV = 512
D_RWKV = 512
HEAD_DIM = 64
N_HEADS = D_RWKV // HEAD_DIM
N_PAIRS = N_HEADS // 2
LORA = 128
P_RWKV = 3 * D_RWKV + 3 * LORA
P_IN = 3 * D_CONV + P_RWKV
GRID_W = 64
N_EXPERTS = 256
TOP_K = 8
D_EXPERT = 256
ROUTED_SCALE = 2.5
RMS_EPS = 1e-6
GN_EPS = 64e-5
DECAY_SCALE = 0.6065306597126334
NORM_EPS = 1e-12

TOK_TILE = 256
SLOTS_TILE = 1024
SCAN_BLOCK = 256
CHUNK = 64
SLOT_BLOCK = 256
SC_CHUNK = 64
SC_SCATTER_CHUNK = 128
LANES = 128
VMEM_LIMIT = 56 * 1024 * 1024

NN = ((1,), (0,))
NT = ((1,), (1,))
TN = ((0,), (0,))


def _dg(a, b, dims=NN):
    return lax.dot_general(a, b, (dims, ((), ())), preferred_element_type=F32)


def _split2(a):
    hi = a.astype(BF16)
    lo = (a - hi.astype(F32)).astype(BF16)
    return hi, lo


def _split3(a):
    hi = a.astype(BF16)
    r1 = a - hi.astype(F32)
    mid = r1.astype(BF16)
    lo = (r1 - mid.astype(F32)).astype(BF16)
    return hi, mid, lo


def _mm1(a, b, dims=NN):
    return _dg(a.astype(BF16), b.astype(BF16), dims)


def _mm3(a, b, dims=NN):
    ah, al = _split2(a)
    bh, bl = _split2(b)
    return _dg(ah, bh, dims) + (_dg(ah, bl, dims) + _dg(al, bh, dims))


def _mm_split_lhs(a, b_bf16, dims=NN):
    h, l = _split2(a)
    return _dg(h, b_bf16, dims) + _dg(l, b_bf16, dims)


def _mm_ones_rhs(a, ones_bf16):
    return _dg(a.astype(BF16), ones_bf16)


def _mm_ones_lhs(ones_bf16, b, dims=NN):
    h, m, l = _split3(b)
    return _dg(ones_bf16, h, dims) + (_dg(ones_bf16, m, dims) + _dg(ones_bf16, l, dims))


HALF = D_MODEL // 2
I32 = jnp.int32


def _pack_bf16_pairs(x):
    return pltpu.pack_elementwise([x[:, 0:HALF], x[:, HALF:]], packed_dtype=BF16)


def _unpack_bf16_pairs(w):
    lo = pltpu.unpack_elementwise(w, index=0, packed_dtype=BF16, unpacked_dtype=F32)
    hi = pltpu.unpack_elementwise(w, index=1, packed_dtype=BF16, unpacked_dtype=F32)
    return lo, hi


def _sigmoid(x):
    return 1.0 / (1.0 + jnp.exp(-x))


def _silu(x):
    return x * _sigmoid(x)


def _rms(xv, g):
    ms = jnp.mean(xv * xv, axis=-1, keepdims=True)
    return xv * lax.rsqrt(ms + RMS_EPS) * g


def _cparams(sem):
    return pltpu.CompilerParams(dimension_semantics=sem, vmem_limit_bytes=VMEM_LIMIT)


def _ada_kernel(c_ref, w_ref, b_ref, o_ref):
    o_ref[...] = _mm3(_silu(c_ref[...]), w_ref[...]) + b_ref[...]


def _ada(cc, ada_w, ada_b):
    n = ada_w.shape[1]
    tn = 1024
    return pl.pallas_call(
        _ada_kernel,
        grid=(n // tn,),
        in_specs=[pl.BlockSpec((8, D_MODEL), lambda j: (0, 0)),
                  pl.BlockSpec((D_MODEL, tn), lambda j: (0, j)),
                  pl.BlockSpec((1, tn), lambda j: (0, j))],
        out_specs=pl.BlockSpec((8, tn), lambda j: (0, j)),
        out_shape=jax.ShapeDtypeStruct((8, n), F32),
        compiler_params=_cparams(("arbitrary",)),
        name="ada",
    )(cc, ada_w, ada_b)


def _proj_kernel(ctx_ref, x_ref, xp_ref, xn_ref, mod_ref, n1g_ref, win_ref, convw_ref, mu_ref,
                 dw0_ref, dup_ref, ia0_ref, iup_ref, xi_ref, kal_ref, rho_ref, gup_ref, bsum_ref,
                 r_ref, v_ref, g_ref, bonus_ref, yconv_ref, lw_ref, kh_ref, kt_ref, a_ref):
    i = pl.program_id(0)
    nt = pl.num_programs(0)
    tm = x_ref.shape[0]
    is_ctx = i == 0
    sh = jnp.where(is_ctx, mod_ref[1:2, 0:D_MODEL], mod_ref[0:1, 0:D_MODEL])
    sc = jnp.where(is_ctx, mod_ref[1:2, D_MODEL:2 * D_MODEL], mod_ref[0:1, D_MODEL:2 * D_MODEL])
    n1g = n1g_ref[...]

    def norm_mod(xv):
        return _rms(xv, n1g) * (1.0 + sc) + sh

    xt = jnp.where(is_ctx, ctx_ref[...], x_ref[...])
    h = norm_mod(jnp.concatenate([xt, xp_ref[...], xn_ref[...]], axis=0)).astype(BF16)
    p_all = _dg(h, win_ref[...])
    p = p_all[0:tm]
    ph = p_all[tm:tm + 16, 3 * D_CONV:]

    rows = lax.broadcasted_iota(jnp.int32, (tm, 1), 0)

    bg = p[:, 0:D_CONV]
    z = p[:, D_CONV:2 * D_CONV] * p[:, 2 * D_CONV:3 * D_CONV]
    col = rows % GRID_W
    zp = jnp.where(col == 0, 0.0, pltpu.roll(z, 1, 0))
    zn = jnp.where(col == GRID_W - 1, 0.0, pltpu.roll(z, tm - 1, 0))
    yconv = bg * (convw_ref[0:1, :] * zp + convw_ref[1:2, :] * z + convw_ref[2:3, :] * zn)
    yconv_ref[...] = yconv.astype(BF16)

    prev_ok = jnp.logical_and(i != 0, i != 1).astype(F32)
    next_ok = jnp.logical_and(i != 0, i != nt - 1).astype(F32)
    cur = p[:, 3 * D_CONV:]
    prev = jnp.where(rows == 0, ph[7:8, :] * prev_ok, pltpu.roll(cur, 1, 0))
    nxt = jnp.where(rows == tm - 1, ph[8:9, :] * next_ok, pltpu.roll(cur, tm - 1, 0))
    ps = cur + mu_ref[0:1, :] * (prev - cur) + mu_ref[1:2, :] * (nxt - cur)

    r = ps[:, 0:D_RWKV]
    k = ps[:, D_RWKV:2 * D_RWKV]
    v = ps[:, 2 * D_RWKV:3 * D_RWKV]
    o = 3 * D_RWKV
    wlo = ps[:, o:o + LORA]
    alo = ps[:, o + LORA:o + 2 * LORA]
    glo = ps[:, o + 2 * LORA:o + 3 * LORA]

    dd = dw0_ref[...] + _mm_split_lhs(jnp.tanh(wlo), dup_ref[...])
    lw = -DECAY_SCALE * _sigmoid(dd)
    a = _sigmoid(ia0_ref[...] + _mm_split_lhs(alo, iup_ref[...]))
    g = _dg(_sigmoid(glo).astype(BF16), gup_ref[...])
    k2 = jnp.concatenate([k, k], axis=1)
    kap = k2 * xi_ref[...]
    kap2 = kap * kap
    bs = bsum_ref[...]
    ss = jnp.concatenate([_mm_ones_rhs(kap2[:, 0:D_RWKV], bs),
                          _mm_ones_rhs(kap2[:, D_RWKV:], bs)], axis=1)
    kh = kap * lax.rsqrt(ss + NORM_EPS)
    kt = k2 * (1.0 + (a - 1.0) * kal_ref[...])
    bon = _mm_ones_rhs(r * rho_ref[...] * (kt[:, 0:D_RWKV] + kt[:, D_RWKV:]), bs)

    g_ref[...] = g
    bonus_ref[...] = bon * v
    for pr in range(N_PAIRS):
        ls = slice(pr * LANES, (pr + 1) * LANES)
        r_ref[pr] = r[:, ls]
        v_ref[pr] = v[:, ls]
        for d in range(2):
            ld = slice(d * D_RWKV + pr * LANES, d * D_RWKV + (pr + 1) * LANES)
            lw_ref[d, pr] = lw[:, ld]
            kh_ref[d, pr] = kh[:, ld]
            kt_ref[d, pr] = kt[:, ld]
            a_ref[d, pr] = a[:, ld]


def _proj(ctx2, x2, mod, n1g, win, convw, mu, dw0, dup, ia0, iup, xi, kal, rho, gup, bsum):
    tm = TOK_TILE
    t = ctx2.shape[0] + x2.shape[0]
    nt = t // tm
    per = tm // 8
    nb8 = x2.shape[0] // 8
    full = lambda arr: pl.BlockSpec(arr.shape, lambda i: (0,) * arr.ndim)
    pair_spec = pl.BlockSpec((N_PAIRS, tm, LANES), lambda i: (0, i, 0))
    dpair_spec = pl.BlockSpec((2, N_PAIRS, tm, LANES), lambda i: (0, 0, i, 0))
    row_spec = pl.BlockSpec((tm, D_RWKV), lambda i: (i, 0))
    pair_shape = jax.ShapeDtypeStruct((N_PAIRS, t, LANES), F32)
    dpair_shape = jax.ShapeDtypeStruct((2, N_PAIRS, t, LANES), F32)
    consts = (mod, n1g, win, convw, mu, dw0, dup, ia0, iup, xi, kal, rho, gup, bsum)
    return pl.pallas_call(
        _proj_kernel,
        grid=(nt,),
        in_specs=[pl.BlockSpec((tm, D_MODEL), lambda i: (0, 0)),
                  pl.BlockSpec((tm, D_MODEL), lambda i: (jnp.maximum(i - 1, 0), 0)),
                  pl.BlockSpec((8, D_MODEL), lambda i: (jnp.maximum((i - 1) * per - 1, 0), 0)),
                  pl.BlockSpec((8, D_MODEL), lambda i: (jnp.clip(i * per, 0, nb8 - 1), 0))]
                 + [full(c) for c in consts],
        out_specs=[pair_spec, pair_spec, row_spec, row_spec, row_spec,
                   dpair_spec, dpair_spec, dpair_spec, dpair_spec],
        out_shape=[pair_shape, pair_shape,
                   jax.ShapeDtypeStruct((t, D_RWKV), F32), jax.ShapeDtypeStruct((t, D_RWKV), F32),
                   jax.ShapeDtypeStruct((t, D_CONV), BF16),
                   dpair_shape, dpair_shape, dpair_shape, dpair_shape],
        compiler_params=_cparams(("arbitrary",)),
        name="proj",
    )(ctx2, x2, x2, x2, *consts)


def _scan_kernel(r_ref, v_ref, lw_ref, kh_ref, kt_ref, a_ref, w1_ref, w3_ref, w2_ref,
                 y_ref, w1b_ref, w3b_ref, w2b_ref, q_ref):
    w1b_ref[...] = w1_ref[...].astype(BF16)
    w3b_ref[...] = w3_ref[...].astype(BF16)
    w2b_ref[...] = w2_ref[...].astype(BF16)

    d = pl.program_id(0)
    rev = d == 1
    c = CHUNK
    npair = r_ref.shape[0]
    nsub = r_ref.shape[1] // c

    row = lax.broadcasted_iota(jnp.int32, (c, c), 0)
    col = lax.broadcasted_iota(jnp.int32, (c, c), 1)
    strict = (col - row) * (1 - 2 * d) < 0
    eye = row == col
    incl = jnp.logical_or(strict, eye)
    incl_bf = jnp.where(incl, 1.0, 0.0).astype(BF16)
    eye_f = jnp.where(eye, 1.0, 0.0)
    head0 = lax.broadcasted_iota(jnp.int32, (c, LANES), 1) < HEAD_DIM
    row2 = lax.broadcasted_iota(jnp.int32, (LANES, LANES), 0)
    col2 = lax.broadcasted_iota(jnp.int32, (LANES, LANES), 1)
    same_head = (row2 < HEAD_DIM) == (col2 < HEAD_DIM)
    eye2 = row2 == col2

    subs = range(npair * nsub)
    units = [(s, h) for s in subs for h in range(2)]
    off = [pl.multiple_of(jnp.where(rev, nsub - 1 - s, s) * c, c) for s in range(nsub)]
    sls = [(pp, pl.ds(off[s], c), slice(None)) for pp in range(npair) for s in range(nsub)]
    v = [v_ref[sl] for sl in sls]
    lw = [lw_ref[sl] for sl in sls]
    cum = [_mm_ones_lhs(incl_bf, lw[s]) for s in subs]
    al, be, kk, rr, bew, kkw, wc = [], [], [], [], [], [], []
    for s in subs:
        cum_last = jnp.where(rev, cum[s][0:1, :], cum[s][c - 1:c, :])
        e_end = jnp.exp(cum_last - cum[s])
        e_neg = jnp.exp(-cum[s])
        kh = kh_ref[sls[s]]
        kt = kt_ref[sls[s]]
        kha = kh * a_ref[sls[s]]
        al.append(kh * jnp.exp(cum[s] - lw[s]))
        be.append(-(kha * e_neg))
        kk.append(kt * e_neg)
        rr.append(r_ref[sls[s]] * jnp.exp(cum[s]))
        bew.append(-(kha * e_end))
        kkw.append(kt * e_end)
        wc.append(jnp.exp(cum_last))

    lhs = {}
    for s, h in units:
        hm = head0 if h == 0 else jnp.logical_not(head0)
        lhs[s, h] = jnp.concatenate([jnp.where(hm, al[s], 0.0), jnp.where(hm, rr[s], 0.0)], axis=0)
    gb = {u: _mm1(lhs[u], be[u[0]], NT) for u in units}
    gk = {u: _mm1(lhs[u], kk[u[0]], NT) for u in units}
    a_ab = {u: jnp.where(strict, gb[u][0:c], 0.0) for u in units}
    a_ak = {u: jnp.where(strict, gk[u][0:c], 0.0) for u in units}
    a_rb = {u: jnp.where(incl, gb[u][c:2 * c], 0.0) for u in units}
    a_rk = {u: jnp.where(incl, gk[u][c:2 * c], 0.0) for u in units}

    pw = dict(a_ab)
    tinv = {u: eye_f + a_ab[u] for u in units}
    akrk_v = {u: _mm1(jnp.concatenate([a_ak[u], a_rk[u]], axis=0), v[u[0]]) for u in units}
    avk = {u: akrk_v[u][0:c] for u in units}
    rk_v = {u: akrk_v[u][c:2 * c] for u in units}
    for _ in range(c.bit_length() - 2):
        pw = {u: _mm1(pw[u], pw[u]) for u in units}
        tinv = {u: tinv[u] + _mm1(tinv[u], pw[u]) for u in units}
    tu = {u: _mm1(tinv[u], jnp.concatenate([al[u[0]], avk[u]], axis=1)) for u in units}
    at = [jnp.where(head0, tu[s, 0][:, 0:LANES], tu[s, 1][:, 0:LANES]) for s in subs]
    ut = [jnp.where(head0, tu[s, 0][:, LANES:], tu[s, 1][:, LANES:]) for s in subs]

    rb = {u: _mm1(a_rb[u], jnp.concatenate([at[u[0]], ut[u[0]]], axis=1)) for u in units}
    rb_at = {u: rb[u][:, 0:LANES] for u in units}
    rb_ut = {u: rb[u][:, LANES:] for u in units}
    md = [_mm1(bew[s], jnp.concatenate([at[s], ut[s]], axis=1), TN) for s in subs]
    m_off = [md[s][:, 0:LANES] for s in subs]
    d_u = [md[s][:, LANES:] for s in subs]
    d_v = [_mm1(kkw[s], v[s], TN) for s in subs]
    rh = [rr[s] + jnp.where(head0, rb_at[s, 0], rb_at[s, 1]) for s in subs]
    yh = [jnp.where(head0, rb_ut[s, 0] + rk_v[s, 0], rb_ut[s, 1] + rk_v[s, 1]) for s in subs]
    mt = [jnp.where(eye2, wc[s], 0.0) + jnp.where(same_head, m_off[s], 0.0) for s in subs]
    dt = [jnp.where(same_head, d_u[s] + d_v[s], 0.0) for s in subs]

    @pl.when(pl.program_id(1) == 0)
    def _():
        q_ref[...] = jnp.zeros(q_ref.shape, F32)

    q = [q_ref[pp] for pp in range(npair)]
    for si in range(nsub):
        ys = [_mm1(rh[pp * nsub + si], q[pp]) for pp in range(npair)]
        qn = [_mm1(mt[pp * nsub + si], q[pp]) for pp in range(npair)]
        for pp in range(npair):
            y_ref[sls[pp * nsub + si]] = ys[pp] + yh[pp * nsub + si]
            q[pp] = qn[pp] + dt[pp * nsub + si]
    for pp in range(npair):
        q_ref[pp] = q[pp]


def _scan(r4, v4, lw, kh, kt, a, w1, w3, w2):
    t = r4.shape[1]
    tb = SCAN_BLOCK
    nb = t // tb
    ne = w1.shape[0]
    epb = -(-ne // (2 * nb))
    assert ne % epb == 0
    last = ne // epb - 1

    def blk(d, j):
        return jnp.where(d == 0, j, jnp.where(j == 0, 0, nb - j))

    def wblk(d, j):
        return (jnp.minimum(d * nb + j, last), 0, 0)

    shared = pl.BlockSpec((N_PAIRS, tb, LANES), lambda d, j: (0, blk(d, j), 0))
    perdir = pl.BlockSpec((None, N_PAIRS, tb, LANES), lambda d, j: (d, 0, blk(d, j), 0))
    w13 = pl.BlockSpec((epb, D_MODEL, D_EXPERT), wblk)
    w2s = pl.BlockSpec((epb, D_EXPERT, D_MODEL), wblk)
    return pl.pallas_call(
        _scan_kernel,
        grid=(2, nb),
        in_specs=[shared, shared, perdir, perdir, perdir, perdir, w13, w13, w2s],
        out_specs=[perdir, w13, w13, w2s],
        out_shape=[jax.ShapeDtypeStruct((2, N_PAIRS, t, LANES), F32),
                   jax.ShapeDtypeStruct(w1.shape, BF16), jax.ShapeDtypeStruct(w3.shape, BF16),
                   jax.ShapeDtypeStruct(w2.shape, BF16)],
        scratch_shapes=[pltpu.VMEM((N_PAIRS, LANES, LANES), F32)],
        compiler_params=_cparams(("arbitrary", "arbitrary")),
        name="scan",
    )(r4, v4, lw, kh, kt, a, w1, w3, w2)


def _mix_kernel(x_ref, yf_ref, yb_ref, g_ref, bonus_ref, yconv_ref, mod_ref, gnw_ref, gnb_ref, bsum_ref,
                wout_ref, n2g_ref, rw_ref, rb_ref, sw1_ref, sw3_ref, sw2_ref,
                base_ref, h2_ref, idx_ref, gate_ref, rank_ref, cnt_ref, run_ref):
    tm = x_ref.shape[0]
    g1 = mod_ref[0:1, 2 * D_MODEL:3 * D_MODEL]
    sh2 = mod_ref[0:1, 3 * D_MODEL:4 * D_MODEL]
    sc2 = mod_ref[0:1, 4 * D_MODEL:5 * D_MODEL]
    g2 = mod_ref[0:1, 5 * D_MODEL:6 * D_MODEL]

    @pl.when(pl.program_id(0) == 0)
    def _():
        run_ref[...] = jnp.zeros(run_ref.shape, F32)

    y = jnp.concatenate([yf_ref[pr] + yb_ref[pr] for pr in range(N_PAIRS)], axis=1)
    bs = bsum_ref[...]
    mu = _mm_ones_rhs(y, bs) * (1.0 / HEAD_DIM)
    yc = y - mu
    var = _mm_ones_rhs(yc * yc, bs) * (1.0 / HEAD_DIM)
    yn = yc * lax.rsqrt(var + GN_EPS) * gnw_ref[...] + gnb_ref[...]
    yrw = ((yn + bonus_ref[...]) * g_ref[...]).astype(BF16)
    mix = _dg(yconv_ref[...], wout_ref[0:D_CONV, :]) + _dg(yrw, wout_ref[D_CONV:, :])
    x1 = x_ref[...] + g1 * mix

    h2 = _rms(x1, n2g_ref[...]) * (1.0 + sc2) + sh2
    hb = h2.astype(BF16)
    h2_ref[...] = _pack_bf16_pairs(h2)

    scores = _sigmoid(_mm3(h2, rw_ref[...]))
    work = scores + rb_ref[...]
    lane_e = lax.broadcasted_iota(jnp.int32, (tm, N_EXPERTS), 1).astype(F32)
    lane_o = lax.broadcasted_iota(jnp.int32, (tm, LANES), 1)
    idx_acc = jnp.zeros((tm, LANES), F32)
    gate_acc = jnp.zeros((tm, LANES), F32)
    gsum = jnp.zeros((tm, 1), F32)
    chosen = jnp.zeros((tm, N_EXPERTS), F32)
    sels = []
    for kk in range(TOP_K):
        m = jnp.max(work, axis=-1, keepdims=True)
        sel = jnp.min(jnp.where(work == m, lane_e, float(N_EXPERTS)), axis=-1, keepdims=True)
        hit = lane_e == sel
        sk = jnp.sum(jnp.where(hit, scores, 0.0), axis=-1, keepdims=True)
        idx_acc = jnp.where(lane_o == kk, sel, idx_acc)
        gate_acc = jnp.where(lane_o == kk, sk, gate_acc)
        gsum = gsum + sk
        work = jnp.where(hit, -jnp.inf, work)
        sels.append(sel)
    chosen = jnp.where(work == -jnp.inf, 1.0, 0.0)
    idx_ref[...] = idx_acc.astype(jnp.int32)
    gate_ref[...] = gate_acc / gsum * ROUTED_SCALE

    trow = lax.broadcasted_iota(jnp.int32, (tm, tm), 0)
    tcol = lax.broadcasted_iota(jnp.int32, (tm, tm), 1)
    earlier = jnp.where(tcol < trow, 1.0, 0.0).astype(BF16)
    before = _dg(earlier, chosen.astype(BF16)) + run_ref[0:1, :]
    rank_acc = jnp.zeros((tm, LANES), F32)
    for kk in range(TOP_K):
        rk = jnp.sum(jnp.where(lane_e == sels[kk], before, 0.0), axis=-1, keepdims=True)
        rank_acc = jnp.where(lane_o == kk, rk, rank_acc)
    rank_ref[...] = rank_acc.astype(jnp.int32)
    run_ref[...] = run_ref[...] + jnp.sum(chosen, axis=0, keepdims=True)
    cnt_ref[...] = run_ref[...]

    act = (_silu(_dg(hb, sw1_ref[...])) * _dg(hb, sw3_ref[...])).astype(BF16)
    base_ref[...] = x1 + g2 * _dg(act, sw2_ref[...])


def _mix(x, y, g, bonus, yconv, mod, gnw, gnb, bsum, wout, n2g, rw, rb, sw1, sw3, sw2):
    t = x.shape[0]
    tm = TOK_TILE
    off = (y.shape[2] - t) // tm
    full = lambda arr: pl.BlockSpec(arr.shape, lambda i: (0,) * arr.ndim)
    consts = (mod, gnw, gnb, bsum, wout, n2g, rw, rb, sw1, sw3, sw2)
    cat_spec = lambda w: pl.BlockSpec((tm, w), lambda i: (i + off, 0))
    tok_spec = lambda w: pl.BlockSpec((tm, w), lambda i: (i, 0))
    yf_spec = pl.BlockSpec((None, N_PAIRS, tm, LANES), lambda i: (0, 0, i + off, 0))
    yb_spec = pl.BlockSpec((None, N_PAIRS, tm, LANES), lambda i: (1, 0, i + off, 0))
    return pl.pallas_call(
        _mix_kernel,
        grid=(t // tm,),
        in_specs=[tok_spec(D_MODEL), yf_spec, yb_spec,
                  cat_spec(D_RWKV), cat_spec(D_RWKV), cat_spec(D_CONV)]
                 + [full(c) for c in consts],
        out_specs=[tok_spec(D_MODEL), tok_spec(HALF), tok_spec(LANES), tok_spec(LANES), tok_spec(LANES),
                   pl.BlockSpec((8, N_EXPERTS), lambda i: (0, 0))],
        out_shape=[jax.ShapeDtypeStruct((t, D_MODEL), F32), jax.ShapeDtypeStruct((t, HALF), I32),
                   jax.ShapeDtypeStruct((t, LANES), jnp.int32), jax.ShapeDtypeStruct((t, LANES), F32),
                   jax.ShapeDtypeStruct((t, LANES), jnp.int32), jax.ShapeDtypeStruct((8, N_EXPERTS), F32)],
        scratch_shapes=[pltpu.VMEM((8, N_EXPERTS), F32)],
        compiler_params=_cparams(("arbitrary",)),
        name="mix",
    )(x, y, y, g, bonus, yconv, *consts)


def _slots_kernel(idx_ref, rank_ref, start_ref, dest_ref):
    tm = idx_ref.shape[0]
    lane_e = lax.broadcasted_iota(jnp.int32, (tm, N_EXPERTS), 1)
    lane_o = lax.broadcasted_iota(jnp.int32, (tm, LANES), 1)
    idx = idx_ref[...]
    start = start_ref[...]
    acc = jnp.zeros((tm, LANES), F32)
    for kk in range(TOP_K):
        st = jnp.sum(jnp.where(lane_e == idx[:, kk:kk + 1], start, 0.0), axis=-1, keepdims=True)
        acc = jnp.where(lane_o == kk, st, acc)
    dest_ref[...] = acc.astype(jnp.int32) + rank_ref[...]


def _slots(idx, rank, pad_start):
    t = idx.shape[0]
    tm = SLOTS_TILE
    spec = pl.BlockSpec((tm, LANES), lambda i: (i, 0))
    return pl.pallas_call(
        _slots_kernel,
        grid=(t // tm,),
        in_specs=[spec, spec, pl.BlockSpec((1, N_EXPERTS), lambda i: (0, 0))],
        out_specs=spec,
        out_shape=jax.ShapeDtypeStruct((t, LANES), jnp.int32),
        compiler_params=_cparams(("arbitrary",)),
        name="slots",
    )(idx, rank, pad_start)


def _sc_scatter_rows(rows, dest_km, n_slots):
    info = plsc.get_sparse_core_info()
    nc, ns = info.num_cores, info.num_subcores
    nw = nc * ns
    t, d = rows.shape
    nk = dest_km.shape[0]
    ch = SC_SCATTER_CHUNK
    per_w = t // nw
    n_ch = per_w // ch
    assert t % nw == 0 and per_w % ch == 0
    mesh = plsc.VectorSubcoreMesh(core_axis_name="c", subcore_axis_name="s")
    idx = dest_km.reshape(nk, nw, n_ch, ch).transpose(1, 2, 0, 3).reshape(nw, n_ch * nk, ch)

    @functools.partial(
        pl.kernel, mesh=mesh,
        out_type=jax.ShapeDtypeStruct((n_slots, d), rows.dtype),
        scratch_types=[pltpu.VMEM((n_ch * nk, ch), jnp.int32),
                       pltpu.VMEM((ch, d), rows.dtype),
                       pltpu.SemaphoreType.DMA],
    )
    def scatter_kernel(rows_hbm, idx_hbm, out_hbm, idx_v, rows_v, sem):
        wid = lax.axis_index("s") * nc + lax.axis_index("c")
        base = wid * per_w
        pltpu.sync_copy(idx_hbm.at[wid], idx_v)

        @pl.loop(0, n_ch)
        def _(j):
            pltpu.sync_copy(rows_hbm.at[pl.ds(base + j * ch, ch)], rows_v)
            copies = [pltpu.make_async_copy(rows_v, out_hbm.at[idx_v.at[j * nk + kk]], sem) for kk in range(nk)]
            for cp in copies:
                cp.start()
            for cp in copies:
                cp.wait()

    return scatter_kernel(rows, idx)


def _experts_kernel(be_ref, nu_ref, nv_ref, hs_ref, w1c, w3c, w2c, o_ref):
    del be_ref
    b = pl.program_id(0)

    @pl.when(b < nu_ref[0])
    def _():
        live = lax.broadcasted_iota(jnp.int32, (hs_ref.shape[0], 1), 0) < nv_ref[b]
        lo, hi = _unpack_bf16_pairs(jnp.where(live, hs_ref[...], 0))
        lo = lo.astype(BF16)
        hi = hi.astype(BF16)
        a1 = _dg(lo, w1c[0:HALF, :]) + _dg(hi, w1c[HALF:, :])
        a3 = _dg(lo, w3c[0:HALF, :]) + _dg(hi, w3c[HALF:, :])
        act = (_silu(a1) * a3).astype(BF16)
        o_ref[...] = _pack_bf16_pairs(_dg(act, w2c[...]))

    @pl.when(b >= nu_ref[0])
    def _():
        o_ref[...] = jnp.zeros(o_ref.shape, I32)


def _experts(block_e, n_used, n_valid, hs, w1, w3, w2):
    nblk = block_e.shape[0]
    sb = SLOT_BLOCK
    row_map = lambda b, be, nu, nv: (jnp.minimum(b, nu[0] - 1), 0)
    w_map = lambda b, be, nu, nv: (be[b], 0, 0)
    grid_spec = pltpu.PrefetchScalarGridSpec(
        num_scalar_prefetch=3,
        grid=(nblk,),
        in_specs=[pl.BlockSpec((sb, HALF), row_map),
                  pl.BlockSpec((None, D_MODEL, D_EXPERT), w_map),
                  pl.BlockSpec((None, D_MODEL, D_EXPERT), w_map),
                  pl.BlockSpec((None, D_EXPERT, D_MODEL), w_map)],
        out_specs=pl.BlockSpec((sb, HALF), lambda b, be, nu, nv: (b, 0)),
    )
    return pl.pallas_call(
        _experts_kernel,
        grid_spec=grid_spec,
        out_shape=jax.ShapeDtypeStruct((nblk * sb, HALF), I32),
        compiler_params=_cparams(("arbitrary",)),
        name="experts",
    )(block_e, n_used, n_valid, hs, w1, w3, w2)


def _sc_gather_rows(table, idx):
    info = plsc.get_sparse_core_info()
    nc, ns = info.num_cores, info.num_subcores
    nw = nc * ns
    n = idx.shape[0]
    d = table.shape[1]
    ch = SC_CHUNK
    per_w = n // nw
    n_ch = per_w // ch
    assert n % nw == 0 and per_w % (2 * ch) == 0
    mesh = plsc.VectorSubcoreMesh(core_axis_name="c", subcore_axis_name="s")

    @functools.partial(
        pl.kernel, mesh=mesh,
        out_type=jax.ShapeDtypeStruct((n, d), table.dtype),
        scratch_types=[pltpu.VMEM((n_ch, ch), jnp.int32),
                       pltpu.VMEM((2, ch, d), table.dtype),
                       pltpu.SemaphoreType.DMA((2,))],
    )
    def gather_kernel(table_hbm, idx_hbm, out_hbm, idx_v, rows_v, sem):
        wid = lax.axis_index("s") * nc + lax.axis_index("c")
        base = wid * per_w
        pltpu.sync_copy(idx_hbm.at[wid], idx_v)

        def gather(j, buf):
            return pltpu.make_async_copy(table_hbm.at[idx_v.at[j]], rows_v.at[buf], sem.at[buf])

        gather(0, 0).start()

        @pl.loop(0, n_ch, step=2)
        def _(j):
            for buf in range(2):
                jj = j + buf
                gather(jj, buf).wait()

                @pl.when(jj + 1 < n_ch)
                def _():
                    gather(jj + 1, 1 - buf).start()

                pltpu.sync_copy(rows_v.at[buf], out_hbm.at[pl.ds(base + jj * ch, ch)])

    return gather_kernel(table, idx.reshape(nw, n_ch, ch))


def _combine_kernel(rows_ref, base_ref, gate_ref, mod_ref, fg_ref, o_ref):
    tm = base_ref.shape[0]
    gate = gate_ref[...]
    acc_lo = jnp.zeros((tm, HALF), F32)
    acc_hi = jnp.zeros((tm, HALF), F32)
    for kk in range(TOP_K):
        lo, hi = _unpack_bf16_pairs(rows_ref[kk])
        acc_lo = acc_lo + gate[:, kk:kk + 1] * lo
        acc_hi = acc_hi + gate[:, kk:kk + 1] * hi
    acc = jnp.concatenate([acc_lo, acc_hi], axis=1)
    g2 = mod_ref[0:1, 5 * D_MODEL:6 * D_MODEL]
    o_ref[...] = _rms(base_ref[...] + g2 * acc, fg_ref[...])


def _combine(dest_km, ye, base, gate, mod, fg):
    t = base.shape[0]
    tm = TOK_TILE
    full = lambda arr: pl.BlockSpec(arr.shape, lambda i: (0,) * arr.ndim)
    rows = _sc_gather_rows(ye, dest_km.reshape(-1)).reshape(TOP_K, t, HALF)
    return pl.pallas_call(
        _combine_kernel,
        grid=(t // tm,),
        in_specs=[pl.BlockSpec((TOP_K, tm, HALF), lambda i: (0, i, 0)),
                  pl.BlockSpec((tm, D_MODEL), lambda i: (i, 0)),
                  pl.BlockSpec((tm, LANES), lambda i: (i, 0)),
                  full(mod), full(fg)],
        out_specs=pl.BlockSpec((tm, D_MODEL), lambda i: (i, 0)),
        out_shape=jax.ShapeDtypeStruct((t, D_MODEL), F32),
        compiler_params=_cparams(("arbitrary",)),
        name="combine",
    )(rows, base, gate, mod, fg)


def _block_tables(counts, n_blocks):
    sb = SLOT_BLOCK
    padded = (counts + sb - 1) // sb * sb
    pad_end = jnp.cumsum(padded)
    pad_start = pad_end - padded
    n_used = pad_end[-1] // sb
    blk = jnp.minimum(jnp.arange(n_blocks, dtype=jnp.int32), n_used - 1)
    block_e = jnp.minimum(jnp.sum(pad_end[None, :] <= (blk * sb)[:, None], axis=1), N_EXPERTS - 1).astype(jnp.int32)
    onehot = block_e[:, None] == jnp.arange(N_EXPERTS, dtype=jnp.int32)[None, :]
    live_end = jnp.sum(jnp.where(onehot, (pad_start + counts)[None, :], 0), axis=1)
    n_valid = jnp.clip(live_end - blk * sb, 0, sb).astype(jnp.int32)
    return block_e, n_used.reshape(1).astype(jnp.int32), n_valid, pad_start


def _blockdiag2(w):
    z = jnp.zeros_like(w[0])
    return jnp.concatenate([jnp.concatenate([w[0], z], axis=1), jnp.concatenate([z, w[1]], axis=1)], axis=0)


def _pad_rows(w, n=8):
    return jnp.concatenate([w, jnp.zeros((n - w.shape[0],) + w.shape[1:], w.dtype)], axis=0)


def kernel(x, c, ctx, c_ctx, ada_w, ada_b, norm1_g, norm2_g, w_in, conv_w, shift_mu, decay_w0, decay_up, iclr_a0, iclr_up, key_xi, key_alpha, bonus_rho, gate_up, gn_w, gn_b, w_out, router_w, router_bias, exp_w1, exp_w3, exp_w2, sh_w1, sh_w3, sh_w2, final_g):
    assert x.shape[0] == 1 and ada_w.shape[0] == 1
    assert ctx.shape[1] == TOK_TILE and x.shape[1] % TOK_TILE == 0
    l = 0
    xs = x[0]
    row = lambda w: w.reshape(1, -1)

    cc = _pad_rows(jnp.stack([c[0], c_ctx], axis=0))
    mod = _ada(cc, ada_w[l], row(ada_b[l]))

    hid = lax.broadcasted_iota(jnp.int32, (D_RWKV, D_RWKV), 0) // HEAD_DIM
    bsum = (hid == hid.T).astype(BF16)

    r4, v4, g, bonus, yconv, lw, kh, kt, a = _proj(
        ctx[0], xs, mod, row(norm1_g[l]), w_in[l].astype(BF16), _pad_rows(conv_w[l]), _pad_rows(shift_mu[l]),
        row(decay_w0[l]), _blockdiag2(decay_up[l]).astype(BF16), row(iclr_a0[l]), _blockdiag2(iclr_up[l]).astype(BF16),
        row(key_xi[l]), row(key_alpha[l]), row(bonus_rho[l]), gate_up[l].astype(BF16), bsum)

    y, w1b, w3b, w2b = _scan(r4, v4, lw, kh, kt, a, exp_w1[l], exp_w3[l], exp_w2[l])

    base, h2, idx, gate, rank, cnt = _mix(
        xs, y, g, bonus, yconv, mod, row(gn_w[l]), row(gn_b[l]), bsum, w_out[l].astype(BF16),
        row(norm2_g[l]), router_w[l], row(router_bias[l]),
        sh_w1[l].astype(BF16), sh_w3[l].astype(BF16), sh_w2[l].astype(BF16))

    n_blocks = xs.shape[0] * TOP_K // SLOT_BLOCK + N_EXPERTS
    block_e, n_used, n_valid, pad_start = _block_tables(cnt[0].astype(jnp.int32), n_blocks)
    dest_km = _slots(idx, rank, pad_start.astype(F32).reshape(1, N_EXPERTS))[:, :TOP_K].T
    hs = _sc_scatter_rows(h2, dest_km, n_blocks * SLOT_BLOCK)
    ye = _experts(block_e, n_used, n_valid, hs, w1b, w3b, w2b)
    out = _combine(dest_km, ye, base, gate, mod, row(final_g))
    return out[None]
```

```python
import functools

import jax
import jax.numpy as jnp
from jax import lax
from jax.experimental import pallas as pl
from jax.experimental.pallas import tpu as pltpu
from jax.experimental.pallas import tpu_sc as plsc

F32 = jnp.float32
BF16 = jnp.bfloat16

D_MODEL = 1024
D_CONV = 512
D_RWKV = 512
HEAD_DIM = 64
N_HEADS = D_RWKV // HEAD_DIM
N_PAIRS = N_HEADS // 2
LORA = 128
P_RWKV = 3 * D_RWKV + 3 * LORA
P_IN = 3 * D_CONV + P_RWKV
GRID_W = 64
N_EXPERTS = 256
TOP_K = 8
D_EXPERT = 256
ROUTED_SCALE = 2.5
RMS_EPS = 1e-6
GN_EPS = 64e-5
DECAY_SCALE = 0.6065306597126334
NORM_EPS = 1e-12

TOK_TILE = 256
SLOTS_TILE = 1024
SCAN_BLOCK = 256
CHUNK = 64
SLOT_BLOCK = 512
SC_CHUNK = 64
SC_SCATTER_CHUNK = 128
LANES = 128
VMEM_LIMIT = 56 * 1024 * 1024

NN = ((1,), (0,))
NT = ((1,), (1,))
TN = ((0,), (0,))


def _dg(a, b, dims=NN):
    return lax.dot_general(a, b, (dims, ((), ())), preferred_element_type=F32)


def _split2(a):
    hi = a.astype(BF16)
    lo = (a - hi.astype(F32)).astype(BF16)
    return hi, lo


def _mm1(a, b, dims=NN):
    return _dg(a.astype(BF16), b.astype(BF16), dims)


def _mm3(a, b, dims=NN):
    ah, al = _split2(a)
    bh, bl = _split2(b)
    return _dg(ah, bh, dims) + (_dg(ah, bl, dims) + _dg(al, bh, dims))


def _mm_split_lhs(a, b_bf16, dims=NN):
    h, l = _split2(a)
    return _dg(h, b_bf16, dims) + _dg(l, b_bf16, dims)


def _mm_ones_rhs(a, ones_bf16):
    return _dg(a.astype(BF16), ones_bf16)


def _mm_ones_lhs(ones_bf16, b, dims=NN):
    h, l = _split2(b)
    return _dg(ones_bf16, h, dims) + _dg(ones_bf16, l, dims)


HALF = D_MODEL // 2
I32 = jnp.int32


def _pack_bf16_pairs(x):
    return pltpu.pack_elementwise([x[:, 0:HALF], x[:, HALF:]], packed_dtype=BF16)


def _unpack_bf16_pairs(w):
    lo = pltpu.unpack_elementwise(w, index=0, packed_dtype=BF16, unpacked_dtype=F32)
    hi = pltpu.unpack_elementwise(w, index=1, packed_dtype=BF16, unpacked_dtype=F32)
    return lo, hi


def _sigmoid(x):
    return 1.0 / (1.0 + jnp.exp(-x))


def _silu(x):
    return x * _sigmoid(x)


def _rms(xv, g):
    ms = jnp.mean(xv * xv, axis=-1, keepdims=True)
    return xv * lax.rsqrt(ms + RMS_EPS) * g


def _cparams(sem):
    return pltpu.CompilerParams(dimension_semantics=sem, vmem_limit_bytes=VMEM_LIMIT)


def _ada_kernel(c_ref, w_ref, b_ref, o_ref):
    o_ref[...] = _mm3(_silu(c_ref[...]), w_ref[...]) + b_ref[...]


def _ada(cc, ada_w, ada_b):
    n = ada_w.shape[1]
    tn = 1024
    return pl.pallas_call(
        _ada_kernel,
        grid=(n // tn,),
        in_specs=[pl.BlockSpec((8, D_MODEL), lambda j: (0, 0)),
                  pl.BlockSpec((D_MODEL, tn), lambda j: (0, j)),
                  pl.BlockSpec((1, tn), lambda j: (0, j))],
        out_specs=pl.BlockSpec((8, tn), lambda j: (0, j)),
        out_shape=jax.ShapeDtypeStruct((8, n), F32),
        compiler_params=_cparams(("arbitrary",)),
        name="ada",
    )(cc, ada_w, ada_b)


def _proj_kernel(ctx_ref, x_ref, xp_ref, xn_ref, mod_ref, n1g_ref, win_ref, convw_ref, mu_ref,
                 dw0_ref, dup_ref, ia0_ref, iup_ref, xi_ref, kal_ref, rho_ref, gup_ref, bsum_ref,
                 r_ref, v_ref, g_ref, bonus_ref, yconv_ref, lw_ref, kh_ref, kt_ref, a_ref):
    i = pl.program_id(0)
    nt = pl.num_programs(0)
    tm = x_ref.shape[0]
    is_ctx = i == 0
    sh = jnp.where(is_ctx, mod_ref[1:2, 0:D_MODEL], mod_ref[0:1, 0:D_MODEL])
    sc = jnp.where(is_ctx, mod_ref[1:2, D_MODEL:2 * D_MODEL], mod_ref[0:1, D_MODEL:2 * D_MODEL])
    n1g = n1g_ref[...]

    def norm_mod(xv):
        return _rms(xv, n1g) * (1.0 + sc) + sh

    xt = jnp.where(is_ctx, ctx_ref[...], x_ref[...])
    h = norm_mod(jnp.concatenate([xt, xp_ref[...], xn_ref[...]], axis=0)).astype(BF16)
    p_all = _dg(h, win_ref[...])
    p = p_all[0:tm]
    ph = p_all[tm:tm + 16, 3 * D_CONV:]

    rows = lax.broadcasted_iota(jnp.int32, (tm, 1), 0)

    bg = p[:, 0:D_CONV]
    z = p[:, D_CONV:2 * D_CONV] * p[:, 2 * D_CONV:3 * D_CONV]
    col = rows % GRID_W
    zp = jnp.where(col == 0, 0.0, pltpu.roll(z, 1, 0))
    zn = jnp.where(col == GRID_W - 1, 0.0, pltpu.roll(z, tm - 1, 0))
    yconv = bg * (convw_ref[0:1, :] * zp + convw_ref[1:2, :] * z + convw_ref[2:3, :] * zn)
    yconv_ref[...] = yconv.astype(BF16)

    prev_ok = jnp.logical_and(i != 0, i != 1).astype(F32)
    next_ok = jnp.logical_and(i != 0, i != nt - 1).astype(F32)
    cur = p[:, 3 * D_CONV:]
    prev = jnp.where(rows == 0, ph[7:8, :] * prev_ok, pltpu.roll(cur, 1, 0))
    nxt = jnp.where(rows == tm - 1, ph[8:9, :] * next_ok, pltpu.roll(cur, tm - 1, 0))
    ps = cur + mu_ref[0:1, :] * (prev - cur) + mu_ref[1:2, :] * (nxt - cur)

    r = ps[:, 0:D_RWKV]
    k = ps[:, D_RWKV:2 * D_RWKV]
    v = ps[:, 2 * D_RWKV:3 * D_RWKV]
    o = 3 * D_RWKV
    wlo = ps[:, o:o + LORA]
    alo = ps[:, o + LORA:o + 2 * LORA]
    glo = ps[:, o + 2 * LORA:o + 3 * LORA]

    dd = dw0_ref[...] + _mm_split_lhs(jnp.tanh(wlo), dup_ref[...])
    lw = -DECAY_SCALE * _sigmoid(dd)
    a = _sigmoid(ia0_ref[...] + _mm_split_lhs(alo, iup_ref[...]))
    g = _dg(_sigmoid(glo).astype(BF16), gup_ref[...])
    k2 = jnp.concatenate([k, k], axis=1)
    kap = k2 * xi_ref[...]
    kap2 = kap * kap
    bs = bsum_ref[...]
    ss = jnp.concatenate([_mm_ones_rhs(kap2[:, 0:D_RWKV], bs),
                          _mm_ones_rhs(kap2[:, D_RWKV:], bs)], axis=1)
    kh = kap * lax.rsqrt(ss + NORM_EPS)
    kt = k2 * (1.0 + (a - 1.0) * kal_ref[...])
    bon = _mm_ones_rhs(r * rho_ref[...] * (kt[:, 0:D_RWKV] + kt[:, D_RWKV:]), bs)

    g_ref[...] = g
    bonus_ref[...] = bon * v
    for pr in range(N_PAIRS):
        ls = slice(pr * LANES, (pr + 1) * LANES)
        r_ref[pr] = r[:, ls]
        v_ref[pr] = v[:, ls]
        for d in range(2):
            ld = slice(d * D_RWKV + pr * LANES, d * D_RWKV + (pr + 1) * LANES)
            lw_ref[d, pr] = lw[:, ld]
            kh_ref[d, pr] = kh[:, ld]
            kt_ref[d, pr] = kt[:, ld]
            a_ref[d, pr] = a[:, ld]


def _proj(ctx2, x2, mod, n1g, win, convw, mu, dw0, dup, ia0, iup, xi, kal, rho, gup, bsum):
    tm = TOK_TILE
    t = ctx2.shape[0] + x2.shape[0]
    nt = t // tm
    per = tm // 8
    nb8 = x2.shape[0] // 8
    full = lambda arr: pl.BlockSpec(arr.shape, lambda i: (0,) * arr.ndim)
    pair_spec = pl.BlockSpec((N_PAIRS, tm, LANES), lambda i: (0, i, 0))
    dpair_spec = pl.BlockSpec((2, N_PAIRS, tm, LANES), lambda i: (0, 0, i, 0))
    row_spec = pl.BlockSpec((tm, D_RWKV), lambda i: (i, 0))
    pair_shape = jax.ShapeDtypeStruct((N_PAIRS, t, LANES), F32)
    dpair_shape = jax.ShapeDtypeStruct((2, N_PAIRS, t, LANES), F32)
    consts = (mod, n1g, win, convw, mu, dw0, dup, ia0, iup, xi, kal, rho, gup, bsum)
    return pl.pallas_call(
        _proj_kernel,
        grid=(nt,),
        in_specs=[pl.BlockSpec((tm, D_MODEL), lambda i: (0, 0)),
                  pl.BlockSpec((tm, D_MODEL), lambda i: (jnp.maximum(i - 1, 0), 0)),
                  pl.BlockSpec((8, D_MODEL), lambda i: (jnp.maximum((i - 1) * per - 1, 0), 0)),
                  pl.BlockSpec((8, D_MODEL), lambda i: (jnp.clip(i * per, 0, nb8 - 1), 0))]
                 + [full(c) for c in consts],
        out_specs=[pair_spec, pair_spec, row_spec, row_spec, row_spec,
                   dpair_spec, dpair_spec, dpair_spec, dpair_spec],
        out_shape=[pair_shape, pair_shape,
                   jax.ShapeDtypeStruct((t, D_RWKV), F32), jax.ShapeDtypeStruct((t, D_RWKV), F32),
                   jax.ShapeDtypeStruct((t, D_CONV), BF16),
                   dpair_shape, dpair_shape, dpair_shape, dpair_shape],
        compiler_params=_cparams(("arbitrary",)),
        name="proj",
    )(ctx2, x2, x2, x2, *consts)


def _scan_kernel(r_ref, v_ref, lw_ref, kh_ref, kt_ref, a_ref, w1_ref, w3_ref, w2_ref,
                 y_ref, w1b_ref, w3b_ref, w2b_ref, q_ref):
    w1b_ref[...] = w1_ref[...].astype(BF16)
    w3b_ref[...] = w3_ref[...].astype(BF16)
    w2b_ref[...] = w2_ref[...].astype(BF16)

    d = pl.program_id(0)
    rev = d == 1
    c = CHUNK
    npair = r_ref.shape[0]
    nsub = r_ref.shape[1] // c

    row = lax.broadcasted_iota(jnp.int32, (c, c), 0)
    col = lax.broadcasted_iota(jnp.int32, (c, c), 1)
    strict = (col - row) * (1 - 2 * d) < 0
    eye = row == col
    incl = jnp.logical_or(strict, eye)
    incl_bf = jnp.where(incl, 1.0, 0.0).astype(BF16)
    eye_f = jnp.where(eye, 1.0, 0.0)
    head0 = lax.broadcasted_iota(jnp.int32, (c, LANES), 1) < HEAD_DIM
    row2 = lax.broadcasted_iota(jnp.int32, (LANES, LANES), 0)
    col2 = lax.broadcasted_iota(jnp.int32, (LANES, LANES), 1)
    same_head = (row2 < HEAD_DIM) == (col2 < HEAD_DIM)
    eye2 = row2 == col2

    subs = range(npair * nsub)
    units = [(s, h) for s in subs for h in range(2)]
    off = [pl.multiple_of(jnp.where(rev, nsub - 1 - s, s) * c, c) for s in range(nsub)]
    sls = [(pp, pl.ds(off[s], c), slice(None)) for pp in range(npair) for s in range(nsub)]
    v = [v_ref[sl] for sl in sls]
    lw = [lw_ref[sl] for sl in sls]
    cum = [_mm_ones_lhs(incl_bf, lw[s]) for s in subs]
    al, be, kk, rr, bew, kkw, wc = [], [], [], [], [], [], []
    for s in subs:
        cum_last = jnp.where(rev, cum[s][0:1, :], cum[s][c - 1:c, :])
        e_end = jnp.exp(cum_last - cum[s])
        e_neg = jnp.exp(-cum[s])
        kh = kh_ref[sls[s]]
        kt = kt_ref[sls[s]]
        kha = kh * a_ref[sls[s]]
        al.append(kh * jnp.exp(cum[s] - lw[s]))
        be.append(-(kha * e_neg))
        kk.append(kt * e_neg)
        rr.append(r_ref[sls[s]] * jnp.exp(cum[s]))
        bew.append(-(kha * e_end))
        kkw.append(kt * e_end)
        wc.append(jnp.exp(cum_last))

    lhs = {}
    for s, h in units:
        hm = head0 if h == 0 else jnp.logical_not(head0)
        lhs[s, h] = jnp.concatenate([jnp.where(hm, al[s], 0.0), jnp.where(hm, rr[s], 0.0)], axis=0)
    gb = {u: _mm1(lhs[u], be[u[0]], NT) for u in units}
    gk = {u: _mm1(lhs[u], kk[u[0]], NT) for u in units}
    a_ab = {u: jnp.where(strict, gb[u][0:c], 0.0) for u in units}
    a_ak = {u: jnp.where(strict, gk[u][0:c], 0.0) for u in units}
    a_rb = {u: jnp.where(incl, gb[u][c:2 * c], 0.0) for u in units}
    a_rk = {u: jnp.where(incl, gk[u][c:2 * c], 0.0) for u in units}

    pw = dict(a_ab)
    tinv = {u: eye_f + a_ab[u] for u in units}
    akrk_v = {u: _mm1(jnp.concatenate([a_ak[u], a_rk[u]], axis=0), v[u[0]]) for u in units}
    avk = {u: akrk_v[u][0:c] for u in units}
    rk_v = {u: akrk_v[u][c:2 * c] for u in units}
    for _ in range(c.bit_length() - 2):
        pw = {u: _mm1(pw[u], pw[u]) for u in units}
        tinv = {u: tinv[u] + _mm1(tinv[u], pw[u]) for u in units}
    tu = {u: _mm1(tinv[u], jnp.concatenate([al[u[0]], avk[u]], axis=1)) for u in units}
    at = [jnp.where(head0, tu[s, 0][:, 0:LANES], tu[s, 1][:, 0:LANES]) for s in subs]
    ut = [jnp.where(head0, tu[s, 0][:, LANES:], tu[s, 1][:, LANES:]) for s in subs]

    rb = {u: _mm1(a_rb[u], jnp.concatenate([at[u[0]], ut[u[0]]], axis=1)) for u in units}
    rb_at = {u: rb[u][:, 0:LANES] for u in units}
    rb_ut = {u: rb[u][:, LANES:] for u in units}
    md = [_mm1(bew[s], jnp.concatenate([at[s], ut[s]], axis=1), TN) for s in subs]
    m_off = [md[s][:, 0:LANES] for s in subs]
    d_u = [md[s][:, LANES:] for s in subs]
    d_v = [_mm1(kkw[s], v[s], TN) for s in subs]
    rh = [rr[s] + jnp.where(head0, rb_at[s, 0], rb_at[s, 1]) for s in subs]
    yh = [jnp.where(head0, rb_ut[s, 0] + rk_v[s, 0], rb_ut[s, 1] + rk_v[s, 1]) for s in subs]
    mt = [jnp.where(eye2, wc[s], 0.0) + jnp.where(same_head, m_off[s], 0.0) for s in subs]
    dt = [jnp.where(same_head, d_u[s] + d_v[s], 0.0) for s in subs]

    @pl.when(pl.program_id(1) == 0)
    def _():
        q_ref[...] = jnp.zeros(q_ref.shape, F32)

    q = [q_ref[pp] for pp in range(npair)]
    for si in range(nsub):
        ys = [_mm1(rh[pp * nsub + si], q[pp]) for pp in range(npair)]
        qn = [_mm1(mt[pp * nsub + si], q[pp]) for pp in range(npair)]
        for pp in range(npair):
            y_ref[sls[pp * nsub + si]] = ys[pp] + yh[pp * nsub + si]
            q[pp] = qn[pp] + dt[pp * nsub + si]
    for pp in range(npair):
        q_ref[pp] = q[pp]


def _scan(r4, v4, lw, kh, kt, a, w1, w3, w2):
    t = r4.shape[1]
    tb = SCAN_BLOCK
    nb = t // tb
    ne = w1.shape[0]
    epb = -(-ne // (2 * nb))
    assert ne % epb == 0
    last = ne // epb - 1

    def blk(d, j):
        return jnp.where(d == 0, j, jnp.where(j == 0, 0, nb - j))

    def wblk(d, j):
        return (jnp.minimum(d * nb + j, last), 0, 0)

    shared = pl.BlockSpec((N_PAIRS, tb, LANES), lambda d, j: (0, blk(d, j), 0))
    perdir = pl.BlockSpec((None, N_PAIRS, tb, LANES), lambda d, j: (d, 0, blk(d, j), 0))
    w13 = pl.BlockSpec((epb, D_MODEL, D_EXPERT), wblk)
    w2s = pl.BlockSpec((epb, D_EXPERT, D_MODEL), wblk)
    return pl.pallas_call(
        _scan_kernel,
        grid=(2, nb),
        in_specs=[shared, shared, perdir, perdir, perdir, perdir, w13, w13, w2s],
        out_specs=[perdir, w13, w13, w2s],
        out_shape=[jax.ShapeDtypeStruct((2, N_PAIRS, t, LANES), F32),
                   jax.ShapeDtypeStruct(w1.shape, BF16), jax.ShapeDtypeStruct(w3.shape, BF16),
                   jax.ShapeDtypeStruct(w2.shape, BF16)],
        scratch_shapes=[pltpu.VMEM((N_PAIRS, LANES, LANES), F32)],
        compiler_params=_cparams(("arbitrary", "arbitrary")),
        name="scan",
    )(r4, v4, lw, kh, kt, a, w1, w3, w2)


def _mix_kernel(x_ref, yf_ref, yb_ref, g_ref, bonus_ref, yconv_ref, mod_ref, gnw_ref, gnb_ref, bsum_ref,
                wout_ref, n2g_ref, rw_ref, rb_ref, sw1_ref, sw3_ref, sw2_ref,
                base_ref, h2_ref, idx_ref, gate_ref, rank_ref, cnt_ref, run_ref):
    tm = x_ref.shape[0]
    g1 = mod_ref[0:1, 2 * D_MODEL:3 * D_MODEL]
    sh2 = mod_ref[0:1, 3 * D_MODEL:4 * D_MODEL]
    sc2 = mod_ref[0:1, 4 * D_MODEL:5 * D_MODEL]
    g2 = mod_ref[0:1, 5 * D_MODEL:6 * D_MODEL]

    @pl.when(pl.program_id(0) == 0)
    def _():
        run_ref[...] = jnp.zeros(run_ref.shape, F32)

    y = jnp.concatenate([yf_ref[pr] + yb_ref[pr] for pr in range(N_PAIRS)], axis=1)
    bs = bsum_ref[...]
    mu = _mm_ones_rhs(y, bs) * (1.0 / HEAD_DIM)
    yc = y - mu
    var = _mm_ones_rhs(yc * yc, bs) * (1.0 / HEAD_DIM)
    yn = yc * lax.rsqrt(var + GN_EPS) * gnw_ref[...] + gnb_ref[...]
    yrw = ((yn + bonus_ref[...]) * g_ref[...]).astype(BF16)
    mix = _dg(yconv_ref[...], wout_ref[0:D_CONV, :]) + _dg(yrw, wout_ref[D_CONV:, :])
    x1 = x_ref[...] + g1 * mix

    h2 = _rms(x1, n2g_ref[...]) * (1.0 + sc2) + sh2
    hb = h2.astype(BF16)
    h2_ref[...] = _pack_bf16_pairs(h2)

    scores = _sigmoid(_mm3(h2, rw_ref[...]))
    work = scores + rb_ref[...]
    lane_e = lax.broadcasted_iota(jnp.int32, (tm, N_EXPERTS), 1).astype(F32)
    lane_o = lax.broadcasted_iota(jnp.int32, (tm, LANES), 1)
    idx_acc = jnp.zeros((tm, LANES), F32)
    gate_acc = jnp.zeros((tm, LANES), F32)
    gsum = jnp.zeros((tm, 1), F32)
    sels = []
    for kk in range(TOP_K):
        m = jnp.max(work, axis=-1, keepdims=True)
        sel = jnp.min(jnp.where(work == m, lane_e, float(N_EXPERTS)), axis=-1, keepdims=True)
        hit = lane_e == sel
        sk = jnp.sum(jnp.where(hit, scores, 0.0), axis=-1, keepdims=True)
        idx_acc = jnp.where(lane_o == kk, sel, idx_acc)
        gate_acc = jnp.where(lane_o == kk, sk, gate_acc)
        gsum = gsum + sk
        work = jnp.where(hit, -jnp.inf, work)
        sels.append(sel)
    chosen = jnp.where(work == -jnp.inf, 1.0, 0.0)
    idx_ref[...] = idx_acc.astype(jnp.int32)
    gate_ref[...] = gate_acc / gsum * ROUTED_SCALE

    trow = lax.broadcasted_iota(jnp.int32, (tm, tm), 0)
    tcol = lax.broadcasted_iota(jnp.int32, (tm, tm), 1)
    earlier = jnp.where(tcol < trow, 1.0, 0.0).astype(BF16)
    before = _dg(earlier, chosen.astype(BF16)) + run_ref[0:1, :]
    rank_acc = jnp.zeros((tm, LANES), F32)
    for kk in range(TOP_K):
        rk = jnp.sum(jnp.where(lane_e == sels[kk], before, 0.0), axis=-1, keepdims=True)
        rank_acc = jnp.where(lane_o == kk, rk, rank_acc)
    rank_ref[...] = rank_acc.astype(jnp.int32)
    run_ref[...] = run_ref[...] + jnp.sum(chosen, axis=0, keepdims=True)
    cnt_ref[...] = run_ref[...]

    act = (_silu(_dg(hb, sw1_ref[...])) * _dg(hb, sw3_ref[...])).astype(BF16)
    base_ref[...] = x1 + g2 * _dg(act, sw2_ref[...])


def _mix(x, y, g, bonus, yconv, mod, gnw, gnb, bsum, wout, n2g, rw, rb, sw1, sw3, sw2):
    t = x.shape[0]
    tm = TOK_TILE
    off = (y.shape[2] - t) // tm
    full = lambda arr: pl.BlockSpec(arr.shape, lambda i: (0,) * arr.ndim)
    consts = (mod, gnw, gnb, bsum, wout, n2g, rw, rb, sw1, sw3, sw2)
    cat_spec = lambda w: pl.BlockSpec((tm, w), lambda i: (i + off, 0))
    tok_spec = lambda w: pl.BlockSpec((tm, w), lambda i: (i, 0))
    yf_spec = pl.BlockSpec((None, N_PAIRS, tm, LANES), lambda i: (0, 0, i + off, 0))
    yb_spec = pl.BlockSpec((None, N_PAIRS, tm, LANES), lambda i: (1, 0, i + off, 0))
    return pl.pallas_call(
        _mix_kernel,
        grid=(t // tm,),
        in_specs=[tok_spec(D_MODEL), yf_spec, yb_spec,
                  cat_spec(D_RWKV), cat_spec(D_RWKV), cat_spec(D_CONV)]
                 + [full(c) for c in consts],
        out_specs=[tok_spec(D_MODEL), tok_spec(HALF), tok_spec(LANES), tok_spec(LANES), tok_spec(LANES),
                   pl.BlockSpec((8, N_EXPERTS), lambda i: (0, 0))],
        out_shape=[jax.ShapeDtypeStruct((t, D_MODEL), F32), jax.ShapeDtypeStruct((t, HALF), I32),
                   jax.ShapeDtypeStruct((t, LANES), jnp.int32), jax.ShapeDtypeStruct((t, LANES), F32),
                   jax.ShapeDtypeStruct((t, LANES), jnp.int32), jax.ShapeDtypeStruct((8, N_EXPERTS), F32)],
        scratch_shapes=[pltpu.VMEM((8, N_EXPERTS), F32)],
        compiler_params=_cparams(("arbitrary",)),
        name="mix",
    )(x, y, y, g, bonus, yconv, *consts)


def _slots_kernel(idx_ref, rank_ref, start_ref, dest_ref):
    tm = idx_ref.shape[0]
    lane_e = lax.broadcasted_iota(jnp.int32, (tm, N_EXPERTS), 1)
    lane_o = lax.broadcasted_iota(jnp.int32, (tm, LANES), 1)
    idx = idx_ref[...]
    start = start_ref[...]
    acc = jnp.zeros((tm, LANES), F32)
    for kk in range(TOP_K):
        st = jnp.sum(jnp.where(lane_e == idx[:, kk:kk + 1], start, 0.0), axis=-1, keepdims=True)
        acc = jnp.where(lane_o == kk, st, acc)
    dest_ref[...] = acc.astype(jnp.int32) + rank_ref[...]


def _slots(idx, rank, pad_start):
    t = idx.shape[0]
    tm = SLOTS_TILE
    spec = pl.BlockSpec((tm, LANES), lambda i: (i, 0))
    return pl.pallas_call(
        _slots_kernel,
        grid=(t // tm,),
        in_specs=[spec, spec, pl.BlockSpec((1, N_EXPERTS), lambda i: (0, 0))],
        out_specs=spec,
        out_shape=jax.ShapeDtypeStruct((t, LANES), jnp.int32),
        compiler_params=_cparams(("arbitrary",)),
        name="slots",
    )(idx, rank, pad_start)


def _sc_scatter_rows(rows, dest_km, n_slots):
    info = plsc.get_sparse_core_info()
    nc, ns = info.num_cores, info.num_subcores
    nw = nc * ns
    t, d = rows.shape
    nk = dest_km.shape[0]
    ch = SC_SCATTER_CHUNK
    per_w = t // nw
    n_ch = per_w // ch
    assert t % nw == 0 and per_w % ch == 0
    mesh = plsc.VectorSubcoreMesh(core_axis_name="c", subcore_axis_name="s")
    idx = dest_km.reshape(nk, nw, n_ch, ch).transpose(1, 2, 0, 3).reshape(nw, n_ch * nk, ch)

    @functools.partial(
        pl.kernel, mesh=mesh,
        out_type=jax.ShapeDtypeStruct((n_slots, d), rows.dtype),
        scratch_types=[pltpu.VMEM((n_ch * nk, ch), jnp.int32),
                       pltpu.VMEM((ch, d), rows.dtype),
                       pltpu.SemaphoreType.DMA],
    )
    def scatter_kernel(rows_hbm, idx_hbm, out_hbm, idx_v, rows_v, sem):
        wid = lax.axis_index("s") * nc + lax.axis_index("c")
        base = wid * per_w
        pltpu.sync_copy(idx_hbm.at[wid], idx_v)

        @pl.loop(0, n_ch)
        def _(j):
            pltpu.sync_copy(rows_hbm.at[pl.ds(base + j * ch, ch)], rows_v)
            copies = [pltpu.make_async_copy(rows_v, out_hbm.at[idx_v.at[j * nk + kk]], sem) for kk in range(nk)]
            for cp in copies:
                cp.start()
            for cp in copies:
                cp.wait()

    return scatter_kernel(rows, idx)


def _experts_kernel(be_ref, nu_ref, nv_ref, hs_ref, w1c, w3c, w2c, o_ref):
    del be_ref
    b = pl.program_id(0)

    @pl.when(b < nu_ref[0])
    def _():
        live = lax.broadcasted_iota(jnp.int32, (hs_ref.shape[0], 1), 0) < nv_ref[b]
        lo, hi = _unpack_bf16_pairs(jnp.where(live, hs_ref[...], 0))
        hb = jnp.concatenate([lo.astype(BF16), hi.astype(BF16)], axis=1)
        act = (_silu(_dg(hb, w1c[...])) * _dg(hb, w3c[...])).astype(BF16)
        o_ref[...] = _pack_bf16_pairs(_dg(act, w2c[...]))

    @pl.when(b >= nu_ref[0])
    def _():
        o_ref[...] = jnp.zeros(o_ref.shape, I32)


def _experts(block_e, n_used, n_valid, hs, w1, w3, w2):
    nblk = block_e.shape[0]
    sb = SLOT_BLOCK
    row_map = lambda b, be, nu, nv: (jnp.minimum(b, nu[0] - 1), 0)
    w_map = lambda b, be, nu, nv: (be[b], 0, 0)
    grid_spec = pltpu.PrefetchScalarGridSpec(
        num_scalar_prefetch=3,
        grid=(nblk,),
        in_specs=[pl.BlockSpec((sb, HALF), row_map),
                  pl.BlockSpec((None, D_MODEL, D_EXPERT), w_map),
                  pl.BlockSpec((None, D_MODEL, D_EXPERT), w_map),
                  pl.BlockSpec((None, D_EXPERT, D_MODEL), w_map)],
        out_specs=pl.BlockSpec((sb, HALF), lambda b, be, nu, nv: (b, 0)),
    )
    return pl.pallas_call(
        _experts_kernel,
        grid_spec=grid_spec,
        out_shape=jax.ShapeDtypeStruct((nblk * sb, HALF), I32),
        compiler_params=_cparams(("arbitrary",)),
        name="experts",
    )(block_e, n_used, n_valid, hs, w1, w3, w2)


def _sc_gather_rows(table, idx):
    info = plsc.get_sparse_core_info()
    nc, ns = info.num_cores, info.num_subcores
    nw = nc * ns
    n = idx.shape[0]
    d = table.shape[1]
    ch = SC_CHUNK
    per_w = n // nw
    n_ch = per_w // ch
    assert n % nw == 0 and per_w % (2 * ch) == 0
    mesh = plsc.VectorSubcoreMesh(core_axis_name="c", subcore_axis_name="s")

    @functools.partial(
        pl.kernel, mesh=mesh,
        out_type=jax.ShapeDtypeStruct((n, d), table.dtype),
        scratch_types=[pltpu.VMEM((n_ch, ch), jnp.int32),
                       pltpu.VMEM((2, ch, d), table.dtype),
                       pltpu.SemaphoreType.DMA((2,))],
    )
    def gather_kernel(table_hbm, idx_hbm, out_hbm, idx_v, rows_v, sem):
        wid = lax.axis_index("s") * nc + lax.axis_index("c")
        base = wid * per_w
        pltpu.sync_copy(idx_hbm.at[wid], idx_v)

        def gather(j, buf):
            return pltpu.make_async_copy(table_hbm.at[idx_v.at[j]], rows_v.at[buf], sem.at[buf])

        gather(0, 0).start()

        @pl.loop(0, n_ch, step=2)
        def _(j):
            for buf in range(2):
                jj = j + buf
                gather(jj, buf).wait()

                @pl.when(jj + 1 < n_ch)
                def _():
                    gather(jj + 1, 1 - buf).start()

                pltpu.sync_copy(rows_v.at[buf], out_hbm.at[pl.ds(base + jj * ch, ch)])

    return gather_kernel(table, idx.reshape(nw, n_ch, ch))


def _combine_kernel(rows_ref, base_ref, gate_ref, mod_ref, fg_ref, o_ref):
    tm = base_ref.shape[0]
    gate = gate_ref[...]
    acc_lo = jnp.zeros((tm, HALF), F32)
    acc_hi = jnp.zeros((tm, HALF), F32)
    for kk in range(TOP_K):
        lo, hi = _unpack_bf16_pairs(rows_ref[kk])
        acc_lo = acc_lo + gate[:, kk:kk + 1] * lo
        acc_hi = acc_hi + gate[:, kk:kk + 1] * hi
    acc = jnp.concatenate([acc_lo, acc_hi], axis=1)
    g2 = mod_ref[0:1, 5 * D_MODEL:6 * D_MODEL]
    o_ref[...] = _rms(base_ref[...] + g2 * acc, fg_ref[...])


def _combine(dest_km, ye, base, gate, mod, fg):
    t = base.shape[0]
    tm = TOK_TILE
    full = lambda arr: pl.BlockSpec(arr.shape, lambda i: (0,) * arr.ndim)
    rows = _sc_gather_rows(ye, dest_km.reshape(-1)).reshape(TOP_K, t, HALF)
    return pl.pallas_call(
        _combine_kernel,
        grid=(t // tm,),
        in_specs=[pl.BlockSpec((TOP_K, tm, HALF), lambda i: (0, i, 0)),
                  pl.BlockSpec((tm, D_MODEL), lambda i: (i, 0)),
                  pl.BlockSpec((tm, LANES), lambda i: (i, 0)),
                  full(mod), full(fg)],
        out_specs=pl.BlockSpec((tm, D_MODEL), lambda i: (i, 0)),
        out_shape=jax.ShapeDtypeStruct((t, D_MODEL), F32),
        compiler_params=_cparams(("arbitrary",)),
        name="combine",
    )(rows, base, gate, mod, fg)


def _block_tables(counts, n_blocks):
    sb = SLOT_BLOCK
    padded = (counts + sb - 1) // sb * sb
    pad_end = jnp.cumsum(padded)
    pad_start = pad_end - padded
    n_used = pad_end[-1] // sb
    blk = jnp.minimum(jnp.arange(n_blocks, dtype=jnp.int32), n_used - 1)
    block_e = jnp.minimum(jnp.sum(pad_end[None, :] <= (blk * sb)[:, None], axis=1), N_EXPERTS - 1).astype(jnp.int32)
    onehot = block_e[:, None] == jnp.arange(N_EXPERTS, dtype=jnp.int32)[None, :]
    live_end = jnp.sum(jnp.where(onehot, (pad_start + counts)[None, :], 0), axis=1)
    n_valid = jnp.clip(live_end - blk * sb, 0, sb).astype(jnp.int32)
    return block_e, n_used.reshape(1).astype(jnp.int32), n_valid, pad_start


def _blockdiag2(w):
    z = jnp.zeros_like(w[0])
    return jnp.concatenate([jnp.concatenate([w[0], z], axis=1), jnp.concatenate([z, w[1]], axis=1)], axis=0)


def _pad_rows(w, n=8):
    return jnp.concatenate([w, jnp.zeros((n - w.shape[0],) + w.shape[1:], w.dtype)], axis=0)


def kernel(x, c, ctx, c_ctx, ada_w, ada_b, norm1_g, norm2_g, w_in, conv_w, shift_mu, decay_w0, decay_up, iclr_a0, iclr_up, key_xi, key_alpha, bonus_rho, gate_up, gn_w, gn_b, w_out, router_w, router_bias, exp_w1, exp_w3, exp_w2, sh_w1, sh_w3, sh_w2, final_g):
    assert x.shape[0] == 1 and ada_w.shape[0] == 1
    assert ctx.shape[1] == TOK_TILE and x.shape[1] % TOK_TILE == 0
    l = 0
    xs = x[0]
    row = lambda w: w.reshape(1, -1)

    cc = _pad_rows(jnp.stack([c[0], c_ctx], axis=0))
    mod = _ada(cc, ada_w[l], row(ada_b[l]))

    hid = lax.broadcasted_iota(jnp.int32, (D_RWKV, D_RWKV), 0) // HEAD_DIM
    bsum = (hid == hid.T).astype(BF16)

    r4, v4, g, bonus, yconv, lw, kh, kt, a = _proj(
        ctx[0], xs, mod, row(norm1_g[l]), w_in[l].astype(BF16), _pad_rows(conv_w[l]), _pad_rows(shift_mu[l]),
        row(decay_w0[l]), _blockdiag2(decay_up[l]).astype(BF16), row(iclr_a0[l]), _blockdiag2(iclr_up[l]).astype(BF16),
        row(key_xi[l]), row(key_alpha[l]), row(bonus_rho[l]), gate_up[l].astype(BF16), bsum)

    y, w1b, w3b, w2b = _scan(r4, v4, lw, kh, kt, a, exp_w1[l], exp_w3[l], exp_w2[l])

    base, h2, idx, gate, rank, cnt = _mix(
        xs, y, g, bonus, yconv, mod, row(gn_w[l]), row(gn_b[l]), bsum, w_out[l].astype(BF16),
        row(norm2_g[l]), router_w[l], row(router_bias[l]),
        sh_w1[l].astype(BF16), sh_w3[l].astype(BF16), sh_w2[l].astype(BF16))

    n_blocks = xs.shape[0] * TOP_K // SLOT_BLOCK + N_EXPERTS
    block_e, n_used, n_valid, pad_start = _block_tables(cnt[0].astype(jnp.int32), n_blocks)
    dest_km = _slots(idx, rank, pad_start.astype(F32).reshape(1, N_EXPERTS))[:, :TOP_K].T
    hs = _sc_scatter_rows(h2, dest_km, n_blocks * SLOT_BLOCK)
    ye = _experts(block_e, n_used, n_valid, hs, w1b, w3b, w2b)
    out = _combine(dest_km, ye, base, gate, mod, row(final_g))
    return out[None]
```

```python
import functools

import jax
import jax.numpy as jnp
from jax import lax
from jax.experimental import pallas as pl
from jax.experimental.pallas import tpu as pltpu
from jax.experimental.pallas import tpu_sc as plsc

F32 = jnp.float32
BF16 = jnp.bfloat16

D_MODEL = 1024
D_CONV = 512
D_RWKV = 512
HEAD_DIM = 64
N_HEADS = D_RWKV // HEAD_DIM
N_PAIRS = N_HEADS // 2
LORA = 128
P_RWKV = 3 * D_RWKV + 3 * LORA
P_IN = 3 * D_CONV + P_RWKV
GRID_W = 64
N_EXPERTS = 256
TOP_K = 8
D_EXPERT = 256
ROUTED_SCALE = 2.5
RMS_EPS = 1e-6
GN_EPS = 64e-5
DECAY_SCALE = 0.6065306597126334
NORM_EPS = 1e-12

TOK_TILE = 256
SLOTS_TILE = 1024
SCAN_BLOCK = 256
CHUNK = 64
SLOT_BLOCK = 512
EXPERT_WEIGHT_SLOTS = 3
SC_CHUNK = 64
SC_SCATTER_CHUNK = 128
LANES = 128
VMEM_LIMIT = 56 * 1024 * 1024

NN = ((1,), (0,))
NT = ((1,), (1,))
TN = ((0,), (0,))


def _dg(a, b, dims=NN):
    return lax.dot_general(a, b, (dims, ((), ())), preferred_element_type=F32)


def _split2(a):
    hi = a.astype(BF16)
    lo = (a - hi.astype(F32)).astype(BF16)
    return hi, lo


def _mm1(a, b, dims=NN):
    return _dg(a.astype(BF16), b.astype(BF16), dims)


def _mm3(a, b, dims=NN):
    ah, al = _split2(a)
    bh, bl = _split2(b)
    return _dg(ah, bh, dims) + (_dg(ah, bl, dims) + _dg(al, bh, dims))


def _mm_split_lhs(a, b_bf16, dims=NN):
    h, l = _split2(a)
    return _dg(h, b_bf16, dims) + _dg(l, b_bf16, dims)


def _mm_ones_rhs(a, ones_bf16):
    return _dg(a.astype(BF16), ones_bf16)


def _mm_ones_lhs(ones_bf16, b, dims=NN):
    h, l = _split2(b)
    return _dg(ones_bf16, h, dims) + _dg(ones_bf16, l, dims)


HALF = D_MODEL // 2
I32 = jnp.int32


def _pack_bf16_pairs(x):
    return pltpu.pack_elementwise([x[:, 0:HALF], x[:, HALF:]], packed_dtype=BF16)


def _unpack_bf16_pairs(w):
    lo = pltpu.unpack_elementwise(w, index=0, packed_dtype=BF16, unpacked_dtype=F32)
    hi = pltpu.unpack_elementwise(w, index=1, packed_dtype=BF16, unpacked_dtype=F32)
    return lo, hi


def _sigmoid(x):
    return 1.0 / (1.0 + jnp.exp(-x))


def _silu(x):
    return x * _sigmoid(x)


def _rms(xv, g):
    ms = jnp.mean(xv * xv, axis=-1, keepdims=True)
    return xv * lax.rsqrt(ms + RMS_EPS) * g


def _cparams(sem):
    return pltpu.CompilerParams(dimension_semantics=sem, vmem_limit_bytes=VMEM_LIMIT)


def _ada_kernel(c_ref, w_ref, b_ref, o_ref):
    o_ref[...] = _mm3(_silu(c_ref[...]), w_ref[...]) + b_ref[...]


def _ada(cc, ada_w, ada_b):
    n = ada_w.shape[1]
    tn = 1024
    return pl.pallas_call(
        _ada_kernel,
        grid=(n // tn,),
        in_specs=[pl.BlockSpec((8, D_MODEL), lambda j: (0, 0)),
                  pl.BlockSpec((D_MODEL, tn), lambda j: (0, j)),
                  pl.BlockSpec((1, tn), lambda j: (0, j))],
        out_specs=pl.BlockSpec((8, tn), lambda j: (0, j)),
        out_shape=jax.ShapeDtypeStruct((8, n), F32),
        compiler_params=_cparams(("arbitrary",)),
        name="ada",
    )(cc, ada_w, ada_b)


def _proj_kernel(ctx_ref, x_ref, xp_ref, xn_ref, mod_ref, n1g_ref, win_ref, convw_ref, mu_ref,
                 dw0_ref, dup_ref, ia0_ref, iup_ref, xi_ref, kal_ref, rho_ref, gup_ref, bsum_ref,
                 r_ref, v_ref, g_ref, bonus_ref, yconv_ref, lw_ref, kh_ref, kt_ref, a_ref):
    i = pl.program_id(0)
    nt = pl.num_programs(0)
    tm = x_ref.shape[0]
    is_ctx = i == 0
    sh = jnp.where(is_ctx, mod_ref[1:2, 0:D_MODEL], mod_ref[0:1, 0:D_MODEL])
    sc = jnp.where(is_ctx, mod_ref[1:2, D_MODEL:2 * D_MODEL], mod_ref[0:1, D_MODEL:2 * D_MODEL])
    n1g = n1g_ref[...]

    def norm_mod(xv):
        return _rms(xv, n1g) * (1.0 + sc) + sh

    xt = jnp.where(is_ctx, ctx_ref[...], x_ref[...])
    h = norm_mod(jnp.concatenate([xt, xp_ref[...], xn_ref[...]], axis=0)).astype(BF16)
    p_all = _dg(h, win_ref[...])
    p = p_all[0:tm]
    ph = p_all[tm:tm + 16, 3 * D_CONV:]

    rows = lax.broadcasted_iota(jnp.int32, (tm, 1), 0)

    bg = p[:, 0:D_CONV]
    z = p[:, D_CONV:2 * D_CONV] * p[:, 2 * D_CONV:3 * D_CONV]
    col = rows % GRID_W
    zp = jnp.where(col == 0, 0.0, pltpu.roll(z, 1, 0))
    zn = jnp.where(col == GRID_W - 1, 0.0, pltpu.roll(z, tm - 1, 0))
    yconv = bg * (convw_ref[0:1, :] * zp + convw_ref[1:2, :] * z + convw_ref[2:3, :] * zn)
    yconv_ref[...] = yconv.astype(BF16)

    prev_ok = jnp.logical_and(i != 0, i != 1).astype(F32)
    next_ok = jnp.logical_and(i != 0, i != nt - 1).astype(F32)
    cur = p[:, 3 * D_CONV:]
    prev = jnp.where(rows == 0, ph[7:8, :] * prev_ok, pltpu.roll(cur, 1, 0))
    nxt = jnp.where(rows == tm - 1, ph[8:9, :] * next_ok, pltpu.roll(cur, tm - 1, 0))
    ps = cur + mu_ref[0:1, :] * (prev - cur) + mu_ref[1:2, :] * (nxt - cur)

    r = ps[:, 0:D_RWKV]
    k = ps[:, D_RWKV:2 * D_RWKV]
    v = ps[:, 2 * D_RWKV:3 * D_RWKV]
    o = 3 * D_RWKV
    wlo = ps[:, o:o + LORA]
    alo = ps[:, o + LORA:o + 2 * LORA]
    glo = ps[:, o + 2 * LORA:o + 3 * LORA]

    dd = dw0_ref[...] + _mm_split_lhs(jnp.tanh(wlo), dup_ref[...])
    lw = -DECAY_SCALE * _sigmoid(dd)
    a = _sigmoid(ia0_ref[...] + _mm_split_lhs(alo, iup_ref[...]))
    g = _dg(_sigmoid(glo).astype(BF16), gup_ref[...])
    k2 = jnp.concatenate([k, k], axis=1)
    kap = k2 * xi_ref[...]
    kap2 = kap * kap
    bs = bsum_ref[...]
    ss = jnp.concatenate([_mm_ones_rhs(kap2[:, 0:D_RWKV], bs),
                          _mm_ones_rhs(kap2[:, D_RWKV:], bs)], axis=1)
    kh = kap * lax.rsqrt(ss + NORM_EPS)
    kt = k2 * (1.0 + (a - 1.0) * kal_ref[...])
    bon = _mm_ones_rhs(r * rho_ref[...] * (kt[:, 0:D_RWKV] + kt[:, D_RWKV:]), bs)

    g_ref[...] = g
    bonus_ref[...] = bon * v
    for pr in range(N_PAIRS):
        ls = slice(pr * LANES, (pr + 1) * LANES)
        r_ref[pr] = r[:, ls]
        v_ref[pr] = v[:, ls]
        for d in range(2):
            ld = slice(d * D_RWKV + pr * LANES, d * D_RWKV + (pr + 1) * LANES)
            lw_ref[d, pr] = lw[:, ld]
            kh_ref[d, pr] = kh[:, ld]
            kt_ref[d, pr] = kt[:, ld]
            a_ref[d, pr] = a[:, ld]


def _proj(ctx2, x2, mod, n1g, win, convw, mu, dw0, dup, ia0, iup, xi, kal, rho, gup, bsum):
    tm = TOK_TILE
    t = ctx2.shape[0] + x2.shape[0]
    nt = t // tm
    per = tm // 8
    nb8 = x2.shape[0] // 8
    full = lambda arr: pl.BlockSpec(arr.shape, lambda i: (0,) * arr.ndim)
    pair_spec = pl.BlockSpec((N_PAIRS, tm, LANES), lambda i: (0, i, 0))
    dpair_spec = pl.BlockSpec((2, N_PAIRS, tm, LANES), lambda i: (0, 0, i, 0))
    row_spec = pl.BlockSpec((tm, D_RWKV), lambda i: (i, 0))
    pair_shape = jax.ShapeDtypeStruct((N_PAIRS, t, LANES), F32)
    dpair_shape = jax.ShapeDtypeStruct((2, N_PAIRS, t, LANES), F32)
    consts = (mod, n1g, win, convw, mu, dw0, dup, ia0, iup, xi, kal, rho, gup, bsum)
    return pl.pallas_call(
        _proj_kernel,
        grid=(nt,),
        in_specs=[pl.BlockSpec((tm, D_MODEL), lambda i: (0, 0)),
                  pl.BlockSpec((tm, D_MODEL), lambda i: (jnp.maximum(i - 1, 0), 0)),
                  pl.BlockSpec((8, D_MODEL), lambda i: (jnp.maximum((i - 1) * per - 1, 0), 0)),
                  pl.BlockSpec((8, D_MODEL), lambda i: (jnp.clip(i * per, 0, nb8 - 1), 0))]
                 + [full(c) for c in consts],
        out_specs=[pair_spec, pair_spec, row_spec, row_spec, row_spec,
                   dpair_spec, dpair_spec, dpair_spec, dpair_spec],
        out_shape=[pair_shape, pair_shape,
                   jax.ShapeDtypeStruct((t, D_RWKV), F32), jax.ShapeDtypeStruct((t, D_RWKV), F32),
                   jax.ShapeDtypeStruct((t, D_CONV), BF16),
                   dpair_shape, dpair_shape, dpair_shape, dpair_shape],
        compiler_params=_cparams(("arbitrary",)),
        name="proj",
    )(ctx2, x2, x2, x2, *consts)


def _scan_kernel(r_ref, v_ref, lw_ref, kh_ref, kt_ref, a_ref, w1_ref, w3_ref, w2_ref,
                 y_ref, w1b_ref, w3b_ref, w2b_ref, q_ref):
    w1b_ref[...] = w1_ref[...].astype(BF16)
    w3b_ref[...] = w3_ref[...].astype(BF16)
    w2b_ref[...] = w2_ref[...].astype(BF16)

    d = pl.program_id(0)
    rev = d == 1
    c = CHUNK
    npair = r_ref.shape[0]
    nsub = r_ref.shape[1] // c

    row = lax.broadcasted_iota(jnp.int32, (c, c), 0)
    col = lax.broadcasted_iota(jnp.int32, (c, c), 1)
    strict = (col - row) * (1 - 2 * d) < 0
    eye = row == col
    incl = jnp.logical_or(strict, eye)
    incl_bf = jnp.where(incl, 1.0, 0.0).astype(BF16)
    eye_f = jnp.where(eye, 1.0, 0.0)
    head0 = lax.broadcasted_iota(jnp.int32, (c, LANES), 1) < HEAD_DIM
    row2 = lax.broadcasted_iota(jnp.int32, (LANES, LANES), 0)
    col2 = lax.broadcasted_iota(jnp.int32, (LANES, LANES), 1)
    same_head = (row2 < HEAD_DIM) == (col2 < HEAD_DIM)
    eye2 = row2 == col2

    subs = range(npair * nsub)
    units = [(s, h) for s in subs for h in range(2)]
    off = [pl.multiple_of(jnp.where(rev, nsub - 1 - s, s) * c, c) for s in range(nsub)]
    sls = [(pp, pl.ds(off[s], c), slice(None)) for pp in range(npair) for s in range(nsub)]
    v = [v_ref[sl] for sl in sls]
    lw = [lw_ref[sl] for sl in sls]
    cum = [_mm_ones_lhs(incl_bf, lw[s]) for s in subs]
    al, be, kk, rr, bew, kkw, wc = [], [], [], [], [], [], []
    for s in subs:
        cum_last = jnp.where(rev, cum[s][0:1, :], cum[s][c - 1:c, :])
        e_end = jnp.exp(cum_last - cum[s])
        e_neg = jnp.exp(-cum[s])
        kh = kh_ref[sls[s]]
        kt = kt_ref[sls[s]]
        kha = kh * a_ref[sls[s]]
        al.append(kh * jnp.exp(cum[s] - lw[s]))
        be.append(-(kha * e_neg))
        kk.append(kt * e_neg)
        rr.append(r_ref[sls[s]] * jnp.exp(cum[s]))
        bew.append(-(kha * e_end))
        kkw.append(kt * e_end)
        wc.append(jnp.exp(cum_last))

    lhs = {}
    for s, h in units:
        hm = head0 if h == 0 else jnp.logical_not(head0)
        lhs[s, h] = jnp.concatenate([jnp.where(hm, al[s], 0.0), jnp.where(hm, rr[s], 0.0)], axis=0)
    gb = {u: _mm1(lhs[u], be[u[0]], NT) for u in units}
    gk = {u: _mm1(lhs[u], kk[u[0]], NT) for u in units}
    a_ab = {u: jnp.where(strict, gb[u][0:c], 0.0) for u in units}
    a_ak = {u: jnp.where(strict, gk[u][0:c], 0.0) for u in units}
    a_rb = {u: jnp.where(incl, gb[u][c:2 * c], 0.0) for u in units}
    a_rk = {u: jnp.where(incl, gk[u][c:2 * c], 0.0) for u in units}

    pw = dict(a_ab)
    tinv = {u: eye_f + a_ab[u] for u in units}
    akrk_v = {u: _mm1(jnp.concatenate([a_ak[u], a_rk[u]], axis=0), v[u[0]]) for u in units}
    avk = {u: akrk_v[u][0:c] for u in units}
    rk_v = {u: akrk_v[u][c:2 * c] for u in units}
    for _ in range(c.bit_length() - 2):
        pw = {u: _mm1(pw[u], pw[u]) for u in units}
        tinv = {u: tinv[u] + _mm1(tinv[u], pw[u]) for u in units}
    tu = {u: _mm1(tinv[u], jnp.concatenate([al[u[0]], avk[u]], axis=1)) for u in units}
    at = [jnp.where(head0, tu[s, 0][:, 0:LANES], tu[s, 1][:, 0:LANES]) for s in subs]
    ut = [jnp.where(head0, tu[s, 0][:, LANES:], tu[s, 1][:, LANES:]) for s in subs]

    rb = {u: _mm1(a_rb[u], jnp.concatenate([at[u[0]], ut[u[0]]], axis=1)) for u in units}
    rb_at = {u: rb[u][:, 0:LANES] for u in units}
    rb_ut = {u: rb[u][:, LANES:] for u in units}
    md = [_mm1(bew[s], jnp.concatenate([at[s], ut[s]], axis=1), TN) for s in subs]
    m_off = [md[s][:, 0:LANES] for s in subs]
    d_u = [md[s][:, LANES:] for s in subs]
    d_v = [_mm1(kkw[s], v[s], TN) for s in subs]
    rh = [rr[s] + jnp.where(head0, rb_at[s, 0], rb_at[s, 1]) for s in subs]
    yh = [jnp.where(head0, rb_ut[s, 0] + rk_v[s, 0], rb_ut[s, 1] + rk_v[s, 1]) for s in subs]
    mt = [jnp.where(eye2, wc[s], 0.0) + jnp.where(same_head, m_off[s], 0.0) for s in subs]
    dt = [jnp.where(same_head, d_u[s] + d_v[s], 0.0) for s in subs]

    @pl.when(pl.program_id(1) == 0)
    def _():
        q_ref[...] = jnp.zeros(q_ref.shape, F32)

    q = [q_ref[pp] for pp in range(npair)]
    for si in range(nsub):
        ys = [_mm1(rh[pp * nsub + si], q[pp]) for pp in range(npair)]
        qn = [_mm1(mt[pp * nsub + si], q[pp]) for pp in range(npair)]
        for pp in range(npair):
            y_ref[sls[pp * nsub + si]] = ys[pp] + yh[pp * nsub + si]
            q[pp] = qn[pp] + dt[pp * nsub + si]
    for pp in range(npair):
        q_ref[pp] = q[pp]


def _scan(r4, v4, lw, kh, kt, a, w1, w3, w2):
    t = r4.shape[1]
    tb = SCAN_BLOCK
    nb = t // tb
    ne = w1.shape[0]
    epb = -(-ne // (2 * nb))
    assert ne % epb == 0
    last = ne // epb - 1

    def blk(d, j):
        return jnp.where(d == 0, j, jnp.where(j == 0, 0, nb - j))

    def wblk(d, j):
        return (jnp.minimum(d * nb + j, last), 0, 0)

    shared = pl.BlockSpec((N_PAIRS, tb, LANES), lambda d, j: (0, blk(d, j), 0))
    perdir = pl.BlockSpec((None, N_PAIRS, tb, LANES), lambda d, j: (d, 0, blk(d, j), 0))
    w13 = pl.BlockSpec((epb, D_MODEL, D_EXPERT), wblk)
    w2s = pl.BlockSpec((epb, D_EXPERT, D_MODEL), wblk)
    return pl.pallas_call(
        _scan_kernel,
        grid=(2, nb),
        in_specs=[shared, shared, perdir, perdir, perdir, perdir, w13, w13, w2s],
        out_specs=[perdir, w13, w13, w2s],
        out_shape=[jax.ShapeDtypeStruct((2, N_PAIRS, t, LANES), F32),
                   jax.ShapeDtypeStruct(w1.shape, BF16), jax.ShapeDtypeStruct(w3.shape, BF16),
                   jax.ShapeDtypeStruct(w2.shape, BF16)],
        scratch_shapes=[pltpu.VMEM((N_PAIRS, LANES, LANES), F32)],
        compiler_params=_cparams(("arbitrary", "arbitrary")),
        name="scan",
    )(r4, v4, lw, kh, kt, a, w1, w3, w2)


def _mix_kernel(x_ref, yf_ref, yb_ref, g_ref, bonus_ref, yconv_ref, mod_ref, gnw_ref, gnb_ref, bsum_ref,
                wout_ref, n2g_ref, rw_ref, rb_ref, sw1_ref, sw3_ref, sw2_ref,
                base_ref, h2_ref, idx_ref, gate_ref, rank_ref, cnt_ref, run_ref):
    tm = x_ref.shape[0]
    g1 = mod_ref[0:1, 2 * D_MODEL:3 * D_MODEL]
    sh2 = mod_ref[0:1, 3 * D_MODEL:4 * D_MODEL]
    sc2 = mod_ref[0:1, 4 * D_MODEL:5 * D_MODEL]
    g2 = mod_ref[0:1, 5 * D_MODEL:6 * D_MODEL]

    @pl.when(pl.program_id(0) == 0)
    def _():
        run_ref[...] = jnp.zeros(run_ref.shape, F32)

    y = jnp.concatenate([yf_ref[pr] + yb_ref[pr] for pr in range(N_PAIRS)], axis=1)
    bs = bsum_ref[...]
    mu = _mm_ones_rhs(y, bs) * (1.0 / HEAD_DIM)
    yc = y - mu
    var = _mm_ones_rhs(yc * yc, bs) * (1.0 / HEAD_DIM)
    yn = yc * lax.rsqrt(var + GN_EPS) * gnw_ref[...] + gnb_ref[...]
    yrw = ((yn + bonus_ref[...]) * g_ref[...]).astype(BF16)
    mix = _dg(yconv_ref[...], wout_ref[0:D_CONV, :]) + _dg(yrw, wout_ref[D_CONV:, :])
    x1 = x_ref[...] + g1 * mix

    h2 = _rms(x1, n2g_ref[...]) * (1.0 + sc2) + sh2
    hb = h2.astype(BF16)
    h2_ref[...] = _pack_bf16_pairs(h2)

    scores = _sigmoid(_mm3(h2, rw_ref[...]))
    work = scores + rb_ref[...]
    lane_e = lax.broadcasted_iota(jnp.int32, (tm, N_EXPERTS), 1).astype(F32)
    lane_o = lax.broadcasted_iota(jnp.int32, (tm, LANES), 1)
    idx_acc = jnp.zeros((tm, LANES), F32)
    gate_acc = jnp.zeros((tm, LANES), F32)
    gsum = jnp.zeros((tm, 1), F32)
    sels = []
    for kk in range(TOP_K):
        m = jnp.max(work, axis=-1, keepdims=True)
        sel = jnp.min(jnp.where(work == m, lane_e, float(N_EXPERTS)), axis=-1, keepdims=True)
        hit = lane_e == sel
        sk = jnp.sum(jnp.where(hit, scores, 0.0), axis=-1, keepdims=True)
        idx_acc = jnp.where(lane_o == kk, sel, idx_acc)
        gate_acc = jnp.where(lane_o == kk, sk, gate_acc)
        gsum = gsum + sk
        work = jnp.where(hit, -jnp.inf, work)
        sels.append(sel)
    chosen = jnp.where(work == -jnp.inf, 1.0, 0.0)
    idx_ref[...] = idx_acc.astype(jnp.int32)
    gate_ref[...] = gate_acc / gsum * ROUTED_SCALE

    trow = lax.broadcasted_iota(jnp.int32, (tm, tm), 0)
    tcol = lax.broadcasted_iota(jnp.int32, (tm, tm), 1)
    earlier = jnp.where(tcol < trow, 1.0, 0.0).astype(BF16)
    before = _dg(earlier, chosen.astype(BF16)) + run_ref[0:1, :]
    rank_acc = jnp.zeros((tm, LANES), F32)
    for kk in range(TOP_K):
        rk = jnp.sum(jnp.where(lane_e == sels[kk], before, 0.0), axis=-1, keepdims=True)
        rank_acc = jnp.where(lane_o == kk, rk, rank_acc)
    rank_ref[...] = rank_acc.astype(jnp.int32)
    run_ref[...] = run_ref[...] + jnp.sum(chosen, axis=0, keepdims=True)
    cnt_ref[...] = run_ref[...]

    act = (_silu(_dg(hb, sw1_ref[...])) * _dg(hb, sw3_ref[...])).astype(BF16)
    base_ref[...] = x1 + g2 * _dg(act, sw2_ref[...])


def _mix(x, y, g, bonus, yconv, mod, gnw, gnb, bsum, wout, n2g, rw, rb, sw1, sw3, sw2):
    t = x.shape[0]
    tm = TOK_TILE
    off = (y.shape[2] - t) // tm
    full = lambda arr: pl.BlockSpec(arr.shape, lambda i: (0,) * arr.ndim)
    consts = (mod, gnw, gnb, bsum, wout, n2g, rw, rb, sw1, sw3, sw2)
    cat_spec = lambda w: pl.BlockSpec((tm, w), lambda i: (i + off, 0))
    tok_spec = lambda w: pl.BlockSpec((tm, w), lambda i: (i, 0))
    yf_spec = pl.BlockSpec((None, N_PAIRS, tm, LANES), lambda i: (0, 0, i + off, 0))
    yb_spec = pl.BlockSpec((None, N_PAIRS, tm, LANES), lambda i: (1, 0, i + off, 0))
    return pl.pallas_call(
        _mix_kernel,
        grid=(t // tm,),
        in_specs=[tok_spec(D_MODEL), yf_spec, yb_spec,
                  cat_spec(D_RWKV), cat_spec(D_RWKV), cat_spec(D_CONV)]
                 + [full(c) for c in consts],
        out_specs=[tok_spec(D_MODEL), tok_spec(HALF), tok_spec(LANES), tok_spec(LANES), tok_spec(LANES),
                   pl.BlockSpec((8, N_EXPERTS), lambda i: (0, 0))],
        out_shape=[jax.ShapeDtypeStruct((t, D_MODEL), F32), jax.ShapeDtypeStruct((t, HALF), I32),
                   jax.ShapeDtypeStruct((t, LANES), jnp.int32), jax.ShapeDtypeStruct((t, LANES), F32),
                   jax.ShapeDtypeStruct((t, LANES), jnp.int32), jax.ShapeDtypeStruct((8, N_EXPERTS), F32)],
        scratch_shapes=[pltpu.VMEM((8, N_EXPERTS), F32)],
        compiler_params=_cparams(("arbitrary",)),
        name="mix",
    )(x, y, y, g, bonus, yconv, *consts)


def _slots_kernel(idx_ref, rank_ref, start_ref, dest_ref):
    tm = idx_ref.shape[0]
    lane_e = lax.broadcasted_iota(jnp.int32, (tm, N_EXPERTS), 1)
    lane_o = lax.broadcasted_iota(jnp.int32, (tm, LANES), 1)
    idx = idx_ref[...]
    start = start_ref[...]
    acc = jnp.zeros((tm, LANES), F32)
    for kk in range(TOP_K):
        st = jnp.sum(jnp.where(lane_e == idx[:, kk:kk + 1], start, 0.0), axis=-1, keepdims=True)
        acc = jnp.where(lane_o == kk, st, acc)
    dest_ref[...] = acc.astype(jnp.int32) + rank_ref[...]


def _slots(idx, rank, pad_start):
    t = idx.shape[0]
    tm = SLOTS_TILE
    spec = pl.BlockSpec((tm, LANES), lambda i: (i, 0))
    return pl.pallas_call(
        _slots_kernel,
        grid=(t // tm,),
        in_specs=[spec, spec, pl.BlockSpec((1, N_EXPERTS), lambda i: (0, 0))],
        out_specs=spec,
        out_shape=jax.ShapeDtypeStruct((t, LANES), jnp.int32),
        compiler_params=_cparams(("arbitrary",)),
        name="slots",
    )(idx, rank, pad_start)


def _sc_scatter_rows(rows, dest_km, n_slots):
    info = plsc.get_sparse_core_info()
    nc, ns = info.num_cores, info.num_subcores
    nw = nc * ns
    t, d = rows.shape
    nk = dest_km.shape[0]
    ch = SC_SCATTER_CHUNK
    per_w = t // nw
    n_ch = per_w // ch
    assert t % nw == 0 and per_w % ch == 0
    mesh = plsc.VectorSubcoreMesh(core_axis_name="c", subcore_axis_name="s")
    idx = dest_km.reshape(nk, nw, n_ch, ch).transpose(1, 2, 0, 3).reshape(nw, n_ch * nk, ch)

    @functools.partial(
        pl.kernel, mesh=mesh,
        out_type=jax.ShapeDtypeStruct((n_slots, d), rows.dtype),
        scratch_types=[pltpu.VMEM((n_ch * nk, ch), jnp.int32),
                       pltpu.VMEM((ch, d), rows.dtype),
                       pltpu.SemaphoreType.DMA],
    )
    def scatter_kernel(rows_hbm, idx_hbm, out_hbm, idx_v, rows_v, sem):
        wid = lax.axis_index("s") * nc + lax.axis_index("c")
        base = wid * per_w
        pltpu.sync_copy(idx_hbm.at[wid], idx_v)

        @pl.loop(0, n_ch)
        def _(j):
            pltpu.sync_copy(rows_hbm.at[pl.ds(base + j * ch, ch)], rows_v)
            copies = [pltpu.make_async_copy(rows_v, out_hbm.at[idx_v.at[j * nk + kk]], sem) for kk in range(nk)]
            for cp in copies:
                cp.start()
            for cp in copies:
                cp.wait()

    return scatter_kernel(rows, idx)


def _experts_kernel(nu_ref, nv_ref, first_ref, run_ref, rune_ref, nr_ref, hs_ref, w1_hbm, w3_hbm, w2_hbm, o_ref,
                    w1s, w3s, w2s, sem):
    b = pl.program_id(0)
    nbuf = w1s.shape[0]
    n_runs = nr_ref[0]

    def copies(r):
        e = rune_ref[r]
        s = r % nbuf
        return (pltpu.make_async_copy(w1_hbm.at[e], w1s.at[s], sem.at[0, s]),
                pltpu.make_async_copy(w3_hbm.at[e], w3s.at[s], sem.at[1, s]),
                pltpu.make_async_copy(w2_hbm.at[e], w2s.at[s], sem.at[2, s]))

    def start(r):
        @pl.when(r < n_runs)
        def _():
            for cp in copies(r):
                cp.start()

    @pl.when(b == 0)
    def _():
        for r in range(nbuf - 1):
            start(r)

    r = run_ref[b]

    @pl.when(first_ref[b] == 1)
    def _():
        start(r + nbuf - 1)
        for cp in copies(r):
            cp.wait()

    @pl.when(b < nu_ref[0])
    def _():
        s = r % nbuf
        live = lax.broadcasted_iota(jnp.int32, (hs_ref.shape[0], 1), 0) < nv_ref[b]
        lo, hi = _unpack_bf16_pairs(jnp.where(live, hs_ref[...], 0))
        hb = jnp.concatenate([lo.astype(BF16), hi.astype(BF16)], axis=1)
        act = (_silu(_dg(hb, w1s[s])) * _dg(hb, w3s[s])).astype(BF16)
        o_ref[...] = _pack_bf16_pairs(_dg(act, w2s[s]))

    @pl.when(b >= nu_ref[0])
    def _():
        o_ref[...] = jnp.zeros(o_ref.shape, I32)


def _experts(block_e, n_used, n_valid, hs, w1, w3, w2):
    nblk = block_e.shape[0]
    sb = SLOT_BLOCK
    nbuf = EXPERT_WEIGHT_SLOTS
    blk = jnp.arange(nblk, dtype=jnp.int32)
    prev_e = jnp.concatenate([jnp.full((1,), -1, jnp.int32), block_e[:-1]])
    first = jnp.logical_and(block_e != prev_e, blk < n_used[0]).astype(jnp.int32)
    run_id = jnp.cumsum(first) - 1
    n_runs = jnp.sum(first).reshape(1).astype(jnp.int32)
    run_e = jnp.sum(jnp.where(jnp.logical_and(first[None, :] == 1, run_id[None, :] == blk[:, None]),
                              block_e[None, :], 0), axis=1).astype(jnp.int32)
    row_map = lambda b, nu, *_: (jnp.minimum(b, nu[0] - 1), 0)
    grid_spec = pltpu.PrefetchScalarGridSpec(
        num_scalar_prefetch=6,
        grid=(nblk,),
        in_specs=[pl.BlockSpec((sb, HALF), row_map),
                  pl.BlockSpec(memory_space=pl.ANY),
                  pl.BlockSpec(memory_space=pl.ANY),
                  pl.BlockSpec(memory_space=pl.ANY)],
        out_specs=pl.BlockSpec((sb, HALF), lambda b, *_: (b, 0)),
        scratch_shapes=[pltpu.VMEM((nbuf, D_MODEL, D_EXPERT), BF16),
                        pltpu.VMEM((nbuf, D_MODEL, D_EXPERT), BF16),
                        pltpu.VMEM((nbuf, D_EXPERT, D_MODEL), BF16),
                        pltpu.SemaphoreType.DMA((3, nbuf))],
    )
    return pl.pallas_call(
        _experts_kernel,
        grid_spec=grid_spec,
        out_shape=jax.ShapeDtypeStruct((nblk * sb, HALF), I32),
        compiler_params=_cparams(("arbitrary",)),
        name="experts",
    )(n_used, n_valid, first, run_id.astype(jnp.int32), run_e, n_runs, hs, w1, w3, w2)


def _sc_gather_rows(table, idx):
    info = plsc.get_sparse_core_info()
    nc, ns = info.num_cores, info.num_subcores
    nw = nc * ns
    n = idx.shape[0]
    d = table.shape[1]
    ch = SC_CHUNK
    per_w = n // nw
    n_ch = per_w // ch
    assert n % nw == 0 and per_w % (2 * ch) == 0
    mesh = plsc.VectorSubcoreMesh(core_axis_name="c", subcore_axis_name="s")

    @functools.partial(
        pl.kernel, mesh=mesh,
        out_type=jax.ShapeDtypeStruct((n, d), table.dtype),
        scratch_types=[pltpu.VMEM((n_ch, ch), jnp.int32),
                       pltpu.VMEM((2, ch, d), table.dtype),
                       pltpu.SemaphoreType.DMA((2,))],
    )
    def gather_kernel(table_hbm, idx_hbm, out_hbm, idx_v, rows_v, sem):
        wid = lax.axis_index("s") * nc + lax.axis_index("c")
        base = wid * per_w
        pltpu.sync_copy(idx_hbm.at[wid], idx_v)

        def gather(j, buf):
            return pltpu.make_async_copy(table_hbm.at[idx_v.at[j]], rows_v.at[buf], sem.at[buf])

        gather(0, 0).start()

        @pl.loop(0, n_ch, step=2)
        def _(j):
            for buf in range(2):
                jj = j + buf
                gather(jj, buf).wait()

                @pl.when(jj + 1 < n_ch)
                def _():
                    gather(jj + 1, 1 - buf).start()

                pltpu.sync_copy(rows_v.at[buf], out_hbm.at[pl.ds(base + jj * ch, ch)])

    return gather_kernel(table, idx.reshape(nw, n_ch, ch))


def _combine_kernel(rows_ref, base_ref, gate_ref, mod_ref, fg_ref, o_ref):
    tm = base_ref.shape[0]
    gate = gate_ref[...]
    acc_lo = jnp.zeros((tm, HALF), F32)
    acc_hi = jnp.zeros((tm, HALF), F32)
    for kk in range(TOP_K):
        lo, hi = _unpack_bf16_pairs(rows_ref[kk])
        acc_lo = acc_lo + gate[:, kk:kk + 1] * lo
        acc_hi = acc_hi + gate[:, kk:kk + 1] * hi
    acc = jnp.concatenate([acc_lo, acc_hi], axis=1)
    g2 = mod_ref[0:1, 5 * D_MODEL:6 * D_MODEL]
    o_ref[...] = _rms(base_ref[...] + g2 * acc, fg_ref[...])


def _combine(dest_km, ye, base, gate, mod, fg):
    t = base.shape[0]
    tm = TOK_TILE
    full = lambda arr: pl.BlockSpec(arr.shape, lambda i: (0,) * arr.ndim)
    rows = _sc_gather_rows(ye, dest_km.reshape(-1)).reshape(TOP_K, t, HALF)
    return pl.pallas_call(
        _combine_kernel,
        grid=(t // tm,),
        in_specs=[pl.BlockSpec((TOP_K, tm, HALF), lambda i: (0, i, 0)),
                  pl.BlockSpec((tm, D_MODEL), lambda i: (i, 0)),
                  pl.BlockSpec((tm, LANES), lambda i: (i, 0)),
                  full(mod), full(fg)],
        out_specs=pl.BlockSpec((tm, D_MODEL), lambda i: (i, 0)),
        out_shape=jax.ShapeDtypeStruct((t, D_MODEL), F32),
        compiler_params=_cparams(("arbitrary",)),
        name="combine",
    )(rows, base, gate, mod, fg)


def _block_tables(counts, n_blocks):
    sb = SLOT_BLOCK
    padded = (counts + sb - 1) // sb * sb
    pad_end = jnp.cumsum(padded)
    pad_start = pad_end - padded
    n_used = pad_end[-1] // sb
    blk = jnp.minimum(jnp.arange(n_blocks, dtype=jnp.int32), n_used - 1)
    block_e = jnp.minimum(jnp.sum(pad_end[None, :] <= (blk * sb)[:, None], axis=1), N_EXPERTS - 1).astype(jnp.int32)
    onehot = block_e[:, None] == jnp.arange(N_EXPERTS, dtype=jnp.int32)[None, :]
    live_end = jnp.sum(jnp.where(onehot, (pad_start + counts)[None, :], 0), axis=1)
    n_valid = jnp.clip(live_end - blk * sb, 0, sb).astype(jnp.int32)
    return block_e, n_used.reshape(1).astype(jnp.int32), n_valid, pad_start


def _blockdiag2(w):
    z = jnp.zeros_like(w[0])
    return jnp.concatenate([jnp.concatenate([w[0], z], axis=1), jnp.concatenate([z, w[1]], axis=1)], axis=0)


def _pad_rows(w, n=8):
    return jnp.concatenate([w, jnp.zeros((n - w.shape[0],) + w.shape[1:], w.dtype)], axis=0)


def kernel(x, c, ctx, c_ctx, ada_w, ada_b, norm1_g, norm2_g, w_in, conv_w, shift_mu, decay_w0, decay_up, iclr_a0, iclr_up, key_xi, key_alpha, bonus_rho, gate_up, gn_w, gn_b, w_out, router_w, router_bias, exp_w1, exp_w3, exp_w2, sh_w1, sh_w3, sh_w2, final_g):
    assert x.shape[0] == 1 and ada_w.shape[0] == 1
    assert ctx.shape[1] == TOK_TILE and x.shape[1] % TOK_TILE == 0
    l = 0
    xs = x[0]
    row = lambda w: w.reshape(1, -1)

    cc = _pad_rows(jnp.stack([c[0], c_ctx], axis=0))
    mod = _ada(cc, ada_w[l], row(ada_b[l]))

    hid = lax.broadcasted_iota(jnp.int32, (D_RWKV, D_RWKV), 0) // HEAD_DIM
    bsum = (hid == hid.T).astype(BF16)

    r4, v4, g, bonus, yconv, lw, kh, kt, a = _proj(
        ctx[0], xs, mod, row(norm1_g[l]), w_in[l].astype(BF16), _pad_rows(conv_w[l]), _pad_rows(shift_mu[l]),
        row(decay_w0[l]), _blockdiag2(decay_up[l]).astype(BF16), row(iclr_a0[l]), _blockdiag2(iclr_up[l]).astype(BF16),
        row(key_xi[l]), row(key_alpha[l]), row(bonus_rho[l]), gate_up[l].astype(BF16), bsum)

    y, w1b, w3b, w2b = _scan(r4, v4, lw, kh, kt, a, exp_w1[l], exp_w3[l], exp_w2[l])

    base, h2, idx, gate, rank, cnt = _mix(
        xs, y, g, bonus, yconv, mod, row(gn_w[l]), row(gn_b[l]), bsum, w_out[l].astype(BF16),
        row(norm2_g[l]), router_w[l], row(router_bias[l]),
        sh_w1[l].astype(BF16), sh_w3[l].astype(BF16), sh_w2[l].astype(BF16))

    n_blocks = xs.shape[0] * TOP_K // SLOT_BLOCK + N_EXPERTS
    block_e, n_used, n_valid, pad_start = _block_tables(cnt[0].astype(jnp.int32), n_blocks)
    dest_km = _slots(idx, rank, pad_start.astype(F32).reshape(1, N_EXPERTS))[:, :TOP_K].T
    hs = _sc_scatter_rows(h2, dest_km, n_blocks * SLOT_BLOCK)
    ye = _experts(block_e, n_used, n_valid, hs, w1b, w3b, w2b)
    out = _combine(dest_km, ye, base, gate, mod, row(final_g))
    return out[None]
```

```python
import functools

import jax
import jax.numpy as jnp
from jax import lax
from jax.experimental import pallas as pl
from jax.experimental.pallas import tpu as pltpu
from jax.experimental.pallas import tpu_sc as plsc

F32 = jnp.float32
BF16 = jnp.bfloat16

D_MODEL = 1024
D_CONV = 512
D_RWKV = 512
HEAD_DIM = 64
N_HEADS = D_RWKV // HEAD_DIM
N_PAIRS = N_HEADS // 2
LORA = 128
P_RWKV = 3 * D_RWKV + 3 * LORA
P_IN = 3 * D_CONV + P_RWKV
GRID_W = 64
N_EXPERTS = 256
TOP_K = 8
D_EXPERT = 256
ROUTED_SCALE = 2.5
RMS_EPS = 1e-6
GN_EPS = 64e-5
DECAY_SCALE = 0.6065306597126334
NORM_EPS = 1e-12

TOK_TILE = 256
SLOTS_TILE = 1024
SCAN_BLOCK = 256
CHUNK = 64
SLOT_BLOCK = 512
EXPERT_WEIGHT_SLOTS = 3
SC_CHUNK = 64
SC_SCATTER_CHUNK = 128
LANES = 128
VMEM_LIMIT = 56 * 1024 * 1024

NN = ((1,), (0,))
NT = ((1,), (1,))
TN = ((0,), (0,))


def _dg(a, b, dims=NN):
    return lax.dot_general(a, b, (dims, ((), ())), preferred_element_type=F32)


def _split2(a):
    hi = a.astype(BF16)
    lo = (a - hi.astype(F32)).astype(BF16)
    return hi, lo


def _mm1(a, b, dims=NN):
    return _dg(a.astype(BF16), b.astype(BF16), dims)


def _mm3(a, b, dims=NN):
    ah, al = _split2(a)
    bh, bl = _split2(b)
    return _dg(ah, bh, dims) + (_dg(ah, bl, dims) + _dg(al, bh, dims))


def _mm_split_lhs(a, b_bf16, dims=NN):
    h, l = _split2(a)
    return _dg(h, b_bf16, dims) + _dg(l, b_bf16, dims)


def _mm_ones_rhs(a, ones_bf16):
    return _dg(a.astype(BF16), ones_bf16)


def _mm_ones_lhs(ones_bf16, b, dims=NN):
    h, l = _split2(b)
    return _dg(ones_bf16, h, dims) + _dg(ones_bf16, l, dims)


HALF = D_MODEL // 2
I32 = jnp.int32


def _pack_bf16_pairs(x):
    return pltpu.pack_elementwise([x[:, 0:HALF], x[:, HALF:]], packed_dtype=BF16)


def _unpack_bf16_pairs(w):
    lo = pltpu.unpack_elementwise(w, index=0, packed_dtype=BF16, unpacked_dtype=F32)
    hi = pltpu.unpack_elementwise(w, index=1, packed_dtype=BF16, unpacked_dtype=F32)
    return lo, hi


def _sigmoid(x):
    return 1.0 / (1.0 + jnp.exp(-x))


def _silu(x):
    return x * _sigmoid(x)


def _rms(xv, g):
    ms = jnp.mean(xv * xv, axis=-1, keepdims=True)
    return xv * lax.rsqrt(ms + RMS_EPS) * g


def _cparams(sem):
    return pltpu.CompilerParams(dimension_semantics=sem, vmem_limit_bytes=VMEM_LIMIT)


def _ada_kernel(c_ref, w_ref, b_ref, o_ref):
    o_ref[...] = _mm3(_silu(c_ref[...]), w_ref[...]) + b_ref[...]


def _ada(cc, ada_w, ada_b):
    n = ada_w.shape[1]
    tn = 1024
    return pl.pallas_call(
        _ada_kernel,
        grid=(n // tn,),
        in_specs=[pl.BlockSpec((8, D_MODEL), lambda j: (0, 0)),
                  pl.BlockSpec((D_MODEL, tn), lambda j: (0, j)),
                  pl.BlockSpec((1, tn), lambda j: (0, j))],
        out_specs=pl.BlockSpec((8, tn), lambda j: (0, j)),
        out_shape=jax.ShapeDtypeStruct((8, n), F32),
        compiler_params=_cparams(("arbitrary",)),
        name="ada",
    )(cc, ada_w, ada_b)


def _proj_kernel(ctx_ref, x_ref, xp_ref, xn_ref, mod_ref, n1g_ref, win_ref, convw_ref, mu_ref,
                 dw0_ref, dup_ref, ia0_ref, iup_ref, xi_ref, kal_ref, rho_ref, gup_ref, bsum_ref,
                 r_ref, v_ref, g_ref, bonus_ref, yconv_ref, lw_ref, kh_ref, kt_ref, a_ref):
    i = pl.program_id(0)
    nt = pl.num_programs(0)
    tm = x_ref.shape[0]
    is_ctx = i == 0
    sh = jnp.where(is_ctx, mod_ref[1:2, 0:D_MODEL], mod_ref[0:1, 0:D_MODEL])
    sc = jnp.where(is_ctx, mod_ref[1:2, D_MODEL:2 * D_MODEL], mod_ref[0:1, D_MODEL:2 * D_MODEL])
    n1g = n1g_ref[...]

    def norm_mod(xv):
        return _rms(xv, n1g) * (1.0 + sc) + sh

    xt = jnp.where(is_ctx, ctx_ref[...], x_ref[...])
    h = norm_mod(jnp.concatenate([xt, xp_ref[...], xn_ref[...]], axis=0)).astype(BF16)
    p_all = _dg(h, win_ref[...])
    p = p_all[0:tm]
    ph = p_all[tm:tm + 16, 3 * D_CONV:]

    rows = lax.broadcasted_iota(jnp.int32, (tm, 1), 0)

    bg = p[:, 0:D_CONV]
    z = p[:, D_CONV:2 * D_CONV] * p[:, 2 * D_CONV:3 * D_CONV]
    col = rows % GRID_W
    zp = jnp.where(col == 0, 0.0, pltpu.roll(z, 1, 0))
    zn = jnp.where(col == GRID_W - 1, 0.0, pltpu.roll(z, tm - 1, 0))
    yconv = bg * (convw_ref[0:1, :] * zp + convw_ref[1:2, :] * z + convw_ref[2:3, :] * zn)
    yconv_ref[...] = yconv.astype(BF16)

    prev_ok = jnp.logical_and(i != 0, i != 1).astype(F32)
    next_ok = jnp.logical_and(i != 0, i != nt - 1).astype(F32)
    cur = p[:, 3 * D_CONV:]
    prev = jnp.where(rows == 0, ph[7:8, :] * prev_ok, pltpu.roll(cur, 1, 0))
    nxt = jnp.where(rows == tm - 1, ph[8:9, :] * next_ok, pltpu.roll(cur, tm - 1, 0))
    ps = cur + mu_ref[0:1, :] * (prev - cur) + mu_ref[1:2, :] * (nxt - cur)

    r = ps[:, 0:D_RWKV]
    k = ps[:, D_RWKV:2 * D_RWKV]
    v = ps[:, 2 * D_RWKV:3 * D_RWKV]
    o = 3 * D_RWKV
    wlo = ps[:, o:o + LORA]
    alo = ps[:, o + LORA:o + 2 * LORA]
    glo = ps[:, o + 2 * LORA:o + 3 * LORA]

    dd = dw0_ref[...] + _mm_split_lhs(jnp.tanh(wlo), dup_ref[...])
    lw = -DECAY_SCALE * _sigmoid(dd)
    a = _sigmoid(ia0_ref[...] + _mm_split_lhs(alo, iup_ref[...]))
    g = _dg(_sigmoid(glo).astype(BF16), gup_ref[...])
    k2 = jnp.concatenate([k, k], axis=1)
    kap = k2 * xi_ref[...]
    kap2 = kap * kap
    bs = bsum_ref[...]
    ss = jnp.concatenate([_mm_ones_rhs(kap2[:, 0:D_RWKV], bs),
                          _mm_ones_rhs(kap2[:, D_RWKV:], bs)], axis=1)
    kh = kap * lax.rsqrt(ss + NORM_EPS)
    kt = k2 * (1.0 + (a - 1.0) * kal_ref[...])
    bon = _mm_ones_rhs(r * rho_ref[...] * (kt[:, 0:D_RWKV] + kt[:, D_RWKV:]), bs)

    g_ref[...] = g
    bonus_ref[...] = bon * v
    for pr in range(N_PAIRS):
        ls = slice(pr * LANES, (pr + 1) * LANES)
        r_ref[pr] = r[:, ls]
        v_ref[pr] = v[:, ls]
        for d in range(2):
            ld = slice(d * D_RWKV + pr * LANES, d * D_RWKV + (pr + 1) * LANES)
            lw_ref[d, pr] = lw[:, ld]
            kh_ref[d, pr] = kh[:, ld]
            kt_ref[d, pr] = kt[:, ld]
            a_ref[d, pr] = a[:, ld]


def _proj(ctx2, x2, mod, n1g, win, convw, mu, dw0, dup, ia0, iup, xi, kal, rho, gup, bsum):
    tm = TOK_TILE
    t = ctx2.shape[0] + x2.shape[0]
    nt = t // tm
    per = tm // 8
    nb8 = x2.shape[0] // 8
    full = lambda arr: pl.BlockSpec(arr.shape, lambda i: (0,) * arr.ndim)
    pair_spec = pl.BlockSpec((N_PAIRS, tm, LANES), lambda i: (0, i, 0))
    dpair_spec = pl.BlockSpec((2, N_PAIRS, tm, LANES), lambda i: (0, 0, i, 0))
    row_spec = pl.BlockSpec((tm, D_RWKV), lambda i: (i, 0))
    pair_shape = jax.ShapeDtypeStruct((N_PAIRS, t, LANES), F32)
    dpair_shape = jax.ShapeDtypeStruct((2, N_PAIRS, t, LANES), F32)
    consts = (mod, n1g, win, convw, mu, dw0, dup, ia0, iup, xi, kal, rho, gup, bsum)
    return pl.pallas_call(
        _proj_kernel,
        grid=(nt,),
        in_specs=[pl.BlockSpec((tm, D_MODEL), lambda i: (0, 0)),
                  pl.BlockSpec((tm, D_MODEL), lambda i: (jnp.maximum(i - 1, 0), 0)),
                  pl.BlockSpec((8, D_MODEL), lambda i: (jnp.maximum((i - 1) * per - 1, 0), 0)),
                  pl.BlockSpec((8, D_MODEL), lambda i: (jnp.clip(i * per, 0, nb8 - 1), 0))]
                 + [full(c) for c in consts],
        out_specs=[pair_spec, pair_spec, row_spec, row_spec, row_spec,
                   dpair_spec, dpair_spec, dpair_spec, dpair_spec],
        out_shape=[pair_shape, pair_shape,
                   jax.ShapeDtypeStruct((t, D_RWKV), F32), jax.ShapeDtypeStruct((t, D_RWKV), F32),
                   jax.ShapeDtypeStruct((t, D_CONV), BF16),
                   dpair_shape, dpair_shape, dpair_shape, dpair_shape],
        compiler_params=_cparams(("arbitrary",)),
        name="proj",
    )(ctx2, x2, x2, x2, *consts)


def _scan_kernel(r_ref, v_ref, lw_ref, kh_ref, kt_ref, a_ref, w1_ref, w3_ref, w2_ref,
                 y_ref, w1b_ref, w3b_ref, w2b_ref, q_ref):
    w1b_ref[...] = w1_ref[...].astype(BF16)
    w3b_ref[...] = w3_ref[...].astype(BF16)
    w2b_ref[...] = w2_ref[...].astype(BF16)

    d = pl.program_id(0)
    rev = d == 1
    c = CHUNK
    npair = r_ref.shape[0]
    nsub = r_ref.shape[1] // c

    row = lax.broadcasted_iota(jnp.int32, (c, c), 0)
    col = lax.broadcasted_iota(jnp.int32, (c, c), 1)
    strict = (col - row) * (1 - 2 * d) < 0
    eye = row == col
    incl = jnp.logical_or(strict, eye)
    incl_bf = jnp.where(incl, 1.0, 0.0).astype(BF16)
    eye_f = jnp.where(eye, 1.0, 0.0)
    head0 = lax.broadcasted_iota(jnp.int32, (c, LANES), 1) < HEAD_DIM
    row2 = lax.broadcasted_iota(jnp.int32, (LANES, LANES), 0)
    col2 = lax.broadcasted_iota(jnp.int32, (LANES, LANES), 1)
    same_head = (row2 < HEAD_DIM) == (col2 < HEAD_DIM)
    eye2 = row2 == col2

    subs = range(npair * nsub)
    units = [(s, h) for s in subs for h in range(2)]
    off = [pl.multiple_of(jnp.where(rev, nsub - 1 - s, s) * c, c) for s in range(nsub)]
    sls = [(pp, pl.ds(off[s], c), slice(None)) for pp in range(npair) for s in range(nsub)]
    v = [v_ref[sl] for sl in sls]
    lw = [lw_ref[sl] for sl in sls]
    cum = [_mm_ones_lhs(incl_bf, lw[s]) for s in subs]
    al, be, kk, rr, bew, kkw, wc = [], [], [], [], [], [], []
    for s in subs:
        cum_last = jnp.where(rev, cum[s][0:1, :], cum[s][c - 1:c, :])
        e_end = jnp.exp(cum_last - cum[s])
        e_neg = jnp.exp(-cum[s])
        kh = kh_ref[sls[s]]
        kt = kt_ref[sls[s]]
        kha = kh * a_ref[sls[s]]
        al.append(kh * jnp.exp(cum[s] - lw[s]))
        be.append(-(kha * e_neg))
        kk.append(kt * e_neg)
        rr.append(r_ref[sls[s]] * jnp.exp(cum[s]))
        bew.append(-(kha * e_end))
        kkw.append(kt * e_end)
        wc.append(jnp.exp(cum_last))

    lhs = {}
    for s, h in units:
        hm = head0 if h == 0 else jnp.logical_not(head0)
        lhs[s, h] = jnp.concatenate([jnp.where(hm, al[s], 0.0), jnp.where(hm, rr[s], 0.0)], axis=0)
    gb = {u: _mm1(lhs[u], be[u[0]], NT) for u in units}
    gk = {u: _mm1(lhs[u], kk[u[0]], NT) for u in units}
    a_ab = {u: jnp.where(strict, gb[u][0:c], 0.0) for u in units}
    a_ak = {u: jnp.where(strict, gk[u][0:c], 0.0) for u in units}
    a_rb = {u: jnp.where(incl, gb[u][c:2 * c], 0.0) for u in units}
    a_rk = {u: jnp.where(incl, gk[u][c:2 * c], 0.0) for u in units}

    pw = dict(a_ab)
    tinv = {u: eye_f + a_ab[u] for u in units}
    akrk_v = {u: _mm1(jnp.concatenate([a_ak[u], a_rk[u]], axis=0), v[u[0]]) for u in units}
    avk = {u: akrk_v[u][0:c] for u in units}
    rk_v = {u: akrk_v[u][c:2 * c] for u in units}
    for _ in range(c.bit_length() - 2):
        pw = {u: _mm1(pw[u], pw[u]) for u in units}
        tinv = {u: tinv[u] + _mm1(tinv[u], pw[u]) for u in units}
    tu = {u: _mm1(tinv[u], jnp.concatenate([al[u[0]], avk[u]], axis=1)) for u in units}
    at = [jnp.where(head0, tu[s, 0][:, 0:LANES], tu[s, 1][:, 0:LANES]) for s in subs]
    ut = [jnp.where(head0, tu[s, 0][:, LANES:], tu[s, 1][:, LANES:]) for s in subs]

    rb = {u: _mm1(a_rb[u], jnp.concatenate([at[u[0]], ut[u[0]]], axis=1)) for u in units}
    rb_at = {u: rb[u][:, 0:LANES] for u in units}
    rb_ut = {u: rb[u][:, LANES:] for u in units}
    md = [_mm1(bew[s], jnp.concatenate([at[s], ut[s]], axis=1), TN) for s in subs]
    m_off = [md[s][:, 0:LANES] for s in subs]
    d_u = [md[s][:, LANES:] for s in subs]
    d_v = [_mm1(kkw[s], v[s], TN) for s in subs]
    rh = [rr[s] + jnp.where(head0, rb_at[s, 0], rb_at[s, 1]) for s in subs]
    yh = [jnp.where(head0, rb_ut[s, 0] + rk_v[s, 0], rb_ut[s, 1] + rk_v[s, 1]) for s in subs]
    mt = [jnp.where(eye2, wc[s], 0.0) + jnp.where(same_head, m_off[s], 0.0) for s in subs]
    dt = [jnp.where(same_head, d_u[s] + d_v[s], 0.0) for s in subs]

    @pl.when(pl.program_id(1) == 0)
    def _():
        q_ref[...] = jnp.zeros(q_ref.shape, F32)

    q = [q_ref[pp] for pp in range(npair)]
    for si in range(nsub):
        ys = [_mm1(rh[pp * nsub + si], q[pp]) for pp in range(npair)]
        qn = [_mm1(mt[pp * nsub + si], q[pp]) for pp in range(npair)]
        for pp in range(npair):
            y_ref[sls[pp * nsub + si]] = ys[pp] + yh[pp * nsub + si]
            q[pp] = qn[pp] + dt[pp * nsub + si]
    for pp in range(npair):
        q_ref[pp] = q[pp]


def _scan(r4, v4, lw, kh, kt, a, w1, w3, w2):
    t = r4.shape[1]
    tb = SCAN_BLOCK
    nb = t // tb
    ne = w1.shape[0]
    epb = -(-ne // (2 * nb))
    assert ne % epb == 0
    last = ne // epb - 1

    def blk(d, j):
        return jnp.where(d == 0, j, jnp.where(j == 0, 0, nb - j))

    def wblk(d, j):
        return (jnp.minimum(d * nb + j, last), 0, 0)

    shared = pl.BlockSpec((N_PAIRS, tb, LANES), lambda d, j: (0, blk(d, j), 0))
    perdir = pl.BlockSpec((None, N_PAIRS, tb, LANES), lambda d, j: (d, 0, blk(d, j), 0))
    w13 = pl.BlockSpec((epb, D_MODEL, D_EXPERT), wblk)
    w2s = pl.BlockSpec((epb, D_EXPERT, D_MODEL), wblk)
    return pl.pallas_call(
        _scan_kernel,
        grid=(2, nb),
        in_specs=[shared, shared, perdir, perdir, perdir, perdir, w13, w13, w2s],
        out_specs=[perdir, w13, w13, w2s],
        out_shape=[jax.ShapeDtypeStruct((2, N_PAIRS, t, LANES), F32),
                   jax.ShapeDtypeStruct(w1.shape, BF16), jax.ShapeDtypeStruct(w3.shape, BF16),
                   jax.ShapeDtypeStruct(w2.shape, BF16)],
        scratch_shapes=[pltpu.VMEM((N_PAIRS, LANES, LANES), F32)],
        compiler_params=_cparams(("arbitrary", "arbitrary")),
        name="scan",
    )(r4, v4, lw, kh, kt, a, w1, w3, w2)


def _mix_kernel(x_ref, yf_ref, yb_ref, g_ref, bonus_ref, yconv_ref, mod_ref, gnw_ref, gnb_ref, bsum_ref,
                wout_ref, n2g_ref, rw_ref, rb_ref, sw1_ref, sw3_ref, sw2_ref,
                base_ref, h2_ref, idx_ref, gate_ref, rank_ref, cnt_ref, run_ref):
    tm = x_ref.shape[0]
    g1 = mod_ref[0:1, 2 * D_MODEL:3 * D_MODEL]
    sh2 = mod_ref[0:1, 3 * D_MODEL:4 * D_MODEL]
    sc2 = mod_ref[0:1, 4 * D_MODEL:5 * D_MODEL]
    g2 = mod_ref[0:1, 5 * D_MODEL:6 * D_MODEL]

    @pl.when(pl.program_id(0) == 0)
    def _():
        run_ref[...] = jnp.zeros(run_ref.shape, F32)

    y = jnp.concatenate([yf_ref[pr] + yb_ref[pr] for pr in range(N_PAIRS)], axis=1)
    bs = bsum_ref[...]
    mu = _mm_ones_rhs(y, bs) * (1.0 / HEAD_DIM)
    yc = y - mu
    var = _mm_ones_rhs(yc * yc, bs) * (1.0 / HEAD_DIM)
    yn = yc * lax.rsqrt(var + GN_EPS) * gnw_ref[...] + gnb_ref[...]
    yrw = ((yn + bonus_ref[...]) * g_ref[...]).astype(BF16)
    mix = _dg(yconv_ref[...], wout_ref[0:D_CONV, :]) + _dg(yrw, wout_ref[D_CONV:, :])
    x1 = x_ref[...] + g1 * mix

    h2 = _rms(x1, n2g_ref[...]) * (1.0 + sc2) + sh2
    hb = h2.astype(BF16)
    h2_ref[...] = _pack_bf16_pairs(h2)

    scores = _sigmoid(_mm3(h2, rw_ref[...]))
    work = scores + rb_ref[...]
    lane_e = lax.broadcasted_iota(jnp.int32, (tm, N_EXPERTS), 1).astype(F32)
    lane_o = lax.broadcasted_iota(jnp.int32, (tm, LANES), 1)
    idx_acc = jnp.zeros((tm, LANES), F32)
    gate_acc = jnp.zeros((tm, LANES), F32)
    gsum = jnp.zeros((tm, 1), F32)
    sels = []
    for kk in range(TOP_K):
        m = jnp.max(work, axis=-1, keepdims=True)
        sel = jnp.min(jnp.where(work == m, lane_e, float(N_EXPERTS)), axis=-1, keepdims=True)
        hit = lane_e == sel
        sk = jnp.sum(jnp.where(hit, scores, 0.0), axis=-1, keepdims=True)
        idx_acc = jnp.where(lane_o == kk, sel, idx_acc)
        gate_acc = jnp.where(lane_o == kk, sk, gate_acc)
        gsum = gsum + sk
        work = jnp.where(hit, -jnp.inf, work)
        sels.append(sel)
    chosen = jnp.where(work == -jnp.inf, 1.0, 0.0)
    idx_ref[...] = idx_acc.astype(jnp.int32)
    gate_ref[...] = gate_acc / gsum * ROUTED_SCALE

    trow = lax.broadcasted_iota(jnp.int32, (tm, tm), 0)
    tcol = lax.broadcasted_iota(jnp.int32, (tm, tm), 1)
    earlier = jnp.where(tcol < trow, 1.0, 0.0).astype(BF16)
    before = _dg(earlier, chosen.astype(BF16)) + run_ref[0:1, :]
    rank_acc = jnp.zeros((tm, LANES), F32)
    for kk in range(TOP_K):
        rk = jnp.sum(jnp.where(lane_e == sels[kk], before, 0.0), axis=-1, keepdims=True)
        rank_acc = jnp.where(lane_o == kk, rk, rank_acc)
    rank_ref[...] = rank_acc.astype(jnp.int32)
    run_ref[...] = run_ref[...] + jnp.sum(chosen, axis=0, keepdims=True)
    cnt_ref[...] = run_ref[...]

    act = (_silu(_dg(hb, sw1_ref[...])) * _dg(hb, sw3_ref[...])).astype(BF16)
    base_ref[...] = x1 + g2 * _dg(act, sw2_ref[...])


def _mix(x, y, g, bonus, yconv, mod, gnw, gnb, bsum, wout, n2g, rw, rb, sw1, sw3, sw2):
    t = x.shape[0]
    tm = TOK_TILE
    off = (y.shape[2] - t) // tm
    full = lambda arr: pl.BlockSpec(arr.shape, lambda i: (0,) * arr.ndim)
    consts = (mod, gnw, gnb, bsum, wout, n2g, rw, rb, sw1, sw3, sw2)
    cat_spec = lambda w: pl.BlockSpec((tm, w), lambda i: (i + off, 0))
    tok_spec = lambda w: pl.BlockSpec((tm, w), lambda i: (i, 0))
    yf_spec = pl.BlockSpec((None, N_PAIRS, tm, LANES), lambda i: (0, 0, i + off, 0))
    yb_spec = pl.BlockSpec((None, N_PAIRS, tm, LANES), lambda i: (1, 0, i + off, 0))
    return pl.pallas_call(
        _mix_kernel,
        grid=(t // tm,),
        in_specs=[tok_spec(D_MODEL), yf_spec, yb_spec,
                  cat_spec(D_RWKV), cat_spec(D_RWKV), cat_spec(D_CONV)]
                 + [full(c) for c in consts],
        out_specs=[tok_spec(D_MODEL), tok_spec(HALF), tok_spec(LANES), tok_spec(LANES), tok_spec(LANES),
                   pl.BlockSpec((8, N_EXPERTS), lambda i: (0, 0))],
        out_shape=[jax.ShapeDtypeStruct((t, D_MODEL), F32), jax.ShapeDtypeStruct((t, HALF), I32),
                   jax.ShapeDtypeStruct((t, LANES), jnp.int32), jax.ShapeDtypeStruct((t, LANES), F32),
                   jax.ShapeDtypeStruct((t, LANES), jnp.int32), jax.ShapeDtypeStruct((8, N_EXPERTS), F32)],
        scratch_shapes=[pltpu.VMEM((8, N_EXPERTS), F32)],
        compiler_params=_cparams(("arbitrary",)),
        name="mix",
    )(x, y, y, g, bonus, yconv, *consts)


def _slots_kernel(idx_ref, rank_ref, start_ref, dest_ref):
    tm = idx_ref.shape[0]
    lane_e = lax.broadcasted_iota(jnp.int32, (tm, N_EXPERTS), 1)
    lane_o = lax.broadcasted_iota(jnp.int32, (tm, LANES), 1)
    idx = idx_ref[...]
    start = start_ref[...]
    acc = jnp.zeros((tm, LANES), F32)
    for kk in range(TOP_K):
        st = jnp.sum(jnp.where(lane_e == idx[:, kk:kk + 1], start, 0.0), axis=-1, keepdims=True)
        acc = jnp.where(lane_o == kk, st, acc)
    dest_ref[...] = acc.astype(jnp.int32) + rank_ref[...]


def _slots(idx, rank, pad_start):
    t = idx.shape[0]
    tm = SLOTS_TILE
    spec = pl.BlockSpec((tm, LANES), lambda i: (i, 0))
    return pl.pallas_call(
        _slots_kernel,
        grid=(t // tm,),
        in_specs=[spec, spec, pl.BlockSpec((1, N_EXPERTS), lambda i: (0, 0))],
        out_specs=spec,
        out_shape=jax.ShapeDtypeStruct((t, LANES), jnp.int32),
        compiler_params=_cparams(("arbitrary",)),
        name="slots",
    )(idx, rank, pad_start)


def _sc_scatter_rows(rows, dest_km, n_slots):
    info = plsc.get_sparse_core_info()
    nc, ns = info.num_cores, info.num_subcores
    nw = nc * ns
    t, d = rows.shape
    nk = dest_km.shape[0]
    ch = SC_SCATTER_CHUNK
    per_w = t // nw
    n_ch = per_w // ch
    assert t % nw == 0 and per_w % ch == 0
    mesh = plsc.VectorSubcoreMesh(core_axis_name="c", subcore_axis_name="s")
    idx = dest_km.reshape(nk, nw, n_ch, ch).transpose(1, 2, 0, 3).reshape(nw, n_ch * nk, ch)

    @functools.partial(
        pl.kernel, mesh=mesh,
        out_type=jax.ShapeDtypeStruct((n_slots, d), rows.dtype),
        scratch_types=[pltpu.VMEM((n_ch * nk, ch), jnp.int32),
                       pltpu.VMEM((ch, d), rows.dtype),
                       pltpu.SemaphoreType.DMA],
    )
    def scatter_kernel(rows_hbm, idx_hbm, out_hbm, idx_v, rows_v, sem):
        wid = lax.axis_index("s") * nc + lax.axis_index("c")
        base = wid * per_w
        pltpu.sync_copy(idx_hbm.at[wid], idx_v)

        @pl.loop(0, n_ch)
        def _(j):
            pltpu.sync_copy(rows_hbm.at[pl.ds(base + j * ch, ch)], rows_v)
            copies = [pltpu.make_async_copy(rows_v, out_hbm.at[idx_v.at[j * nk + kk]], sem) for kk in range(nk)]
            for cp in copies:
                cp.start()
            for cp in copies:
                cp.wait()

    return scatter_kernel(rows, idx)


def _experts_kernel(nu_ref, nv_ref, first_ref, run_ref, rune_ref, nr_ref, hs_hbm, w1_hbm, w3_hbm, w2_hbm, o_ref,
                    w1s, w3s, w2s, hss, sem, hsem):
    b = pl.program_id(0)
    nbuf = w1s.shape[0]
    n_runs = nr_ref[0]

    def copies(r):
        e = rune_ref[r]
        s = r % nbuf
        return (pltpu.make_async_copy(w1_hbm.at[e], w1s.at[s], sem.at[0, s]),
                pltpu.make_async_copy(w3_hbm.at[e], w3s.at[s], sem.at[1, s]),
                pltpu.make_async_copy(w2_hbm.at[e], w2s.at[s], sem.at[2, s]))

    def start(r):
        @pl.when(r < n_runs)
        def _():
            for cp in copies(r):
                cp.start()

    sb = hss.shape[1]

    def rows_copy(q):
        row0 = q * sb if isinstance(q, int) else pl.multiple_of(q * sb, sb)
        return pltpu.make_async_copy(hs_hbm.at[pl.ds(row0, sb)], hss.at[q % nbuf], hsem.at[q % nbuf])

    def start_rows(q):
        @pl.when(q < nu_ref[0])
        def _():
            rows_copy(q).start()

    @pl.when(b == 0)
    def _():
        for r in range(nbuf - 1):
            start(r)
            start_rows(r)

    start_rows(b + nbuf - 1)
    r = run_ref[b]

    @pl.when(first_ref[b] == 1)
    def _():
        start(r + nbuf - 1)
        for cp in copies(r):
            cp.wait()

    @pl.when(b < nu_ref[0])
    def _():
        s = r % nbuf
        rows_copy(b).wait()
        live = lax.broadcasted_iota(jnp.int32, (sb, 1), 0) < nv_ref[b]
        lo, hi = _unpack_bf16_pairs(jnp.where(live, hss[b % nbuf], 0))
        hb = jnp.concatenate([lo.astype(BF16), hi.astype(BF16)], axis=1)
        act = (_silu(_dg(hb, w1s[s])) * _dg(hb, w3s[s])).astype(BF16)
        o_ref[...] = _pack_bf16_pairs(_dg(act, w2s[s]))

    @pl.when(b >= nu_ref[0])
    def _():
        o_ref[...] = jnp.zeros(o_ref.shape, I32)


def _experts(block_e, n_used, n_valid, hs, w1, w3, w2):
    nblk = block_e.shape[0]
    sb = SLOT_BLOCK
    nbuf = EXPERT_WEIGHT_SLOTS
    blk = jnp.arange(nblk, dtype=jnp.int32)
    prev_e = jnp.concatenate([jnp.full((1,), -1, jnp.int32), block_e[:-1]])
    first = jnp.logical_and(block_e != prev_e, blk < n_used[0]).astype(jnp.int32)
    run_id = jnp.cumsum(first) - 1
    n_runs = jnp.sum(first).reshape(1).astype(jnp.int32)
    run_e = jnp.sum(jnp.where(jnp.logical_and(first[None, :] == 1, run_id[None, :] == blk[:, None]),
                              block_e[None, :], 0), axis=1).astype(jnp.int32)
    grid_spec = pltpu.PrefetchScalarGridSpec(
        num_scalar_prefetch=6,
        grid=(nblk,),
        in_specs=[pl.BlockSpec(memory_space=pl.ANY),
                  pl.BlockSpec(memory_space=pl.ANY),
                  pl.BlockSpec(memory_space=pl.ANY),
                  pl.BlockSpec(memory_space=pl.ANY)],
        out_specs=pl.BlockSpec((sb, HALF), lambda b, *_: (b, 0)),
        scratch_shapes=[pltpu.VMEM((nbuf, D_MODEL, D_EXPERT), BF16),
                        pltpu.VMEM((nbuf, D_MODEL, D_EXPERT), BF16),
                        pltpu.VMEM((nbuf, D_EXPERT, D_MODEL), BF16),
                        pltpu.VMEM((nbuf, sb, HALF), I32),
                        pltpu.SemaphoreType.DMA((3, nbuf)),
                        pltpu.SemaphoreType.DMA((nbuf,))],
    )
    return pl.pallas_call(
        _experts_kernel,
        grid_spec=grid_spec,
        out_shape=jax.ShapeDtypeStruct((nblk * sb, HALF), I32),
        compiler_params=_cparams(("arbitrary",)),
        name="experts",
    )(n_used, n_valid, first, run_id.astype(jnp.int32), run_e, n_runs, hs, w1, w3, w2)


def _sc_gather_rows(table, idx):
    info = plsc.get_sparse_core_info()
    nc, ns = info.num_cores, info.num_subcores
    nw = nc * ns
    n = idx.shape[0]
    d = table.shape[1]
    ch = SC_CHUNK
    per_w = n // nw
    n_ch = per_w // ch
    assert n % nw == 0 and per_w % (2 * ch) == 0
    mesh = plsc.VectorSubcoreMesh(core_axis_name="c", subcore_axis_name="s")

    @functools.partial(
        pl.kernel, mesh=mesh,
        out_type=jax.ShapeDtypeStruct((n, d), table.dtype),
        scratch_types=[pltpu.VMEM((n_ch, ch), jnp.int32),
                       pltpu.VMEM((2, ch, d), table.dtype),
                       pltpu.SemaphoreType.DMA((2,))],
    )
    def gather_kernel(table_hbm, idx_hbm, out_hbm, idx_v, rows_v, sem):
        wid = lax.axis_index("s") * nc + lax.axis_index("c")
        base = wid * per_w
        pltpu.sync_copy(idx_hbm.at[wid], idx_v)

        def gather(j, buf):
            return pltpu.make_async_copy(table_hbm.at[idx_v.at[j]], rows_v.at[buf], sem.at[buf])

        gather(0, 0).start()

        @pl.loop(0, n_ch, step=2)
        def _(j):
            for buf in range(2):
                jj = j + buf
                gather(jj, buf).wait()

                @pl.when(jj + 1 < n_ch)
                def _():
                    gather(jj + 1, 1 - buf).start()

                pltpu.sync_copy(rows_v.at[buf], out_hbm.at[pl.ds(base + jj * ch, ch)])

    return gather_kernel(table, idx.reshape(nw, n_ch, ch))


def _combine_kernel(rows_ref, base_ref, gate_ref, mod_ref, fg_ref, o_ref):
    tm = base_ref.shape[0]
    gate = gate_ref[...]
    acc_lo = jnp.zeros((tm, HALF), F32)
    acc_hi = jnp.zeros((tm, HALF), F32)
    for kk in range(TOP_K):
        lo, hi = _unpack_bf16_pairs(rows_ref[kk])
        acc_lo = acc_lo + gate[:, kk:kk + 1] * lo
        acc_hi = acc_hi + gate[:, kk:kk + 1] * hi
    acc = jnp.concatenate([acc_lo, acc_hi], axis=1)
    g2 = mod_ref[0:1, 5 * D_MODEL:6 * D_MODEL]
    o_ref[...] = _rms(base_ref[...] + g2 * acc, fg_ref[...])


def _combine(dest_km, ye, base, gate, mod, fg):
    t = base.shape[0]
    tm = TOK_TILE
    full = lambda arr: pl.BlockSpec(arr.shape, lambda i: (0,) * arr.ndim)
    rows = _sc_gather_rows(ye, dest_km.reshape(-1)).reshape(TOP_K, t, HALF)
    return pl.pallas_call(
        _combine_kernel,
        grid=(t // tm,),
        in_specs=[pl.BlockSpec((TOP_K, tm, HALF), lambda i: (0, i, 0)),
                  pl.BlockSpec((tm, D_MODEL), lambda i: (i, 0)),
                  pl.BlockSpec((tm, LANES), lambda i: (i, 0)),
                  full(mod), full(fg)],
        out_specs=pl.BlockSpec((tm, D_MODEL), lambda i: (i, 0)),
        out_shape=jax.ShapeDtypeStruct((t, D_MODEL), F32),
        compiler_params=_cparams(("arbitrary",)),
        name="combine",
    )(rows, base, gate, mod, fg)


def _block_tables(counts, n_blocks):
    sb = SLOT_BLOCK
    padded = (counts + sb - 1) // sb * sb
    pad_end = jnp.cumsum(padded)
    pad_start = pad_end - padded
    n_used = pad_end[-1] // sb
    blk = jnp.minimum(jnp.arange(n_blocks, dtype=jnp.int32), n_used - 1)
    block_e = jnp.minimum(jnp.sum(pad_end[None, :] <= (blk * sb)[:, None], axis=1), N_EXPERTS - 1).astype(jnp.int32)
    onehot = block_e[:, None] == jnp.arange(N_EXPERTS, dtype=jnp.int32)[None, :]
    live_end = jnp.sum(jnp.where(onehot, (pad_start + counts)[None, :], 0), axis=1)
    n_valid = jnp.clip(live_end - blk * sb, 0, sb).astype(jnp.int32)
    return block_e, n_used.reshape(1).astype(jnp.int32), n_valid, pad_start


def _blockdiag2(w):
    z = jnp.zeros_like(w[0])
    return jnp.concatenate([jnp.concatenate([w[0], z], axis=1), jnp.concatenate([z, w[1]], axis=1)], axis=0)


def _pad_rows(w, n=8):
    return jnp.concatenate([w, jnp.zeros((n - w.shape[0],) + w.shape[1:], w.dtype)], axis=0)


def kernel(x, c, ctx, c_ctx, ada_w, ada_b, norm1_g, norm2_g, w_in, conv_w, shift_mu, decay_w0, decay_up, iclr_a0, iclr_up, key_xi, key_alpha, bonus_rho, gate_up, gn_w, gn_b, w_out, router_w, router_bias, exp_w1, exp_w3, exp_w2, sh_w1, sh_w3, sh_w2, final_g):
    assert x.shape[0] == 1 and ada_w.shape[0] == 1
    assert ctx.shape[1] == TOK_TILE and x.shape[1] % TOK_TILE == 0
    l = 0
    xs = x[0]
    row = lambda w: w.reshape(1, -1)

    cc = _pad_rows(jnp.stack([c[0], c_ctx], axis=0))
    mod = _ada(cc, ada_w[l], row(ada_b[l]))

    hid = lax.broadcasted_iota(jnp.int32, (D_RWKV, D_RWKV), 0) // HEAD_DIM
    bsum = (hid == hid.T).astype(BF16)

    r4, v4, g, bonus, yconv, lw, kh, kt, a = _proj(
        ctx[0], xs, mod, row(norm1_g[l]), w_in[l].astype(BF16), _pad_rows(conv_w[l]), _pad_rows(shift_mu[l]),
        row(decay_w0[l]), _blockdiag2(decay_up[l]).astype(BF16), row(iclr_a0[l]), _blockdiag2(iclr_up[l]).astype(BF16),
        row(key_xi[l]), row(key_alpha[l]), row(bonus_rho[l]), gate_up[l].astype(BF16), bsum)

    y, w1b, w3b, w2b = _scan(r4, v4, lw, kh, kt, a, exp_w1[l], exp_w3[l], exp_w2[l])

    base, h2, idx, gate, rank, cnt = _mix(
        xs, y, g, bonus, yconv, mod, row(gn_w[l]), row(gn_b[l]), bsum, w_out[l].astype(BF16),
        row(norm2_g[l]), router_w[l], row(router_bias[l]),
        sh_w1[l].astype(BF16), sh_w3[l].astype(BF16), sh_w2[l].astype(BF16))

    n_blocks = xs.shape[0] * TOP_K // SLOT_BLOCK + N_EXPERTS
    block_e, n_used, n_valid, pad_start = _block_tables(cnt[0].astype(jnp.int32), n_blocks)
    dest_km = _slots(idx, rank, pad_start.astype(F32).reshape(1, N_EXPERTS))[:, :TOP_K].T
    hs = _sc_scatter_rows(h2, dest_km, n_blocks * SLOT_BLOCK)
    ye = _experts(block_e, n_used, n_valid, hs, w1b, w3b, w2b)
    out = _combine(dest_km, ye, base, gate, mod, row(final_g))
    return out[None]
```

```python
import functools

import jax
import jax.numpy as jnp
from jax import lax
from jax.experimental import pallas as pl
from jax.experimental.pallas import tpu as pltpu
from jax.experimental.pallas import tpu_sc as plsc

F32 = jnp.float32
BF16 = jnp.bfloat16

D_MODEL = 1024
D_CONV = 512
D_RWKV = 512
HEAD_DIM = 64
N_HEADS = D_RWKV // HEAD_DIM
N_PAIRS = N_HEADS // 2
LORA = 128
P_RWKV = 3 * D_RWKV + 3 * LORA
P_IN = 3 * D_CONV + P_RWKV
GRID_W = 64
N_EXPERTS = 256
TOP_K = 8
D_EXPERT = 256
ROUTED_SCALE = 2.5
RMS_EPS = 1e-6
GN_EPS = 64e-5
DECAY_SCALE = 0.6065306597126334
NORM_EPS = 1e-12

TOK_TILE = 256
SLOTS_TILE = 1024
SCAN_BLOCK = 256
CHUNK = 64
SLOT_BLOCK = 512
EXPERT_WEIGHT_SLOTS = 4
SC_CHUNK = 64
SC_SCATTER_CHUNK = 128
LANES = 128
VMEM_LIMIT = 56 * 1024 * 1024

NN = ((1,), (0,))
NT = ((1,), (1,))
TN = ((0,), (0,))


def _dg(a, b, dims=NN):
    return lax.dot_general(a, b, (dims, ((), ())), preferred_element_type=F32)


def _split2(a):
    hi = a.astype(BF16)
    lo = (a - hi.astype(F32)).astype(BF16)
    return hi, lo


def _mm1(a, b, dims=NN):
    return _dg(a.astype(BF16), b.astype(BF16), dims)


def _mm3(a, b, dims=NN):
    ah, al = _split2(a)
    bh, bl = _split2(b)
    return _dg(ah, bh, dims) + (_dg(ah, bl, dims) + _dg(al, bh, dims))


def _mm_split_lhs(a, b_bf16, dims=NN):
    h, l = _split2(a)
    return _dg(h, b_bf16, dims) + _dg(l, b_bf16, dims)


def _mm_ones_rhs(a, ones_bf16):
    return _dg(a.astype(BF16), ones_bf16)


def _mm_ones_lhs(ones_bf16, b, dims=NN):
    h, l = _split2(b)
    return _dg(ones_bf16, h, dims) + _dg(ones_bf16, l, dims)


HALF = D_MODEL // 2
I32 = jnp.int32


def _pack_bf16_pairs(x):
    return pltpu.pack_elementwise([x[:, 0:HALF], x[:, HALF:]], packed_dtype=BF16)


def _unpack_bf16_pairs(w):
    lo = pltpu.unpack_elementwise(w, index=0, packed_dtype=BF16, unpacked_dtype=F32)
    hi = pltpu.unpack_elementwise(w, index=1, packed_dtype=BF16, unpacked_dtype=F32)
    return lo, hi


def _sigmoid(x):
    return 1.0 / (1.0 + jnp.exp(-x))


def _silu(x):
    return x * _sigmoid(x)


def _rms(xv, g):
    ms = jnp.mean(xv * xv, axis=-1, keepdims=True)
    return xv * lax.rsqrt(ms + RMS_EPS) * g


def _cparams(sem):
    return pltpu.CompilerParams(dimension_semantics=sem, vmem_limit_bytes=VMEM_LIMIT)


def _ada_kernel(c_ref, w_ref, b_ref, o_ref):
    o_ref[...] = _mm3(_silu(c_ref[...]), w_ref[...]) + b_ref[...]


def _ada(cc, ada_w, ada_b):
    n = ada_w.shape[1]
    tn = 1024
    return pl.pallas_call(
        _ada_kernel,
        grid=(n // tn,),
        in_specs=[pl.BlockSpec((8, D_MODEL), lambda j: (0, 0)),
                  pl.BlockSpec((D_MODEL, tn), lambda j: (0, j)),
                  pl.BlockSpec((1, tn), lambda j: (0, j))],
        out_specs=pl.BlockSpec((8, tn), lambda j: (0, j)),
        out_shape=jax.ShapeDtypeStruct((8, n), F32),
        compiler_params=_cparams(("arbitrary",)),
        name="ada",
    )(cc, ada_w, ada_b)


def _proj_kernel(ctx_ref, x_ref, xp_ref, xn_ref, mod_ref, n1g_ref, win_ref, convw_ref, mu_ref,
                 dw0_ref, dup_ref, ia0_ref, iup_ref, xi_ref, kal_ref, rho_ref, gup_ref, bsum_ref,
                 r_ref, v_ref, g_ref, bonus_ref, yconv_ref, lw_ref, kh_ref, kt_ref, a_ref):
    i = pl.program_id(0)
    nt = pl.num_programs(0)
    tm = x_ref.shape[0]
    is_ctx = i == 0
    sh = jnp.where(is_ctx, mod_ref[1:2, 0:D_MODEL], mod_ref[0:1, 0:D_MODEL])
    sc = jnp.where(is_ctx, mod_ref[1:2, D_MODEL:2 * D_MODEL], mod_ref[0:1, D_MODEL:2 * D_MODEL])
    n1g = n1g_ref[...]

    def norm_mod(xv):
        return _rms(xv, n1g) * (1.0 + sc) + sh

    xt = jnp.where(is_ctx, ctx_ref[...], x_ref[...])
    h = norm_mod(jnp.concatenate([xt, xp_ref[...], xn_ref[...]], axis=0)).astype(BF16)
    p_all = _dg(h, win_ref[...])
    p = p_all[0:tm]
    ph = p_all[tm:tm + 16, 3 * D_CONV:]

    rows = lax.broadcasted_iota(jnp.int32, (tm, 1), 0)

    bg = p[:, 0:D_CONV]
    z = p[:, D_CONV:2 * D_CONV] * p[:, 2 * D_CONV:3 * D_CONV]
    col = rows % GRID_W
    zp = jnp.where(col == 0, 0.0, pltpu.roll(z, 1, 0))
    zn = jnp.where(col == GRID_W - 1, 0.0, pltpu.roll(z, tm - 1, 0))
    yconv = bg * (convw_ref[0:1, :] * zp + convw_ref[1:2, :] * z + convw_ref[2:3, :] * zn)
    yconv_ref[...] = yconv.astype(BF16)

    prev_ok = jnp.logical_and(i != 0, i != 1).astype(F32)
    next_ok = jnp.logical_and(i != 0, i != nt - 1).astype(F32)
    cur = p[:, 3 * D_CONV:]
    prev = jnp.where(rows == 0, ph[7:8, :] * prev_ok, pltpu.roll(cur, 1, 0))
    nxt = jnp.where(rows == tm - 1, ph[8:9, :] * next_ok, pltpu.roll(cur, tm - 1, 0))
    ps = cur + mu_ref[0:1, :] * (prev - cur) + mu_ref[1:2, :] * (nxt - cur)

    r = ps[:, 0:D_RWKV]
    k = ps[:, D_RWKV:2 * D_RWKV]
    v = ps[:, 2 * D_RWKV:3 * D_RWKV]
    o = 3 * D_RWKV
    wlo = ps[:, o:o + LORA]
    alo = ps[:, o + LORA:o + 2 * LORA]
    glo = ps[:, o + 2 * LORA:o + 3 * LORA]

    dd = dw0_ref[...] + _mm_split_lhs(jnp.tanh(wlo), dup_ref[...])
    lw = -DECAY_SCALE * _sigmoid(dd)
    a = _sigmoid(ia0_ref[...] + _mm_split_lhs(alo, iup_ref[...]))
    g = _dg(_sigmoid(glo).astype(BF16), gup_ref[...])
    k2 = jnp.concatenate([k, k], axis=1)
    kap = k2 * xi_ref[...]
    kap2 = kap * kap
    bs = bsum_ref[...]
    ss = jnp.concatenate([_mm_ones_rhs(kap2[:, 0:D_RWKV], bs),
                          _mm_ones_rhs(kap2[:, D_RWKV:], bs)], axis=1)
    kh = kap * lax.rsqrt(ss + NORM_EPS)
    kt = k2 * (1.0 + (a - 1.0) * kal_ref[...])
    bon = _mm_ones_rhs(r * rho_ref[...] * (kt[:, 0:D_RWKV] + kt[:, D_RWKV:]), bs)

    g_ref[...] = g
    bonus_ref[...] = bon * v
    for pr in range(N_PAIRS):
        ls = slice(pr * LANES, (pr + 1) * LANES)
        r_ref[pr] = r[:, ls]
        v_ref[pr] = v[:, ls]
        for d in range(2):
            ld = slice(d * D_RWKV + pr * LANES, d * D_RWKV + (pr + 1) * LANES)
            lw_ref[d, pr] = lw[:, ld]
            kh_ref[d, pr] = kh[:, ld]
            kt_ref[d, pr] = kt[:, ld]
            a_ref[d, pr] = a[:, ld]


def _proj(ctx2, x2, mod, n1g, win, convw, mu, dw0, dup, ia0, iup, xi, kal, rho, gup, bsum):
    tm = TOK_TILE
    t = ctx2.shape[0] + x2.shape[0]
    nt = t // tm
    per = tm // 8
    nb8 = x2.shape[0] // 8
    full = lambda arr: pl.BlockSpec(arr.shape, lambda i: (0,) * arr.ndim)
    pair_spec = pl.BlockSpec((N_PAIRS, tm, LANES), lambda i: (0, i, 0))
    dpair_spec = pl.BlockSpec((2, N_PAIRS, tm, LANES), lambda i: (0, 0, i, 0))
    row_spec = pl.BlockSpec((tm, D_RWKV), lambda i: (i, 0))
    pair_shape = jax.ShapeDtypeStruct((N_PAIRS, t, LANES), F32)
    dpair_shape = jax.ShapeDtypeStruct((2, N_PAIRS, t, LANES), F32)
    consts = (mod, n1g, win, convw, mu, dw0, dup, ia0, iup, xi, kal, rho, gup, bsum)
    return pl.pallas_call(
        _proj_kernel,
        grid=(nt,),
        in_specs=[pl.BlockSpec((tm, D_MODEL), lambda i: (0, 0)),
                  pl.BlockSpec((tm, D_MODEL), lambda i: (jnp.maximum(i - 1, 0), 0)),
                  pl.BlockSpec((8, D_MODEL), lambda i: (jnp.maximum((i - 1) * per - 1, 0), 0)),
                  pl.BlockSpec((8, D_MODEL), lambda i: (jnp.clip(i * per, 0, nb8 - 1), 0))]
                 + [full(c) for c in consts],
        out_specs=[pair_spec, pair_spec, row_spec, row_spec, row_spec,
                   dpair_spec, dpair_spec, dpair_spec, dpair_spec],
        out_shape=[pair_shape, pair_shape,
                   jax.ShapeDtypeStruct((t, D_RWKV), F32), jax.ShapeDtypeStruct((t, D_RWKV), F32),
                   jax.ShapeDtypeStruct((t, D_CONV), BF16),
                   dpair_shape, dpair_shape, dpair_shape, dpair_shape],
        compiler_params=_cparams(("arbitrary",)),
        name="proj",
    )(ctx2, x2, x2, x2, *consts)


def _scan_kernel(r_ref, v_ref, lw_ref, kh_ref, kt_ref, a_ref, w1_ref, w3_ref, w2_ref,
                 y_ref, w1b_ref, w3b_ref, w2b_ref, q_ref):
    w1b_ref[...] = w1_ref[...].astype(BF16)
    w3b_ref[...] = w3_ref[...].astype(BF16)
    w2b_ref[...] = w2_ref[...].astype(BF16)

    d = pl.program_id(0)
    rev = d == 1
    c = CHUNK
    npair = r_ref.shape[0]
    nsub = r_ref.shape[1] // c

    row = lax.broadcasted_iota(jnp.int32, (c, c), 0)
    col = lax.broadcasted_iota(jnp.int32, (c, c), 1)
    strict = (col - row) * (1 - 2 * d) < 0
    eye = row == col
    incl = jnp.logical_or(strict, eye)
    incl_bf = jnp.where(incl, 1.0, 0.0).astype(BF16)
    eye_f = jnp.where(eye, 1.0, 0.0)
    head0 = lax.broadcasted_iota(jnp.int32, (c, LANES), 1) < HEAD_DIM
    row2 = lax.broadcasted_iota(jnp.int32, (LANES, LANES), 0)
    col2 = lax.broadcasted_iota(jnp.int32, (LANES, LANES), 1)
    same_head = (row2 < HEAD_DIM) == (col2 < HEAD_DIM)
    eye2 = row2 == col2

    subs = range(npair * nsub)
    units = [(s, h) for s in subs for h in range(2)]
    off = [pl.multiple_of(jnp.where(rev, nsub - 1 - s, s) * c, c) for s in range(nsub)]
    sls = [(pp, pl.ds(off[s], c), slice(None)) for pp in range(npair) for s in range(nsub)]
    v = [v_ref[sl] for sl in sls]
    lw = [lw_ref[sl] for sl in sls]
    cum = [_mm_ones_lhs(incl_bf, lw[s]) for s in subs]
    al, be, kk, rr, bew, kkw, wc = [], [], [], [], [], [], []
    for s in subs:
        cum_last = jnp.where(rev, cum[s][0:1, :], cum[s][c - 1:c, :])
        e_end = jnp.exp(cum_last - cum[s])
        e_neg = jnp.exp(-cum[s])
        kh = kh_ref[sls[s]]
        kt = kt_ref[sls[s]]
        kha = kh * a_ref[sls[s]]
        al.append(kh * jnp.exp(cum[s] - lw[s]))
        be.append(-(kha * e_neg))
        kk.append(kt * e_neg)
        rr.append(r_ref[sls[s]] * jnp.exp(cum[s]))
        bew.append(-(kha * e_end))
        kkw.append(kt * e_end)
        wc.append(jnp.exp(cum_last))

    lhs = {}
    for s, h in units:
        hm = head0 if h == 0 else jnp.logical_not(head0)
        lhs[s, h] = jnp.concatenate([jnp.where(hm, al[s], 0.0), jnp.where(hm, rr[s], 0.0)], axis=0)
    gb = {u: _mm1(lhs[u], be[u[0]], NT) for u in units}
    gk = {u: _mm1(lhs[u], kk[u[0]], NT) for u in units}
    a_ab = {u: jnp.where(strict, gb[u][0:c], 0.0) for u in units}
    a_ak = {u: jnp.where(strict, gk[u][0:c], 0.0) for u in units}
    a_rb = {u: jnp.where(incl, gb[u][c:2 * c], 0.0) for u in units}
    a_rk = {u: jnp.where(incl, gk[u][c:2 * c], 0.0) for u in units}

    pw = dict(a_ab)
    tinv = {u: eye_f + a_ab[u] for u in units}
    akrk_v = {u: _mm1(jnp.concatenate([a_ak[u], a_rk[u]], axis=0), v[u[0]]) for u in units}
    avk = {u: akrk_v[u][0:c] for u in units}
    rk_v = {u: akrk_v[u][c:2 * c] for u in units}
    for _ in range(c.bit_length() - 2):
        pw = {u: _mm1(pw[u], pw[u]) for u in units}
        tinv = {u: tinv[u] + _mm1(tinv[u], pw[u]) for u in units}
    tu = {u: _mm1(tinv[u], jnp.concatenate([al[u[0]], avk[u]], axis=1)) for u in units}
    at = [jnp.where(head0, tu[s, 0][:, 0:LANES], tu[s, 1][:, 0:LANES]) for s in subs]
    ut = [jnp.where(head0, tu[s, 0][:, LANES:], tu[s, 1][:, LANES:]) for s in subs]

    rb = {u: _mm1(a_rb[u], jnp.concatenate([at[u[0]], ut[u[0]]], axis=1)) for u in units}
    rb_at = {u: rb[u][:, 0:LANES] for u in units}
    rb_ut = {u: rb[u][:, LANES:] for u in units}
    md = [_mm1(bew[s], jnp.concatenate([at[s], ut[s]], axis=1), TN) for s in subs]
    m_off = [md[s][:, 0:LANES] for s in subs]
    d_u = [md[s][:, LANES:] for s in subs]
    d_v = [_mm1(kkw[s], v[s], TN) for s in subs]
    rh = [rr[s] + jnp.where(head0, rb_at[s, 0], rb_at[s, 1]) for s in subs]
    yh = [jnp.where(head0, rb_ut[s, 0] + rk_v[s, 0], rb_ut[s, 1] + rk_v[s, 1]) for s in subs]
    mt = [jnp.where(eye2, wc[s], 0.0) + jnp.where(same_head, m_off[s], 0.0) for s in subs]
    dt = [jnp.where(same_head, d_u[s] + d_v[s], 0.0) for s in subs]

    @pl.when(pl.program_id(1) == 0)
    def _():
        q_ref[...] = jnp.zeros(q_ref.shape, F32)

    q = [q_ref[pp] for pp in range(npair)]
    for si in range(nsub):
        ys = [_mm1(rh[pp * nsub + si], q[pp]) for pp in range(npair)]
        qn = [_mm1(mt[pp * nsub + si], q[pp]) for pp in range(npair)]
        for pp in range(npair):
            y_ref[sls[pp * nsub + si]] = ys[pp] + yh[pp * nsub + si]
            q[pp] = qn[pp] + dt[pp * nsub + si]
    for pp in range(npair):
        q_ref[pp] = q[pp]


def _scan(r4, v4, lw, kh, kt, a, w1, w3, w2):
    t = r4.shape[1]
    tb = SCAN_BLOCK
    nb = t // tb
    ne = w1.shape[0]
    epb = -(-ne // (2 * nb))
    assert ne % epb == 0
    last = ne // epb - 1

    def blk(d, j):
        return jnp.where(d == 0, j, jnp.where(j == 0, 0, nb - j))

    def wblk(d, j):
        return (jnp.minimum(d * nb + j, last), 0, 0)

    shared = pl.BlockSpec((N_PAIRS, tb, LANES), lambda d, j: (0, blk(d, j), 0))
    perdir = pl.BlockSpec((None, N_PAIRS, tb, LANES), lambda d, j: (d, 0, blk(d, j), 0))
    w13 = pl.BlockSpec((epb, D_MODEL, D_EXPERT), wblk)
    w2s = pl.BlockSpec((epb, D_EXPERT, D_MODEL), wblk)
    return pl.pallas_call(
        _scan_kernel,
        grid=(2, nb),
        in_specs=[shared, shared, perdir, perdir, perdir, perdir, w13, w13, w2s],
        out_specs=[perdir, w13, w13, w2s],
        out_shape=[jax.ShapeDtypeStruct((2, N_PAIRS, t, LANES), F32),
                   jax.ShapeDtypeStruct(w1.shape, BF16), jax.ShapeDtypeStruct(w3.shape, BF16),
                   jax.ShapeDtypeStruct(w2.shape, BF16)],
        scratch_shapes=[pltpu.VMEM((N_PAIRS, LANES, LANES), F32)],
        compiler_params=_cparams(("arbitrary", "arbitrary")),
        name="scan",
    )(r4, v4, lw, kh, kt, a, w1, w3, w2)


def _mix_kernel(x_ref, yf_ref, yb_ref, g_ref, bonus_ref, yconv_ref, mod_ref, gnw_ref, gnb_ref, bsum_ref,
                wout_ref, n2g_ref, rw_ref, rb_ref, sw1_ref, sw3_ref, sw2_ref,
                base_ref, h2_ref, idx_ref, gate_ref, rank_ref, cnt_ref, run_ref):
    tm = x_ref.shape[0]
    g1 = mod_ref[0:1, 2 * D_MODEL:3 * D_MODEL]
    sh2 = mod_ref[0:1, 3 * D_MODEL:4 * D_MODEL]
    sc2 = mod_ref[0:1, 4 * D_MODEL:5 * D_MODEL]
    g2 = mod_ref[0:1, 5 * D_MODEL:6 * D_MODEL]

    @pl.when(pl.program_id(0) == 0)
    def _():
        run_ref[...] = jnp.zeros(run_ref.shape, F32)

    y = jnp.concatenate([yf_ref[pr] + yb_ref[pr] for pr in range(N_PAIRS)], axis=1)
    bs = bsum_ref[...]
    mu = _mm_ones_rhs(y, bs) * (1.0 / HEAD_DIM)
    yc = y - mu
    var = _mm_ones_rhs(yc * yc, bs) * (1.0 / HEAD_DIM)
    yn = yc * lax.rsqrt(var + GN_EPS) * gnw_ref[...] + gnb_ref[...]
    yrw = ((yn + bonus_ref[...]) * g_ref[...]).astype(BF16)
    mix = _dg(yconv_ref[...], wout_ref[0:D_CONV, :]) + _dg(yrw, wout_ref[D_CONV:, :])
    x1 = x_ref[...] + g1 * mix

    h2 = _rms(x1, n2g_ref[...]) * (1.0 + sc2) + sh2
    hb = h2.astype(BF16)
    h2_ref[...] = _pack_bf16_pairs(h2)

    scores = _sigmoid(_mm3(h2, rw_ref[...]))
    work = scores + rb_ref[...]
    lane_e = lax.broadcasted_iota(jnp.int32, (tm, N_EXPERTS), 1).astype(F32)
    lane_o = lax.broadcasted_iota(jnp.int32, (tm, LANES), 1)
    idx_acc = jnp.zeros((tm, LANES), F32)
    gate_acc = jnp.zeros((tm, LANES), F32)
    gsum = jnp.zeros((tm, 1), F32)
    sels = []
    for kk in range(TOP_K):
        m = jnp.max(work, axis=-1, keepdims=True)
        sel = jnp.min(jnp.where(work == m, lane_e, float(N_EXPERTS)), axis=-1, keepdims=True)
        hit = lane_e == sel
        sk = jnp.sum(jnp.where(hit, scores, 0.0), axis=-1, keepdims=True)
        idx_acc = jnp.where(lane_o == kk, sel, idx_acc)
        gate_acc = jnp.where(lane_o == kk, sk, gate_acc)
        gsum = gsum + sk
        work = jnp.where(hit, -jnp.inf, work)
        sels.append(sel)
    chosen = jnp.where(work == -jnp.inf, 1.0, 0.0)
    idx_ref[...] = idx_acc.astype(jnp.int32)
    gate_ref[...] = gate_acc / gsum * ROUTED_SCALE

    trow = lax.broadcasted_iota(jnp.int32, (tm, tm), 0)
    tcol = lax.broadcasted_iota(jnp.int32, (tm, tm), 1)
    earlier = jnp.where(tcol < trow, 1.0, 0.0).astype(BF16)
    before = _dg(earlier, chosen.astype(BF16)) + run_ref[0:1, :]
    rank_acc = jnp.zeros((tm, LANES), F32)
    for kk in range(TOP_K):
        rk = jnp.sum(jnp.where(lane_e == sels[kk], before, 0.0), axis=-1, keepdims=True)
        rank_acc = jnp.where(lane_o == kk, rk, rank_acc)
    rank_ref[...] = rank_acc.astype(jnp.int32)
    run_ref[...] = run_ref[...] + jnp.sum(chosen, axis=0, keepdims=True)
    cnt_ref[...] = run_ref[...]

    act = (_silu(_dg(hb, sw1_ref[...])) * _dg(hb, sw3_ref[...])).astype(BF16)
    base_ref[...] = x1 + g2 * _dg(act, sw2_ref[...])


def _mix(x, y, g, bonus, yconv, mod, gnw, gnb, bsum, wout, n2g, rw, rb, sw1, sw3, sw2):
    t = x.shape[0]
    tm = TOK_TILE
    off = (y.shape[2] - t) // tm
    full = lambda arr: pl.BlockSpec(arr.shape, lambda i: (0,) * arr.ndim)
    consts = (mod, gnw, gnb, bsum, wout, n2g, rw, rb, sw1, sw3, sw2)
    cat_spec = lambda w: pl.BlockSpec((tm, w), lambda i: (i + off, 0))
    tok_spec = lambda w: pl.BlockSpec((tm, w), lambda i: (i, 0))
    yf_spec = pl.BlockSpec((None, N_PAIRS, tm, LANES), lambda i: (0, 0, i + off, 0))
    yb_spec = pl.BlockSpec((None, N_PAIRS, tm, LANES), lambda i: (1, 0, i + off, 0))
    return pl.pallas_call(
        _mix_kernel,
        grid=(t // tm,),
        in_specs=[tok_spec(D_MODEL), yf_spec, yb_spec,
                  cat_spec(D_RWKV), cat_spec(D_RWKV), cat_spec(D_CONV)]
                 + [full(c) for c in consts],
        out_specs=[tok_spec(D_MODEL), tok_spec(HALF), tok_spec(LANES), tok_spec(LANES), tok_spec(LANES),
                   pl.BlockSpec((8, N_EXPERTS), lambda i: (0, 0))],
        out_shape=[jax.ShapeDtypeStruct((t, D_MODEL), F32), jax.ShapeDtypeStruct((t, HALF), I32),
                   jax.ShapeDtypeStruct((t, LANES), jnp.int32), jax.ShapeDtypeStruct((t, LANES), F32),
                   jax.ShapeDtypeStruct((t, LANES), jnp.int32), jax.ShapeDtypeStruct((8, N_EXPERTS), F32)],
        scratch_shapes=[pltpu.VMEM((8, N_EXPERTS), F32)],
        compiler_params=_cparams(("arbitrary",)),
        name="mix",
    )(x, y, y, g, bonus, yconv, *consts)


def _slots_kernel(idx_ref, rank_ref, start_ref, dest_ref):
    tm = idx_ref.shape[0]
    lane_e = lax.broadcasted_iota(jnp.int32, (tm, N_EXPERTS), 1)
    lane_o = lax.broadcasted_iota(jnp.int32, (tm, LANES), 1)
    idx = idx_ref[...]
    start = start_ref[...]
    acc = jnp.zeros((tm, LANES), F32)
    for kk in range(TOP_K):
        st = jnp.sum(jnp.where(lane_e == idx[:, kk:kk + 1], start, 0.0), axis=-1, keepdims=True)
        acc = jnp.where(lane_o == kk, st, acc)
    dest_ref[...] = acc.astype(jnp.int32) + rank_ref[...]


def _slots(idx, rank, pad_start):
    t = idx.shape[0]
    tm = SLOTS_TILE
    spec = pl.BlockSpec((tm, LANES), lambda i: (i, 0))
    return pl.pallas_call(
        _slots_kernel,
        grid=(t // tm,),
        in_specs=[spec, spec, pl.BlockSpec((1, N_EXPERTS), lambda i: (0, 0))],
        out_specs=spec,
        out_shape=jax.ShapeDtypeStruct((t, LANES), jnp.int32),
        compiler_params=_cparams(("arbitrary",)),
        name="slots",
    )(idx, rank, pad_start)


def _sc_scatter_rows(rows, dest_km, n_slots):
    info = plsc.get_sparse_core_info()
    nc, ns = info.num_cores, info.num_subcores
    nw = nc * ns
    t, d = rows.shape
    nk = dest_km.shape[0]
    ch = SC_SCATTER_CHUNK
    per_w = t // nw
    n_ch = per_w // ch
    assert t % nw == 0 and per_w % ch == 0
    mesh = plsc.VectorSubcoreMesh(core_axis_name="c", subcore_axis_name="s")
    idx = dest_km.reshape(nk, nw, n_ch, ch).transpose(1, 2, 0, 3).reshape(nw, n_ch * nk, ch)

    @functools.partial(
        pl.kernel, mesh=mesh,
        out_type=jax.ShapeDtypeStruct((n_slots, d), rows.dtype),
        scratch_types=[pltpu.VMEM((n_ch * nk, ch), jnp.int32),
                       pltpu.VMEM((ch, d), rows.dtype),
                       pltpu.SemaphoreType.DMA],
    )
    def scatter_kernel(rows_hbm, idx_hbm, out_hbm, idx_v, rows_v, sem):
        wid = lax.axis_index("s") * nc + lax.axis_index("c")
        base = wid * per_w
        pltpu.sync_copy(idx_hbm.at[wid], idx_v)

        @pl.loop(0, n_ch)
        def _(j):
            pltpu.sync_copy(rows_hbm.at[pl.ds(base + j * ch, ch)], rows_v)
            copies = [pltpu.make_async_copy(rows_v, out_hbm.at[idx_v.at[j * nk + kk]], sem) for kk in range(nk)]
            for cp in copies:
                cp.start()
            for cp in copies:
                cp.wait()

    return scatter_kernel(rows, idx)


def _experts_kernel(nu_ref, nv_ref, first_ref, run_ref, rune_ref, nr_ref, hs_hbm, w1_hbm, w3_hbm, w2_hbm, o_ref,
                    w1s, w3s, w2s, hss, sem, hsem):
    b = pl.program_id(0)
    nbuf = w1s.shape[0]
    n_runs = nr_ref[0]

    def copies(r):
        e = rune_ref[r]
        s = r % nbuf
        return (pltpu.make_async_copy(w1_hbm.at[e], w1s.at[s], sem.at[0, s]),
                pltpu.make_async_copy(w3_hbm.at[e], w3s.at[s], sem.at[1, s]),
                pltpu.make_async_copy(w2_hbm.at[e], w2s.at[s], sem.at[2, s]))

    def start(r):
        @pl.when(r < n_runs)
        def _():
            for cp in copies(r):
                cp.start()

    sb = hss.shape[1]

    def rows_copy(q):
        row0 = q * sb if isinstance(q, int) else pl.multiple_of(q * sb, sb)
        return pltpu.make_async_copy(hs_hbm.at[pl.ds(row0, sb)], hss.at[q % nbuf], hsem.at[q % nbuf])

    def start_rows(q):
        @pl.when(q < nu_ref[0])
        def _():
            rows_copy(q).start()

    @pl.when(b == 0)
    def _():
        for r in range(nbuf - 1):
            start(r)
            start_rows(r)

    start_rows(b + nbuf - 1)
    r = run_ref[b]

    @pl.when(first_ref[b] == 1)
    def _():
        start(r + nbuf - 1)
        for cp in copies(r):
            cp.wait()

    @pl.when(b < nu_ref[0])
    def _():
        s = r % nbuf
        rows_copy(b).wait()
        live = lax.broadcasted_iota(jnp.int32, (sb, 1), 0) < nv_ref[b]
        lo, hi = _unpack_bf16_pairs(jnp.where(live, hss[b % nbuf], 0))
        hb = jnp.concatenate([lo.astype(BF16), hi.astype(BF16)], axis=1)
        act = (_silu(_dg(hb, w1s[s])) * _dg(hb, w3s[s])).astype(BF16)
        o_ref[...] = _pack_bf16_pairs(_dg(act, w2s[s]))

    @pl.when(b >= nu_ref[0])
    def _():
        o_ref[...] = jnp.zeros(o_ref.shape, I32)


def _experts(block_e, n_used, n_valid, hs, w1, w3, w2):
    nblk = block_e.shape[0]
    sb = SLOT_BLOCK
    nbuf = EXPERT_WEIGHT_SLOTS
    blk = jnp.arange(nblk, dtype=jnp.int32)
    prev_e = jnp.concatenate([jnp.full((1,), -1, jnp.int32), block_e[:-1]])
    first = jnp.logical_and(block_e != prev_e, blk < n_used[0]).astype(jnp.int32)
    run_id = jnp.cumsum(first) - 1
    n_runs = jnp.sum(first).reshape(1).astype(jnp.int32)
    run_e = jnp.sum(jnp.where(jnp.logical_and(first[None, :] == 1, run_id[None, :] == blk[:, None]),
                              block_e[None, :], 0), axis=1).astype(jnp.int32)
    grid_spec = pltpu.PrefetchScalarGridSpec(
        num_scalar_prefetch=6,
        grid=(nblk,),
        in_specs=[pl.BlockSpec(memory_space=pl.ANY),
                  pl.BlockSpec(memory_space=pl.ANY),
                  pl.BlockSpec(memory_space=pl.ANY),
                  pl.BlockSpec(memory_space=pl.ANY)],
        out_specs=pl.BlockSpec((sb, HALF), lambda b, *_: (b, 0)),
        scratch_shapes=[pltpu.VMEM((nbuf, D_MODEL, D_EXPERT), BF16),
                        pltpu.VMEM((nbuf, D_MODEL, D_EXPERT), BF16),
                        pltpu.VMEM((nbuf, D_EXPERT, D_MODEL), BF16),
                        pltpu.VMEM((nbuf, sb, HALF), I32),
                        pltpu.SemaphoreType.DMA((3, nbuf)),
                        pltpu.SemaphoreType.DMA((nbuf,))],
    )
    return pl.pallas_call(
        _experts_kernel,
        grid_spec=grid_spec,
        out_shape=jax.ShapeDtypeStruct((nblk * sb, HALF), I32),
        compiler_params=_cparams(("arbitrary",)),
        name="experts",
    )(n_used, n_valid, first, run_id.astype(jnp.int32), run_e, n_runs, hs, w1, w3, w2)


def _sc_gather_rows(table, idx):
    info = plsc.get_sparse_core_info()
    nc, ns = info.num_cores, info.num_subcores
    nw = nc * ns
    n = idx.shape[0]
    d = table.shape[1]
    ch = SC_CHUNK
    per_w = n // nw
    n_ch = per_w // ch
    assert n % nw == 0 and per_w % (2 * ch) == 0
    mesh = plsc.VectorSubcoreMesh(core_axis_name="c", subcore_axis_name="s")

    @functools.partial(
        pl.kernel, mesh=mesh,
        out_type=jax.ShapeDtypeStruct((n, d), table.dtype),
        scratch_types=[pltpu.VMEM((n_ch, ch), jnp.int32),
                       pltpu.VMEM((2, ch, d), table.dtype),
                       pltpu.SemaphoreType.DMA((2,))],
    )
    def gather_kernel(table_hbm, idx_hbm, out_hbm, idx_v, rows_v, sem):
        wid = lax.axis_index("s") * nc + lax.axis_index("c")
        base = wid * per_w
        pltpu.sync_copy(idx_hbm.at[wid], idx_v)

        def gather(j, buf):
            return pltpu.make_async_copy(table_hbm.at[idx_v.at[j]], rows_v.at[buf], sem.at[buf])

        gather(0, 0).start()

        @pl.loop(0, n_ch, step=2)
        def _(j):
            for buf in range(2):
                jj = j + buf
                gather(jj, buf).wait()

                @pl.when(jj + 1 < n_ch)
                def _():
                    gather(jj + 1, 1 - buf).start()

                pltpu.sync_copy(rows_v.at[buf], out_hbm.at[pl.ds(base + jj * ch, ch)])

    return gather_kernel(table, idx.reshape(nw, n_ch, ch))


def _combine_kernel(rows_ref, base_ref, gate_ref, mod_ref, fg_ref, o_ref):
    tm = base_ref.shape[0]
    gate = gate_ref[...]
    acc_lo = jnp.zeros((tm, HALF), F32)
    acc_hi = jnp.zeros((tm, HALF), F32)
    for kk in range(TOP_K):
        lo, hi = _unpack_bf16_pairs(rows_ref[kk])
        acc_lo = acc_lo + gate[:, kk:kk + 1] * lo
        acc_hi = acc_hi + gate[:, kk:kk + 1] * hi
    acc = jnp.concatenate([acc_lo, acc_hi], axis=1)
    g2 = mod_ref[0:1, 5 * D_MODEL:6 * D_MODEL]
    o_ref[...] = _rms(base_ref[...] + g2 * acc, fg_ref[...])


def _combine(dest_km, ye, base, gate, mod, fg):
    t = base.shape[0]
    tm = TOK_TILE
    full = lambda arr: pl.BlockSpec(arr.shape, lambda i: (0,) * arr.ndim)
    rows = _sc_gather_rows(ye, dest_km.reshape(-1)).reshape(TOP_K, t, HALF)
    return pl.pallas_call(
        _combine_kernel,
        grid=(t // tm,),
        in_specs=[pl.BlockSpec((TOP_K, tm, HALF), lambda i: (0, i, 0)),
                  pl.BlockSpec((tm, D_MODEL), lambda i: (i, 0)),
                  pl.BlockSpec((tm, LANES), lambda i: (i, 0)),
                  full(mod), full(fg)],
        out_specs=pl.BlockSpec((tm, D_MODEL), lambda i: (i, 0)),
        out_shape=jax.ShapeDtypeStruct((t, D_MODEL), F32),
        compiler_params=_cparams(("arbitrary",)),
        name="combine",
    )(rows, base, gate, mod, fg)


def _block_tables(counts, n_blocks):
    sb = SLOT_BLOCK
    padded = (counts + sb - 1) // sb * sb
    pad_end = jnp.cumsum(padded)
    pad_start = pad_end - padded
    n_used = pad_end[-1] // sb
    blk = jnp.minimum(jnp.arange(n_blocks, dtype=jnp.int32), n_used - 1)
    block_e = jnp.minimum(jnp.sum(pad_end[None, :] <= (blk * sb)[:, None], axis=1), N_EXPERTS - 1).astype(jnp.int32)
    onehot = block_e[:, None] == jnp.arange(N_EXPERTS, dtype=jnp.int32)[None, :]
    live_end = jnp.sum(jnp.where(onehot, (pad_start + counts)[None, :], 0), axis=1)
    n_valid = jnp.clip(live_end - blk * sb, 0, sb).astype(jnp.int32)
    return block_e, n_used.reshape(1).astype(jnp.int32), n_valid, pad_start


def _blockdiag2(w):
    z = jnp.zeros_like(w[0])
    return jnp.concatenate([jnp.concatenate([w[0], z], axis=1), jnp.concatenate([z, w[1]], axis=1)], axis=0)


def _pad_rows(w, n=8):
    return jnp.concatenate([w, jnp.zeros((n - w.shape[0],) + w.shape[1:], w.dtype)], axis=0)


def kernel(x, c, ctx, c_ctx, ada_w, ada_b, norm1_g, norm2_g, w_in, conv_w, shift_mu, decay_w0, decay_up, iclr_a0, iclr_up, key_xi, key_alpha, bonus_rho, gate_up, gn_w, gn_b, w_out, router_w, router_bias, exp_w1, exp_w3, exp_w2, sh_w1, sh_w3, sh_w2, final_g):
    assert x.shape[0] == 1 and ada_w.shape[0] == 1
    assert ctx.shape[1] == TOK_TILE and x.shape[1] % TOK_TILE == 0
    l = 0
    xs = x[0]
    row = lambda w: w.reshape(1, -1)

    cc = _pad_rows(jnp.stack([c[0], c_ctx], axis=0))
    mod = _ada(cc, ada_w[l], row(ada_b[l]))

    hid = lax.broadcasted_iota(jnp.int32, (D_RWKV, D_RWKV), 0) // HEAD_DIM
    bsum = (hid == hid.T).astype(BF16)

    r4, v4, g, bonus, yconv, lw, kh, kt, a = _proj(
        ctx[0], xs, mod, row(norm1_g[l]), w_in[l].astype(BF16), _pad_rows(conv_w[l]), _pad_rows(shift_mu[l]),
        row(decay_w0[l]), _blockdiag2(decay_up[l]).astype(BF16), row(iclr_a0[l]), _blockdiag2(iclr_up[l]).astype(BF16),
        row(key_xi[l]), row(key_alpha[l]), row(bonus_rho[l]), gate_up[l].astype(BF16), bsum)

    y, w1b, w3b, w2b = _scan(r4, v4, lw, kh, kt, a, exp_w1[l], exp_w3[l], exp_w2[l])

    base, h2, idx, gate, rank, cnt = _mix(
        xs, y, g, bonus, yconv, mod, row(gn_w[l]), row(gn_b[l]), bsum, w_out[l].astype(BF16),
        row(norm2_g[l]), router_w[l], row(router_bias[l]),
        sh_w1[l].astype(BF16), sh_w3[l].astype(BF16), sh_w2[l].astype(BF16))

    n_blocks = xs.shape[0] * TOP_K // SLOT_BLOCK + N_EXPERTS
    block_e, n_used, n_valid, pad_start = _block_tables(cnt[0].astype(jnp.int32), n_blocks)
    dest_km = _slots(idx, rank, pad_start.astype(F32).reshape(1, N_EXPERTS))[:, :TOP_K].T
    hs = _sc_scatter_rows(h2, dest_km, n_blocks * SLOT_BLOCK)
    ye = _experts(block_e, n_used, n_valid, hs, w1b, w3b, w2b)
    out = _combine(dest_km, ye, base, gate, mod, row(final_g))
    return out[None]
```
